```python
import math
import jax
import jax.numpy as jnp
from jax import lax
import numpy as np

D_MODEL = 1024
BATCH = 16
SEQ = 2048
DEPTH = 2

HEAD_DIM = 64
DSA_HEADS = 4
IDX_HEADS = 4
IDX_DIM = 32
DSA_TOPK = 256
MOBA_HEADS = 4
MOBA_BLOCK = 256
MOBA_TOPK = 3
MOBA_Q_CHUNK = 32
MLA_HEADS = 4
MLA_Q_LORA = 384
MLA_KV_LORA = 256
MLA_NOPE = 64
MLA_ROPE = 32
MLA_V = 64
ROPE_THETA = 10000.0
DIL_PATTERNS = ((128, 1), (512, 4), (2048, 16))
DIL_GROUPS = 3
DIL_HEADS = 4
N_BRANCH = 4
BRANCH_WIDTH = 4 * HEAD_DIM
D_FF = 2816
REL_BUCKETS = 32
REL_MAX_DIST = 2048
N_BIAS_HEADS = DSA_HEADS + MOBA_HEADS + DIL_GROUPS * DIL_HEADS
Q_BLOCK = 128
RMS_EPS = 1e-6

IN_SPLITS = (
    DSA_HEADS * HEAD_DIM,
    HEAD_DIM,
    HEAD_DIM,
    IDX_HEADS * IDX_DIM,
    IDX_DIM,
    IDX_HEADS,
    3 * MOBA_HEADS * HEAD_DIM,
    MLA_Q_LORA,
    MLA_KV_LORA,
    MLA_ROPE,
    3 * DIL_GROUPS * DIL_HEADS * HEAD_DIM,
    N_BRANCH * D_MODEL,
)
IN_WIDTH = sum(IN_SPLITS)
IN_OFFSETS = tuple(int(o) for o in np.cumsum(IN_SPLITS)[:-1])

kernel_name = 'hybrid_gated_sparse_mixer_trunk'


def rms_norm(x, gain):
    xf = x.astype(jnp.float32)
    xf = xf * lax.rsqrt(jnp.mean(xf * xf, axis=-1, keepdims=True) + RMS_EPS)
    return (xf * gain.astype(jnp.float32)).astype(x.dtype)


def rel_bucket(dist):
    n = jnp.maximum(dist, 0)
    max_exact = REL_BUCKETS // 2
    nf = jnp.maximum(n, 1).astype(jnp.float32)
    log_b = max_exact + (jnp.log(nf / max_exact) / math.log(REL_MAX_DIST / max_exact)
                         * (REL_BUCKETS - max_exact)).astype(jnp.int32)
    return jnp.where(n < max_exact, n, jnp.minimum(log_b, REL_BUCKETS - 1))


def masked_softmax(logits, mask):
    return jax.nn.softmax(jnp.where(mask, logits, -jnp.inf), axis=-1)


def apply_rope(x, pos):
    half = x.shape[-1] // 2
    freqs = ROPE_THETA ** (-jnp.arange(half, dtype=jnp.float32) / half)
    ang = pos.astype(jnp.float32)[:, None] * freqs[None, :]
    cos = jnp.cos(ang)[None, :, None, :]
    sin = jnp.sin(ang)[None, :, None, :]
    xf = x.astype(jnp.float32)
    x1, x2 = xf[..., :half], xf[..., half:]
    return jnp.concatenate([x1 * cos - x2 * sin, x1 * sin + x2 * cos], axis=-1).astype(x.dtype)


def to_blocks(a, block):
    b, t = a.shape[:2]
    return jnp.swapaxes(a.reshape(b, t // block, block, *a.shape[2:]), 0, 1)


def from_blocks(a):
    a = jnp.swapaxes(a, 0, 1)
    return a.reshape(a.shape[0], a.shape[1] * a.shape[2], *a.shape[3:])


def dsa_attention(q, k, v, iq, ik, iw, bias_table):
    b, t_len, _, dh = q.shape
    n_sel = min(DSA_TOPK, t_len // 4)
    key_pos = jnp.arange(t_len)
    gather = jax.vmap(lambda a, i: a[i])
    scale = dh ** -0.5

    def block(args):
        qb, iqb, iwb, start = args
        t = start + jnp.arange(Q_BLOCK)
        rel = jax.nn.relu(jnp.einsum('bqhc,bsc->bqsh', iqb, ik).astype(jnp.float32) * IDX_DIM ** -0.5)
        score = jnp.einsum('bqsh,bqh->bqs', rel, iwb.astype(jnp.float32))
        score = jnp.where(key_pos[None, None, :] <= t[None, :, None], score, -jnp.inf)
        _, sel = lax.top_k(score, n_sel)
        kg = gather(k, sel)
        vg = gather(v, sel)
        dist = t[None, :, None] - sel
        bias = jnp.moveaxis(bias_table[:, rel_bucket(dist)], 0, 2)
        logits = jnp.einsum('bqhc,bqkc->bqhk', qb, kg).astype(jnp.float32) * scale + bias
        p = masked_softmax(logits, (dist >= 0)[:, :, None, :])
        return jnp.einsum('bqhk,bqkc->bqhc', p.astype(vg.dtype), vg)

    starts = jnp.arange(t_len // Q_BLOCK) * Q_BLOCK
    out = lax.map(block, (to_blocks(q, Q_BLOCK), to_blocks(iq, Q_BLOCK), to_blocks(iw, Q_BLOCK), starts))
    return from_blocks(out)


def moba_attention(q, k, v, bias_table):
    b, t_len, h, dh = q.shape
    nb = -(-t_len // MOBA_BLOCK)
    t_pad = nb * MOBA_BLOCK
    pad = ((0, 0), (0, t_pad - t_len), (0, 0), (0, 0))
    kp = jnp.pad(k, pad)
    vp = jnp.pad(v, pad)
    kb = kp.reshape(b, nb, MOBA_BLOCK, h, dh)
    k_mean = jnp.mean(kb.astype(jnp.float32), axis=2)
    kbh = kb.transpose(0, 3, 1, 2, 4)
    vbh = vp.reshape(b, nb, MOBA_BLOCK, h, dh).transpose(0, 3, 1, 2, 4)
    n_sel = min(MOBA_TOPK, nb - 1)
    blk_ids = jnp.arange(nb)
    offs = jnp.arange(MOBA_BLOCK)
    head_ids = jnp.arange(h)[None, :, None, None]
    gather = jax.vmap(jax.vmap(lambda a, i: a[i]))
    scale = dh ** -0.5
    qc_len = MOBA_Q_CHUNK

    def chunk(args):
        qc, start = args
        t = start + jnp.arange(qc_len)
        own = start // MOBA_BLOCK
        ko = lax.dynamic_slice_in_dim(kp, own * MOBA_BLOCK, MOBA_BLOCK, axis=1)
        vo = lax.dynamic_slice_in_dim(vp, own * MOBA_BLOCK, MOBA_BLOCK, axis=1)
        own_dist = t[:, None] - (own * MOBA_BLOCK + offs)[None, :]
        logits = [jnp.einsum('bqhc,bkhc->bhqk', qc, ko).astype(jnp.float32) * scale
                  + bias_table[:, rel_bucket(own_dist)][None]]
        masks = [jnp.broadcast_to((own_dist >= 0)[None, None], (b, h, qc_len, MOBA_BLOCK))]
        if n_sel > 0:
            gate = jnp.einsum('bqhc,bnhc->bhqn', qc.astype(jnp.float32), k_mean)
            gate = jnp.where(blk_ids < own, gate, -jnp.inf)
            _, sel = lax.top_k(gate, n_sel)
            kg = gather(kbh, sel)
            vg = gather(vbh, sel)
            kpos = sel[..., None] * MOBA_BLOCK + offs
            past_dist = (t[:, None, None] - kpos).reshape(b, h, qc_len, n_sel * MOBA_BLOCK)
            past_logits = jnp.einsum('bqhc,bhqnkc->bhqnk', qc, kg).astype(jnp.float32) * scale
            logits.append(past_logits.reshape(b, h, qc_len, n_sel * MOBA_BLOCK)
                          + bias_table[head_ids, rel_bucket(past_dist)])
            masks.append(jnp.broadcast_to((sel < own)[..., None], sel.shape + (MOBA_BLOCK,))
                         .reshape(b, h, qc_len, n_sel * MOBA_BLOCK))
        p = masked_softmax(jnp.concatenate(logits, axis=-1), jnp.concatenate(masks, axis=-1))
        out = jnp.einsum('bhqk,bkhc->bqhc', p[..., :MOBA_BLOCK].astype(vo.dtype), vo)
        if n_sel > 0:
            p_past = p[..., MOBA_BLOCK:].reshape(b, h, qc_len, n_sel, MOBA_BLOCK)
            out = out + jnp.einsum('bhqnk,bhqnkc->bqhc', p_past.astype(vg.dtype), vg)
        return out

    starts = jnp.arange(t_len // qc_len) * qc_len
    out = lax.map(chunk, (to_blocks(q, qc_len), starts))
    return from_blocks(out)


def causal_dense_attention(q, k, v):
    t_len, c = q.shape[1], q.shape[-1]
    key_pos = jnp.arange(t_len)
    scale = c ** -0.5

    def block(args):
        qb, start = args
        t = start + jnp.arange(Q_BLOCK)
        logits = jnp.einsum('bqhc,bkhc->bhqk', qb, k).astype(jnp.float32) * scale
        p = masked_softmax(logits, key_pos[None, :] <= t[:, None])
        return jnp.einsum('bhqk,bkhc->bqhc', p.astype(v.dtype), v)

    starts = jnp.arange(t_len // Q_BLOCK) * Q_BLOCK
    return from_blocks(lax.map(block, (to_blocks(q, Q_BLOCK), starts)))


def dilated_attention(q, k, v, bias_table):
    b, t_len, _, h, dh = q.shape
    scale = dh ** -0.5
    outs, lses = [], []
    for g, (window, dil) in enumerate(DIL_PATTERNS):
        span = window // dil
        m = t_len // dil
        nbk = -(-m // span)
        mp = nbk * span

        def to_band(a):
            a = a.reshape(b, m, dil, h, dh).transpose(0, 2, 1, 3, 4)
            a = jnp.pad(a, ((0, 0), (0, 0), (0, mp - m), (0, 0), (0, 0)))
            return a.reshape(b, dil, nbk, span, h, dh)

        def with_prev(a):
            prev = jnp.pad(a[:, :, :-1], ((0, 0), (0, 0), (1, 0), (0, 0), (0, 0), (0, 0)))
            return jnp.concatenate([prev, a], axis=3)

        qs = to_band(q[:, :, g])
        kc = with_prev(to_band(k[:, :, g]))
        vc = with_prev(to_band(v[:, :, g]))
        qi = jnp.arange(span)[:, None]
        kj = jnp.arange(2 * span)[None, :]
        dist = span + qi - kj
        key_sub = (jnp.arange(nbk)[:, None, None] - 1) * span + kj[None]
        mask = (dist >= 0) & (dist <= span) & (key_sub >= 0)
        bias = bias_table[g * h:(g + 1) * h][:, rel_bucket(dist * dil)]
        logits = jnp.einsum('brnqhc,brnkhc->brnhqk', qs, kc).astype(jnp.float32) * scale + bias
        logits = jnp.where(mask[:, None], logits, -jnp.inf)
        lse = jax.nn.logsumexp(logits, axis=-1, keepdims=True)
        p = jnp.exp(logits - lse)
        o = jnp.einsum('brnhqk,brnkhc->brnqhc', p.astype(vc.dtype), vc)
        o = o.reshape(b, dil, mp, h, dh)[:, :, :m].transpose(0, 2, 1, 3, 4).reshape(b, t_len, h, dh)
        l = lse[..., 0].transpose(0, 1, 2, 4, 3).reshape(b, dil, mp, h)[:, :, :m]
        l = l.transpose(0, 2, 1, 3).reshape(b, t_len, h)
        outs.append(o)
        lses.append(l)
    alpha = jax.nn.softmax(jnp.stack(lses, axis=0), axis=0)
    return jnp.einsum('gbth,gbthc->bthc', alpha.astype(q.dtype), jnp.stack(outs, axis=0))


def swiglu(h, w_in, w_out):
    gate, up = jnp.split(h @ w_in, 2, axis=-1)
    return (jax.nn.silu(gate) * up) @ w_out


def hybrid_mixer(h, w_in, b_gate, qk_a, qk_b, qk_c, qk_d, mla_nq, w_uq, mla_nkv, w_ukv,
                 w_branch, w_out, rel_bias):
    b, t_len, _ = h.shape
    pos = jnp.arange(t_len)
    (a_q, a_k, a_v, i_q, i_k, i_w, b_qkv, c_q, c_kv, c_kr, d_qkv, gate_logits) = \
        jnp.split(h @ w_in, IN_OFFSETS, axis=-1)

    qa = rms_norm(a_q.reshape(b, t_len, DSA_HEADS, HEAD_DIM), qk_a[0])
    ka = rms_norm(a_k, qk_a[1])
    out_a = dsa_attention(qa, ka, a_v, i_q.reshape(b, t_len, IDX_HEADS, IDX_DIM), i_k,
                          i_w * IDX_HEADS ** -0.5, rel_bias[:DSA_HEADS])

    bqkv = b_qkv.reshape(b, t_len, 3, MOBA_HEADS, HEAD_DIM)
    out_b = moba_attention(rms_norm(bqkv[:, :, 0], qk_b[0]), rms_norm(bqkv[:, :, 1], qk_b[1]),
                           bqkv[:, :, 2], rel_bias[DSA_HEADS:DSA_HEADS + MOBA_HEADS])

    cq = (rms_norm(c_q, mla_nq) @ w_uq).reshape(b, t_len, MLA_HEADS, MLA_NOPE + MLA_ROPE)
    ckv = (rms_norm(c_kv, mla_nkv) @ w_ukv).reshape(b, t_len, MLA_HEADS, MLA_NOPE + MLA_V)
    kfull = jnp.concatenate(
        [ckv[..., :MLA_NOPE], jnp.broadcast_to(c_kr[:, :, None, :], (b, t_len, MLA_HEADS, MLA_ROPE))], axis=-1)
    qn = rms_norm(cq, qk_c[0])
    kn = rms_norm(kfull, qk_c[1])
    qn = jnp.concatenate([qn[..., :MLA_NOPE], apply_rope(qn[..., MLA_NOPE:], pos)], axis=-1)
    kn = jnp.concatenate([kn[..., :MLA_NOPE], apply_rope(kn[..., MLA_NOPE:], pos)], axis=-1)
    out_c = causal_dense_attention(qn, kn, ckv[..., MLA_NOPE:])

    dqkv = d_qkv.reshape(b, t_len, 3, DIL_GROUPS, DIL_HEADS, HEAD_DIM)
    out_d = dilated_attention(rms_norm(dqkv[:, :, 0], qk_d[0]), rms_norm(dqkv[:, :, 1], qk_d[1]),
                              dqkv[:, :, 2], rel_bias[DSA_HEADS + MOBA_HEADS:])

    branches = jnp.stack([o.reshape(b, t_len, BRANCH_WIDTH) for o in (out_a, out_b, out_c, out_d)], axis=2)
    gates = jax.nn.sigmoid(gate_logits.reshape(b, t_len, N_BRANCH, D_MODEL) + b_gate)
    merged = jnp.einsum('btnc,ncd->btnd', branches, w_branch)
    return jnp.sum(gates * merged, axis=2) @ w_out


def setup_inputs(seed: int = 0) -> dict:
    key = jax.random.key(seed)
    ks = jax.random.split(key, 17)

    def nrm(k, shape, scale):
        return scale * jax.random.normal(k, shape, jnp.float32)

    def gain(k, shape):
        return 1.0 + 0.05 * jax.random.normal(k, shape, jnp.float32)

    return {
        'x': nrm(ks[0], (BATCH, SEQ, D_MODEL), 1.0),
        'norm_gain': gain(ks[1], (DEPTH, 3, D_MODEL)),
        'w_in': nrm(ks[2], (DEPTH, D_MODEL, IN_WIDTH), D_MODEL ** -0.5),
        'b_gate': nrm(ks[3], (DEPTH, N_BRANCH, D_MODEL), 0.02),
        'qk_gain_a': gain(ks[4], (DEPTH, 2, HEAD_DIM)),
        'qk_gain_b': gain(ks[5], (DEPTH, 2, HEAD_DIM)),
        'qk_gain_c': gain(ks[6], (DEPTH, 2, MLA_NOPE + MLA_ROPE)),
        'qk_gain_d': gain(ks[7], (DEPTH, 2, HEAD_DIM)),
        'mla_norm_q': gain(ks[8], (DEPTH, MLA_Q_LORA)),
        'w_mla_uq': nrm(ks[9], (DEPTH, MLA_Q_LORA, MLA_HEADS * (MLA_NOPE + MLA_ROPE)), MLA_Q_LORA ** -0.5),
        'mla_norm_kv': gain(ks[10], (DEPTH, MLA_KV_LORA)),
        'w_mla_ukv': nrm(ks[11], (DEPTH, MLA_KV_LORA, MLA_HEADS * (MLA_NOPE + MLA_V)), MLA_KV_LORA ** -0.5),
        'w_branch': nrm(ks[12], (DEPTH, N_BRANCH, BRANCH_WIDTH, D_MODEL), BRANCH_WIDTH ** -0.5),
        'w_out': nrm(ks[13], (DEPTH, D_MODEL, D_MODEL), D_MODEL ** -0.5),
        'rel_bias': nrm(ks[14], (N_BIAS_HEADS, REL_BUCKETS), 0.5),
        'w_ffn_in': nrm(ks[15], (DEPTH, 2, D_MODEL, 2 * D_FF), D_MODEL ** -0.5),
        'w_ffn_out': nrm(ks[16], (DEPTH, 2, D_FF, D_MODEL), D_FF ** -0.5),
    }


def reference(x, norm_gain, w_in, b_gate, qk_gain_a, qk_gain_b, qk_gain_c, qk_gain_d, mla_norm_q,
              w_mla_uq, mla_norm_kv, w_mla_ukv, w_branch, w_out, rel_bias, w_ffn_in, w_ffn_out):
    for l in range(DEPTH):
        x = x + 0.5 * swiglu(rms_norm(x, norm_gain[l, 0]), w_ffn_in[l, 0], w_ffn_out[l, 0])
        x = x + hybrid_mixer(rms_norm(x, norm_gain[l, 1]), w_in[l], b_gate[l], qk_gain_a[l], qk_gain_b[l],
                             qk_gain_c[l], qk_gain_d[l], mla_norm_q[l], w_mla_uq[l], mla_norm_kv[l],
                             w_mla_ukv[l], w_branch[l], w_out[l], rel_bias)
        x = x + 0.5 * swiglu(rms_norm(x, norm_gain[l, 2]), w_ffn_in[l, 1], w_ffn_out[l, 1])
    return x
```

```python
import functools
import math

import numpy as np
import jax
import jax.numpy as jnp
from jax import lax
from jax.experimental import pallas as pl
from jax.experimental.pallas import tpu as pltpu

D_MODEL = 1024
HEAD_DIM = 64
N_HEADS = 4
BRANCH_WIDTH = N_HEADS * HEAD_DIM
IDX_DIM = 32
DSA_TOPK = 256
MOBA_BLOCK = 256
MOBA_TOPK = 3
MLA_Q_LORA = 384
MLA_KV_LORA = 256
MLA_NOPE = 64
MLA_ROPE = 32
MLA_QK = MLA_NOPE + MLA_ROPE
MLA_V = 64
MLA_SLOT = 128
ROPE_THETA = 10000.0
DIL_PATTERNS = ((128, 1), (512, 4), (2048, 16))
DIL_GROUPS = 3
N_BRANCH = 4
D_FF = 2816
REL_BUCKETS = 32
REL_MAX_DIST = 2048
RMS_EPS = 1e-6

OFF_A = 0
W_A = 4 * HEAD_DIM + 2 * HEAD_DIM + 4 * IDX_DIM + IDX_DIM + 4
OFF_B = OFF_A + W_A
W_B = 3 * BRANCH_WIDTH
OFF_C = OFF_B + W_B
W_C = MLA_Q_LORA + MLA_KV_LORA + MLA_ROPE
OFF_D = OFF_C + W_C
W_D = 3 * DIL_GROUPS * BRANCH_WIDTH
OFF_G = OFF_D + W_D
W_G = N_BRANCH * D_MODEL

LANE = 128
ROW_TILE = 512
FF_CHUNK = 1408
VMEM_LIMIT = 56 * 1024 * 1024

F32 = jnp.float32
BF16 = jnp.bfloat16
NEG_INF = float("-inf")
INT_MIN = -2 ** 31

_CONTRACT_LAST = (((1,), (1,)), ((), ()))


def _dot(a, b, precision=None):
    return jnp.dot(a, b, preferred_element_type=F32, precision=precision)


def _dot_t(a, b, precision=None):
    return lax.dot_general(a, b, _CONTRACT_LAST, preferred_element_type=F32, precision=precision)


def _rms(x, gain):
    return x * lax.rsqrt(jnp.mean(x * x, axis=-1, keepdims=True) + RMS_EPS) * gain


def _sigmoid(x):
    return 1.0 / (1.0 + jnp.exp(-x))


def _head_sumsq(y, e_ref):
    y2 = y * y
    hi = y2.astype(BF16)
    lo = (y2 - hi.astype(F32)).astype(BF16)
    e = e_ref[...]
    return _dot(hi, e) + _dot(lo, e)


def _head_norm(y, e_ref, gain, width):
    return y * lax.rsqrt(_head_sumsq(y, e_ref) * (1.0 / width) + RMS_EPS) * gain


def _params(*sem):
    return pltpu.CompilerParams(dimension_semantics=sem, vmem_limit_bytes=VMEM_LIMIT)


def _full(shape):
    return pl.BlockSpec(shape, lambda *_: (0,) * len(shape))


def _ffn_kernel(x_ref, g_ref, gn_ref, wg_ref, wu_ref, wo_ref, xo_ref, xn_ref, xb_ref, acc_ref):
    j = pl.program_id(1)

    @pl.when(j == 0)
    def _():
        xb_ref[...] = _rms(x_ref[...], g_ref[...]).astype(BF16)
        acc_ref[...] = jnp.zeros_like(acc_ref)

    xb = xb_ref[...]
    gate = _dot(xb, wg_ref[...])
    up = _dot(xb, wu_ref[...])
    h = (gate * _sigmoid(gate) * up).astype(BF16)
    acc_ref[...] += _dot(h, wo_ref[...])

    @pl.when(j == pl.num_programs(1) - 1)
    def _():
        xo = x_ref[...] + 0.5 * acc_ref[...]
        xo_ref[...] = xo
        xn_ref[...] = _rms(xo, gn_ref[...]).astype(BF16)


def _ffn(x, gain, gain_next, w_in, w_out):
    n = x.shape[0]
    nchunk = D_FF // FF_CHUNK
    return pl.pallas_call(
        _ffn_kernel,
        grid=(n // ROW_TILE, nchunk),
        in_specs=[
            pl.BlockSpec((ROW_TILE, D_MODEL), lambda i, j: (i, 0)),
            _full((1, D_MODEL)),
            _full((1, D_MODEL)),
            pl.BlockSpec((D_MODEL, FF_CHUNK), lambda i, j: (0, j)),
            pl.BlockSpec((D_MODEL, FF_CHUNK), lambda i, j: (0, j + nchunk)),
            pl.BlockSpec((FF_CHUNK, D_MODEL), lambda i, j: (j, 0)),
        ],
        out_specs=[
            pl.BlockSpec((ROW_TILE, D_MODEL), lambda i, j: (i, 0)),
            pl.BlockSpec((ROW_TILE, D_MODEL), lambda i, j: (i, 0)),
        ],
        out_shape=[jax.ShapeDtypeStruct((n, D_MODEL), F32), jax.ShapeDtypeStruct((n, D_MODEL), BF16)],
        scratch_shapes=[pltpu.VMEM((ROW_TILE, D_MODEL), BF16), pltpu.VMEM((ROW_TILE, D_MODEL), F32)],
        compiler_params=_params("parallel", "arbitrary"),
        name="ffn",
    )(x, gain, gain_next, w_in, w_in, w_out)


def _proj_a_kernel(xn_ref, w_ref, e_ref, gq_ref, gk_ref, q_ref, k_ref, v_ref, iq_ref, ikw_ref):
    p = _dot(xn_ref[...], w_ref[...])
    q = _head_norm(p[:, :256], e_ref, gq_ref[...], HEAD_DIM) * HEAD_DIM ** -0.5
    q_ref[...] = q.astype(BF16)
    k_ref[...] = _rms(p[:, 256:320], gk_ref[...]).astype(BF16)
    v_ref[...] = p[:, 320:384].astype(BF16)
    iq_ref[...] = p[:, 384:512]
    ikw_ref[...] = p[:, 512:640]


def _proj_a(xn, w, e64, gq, gk):
    n = xn.shape[0]
    rows = lambda width: pl.BlockSpec((ROW_TILE, width), lambda i: (i, 0))
    return pl.pallas_call(
        _proj_a_kernel,
        grid=(n // ROW_TILE,),
        in_specs=[rows(D_MODEL), _full(w.shape), _full(e64.shape), _full(gq.shape), _full(gk.shape)],
        out_specs=[rows(256), rows(64), rows(64), rows(128), rows(128)],
        out_shape=[jax.ShapeDtypeStruct((n, 256), BF16), jax.ShapeDtypeStruct((n, 64), BF16),
                   jax.ShapeDtypeStruct((n, 64), BF16), jax.ShapeDtypeStruct((n, 128), F32),
                   jax.ShapeDtypeStruct((n, 128), F32)],
        compiler_params=_params("parallel"),
        name="proj_a",
    )(xn, w, e64, gq, gk)


def _proj_b_kernel(xn_ref, w_ref, e_ref, gq_ref, gk_ref, q_ref, k_ref, v_ref):
    p = _dot(xn_ref[...], w_ref[...])
    q_ref[...] = _head_norm(p[:, :256], e_ref, gq_ref[...], HEAD_DIM)
    k_ref[...] = _head_norm(p[:, 256:512], e_ref, gk_ref[...], HEAD_DIM)
    v_ref[...] = p[:, 512:768].astype(BF16)


def _proj_b(xn, w, e64, gq, gk):
    n = xn.shape[0]
    rows = lambda width: pl.BlockSpec((ROW_TILE, width), lambda i: (i, 0))
    return pl.pallas_call(
        _proj_b_kernel,
        grid=(n // ROW_TILE,),
        in_specs=[rows(D_MODEL), _full(w.shape), _full(e64.shape), _full(gq.shape), _full(gk.shape)],
        out_specs=[rows(256), rows(256), rows(256)],
        out_shape=[jax.ShapeDtypeStruct((n, 256), F32), jax.ShapeDtypeStruct((n, 256), F32),
                   jax.ShapeDtypeStruct((n, 256), BF16)],
        compiler_params=_params("parallel"),
        name="proj_b",
    )(xn, w, e64, gq, gk)


def _proj_c_kernel(xn_ref, w_ref, e_ref, nq_ref, nkv_ref, wuq_ref, wuqs_ref, wuk_ref, wuv_ref,
                   gq_ref, gqs_ref, gk_ref, gks_ref, cos_ref, sin_ref, q_ref, k_ref, v_ref):
    p = _dot(xn_ref[...], w_ref[...])
    cos = cos_ref[...]
    sin = sin_ref[...]
    xq = _rms(p[:, :384], nq_ref[...]).astype(BF16)
    qa = _dot(xq, wuq_ref[...])
    qs = _dot(xq, wuqs_ref[...])
    rq = lax.rsqrt(_head_sumsq(qa, e_ref) * (1.0 / MLA_QK) + RMS_EPS)
    q_ref[...] = (rq * (qa * gq_ref[...] * cos + qs * gqs_ref[...] * sin)).astype(BF16)
    xkv = _rms(p[:, 384:640], nkv_ref[...]).astype(BF16)
    ka = _dot(xkv, wuk_ref[...]) + p[:, 640:1152]
    ks = p[:, 1152:1664]
    rk = lax.rsqrt(_head_sumsq(ka, e_ref) * (1.0 / MLA_QK) + RMS_EPS)
    k_ref[...] = (rk * (ka * gk_ref[...] * cos + ks * gks_ref[...] * sin)).astype(BF16)
    v_ref[...] = _dot(xkv, wuv_ref[...]).astype(BF16)


def _proj_c(xn, w, e96, nq, nkv, wuq, wuqs, wuk, wuv, gq, gqs, gk, gks, cos, sin, seq):
    n = xn.shape[0]
    rows = lambda width: pl.BlockSpec((ROW_TILE, width), lambda i: (i, 0))
    pos = pl.BlockSpec((ROW_TILE, N_HEADS * MLA_SLOT), lambda i: (i % (seq // ROW_TILE), 0))
    consts = (w, e96, nq, nkv, wuq, wuqs, wuk, wuv, gq, gqs, gk, gks)
    return pl.pallas_call(
        _proj_c_kernel,
        grid=(n // ROW_TILE,),
        in_specs=[rows(D_MODEL)] + [_full(c.shape) for c in consts] + [pos, pos],
        out_specs=[rows(N_HEADS * MLA_SLOT), rows(N_HEADS * MLA_SLOT), rows(256)],
        out_shape=[jax.ShapeDtypeStruct((n, N_HEADS * MLA_SLOT), BF16),
                   jax.ShapeDtypeStruct((n, N_HEADS * MLA_SLOT), BF16),
                   jax.ShapeDtypeStruct((n, 256), BF16)],
        compiler_params=_params("parallel"),
        name="proj_c",
    )(xn, *consts, cos, sin)


def _proj_d_kernel(xn_ref, w_ref, e_ref, gq_ref, gk_ref, q_ref, k_ref, v_ref):
    xn = xn_ref[...]
    for c in range(DIL_GROUPS):
        lo, hi = c * 256, (c + 1) * 256
        pq = _dot(xn, w_ref[:, lo:hi])
        q_ref[:, lo:hi] = (_head_norm(pq, e_ref, gq_ref[...], HEAD_DIM) * HEAD_DIM ** -0.5).astype(BF16)
        pk = _dot(xn, w_ref[:, 768 + lo:768 + hi])
        k_ref[:, lo:hi] = _head_norm(pk, e_ref, gk_ref[...], HEAD_DIM).astype(BF16)
        v_ref[:, lo:hi] = _dot(xn, w_ref[:, 1536 + lo:1536 + hi]).astype(BF16)


def _proj_d(xn, w, e64, gq, gk):
    n = xn.shape[0]
    rows = lambda width: pl.BlockSpec((ROW_TILE, width), lambda i: (i, 0))
    return pl.pallas_call(
        _proj_d_kernel,
        grid=(n // ROW_TILE,),
        in_specs=[rows(D_MODEL), _full(w.shape), _full(e64.shape), _full(gq.shape), _full(gk.shape)],
        out_specs=[rows(768), rows(768), rows(768)],
        out_shape=[jax.ShapeDtypeStruct((n, 768), BF16)] * 3,
        compiler_params=_params("parallel"),
        name="proj_d",
    )(xn, w, e64, gq, gk)


def _bias_rows(toep_ref, h, qblk, nchunk):
    return jnp.concatenate([toep_ref[h, jnp.maximum(qblk - c, 0)] for c in range(nchunk)], axis=1)


def _softmax_pv(logits, mask, v):
    logits = jnp.where(mask, logits, NEG_INF)
    m = jnp.max(logits, axis=1, keepdims=True)
    p = jnp.exp(logits - m)
    l = jnp.sum(p, axis=1, keepdims=True)
    return _dot(p.astype(BF16), v) / l


def _dsa_kernel(q_ref, k_ref, v_ref, iq_ref, ik_ref, iw_ref, toep_ref, tri_ref, o_ref, sel_ref,
                *, seq, n_sel):
    qi = pl.program_id(1)
    nchunk = seq // LANE
    iq = iq_ref[0]
    ik = ik_ref[0][:, :IDX_DIM]
    iw = iw_ref[0][:, IDX_DIM:IDX_DIM + N_HEADS] * N_HEADS ** -0.5

    score = jnp.zeros((LANE, seq), F32)
    for h in range(N_HEADS):
        r = _dot_t(iq[:, h * IDX_DIM:(h + 1) * IDX_DIM], ik, precision=lax.Precision.HIGHEST)
        score = score + jnp.maximum(r * IDX_DIM ** -0.5, 0.0) * iw[:, h:h + 1]

    t = qi * LANE + lax.broadcasted_iota(jnp.int32, (LANE, seq), 0)
    s = lax.broadcasted_iota(jnp.int32, (LANE, seq), 1)
    causal = s <= t

    bits = pltpu.bitcast(score, jnp.int32)
    key = jnp.where(bits < 0, bits ^ jnp.int32(0x7FFFFFFF), bits)
    key = jnp.where(causal, key, jnp.int32(INT_MIN))

    def bisect(i, ans):
        cand = ans | lax.shift_left(jnp.int32(1), 31 - i)
        cnt = jnp.sum(jnp.where(key >= (cand ^ jnp.int32(INT_MIN)), 1.0, 0.0), axis=1, keepdims=True)
        return jnp.where(cnt >= n_sel, cand, ans)

    thr = lax.fori_loop(0, 32, bisect, jnp.zeros((LANE, 1), jnp.int32)) ^ jnp.int32(INT_MIN)
    above = key > thr
    tied = key == thr
    n_above = jnp.sum(jnp.where(above, 1.0, 0.0), axis=1, keepdims=True)
    n_tied = jnp.sum(jnp.where(tied, 1.0, 0.0), axis=1, keepdims=True)
    room = n_sel - n_above
    sel_ref[...] = jnp.where(key >= thr, 1.0, 0.0)

    overflow = (n_tied > room) & (thr != jnp.int32(INT_MIN))

    @pl.when(jnp.max(jnp.where(overflow, 1.0, 0.0)) > 0.0)
    def _():
        before = jnp.zeros((LANE, 1), F32)
        for c in range(nchunk):
            tc = jnp.where(tied[:, c * LANE:(c + 1) * LANE], 1.0, 0.0)
            rank = before + _dot(tc.astype(BF16), tri_ref[...])
            keep = above[:, c * LANE:(c + 1) * LANE] | ((tc > 0.0) & (rank < room))
            sel_ref[:, c * LANE:(c + 1) * LANE] = jnp.where(keep, 1.0, 0.0)
            before = before + jnp.sum(tc, axis=1, keepdims=True)

    mask = causal & (sel_ref[...] > 0.0)
    k = k_ref[0]
    v = v_ref[0]
    for h in range(N_HEADS):
        logits = _dot_t(q_ref[0, :, h * HEAD_DIM:(h + 1) * HEAD_DIM], k) + _bias_rows(toep_ref, h, qi, nchunk)
        o_ref[0, :, h * HEAD_DIM:(h + 1) * HEAD_DIM] = _softmax_pv(logits, mask, v).astype(o_ref.dtype)


def _dsa(q, k, v, iq, ikw, toep, tri):
    b, seq, _ = q.shape
    n_sel = min(DSA_TOPK, seq // 4)
    qblock = lambda width: pl.BlockSpec((1, LANE, width), lambda bi, qi: (bi, qi, 0))
    whole = lambda width: pl.BlockSpec((1, seq, width), lambda bi, qi: (bi, 0, 0))
    return pl.pallas_call(
        functools.partial(_dsa_kernel, seq=seq, n_sel=n_sel),
        grid=(b, seq // LANE),
        in_specs=[qblock(256), whole(64), whole(64), qblock(128), whole(128), qblock(128),
                  _full(toep.shape), _full(tri.shape)],
        out_specs=qblock(256),
        out_shape=jax.ShapeDtypeStruct((b, seq, 256), BF16),
        scratch_shapes=[pltpu.VMEM((LANE, seq), F32)],
        compiler_params=_params("parallel", "parallel"),
        name="dsa",
    )(q, k, v, iq, ikw, ikw, toep, tri)


def _moba_kernel(q_ref, k_ref, v_ref, toep_ref, o_ref, *, seq):
    qi = pl.program_id(1)
    nchunk = seq // LANE
    nblk = seq // MOBA_BLOCK
    n_sel = min(MOBA_TOPK, nblk - 1)
    own = (qi * LANE) // MOBA_BLOCK
    q = q_ref[0]
    k = k_ref[0]
    v = v_ref[0]
    k_mean = jnp.mean(k.reshape(nblk, MOBA_BLOCK, BRANCH_WIDTH), axis=1)
    k_mean = jnp.concatenate([k_mean, jnp.zeros((LANE - nblk, BRANCH_WIDTH), F32)], axis=0)
    qb = (q * HEAD_DIM ** -0.5).astype(BF16)
    kb = k.astype(BF16)

    t = qi * LANE + lax.broadcasted_iota(jnp.int32, (LANE, seq), 0)
    s = lax.broadcasted_iota(jnp.int32, (LANE, seq), 1)
    causal = s <= t
    key_blk = lax.shift_right_arithmetic(s, int(math.log2(MOBA_BLOCK)))
    blk = lax.broadcasted_iota(jnp.int32, (LANE, LANE), 1)

    for h in range(N_HEADS):
        hs = slice(h * HEAD_DIM, (h + 1) * HEAD_DIM)
        gate = _dot_t(q[:, hs], k_mean[:, hs], precision=lax.Precision.HIGHEST)
        gate = jnp.where(blk < own, gate, NEG_INF)
        rank = jnp.zeros((LANE, LANE), F32)
        for m in range(nblk):
            gm = gate[:, m:m + 1]
            rank = rank + jnp.where((gm > gate) | ((gm == gate) & (m < blk)), 1.0, 0.0)
        picked = jnp.where((blk < own) & (rank < n_sel), 1.0, 0.0)
        allowed = jnp.concatenate(
            [jnp.broadcast_to(picked[:, n:n + 1], (LANE, MOBA_BLOCK)) for n in range(nblk)], axis=1)
        mask = causal & ((key_blk == own) | (allowed > 0.0))
        logits = _dot_t(qb[:, hs], kb[:, hs]) + _bias_rows(toep_ref, h, qi, nchunk)
        o_ref[0, :, hs] = _softmax_pv(logits, mask, v[:, hs]).astype(o_ref.dtype)


def _moba(q, k, v, toep):
    b, seq, _ = q.shape
    qblock = pl.BlockSpec((1, LANE, 256), lambda bi, qi: (bi, qi, 0))
    whole = pl.BlockSpec((1, seq, 256), lambda bi, qi: (bi, 0, 0))
    return pl.pallas_call(
        functools.partial(_moba_kernel, seq=seq),
        grid=(b, seq // LANE),
        in_specs=[qblock, whole, whole, _full(toep.shape)],
        out_specs=qblock,
        out_shape=jax.ShapeDtypeStruct((b, seq, 256), BF16),
        compiler_params=_params("parallel", "parallel"),
        name="moba",
    )(q, k, v, toep)


def _mla_kernel(q_ref, k_ref, v_ref, o_ref, *, seq):
    qi = pl.program_id(1)
    t = qi * LANE + lax.broadcasted_iota(jnp.int32, (LANE, seq), 0)
    s = lax.broadcasted_iota(jnp.int32, (LANE, seq), 1)
    causal = s <= t
    q = q_ref[0]
    k = k_ref[0]
    v = v_ref[0]
    for h in range(N_HEADS):
        qs = slice(h * MLA_SLOT, (h + 1) * MLA_SLOT)
        vs = slice(h * MLA_V, (h + 1) * MLA_V)
        logits = _dot_t(q[:, qs], k[:, qs]) * MLA_QK ** -0.5
        o_ref[0, :, vs] = _softmax_pv(logits, causal, v[:, vs]).astype(o_ref.dtype)


def _mla(q, k, v):
    b, seq, _ = q.shape
    return pl.pallas_call(
        functools.partial(_mla_kernel, seq=seq),
        grid=(b, seq // LANE),
        in_specs=[pl.BlockSpec((1, LANE, N_HEADS * MLA_SLOT), lambda bi, qi: (bi, qi, 0)),
                  pl.BlockSpec((1, seq, N_HEADS * MLA_SLOT), lambda bi, qi: (bi, 0, 0)),
                  pl.BlockSpec((1, seq, 256), lambda bi, qi: (bi, 0, 0))],
        out_specs=pl.BlockSpec((1, LANE, 256), lambda bi, qi: (bi, qi, 0)),
        out_shape=jax.ShapeDtypeStruct((b, seq, 256), BF16),
        compiler_params=_params("parallel", "parallel"),
        name="mla",
    )(q, k, v)


def _dil_kernel(q_ref, kp_ref, kc_ref, vp_ref, vc_ref, toep_ref, o_ref, lse_ref):
    n = pl.program_id(2)
    qi = lax.broadcasted_iota(jnp.int32, (LANE, 2 * LANE), 0)
    kj = lax.broadcasted_iota(jnp.int32, (LANE, 2 * LANE), 1)
    dist = LANE + qi - kj
    mask = (dist >= 0) & (dist <= LANE) & ((kj >= LANE) | (n > 0))
    q = q_ref[0]
    kk = jnp.concatenate([kp_ref[0], kc_ref[0]], axis=0)
    vv = jnp.concatenate([vp_ref[0], vc_ref[0]], axis=0)
    for h in range(N_HEADS):
        hs = slice(h * HEAD_DIM, (h + 1) * HEAD_DIM)
        bias = jnp.concatenate([toep_ref[h, 1], toep_ref[h, 0]], axis=1)
        logits = jnp.where(mask, _dot_t(q[:, hs], kk[:, hs]) + bias, NEG_INF)
        m = jnp.max(logits, axis=1, keepdims=True)
        p = jnp.exp(logits - m)
        l = jnp.sum(p, axis=1, keepdims=True)
        o_ref[0, :, hs] = _dot(p.astype(BF16), vv[:, hs]) / l
        lse_ref[0, :, hs] = jnp.broadcast_to(m + jnp.log(l), (LANE, HEAD_DIM))


def _dil(q, k, v, toep, g, dil):
    b, seq, _ = q.shape
    m = seq // dil
    view = lambda a: a.reshape(b, m, dil * 768)
    cur = pl.BlockSpec((1, LANE, 256), lambda bi, r, n: (bi, n, 3 * r + g))
    prev = pl.BlockSpec((1, LANE, 256), lambda bi, r, n: (bi, jnp.maximum(n - 1, 0), 3 * r + g))
    out = pl.BlockSpec((1, LANE, 256), lambda bi, r, n: (bi, n, r))
    o, lse = pl.pallas_call(
        _dil_kernel,
        grid=(b, dil, m // LANE),
        in_specs=[cur, prev, cur, prev, cur, _full(toep.shape)],
        out_specs=[out, out],
        out_shape=[jax.ShapeDtypeStruct((b, m, dil * 256), F32)] * 2,
        compiler_params=_params("parallel", "parallel", "parallel"),
        name=f"dil{g}",
    )(view(q), view(k), view(k), view(v), view(v), toep)
    return o.reshape(b * seq, 256), lse.reshape(b * seq, 256)


def _merge_kernel(x_ref, xn_ref, oa_ref, ob_ref, oc_ref, od0_ref, od1_ref, od2_ref,
                  l0_ref, l1_ref, l2_ref, wg_ref, bg_ref, wb_ref, wo_ref, xo_ref):
    l0, l1, l2 = l0_ref[...], l1_ref[...], l2_ref[...]
    lm = jnp.maximum(jnp.maximum(l0, l1), l2)
    e0, e1, e2 = jnp.exp(l0 - lm), jnp.exp(l1 - lm), jnp.exp(l2 - lm)
    od = (e0 * od0_ref[...] + e1 * od1_ref[...] + e2 * od2_ref[...]) / (e0 + e1 + e2)
    branches = (oa_ref[...], ob_ref[...], oc_ref[...], od.astype(BF16))
    xn = xn_ref[...]
    total = jnp.zeros(x_ref.shape, F32)
    for n in range(N_BRANCH):
        cols = slice(n * D_MODEL, (n + 1) * D_MODEL)
        gate = _sigmoid(_dot(xn, wg_ref[:, cols]) + bg_ref[:, cols])
        total = total + gate * _dot(branches[n], wb_ref[n])
    xo_ref[...] = x_ref[...] + _dot(total.astype(BF16), wo_ref[...])


def _merge(x, xn, oa, ob, oc, od, lse, wg, bg, wb, wo):
    n = x.shape[0]
    rows = lambda width: pl.BlockSpec((ROW_TILE, width), lambda i: (i, 0))
    return pl.pallas_call(
        _merge_kernel,
        grid=(n // ROW_TILE,),
        in_specs=[rows(D_MODEL), rows(D_MODEL)] + [rows(256)] * 9
                 + [_full(wg.shape), _full(bg.shape), _full(wb.shape), _full(wo.shape)],
        out_specs=rows(D_MODEL),
        out_shape=jax.ShapeDtypeStruct((n, D_MODEL), F32),
        compiler_params=_params("parallel"),
        name="merge",
    )(x, xn, oa, ob, oc, *od, *lse, wg, bg, wb, wo)


def _bucket_of_distance():
    d = np.arange(REL_MAX_DIST + 1)
    exact = REL_BUCKETS // 2
    nf = np.maximum(d, 1).astype(np.float32)
    log_b = exact + (np.log(nf / exact) / math.log(REL_MAX_DIST / exact) * (REL_BUCKETS - exact)).astype(np.int32)
    return np.where(d < exact, d, np.minimum(log_b, REL_BUCKETS - 1))


def _toeplitz_index(n_off, dil):
    off = np.arange(n_off)[:, None, None]
    i = np.arange(LANE)[None, :, None]
    j = np.arange(LANE)[None, None, :]
    return _bucket_of_distance()[np.clip((off * LANE + i - j) * dil, 0, REL_MAX_DIST)]


def _same_head(width, head):
    idx = np.arange(width) // head
    return jnp.asarray(idx[:, None] == idx[None, :], BF16)


def _rope_tables(seq):
    half = MLA_ROPE // 2
    freqs = ROPE_THETA ** (-np.arange(half, dtype=np.float64) / half)
    ang = np.arange(seq, dtype=np.float64)[:, None] * freqs[None, :]
    pad = np.zeros((seq, MLA_SLOT - MLA_QK))
    cos_h = np.concatenate([np.ones((seq, MLA_NOPE)), np.cos(ang), np.cos(ang), pad], axis=1)
    sin_h = np.concatenate([np.zeros((seq, MLA_NOPE)), np.sin(ang), np.sin(ang), pad], axis=1)
    return (jnp.asarray(np.tile(cos_h, (1, N_HEADS)), F32), jnp.asarray(np.tile(sin_h, (1, N_HEADS)), F32))


def _rot_half_cols():
    half = MLA_ROPE // 2
    src = np.arange(MLA_SLOT)
    sign = np.zeros(MLA_SLOT, np.float32)
    src[MLA_NOPE:MLA_NOPE + half] = np.arange(MLA_NOPE + half, MLA_QK)
    sign[MLA_NOPE:MLA_NOPE + half] = -1.0
    src[MLA_NOPE + half:MLA_QK] = np.arange(MLA_NOPE, MLA_NOPE + half)
    sign[MLA_NOPE + half:MLA_QK] = 1.0
    return src, sign


def _mixer(x, xn, seq, w_in, b_gate, qk_a, qk_b, qk_c, qk_d, mla_nq, w_uq, mla_nkv, w_ukv,
           w_branch, w_out, toeps, consts):
    n = x.shape[0]
    b = n // seq
    e64, e96, tri, cos, sin = consts
    toep_a, toep_b, toep_d = toeps
    row = lambda a: a.reshape(1, -1)
    tile4 = lambda g: jnp.tile(g, N_HEADS).reshape(1, -1)

    wa = jnp.pad(w_in[:, OFF_A:OFF_A + W_A], ((0, 0), (0, 640 - W_A))).astype(BF16)
    qa, ka, va, iq, ikw = _proj_a(xn, wa, e64, tile4(qk_a[0]), row(qk_a[1]))
    r3 = lambda a: a.reshape(b, seq, a.shape[-1])
    out_a = _dsa(r3(qa), r3(ka), r3(va), r3(iq), r3(ikw), toep_a, tri).reshape(n, 256)

    wb_in = w_in[:, OFF_B:OFF_B + W_B].astype(BF16)
    qb, kb, vb = _proj_b(xn, wb_in, e64, tile4(qk_b[0]), tile4(qk_b[1]))
    out_b = _moba(r3(qb), r3(kb), r3(vb), toep_b).reshape(n, 256)

    src, sign = _rot_half_cols()
    src4 = np.concatenate([h * MLA_SLOT + src for h in range(N_HEADS)])
    sign4 = jnp.asarray(np.tile(sign, N_HEADS))
    slot_pad = MLA_SLOT - MLA_QK
    w_kr = w_in[:, OFF_C + MLA_Q_LORA + MLA_KV_LORA:OFF_C + W_C]
    kr_slots = jnp.pad(w_kr, ((0, 0), (MLA_NOPE, slot_pad)))
    kr_slots = jnp.tile(kr_slots, (1, N_HEADS))
    wc = jnp.concatenate([w_in[:, OFF_C:OFF_C + MLA_Q_LORA + MLA_KV_LORA], kr_slots,
                          kr_slots[:, src4] * sign4], axis=1).astype(BF16)
    slots = lambda w: jnp.pad(w.reshape(w.shape[0], N_HEADS, -1),
                              ((0, 0), (0, 0), (0, MLA_SLOT - w.shape[1] // N_HEADS))).reshape(w.shape[0], -1)
    wuq = slots(w_uq)
    ukv = w_ukv.reshape(MLA_KV_LORA, N_HEADS, MLA_NOPE + MLA_V)
    wuk = slots(ukv[:, :, :MLA_NOPE].reshape(MLA_KV_LORA, -1))
    wuv = ukv[:, :, MLA_NOPE:].reshape(MLA_KV_LORA, -1)
    gq = jnp.tile(jnp.pad(qk_c[0], (0, slot_pad)), N_HEADS)
    gk = jnp.tile(jnp.pad(qk_c[1], (0, slot_pad)), N_HEADS)
    qc, kc, vc = _proj_c(xn, wc, e96, row(mla_nq), row(mla_nkv), wuq.astype(BF16),
                         (wuq[:, src4] * sign4).astype(BF16), wuk.astype(BF16), wuv.astype(BF16),
                         row(gq), row(gq[src4]), row(gk), row(gk[src4]), cos, sin, seq)
    out_c = _mla(r3(qc), r3(kc), r3(vc)).reshape(n, 256)

    wd = w_in[:, OFF_D:OFF_D + W_D].astype(BF16)
    qd, kd, vd = _proj_d(xn, wd, e64, tile4(qk_d[0]), tile4(qk_d[1]))
    od, lse = [], []
    for g, (_, dil) in enumerate(DIL_PATTERNS):
        o, l = _dil(r3(qd), r3(kd), r3(vd), toep_d[g], g, dil)
        od.append(o)
        lse.append(l)

    wg = w_in[:, OFF_G:OFF_G + W_G].astype(BF16)
    return _merge(x, xn, out_a, out_b, out_c, od, lse, wg, b_gate.reshape(1, -1),
                  w_branch.astype(BF16), w_out.astype(BF16))


def kernel(x, norm_gain, w_in, b_gate, qk_gain_a, qk_gain_b, qk_gain_c, qk_gain_d, mla_norm_q,
           w_mla_uq, mla_norm_kv, w_mla_ukv, w_branch, w_out, rel_bias, w_ffn_in, w_ffn_out):
    b, seq, d = x.shape
    depth = norm_gain.shape[0]
    n_off = seq // LANE
    dense_idx = _toeplitz_index(n_off, 1)
    toep_a = rel_bias[0:4][:, dense_idx]
    toep_b = rel_bias[4:8][:, dense_idx]
    toep_d = [rel_bias[8 + 4 * g:12 + 4 * g][:, _toeplitz_index(2, dil)]
              for g, (_, dil) in enumerate(DIL_PATTERNS)]
    tri = jnp.asarray(np.arange(LANE)[:, None] < np.arange(LANE)[None, :], BF16)
    consts = (_same_head(256, HEAD_DIM), _same_head(N_HEADS * MLA_SLOT, MLA_SLOT), tri) + _rope_tables(seq)

    x = x.reshape(b * seq, d)
    for l in range(depth):
        g = norm_gain[l]
        x, xn = _ffn(x, g[0:1], g[1:2], w_ffn_in[l, 0].astype(BF16), w_ffn_out[l, 0].astype(BF16))
        x = _mixer(x, xn, seq, w_in[l], b_gate[l], qk_gain_a[l], qk_gain_b[l], qk_gain_c[l],
                   qk_gain_d[l], mla_norm_q[l], w_mla_uq[l], mla_norm_kv[l], w_mla_ukv[l],
                   w_branch[l], w_out[l], (toep_a, toep_b, toep_d), consts)
        x, _ = _ffn(x, g[2:3], g[2:3], w_ffn_in[l, 1].astype(BF16), w_ffn_out[l, 1].astype(BF16))
    return x.reshape(b, seq, d)
```

```python
import functools
import math

import numpy as np
import jax
import jax.numpy as jnp
from jax import lax
from jax.experimental import pallas as pl
from jax.experimental.pallas import tpu as pltpu

D_MODEL = 1024
HEAD_DIM = 64
N_HEADS = 4
BRANCH_WIDTH = N_HEADS * HEAD_DIM
IDX_DIM = 32
DSA_TOPK = 256
MOBA_BLOCK = 256
MOBA_TOPK = 3
MLA_Q_LORA = 384
MLA_KV_LORA = 256
MLA_NOPE = 64
MLA_ROPE = 32
MLA_QK = MLA_NOPE + MLA_ROPE
MLA_V = 64
MLA_SLOT = 128
ROPE_THETA = 10000.0
DIL_PATTERNS = ((128, 1), (512, 4), (2048, 16))
DIL_GROUPS = 3
N_BRANCH = 4
D_FF = 2816
REL_BUCKETS = 32
REL_MAX_DIST = 2048
RMS_EPS = 1e-6

OFF_A = 0
W_A = 4 * HEAD_DIM + 2 * HEAD_DIM + 4 * IDX_DIM + IDX_DIM + 4
OFF_B = OFF_A + W_A
W_B = 3 * BRANCH_WIDTH
OFF_C = OFF_B + W_B
W_C = MLA_Q_LORA + MLA_KV_LORA + MLA_ROPE
OFF_D = OFF_C + W_C
W_D = 3 * DIL_GROUPS * BRANCH_WIDTH
OFF_G = OFF_D + W_D
W_G = N_BRANCH * D_MODEL

LANE = 128
ROW_TILE = 512
FF_CHUNK = 1408
WIDTH_CLASSES = 4
VMEM_LIMIT = 56 * 1024 * 1024

F32 = jnp.float32
BF16 = jnp.bfloat16
NEG_INF = float("-inf")
INT_MIN = -2 ** 31

_CONTRACT_LAST = (((1,), (1,)), ((), ()))


def _dot(a, b, precision=None):
    return jnp.dot(a, b, preferred_element_type=F32, precision=precision)


def _dot_t(a, b, precision=None):
    return lax.dot_general(a, b, _CONTRACT_LAST, preferred_element_type=F32, precision=precision)


def _rms(x, gain):
    return x * lax.rsqrt(jnp.mean(x * x, axis=-1, keepdims=True) + RMS_EPS) * gain


def _sigmoid(x):
    return 1.0 / (1.0 + jnp.exp(-x))


def _head_sumsq(y, e_ref):
    y2 = y * y
    hi = y2.astype(BF16)
    lo = (y2 - hi.astype(F32)).astype(BF16)
    e = e_ref[...]
    return _dot(hi, e) + _dot(lo, e)


def _head_norm(y, e_ref, gain, width):
    return y * lax.rsqrt(_head_sumsq(y, e_ref) * (1.0 / width) + RMS_EPS) * gain


def _params(*sem):
    return pltpu.CompilerParams(dimension_semantics=sem, vmem_limit_bytes=VMEM_LIMIT)


def _full(shape):
    return pl.BlockSpec(shape, lambda *_: (0,) * len(shape))


def _ffn_kernel(x_ref, g_ref, gn_ref, wg_ref, wu_ref, wo_ref, xo_ref, xn_ref, xb_ref, acc_ref):
    j = pl.program_id(1)

    @pl.when(j == 0)
    def _():
        xb_ref[...] = _rms(x_ref[...], g_ref[...]).astype(BF16)
        acc_ref[...] = jnp.zeros_like(acc_ref)

    xb = xb_ref[...]
    gate = _dot(xb, wg_ref[...])
    up = _dot(xb, wu_ref[...])
    h = (gate * _sigmoid(gate) * up).astype(BF16)
    acc_ref[...] += _dot(h, wo_ref[...])

    @pl.when(j == pl.num_programs(1) - 1)
    def _():
        xo = x_ref[...] + 0.5 * acc_ref[...]
        xo_ref[...] = xo
        xn_ref[...] = _rms(xo, gn_ref[...]).astype(BF16)


def _ffn(x, gain, gain_next, w_in, w_out):
    n = x.shape[0]
    nchunk = D_FF // FF_CHUNK
    return pl.pallas_call(
        _ffn_kernel,
        grid=(n // ROW_TILE, nchunk),
        in_specs=[
            pl.BlockSpec((ROW_TILE, D_MODEL), lambda i, j: (i, 0)),
            _full((1, D_MODEL)),
            _full((1, D_MODEL)),
            pl.BlockSpec((D_MODEL, FF_CHUNK), lambda i, j: (0, j)),
            pl.BlockSpec((D_MODEL, FF_CHUNK), lambda i, j: (0, j + nchunk)),
            pl.BlockSpec((FF_CHUNK, D_MODEL), lambda i, j: (j, 0)),
        ],
        out_specs=[
            pl.BlockSpec((ROW_TILE, D_MODEL), lambda i, j: (i, 0)),
            pl.BlockSpec((ROW_TILE, D_MODEL), lambda i, j: (i, 0)),
        ],
        out_shape=[jax.ShapeDtypeStruct((n, D_MODEL), F32), jax.ShapeDtypeStruct((n, D_MODEL), BF16)],
        scratch_shapes=[pltpu.VMEM((ROW_TILE, D_MODEL), BF16), pltpu.VMEM((ROW_TILE, D_MODEL), F32)],
        compiler_params=_params("parallel", "arbitrary"),
        name="ffn",
    )(x, gain, gain_next, w_in, w_in, w_out)


def _split_terms(rep, hi_lanes):
    hi = rep.astype(BF16).astype(F32)
    return jnp.where(hi_lanes, hi, rep - hi).astype(BF16)


def _proj_a_kernel(xn_ref, w_ref, e_ref, gq_ref, gk_ref, q_ref, k_ref, v_ref, iq_ref, ik_ref, iw_ref):
    p = _dot(xn_ref[...], w_ref[...])
    q = _head_norm(p[:, :256], e_ref, gq_ref[...], HEAD_DIM) * HEAD_DIM ** -0.5
    q_ref[...] = q.astype(BF16)
    k_ref[...] = _rms(p[:, 256:320], gk_ref[...]).astype(BF16)
    v_ref[...] = p[:, 320:384].astype(BF16)
    lane = lax.broadcasted_iota(jnp.int32, (1, 4 * LANE), 1)
    copy = lax.shift_right_logical(lane, int(math.log2(IDX_DIM))) & 3
    iq_ref[...] = _split_terms(p[:, 384:896], copy < 2)
    ik_ref[...] = _split_terms(p[:, 896:1024], (copy[:, :LANE] & 1) == 0)
    iw_ref[...] = p[:, 1024:1152]


def _proj_a(xn, w, e64, gq, gk):
    n = xn.shape[0]
    rows = lambda width: pl.BlockSpec((ROW_TILE, width), lambda i: (i, 0))
    return pl.pallas_call(
        _proj_a_kernel,
        grid=(n // ROW_TILE,),
        in_specs=[rows(D_MODEL), _full(w.shape), _full(e64.shape), _full(gq.shape), _full(gk.shape)],
        out_specs=[rows(256), rows(64), rows(64), rows(512), rows(128), rows(128)],
        out_shape=[jax.ShapeDtypeStruct((n, 256), BF16), jax.ShapeDtypeStruct((n, 64), BF16),
                   jax.ShapeDtypeStruct((n, 64), BF16), jax.ShapeDtypeStruct((n, 512), BF16),
                   jax.ShapeDtypeStruct((n, 128), BF16), jax.ShapeDtypeStruct((n, 128), F32)],
        compiler_params=_params("parallel"),
        name="proj_a",
    )(xn, w, e64, gq, gk)


def _proj_b_kernel(xn_ref, w_ref, e_ref, gq_ref, gk_ref, q_ref, k_ref, v_ref, km_ref):
    p = _dot(xn_ref[...], w_ref[...])
    q_ref[...] = _head_norm(p[:, :256], e_ref, gq_ref[...], HEAD_DIM)
    k = _head_norm(p[:, 256:512], e_ref, gk_ref[...], HEAD_DIM)
    k_ref[...] = k.astype(BF16)
    v_ref[...] = p[:, 512:768].astype(BF16)
    km = jnp.mean(k.reshape(ROW_TILE // MOBA_BLOCK, MOBA_BLOCK, BRANCH_WIDTH), axis=1)
    for j in range(ROW_TILE // MOBA_BLOCK):
        km_ref[j] = km[j:j + 1]


def _proj_b(xn, w, e64, gq, gk):
    n = xn.shape[0]
    rows = lambda width: pl.BlockSpec((ROW_TILE, width), lambda i: (i, 0))
    per_tile = ROW_TILE // MOBA_BLOCK
    return pl.pallas_call(
        _proj_b_kernel,
        grid=(n // ROW_TILE,),
        in_specs=[rows(D_MODEL), _full(w.shape), _full(e64.shape), _full(gq.shape), _full(gk.shape)],
        out_specs=[rows(256), rows(256), rows(256),
                   pl.BlockSpec((per_tile, 1, BRANCH_WIDTH), lambda i: (i, 0, 0))],
        out_shape=[jax.ShapeDtypeStruct((n, 256), F32), jax.ShapeDtypeStruct((n, 256), BF16),
                   jax.ShapeDtypeStruct((n, 256), BF16),
                   jax.ShapeDtypeStruct((n // MOBA_BLOCK, 1, BRANCH_WIDTH), F32)],
        compiler_params=_params("parallel"),
        name="proj_b",
    )(xn, w, e64, gq, gk)


def _proj_c_kernel(xn_ref, w_ref, e_ref, nq_ref, nkv_ref, wuq_ref, wuqs_ref, wuk_ref, wuv_ref,
                   gq_ref, gqs_ref, gk_ref, gks_ref, cos_ref, sin_ref, q_ref, k_ref, v_ref):
    p = _dot(xn_ref[...], w_ref[...])
    cos = cos_ref[...]
    sin = sin_ref[...]
    xq = _rms(p[:, :384], nq_ref[...]).astype(BF16)
    qa = _dot(xq, wuq_ref[...])
    qs = _dot(xq, wuqs_ref[...])
    rq = lax.rsqrt(_head_sumsq(qa, e_ref) * (1.0 / MLA_QK) + RMS_EPS)
    q_ref[...] = (rq * (qa * gq_ref[...] * cos + qs * gqs_ref[...] * sin)).astype(BF16)
    xkv = _rms(p[:, 384:640], nkv_ref[...]).astype(BF16)
    ka = _dot(xkv, wuk_ref[...]) + p[:, 640:1152]
    ks = p[:, 1152:1664]
    rk = lax.rsqrt(_head_sumsq(ka, e_ref) * (1.0 / MLA_QK) + RMS_EPS)
    k_ref[...] = (rk * (ka * gk_ref[...] * cos + ks * gks_ref[...] * sin)).astype(BF16)
    v_ref[...] = _dot(xkv, wuv_ref[...]).astype(BF16)


def _proj_c(xn, w, e96, nq, nkv, wuq, wuqs, wuk, wuv, gq, gqs, gk, gks, cos, sin, seq):
    n = xn.shape[0]
    rows = lambda width: pl.BlockSpec((ROW_TILE, width), lambda i: (i, 0))
    pos = pl.BlockSpec((ROW_TILE, N_HEADS * MLA_SLOT), lambda i: (i % (seq // ROW_TILE), 0))
    consts = (w, e96, nq, nkv, wuq, wuqs, wuk, wuv, gq, gqs, gk, gks)
    return pl.pallas_call(
        _proj_c_kernel,
        grid=(n // ROW_TILE,),
        in_specs=[rows(D_MODEL)] + [_full(c.shape) for c in consts] + [pos, pos],
        out_specs=[rows(N_HEADS * MLA_SLOT), rows(N_HEADS * MLA_SLOT), rows(256)],
        out_shape=[jax.ShapeDtypeStruct((n, N_HEADS * MLA_SLOT), BF16),
                   jax.ShapeDtypeStruct((n, N_HEADS * MLA_SLOT), BF16),
                   jax.ShapeDtypeStruct((n, 256), BF16)],
        compiler_params=_params("parallel"),
        name="proj_c",
    )(xn, *consts, cos, sin)


def _proj_d_kernel(xn_ref, w_ref, e_ref, gq_ref, gk_ref, q_ref, k_ref, v_ref):
    xn = xn_ref[...]
    for c in range(DIL_GROUPS):
        lo, hi = c * 256, (c + 1) * 256
        pq = _dot(xn, w_ref[:, lo:hi])
        q_ref[:, lo:hi] = (_head_norm(pq, e_ref, gq_ref[...], HEAD_DIM) * HEAD_DIM ** -0.5).astype(BF16)
        pk = _dot(xn, w_ref[:, 768 + lo:768 + hi])
        k_ref[:, lo:hi] = _head_norm(pk, e_ref, gk_ref[...], HEAD_DIM).astype(BF16)
        v_ref[:, lo:hi] = _dot(xn, w_ref[:, 1536 + lo:1536 + hi]).astype(BF16)


def _proj_d(xn, w, e64, gq, gk):
    n = xn.shape[0]
    rows = lambda width: pl.BlockSpec((ROW_TILE, width), lambda i: (i, 0))
    return pl.pallas_call(
        _proj_d_kernel,
        grid=(n // ROW_TILE,),
        in_specs=[rows(D_MODEL), _full(w.shape), _full(e64.shape), _full(gq.shape), _full(gk.shape)],
        out_specs=[rows(768), rows(768), rows(768)],
        out_shape=[jax.ShapeDtypeStruct((n, 768), BF16)] * 3,
        compiler_params=_params("parallel"),
        name="proj_d",
    )(xn, w, e64, gq, gk)


def _for_causal_width(qi, n_qblocks, body):
    n_cls = min(WIDTH_CLASSES, n_qblocks)
    per = n_qblocks // n_cls
    for c in range(n_cls):
        pl.when((qi >= c * per) & (qi < (c + 1) * per))(functools.partial(body, (c + 1) * per * LANE))


def _causal_mask(qi, width):
    t = qi * LANE + lax.broadcasted_iota(jnp.int32, (LANE, width), 0)
    s = lax.broadcasted_iota(jnp.int32, (LANE, width), 1)
    return s <= t, s


def _bias_rows(toep_ref, h, qblk, nchunk):
    return jnp.concatenate([toep_ref[h, jnp.maximum(qblk - c, 0)] for c in range(nchunk)], axis=1)


def _softmax_pv(logits, mask, v):
    logits = jnp.where(mask, logits, NEG_INF)
    m = jnp.max(logits, axis=1, keepdims=True)
    p = jnp.exp(logits - m)
    l = jnp.sum(p, axis=1, keepdims=True)
    return _dot(p.astype(BF16), v) / l


def _dsa_body(width, qi, q_ref, k_ref, v_ref, iq_ref, ik_ref, iw_ref, toep_ref, tri_ref, o_ref, sel_ref,
              n_sel):
    nchunk = width // LANE
    ik = ik_ref[0, :width, :]
    iw = iw_ref[0][:, :N_HEADS] * (N_HEADS ** -0.5 * IDX_DIM ** -0.5)

    score = jnp.zeros((LANE, width), F32)
    for h in range(N_HEADS):
        r = _dot_t(iq_ref[0, :, h * LANE:(h + 1) * LANE], ik)
        score = score + jnp.maximum(r, 0.0) * iw[:, h:h + 1]

    causal, _ = _causal_mask(qi, width)

    bits = pltpu.bitcast(score, jnp.int32)
    key = jnp.where(bits < 0, bits ^ jnp.int32(0x7FFFFFFF), bits)
    key = jnp.where(causal, key, jnp.int32(INT_MIN))

    def bisect(i, ans):
        cand = ans | lax.shift_left(jnp.int32(1), 31 - i)
        cnt = jnp.sum(jnp.where(key >= (cand ^ jnp.int32(INT_MIN)), 1.0, 0.0), axis=1, keepdims=True)
        return jnp.where(cnt >= n_sel, cand, ans)

    thr = lax.fori_loop(0, 32, bisect, jnp.zeros((LANE, 1), jnp.int32)) ^ jnp.int32(INT_MIN)
    above = key > thr
    tied = key == thr
    n_above = jnp.sum(jnp.where(above, 1.0, 0.0), axis=1, keepdims=True)
    n_tied = jnp.sum(jnp.where(tied, 1.0, 0.0), axis=1, keepdims=True)
    room = n_sel - n_above
    sel_ref[:, :width] = jnp.where(key >= thr, 1.0, 0.0)

    overflow = (n_tied > room) & (thr != jnp.int32(INT_MIN))

    @pl.when(jnp.max(jnp.where(overflow, 1.0, 0.0)) > 0.0)
    def _():
        before = jnp.zeros((LANE, 1), F32)
        for c in range(nchunk):
            tc = jnp.where(tied[:, c * LANE:(c + 1) * LANE], 1.0, 0.0)
            rank = before + _dot(tc.astype(BF16), tri_ref[...])
            keep = above[:, c * LANE:(c + 1) * LANE] | ((tc > 0.0) & (rank < room))
            sel_ref[:, c * LANE:(c + 1) * LANE] = jnp.where(keep, 1.0, 0.0)
            before = before + jnp.sum(tc, axis=1, keepdims=True)

    mask = causal & (sel_ref[:, :width] > 0.0)
    k = k_ref[0, :width, :]
    v = v_ref[0, :width, :]
    for h in range(N_HEADS):
        hs = slice(h * HEAD_DIM, (h + 1) * HEAD_DIM)
        logits = _dot_t(q_ref[0, :, hs], k) + _bias_rows(toep_ref, h, qi, nchunk)
        o_ref[0, :, hs] = _softmax_pv(logits, mask, v).astype(o_ref.dtype)


def _dsa_kernel(*refs, seq, n_sel):
    qi = pl.program_id(1)
    _for_causal_width(qi, seq // LANE, functools.partial(_dsa_body, qi=qi, n_sel=n_sel, **_named(refs)))


def _named(refs):
    names = ("q_ref", "k_ref", "v_ref", "iq_ref", "ik_ref", "iw_ref", "toep_ref", "tri_ref", "o_ref", "sel_ref")
    return dict(zip(names, refs, strict=True))


def _dsa(q, k, v, iq, ik, iw, toep, tri):
    b, seq, _ = q.shape
    n_sel = min(DSA_TOPK, seq // 4)
    qblock = lambda width: pl.BlockSpec((1, LANE, width), lambda bi, qi: (bi, qi, 0))
    whole = lambda width: pl.BlockSpec((1, seq, width), lambda bi, qi: (bi, 0, 0))
    return pl.pallas_call(
        functools.partial(_dsa_kernel, seq=seq, n_sel=n_sel),
        grid=(b, seq // LANE),
        in_specs=[qblock(256), whole(64), whole(64), qblock(512), whole(128), qblock(128),
                  _full(toep.shape), _full(tri.shape)],
        out_specs=qblock(256),
        out_shape=jax.ShapeDtypeStruct((b, seq, 256), BF16),
        scratch_shapes=[pltpu.VMEM((LANE, seq), F32)],
        compiler_params=_params("parallel", "parallel"),
        name="dsa",
    )(q, k, v, iq, ik, iw, toep, tri)


def _moba_body(width, qi, q_ref, k_ref, v_ref, km_ref, toep_ref, o_ref, nblk):
    nchunk = width // LANE
    n_sel = min(MOBA_TOPK, nblk - 1)
    own = lax.shift_right_logical(qi * LANE, int(math.log2(MOBA_BLOCK)))
    q = q_ref[0]
    qb = (q * HEAD_DIM ** -0.5).astype(BF16)
    k_mean = jnp.concatenate([km_ref[0], jnp.zeros((LANE - nblk, BRANCH_WIDTH), F32)], axis=0)

    causal, s = _causal_mask(qi, width)
    key_blk = lax.shift_right_logical(s, int(math.log2(MOBA_BLOCK)))
    blk = lax.broadcasted_iota(jnp.int32, (LANE, LANE), 1)

    for h in range(N_HEADS):
        hs = slice(h * HEAD_DIM, (h + 1) * HEAD_DIM)
        gate = _dot_t(q[:, hs], k_mean[:, hs], precision=lax.Precision.HIGHEST)
        gate = jnp.where(blk < own, gate, NEG_INF)
        rank = jnp.zeros((LANE, LANE), F32)
        for m in range(nblk):
            gm = gate[:, m:m + 1]
            rank = rank + jnp.where((gm > gate) | ((gm == gate) & (m < blk)), 1.0, 0.0)
        picked = jnp.where((blk < own) & (rank < n_sel), 1.0, 0.0)
        allowed = jnp.concatenate(
            [jnp.broadcast_to(picked[:, n:n + 1], (LANE, MOBA_BLOCK)) for n in range(width // MOBA_BLOCK)],
            axis=1)
        mask = causal & ((key_blk == own) | (allowed > 0.0))
        logits = _dot_t(qb[:, hs], k_ref[0, :width, hs]) + _bias_rows(toep_ref, h, qi, nchunk)
        o_ref[0, :, hs] = _softmax_pv(logits, mask, v_ref[0, :width, hs]).astype(o_ref.dtype)


def _moba_kernel(q_ref, k_ref, v_ref, km_ref, toep_ref, o_ref, *, seq):
    qi = pl.program_id(1)
    body = functools.partial(_moba_body, qi=qi, q_ref=q_ref, k_ref=k_ref, v_ref=v_ref, km_ref=km_ref,
                             toep_ref=toep_ref, o_ref=o_ref, nblk=seq // MOBA_BLOCK)
    _for_causal_width(qi, seq // LANE, body)


def _moba(q, k, v, k_mean, toep):
    b, seq, _ = q.shape
    qblock = pl.BlockSpec((1, LANE, 256), lambda bi, qi: (bi, qi, 0))
    whole = pl.BlockSpec((1, seq, 256), lambda bi, qi: (bi, 0, 0))
    means = pl.BlockSpec((1, seq // MOBA_BLOCK, 256), lambda bi, qi: (bi, 0, 0))
    return pl.pallas_call(
        functools.partial(_moba_kernel, seq=seq),
        grid=(b, seq // LANE),
        in_specs=[qblock, whole, whole, means, _full(toep.shape)],
        out_specs=qblock,
        out_shape=jax.ShapeDtypeStruct((b, seq, 256), BF16),
        compiler_params=_params("parallel", "parallel"),
        name="moba",
    )(q, k, v, k_mean, toep)


def _mla_body(width, qi, q_ref, k_ref, v_ref, o_ref):
    causal, _ = _causal_mask(qi, width)
    for h in range(N_HEADS):
        qs = slice(h * MLA_SLOT, (h + 1) * MLA_SLOT)
        vs = slice(h * MLA_V, (h + 1) * MLA_V)
        logits = _dot_t(q_ref[0, :, qs], k_ref[0, :width, qs]) * MLA_QK ** -0.5
        o_ref[0, :, vs] = _softmax_pv(logits, causal, v_ref[0, :width, vs]).astype(o_ref.dtype)


def _mla_kernel(q_ref, k_ref, v_ref, o_ref, *, seq):
    qi = pl.program_id(1)
    body = functools.partial(_mla_body, qi=qi, q_ref=q_ref, k_ref=k_ref, v_ref=v_ref, o_ref=o_ref)
    _for_causal_width(qi, seq // LANE, body)


def _mla(q, k, v):
    b, seq, _ = q.shape
    return pl.pallas_call(
        functools.partial(_mla_kernel, seq=seq),
        grid=(b, seq // LANE),
        in_specs=[pl.BlockSpec((1, LANE, N_HEADS * MLA_SLOT), lambda bi, qi: (bi, qi, 0)),
                  pl.BlockSpec((1, seq, N_HEADS * MLA_SLOT), lambda bi, qi: (bi, 0, 0)),
                  pl.BlockSpec((1, seq, 256), lambda bi, qi: (bi, 0, 0))],
        out_specs=pl.BlockSpec((1, LANE, 256), lambda bi, qi: (bi, qi, 0)),
        out_shape=jax.ShapeDtypeStruct((b, seq, 256), BF16),
        compiler_params=_params("parallel", "parallel"),
        name="mla",
    )(q, k, v)


def _dil_group(dil, seq, toep_ref, g, qf_ref, kf_ref, vf_ref, m_ref, l_ref, acc_ref):
    n_band = seq // (dil * LANE)

    def unit(u, carry):
        r = lax.div(u, n_band)
        n = lax.rem(u, n_band)
        if dil == 1:
            cur = pl.ds(pl.multiple_of(n * LANE, LANE), LANE)
            prev = pl.ds(pl.multiple_of(jnp.maximum(n - 1, 0) * LANE, LANE), LANE)
        else:
            cur = pl.ds(n * (LANE * dil) + r, LANE, stride=dil)
            prev = pl.ds(jnp.maximum(n - 1, 0) * (LANE * dil) + r, LANE, stride=dil)
        qi = lax.broadcasted_iota(jnp.int32, (LANE, 2 * LANE), 0)
        kj = lax.broadcasted_iota(jnp.int32, (LANE, 2 * LANE), 1)
        dist = LANE + qi - kj
        mask = (dist >= 0) & (dist <= LANE) & ((kj >= LANE) | (n > 0))
        q = _get_rows(qf_ref, cur).astype(BF16)
        kk = jnp.concatenate([_get_rows(kf_ref, prev), _get_rows(kf_ref, cur)], axis=0).astype(BF16)
        vv = jnp.concatenate([_get_rows(vf_ref, prev), _get_rows(vf_ref, cur)], axis=0).astype(BF16)
        ms, ls, pvs = [], [], []
        for h in range(N_HEADS):
            hs = slice(h * HEAD_DIM, (h + 1) * HEAD_DIM)
            bias = jnp.concatenate([toep_ref[g, h, 1], toep_ref[g, h, 0]], axis=1)
            logits = jnp.where(mask, _dot_t(q[:, hs], kk[:, hs]) + bias, NEG_INF)
            m = jnp.max(logits, axis=1, keepdims=True)
            p = jnp.exp(logits - m)
            ms.append(jnp.broadcast_to(m, (LANE, HEAD_DIM)))
            ls.append(jnp.broadcast_to(jnp.sum(p, axis=1, keepdims=True), (LANE, HEAD_DIM)))
            pvs.append(_dot(p.astype(BF16), vv[:, hs]))
        m_new, l_new, pv = (jnp.concatenate(z, axis=1) for z in (ms, ls, pvs))
        m_old = _get_rows(m_ref, cur)
        m_tot = jnp.maximum(m_old, m_new)
        a_old = jnp.exp(m_old - m_tot)
        a_new = jnp.exp(m_new - m_tot)
        _set_rows(m_ref, cur, m_tot)
        _set_rows(l_ref, cur, a_old * _get_rows(l_ref, cur) + a_new * l_new)
        _set_rows(acc_ref, cur, a_old * _get_rows(acc_ref, cur) + a_new * pv)
        return carry

    lax.fori_loop(0, dil * n_band, unit, 0)


def _get_rows(ref, rows):
    return jnp.concatenate([ref[0, rows, :], ref[1, rows, :]], axis=1)


def _set_rows(ref, rows, val):
    ref[0, rows, :] = val[:, :LANE]
    ref[1, rows, :] = val[:, LANE:]


def _dil_kernel(q_ref, k_ref, v_ref, toep_ref, o_ref, qf_ref, kf_ref, vf_ref, m_ref, l_ref, acc_ref, *, seq):
    g = pl.program_id(1)
    everything = pl.ds(0, seq)
    _set_rows(qf_ref, everything, q_ref[0].astype(F32))
    _set_rows(kf_ref, everything, k_ref[0].astype(F32))
    _set_rows(vf_ref, everything, v_ref[0].astype(F32))

    @pl.when(g == 0)
    def _():
        m_ref[...] = jnp.full(m_ref.shape, NEG_INF, F32)
        l_ref[...] = jnp.zeros_like(l_ref)
        acc_ref[...] = jnp.zeros_like(acc_ref)

    for gi, (_, dil) in enumerate(DIL_PATTERNS):
        pl.when(g == gi)(functools.partial(_dil_group, dil, seq, toep_ref, gi, qf_ref, kf_ref, vf_ref,
                                           m_ref, l_ref, acc_ref))

    @pl.when(g == DIL_GROUPS - 1)
    def _():
        o_ref[0] = (_get_rows(acc_ref, everything) / _get_rows(l_ref, everything)).astype(o_ref.dtype)


def _dil(q, k, v, toep):
    b, seq, _ = q.shape
    group = pl.BlockSpec((1, seq, 256), lambda bi, g: (bi, 0, g))
    state = pltpu.VMEM((2, seq, LANE), F32)
    return pl.pallas_call(
        functools.partial(_dil_kernel, seq=seq),
        grid=(b, DIL_GROUPS),
        in_specs=[group, group, group, _full(toep.shape)],
        out_specs=pl.BlockSpec((1, seq, 256), lambda bi, g: (bi, 0, 0)),
        out_shape=jax.ShapeDtypeStruct((b, seq, 256), BF16),
        scratch_shapes=[state] * 6,
        compiler_params=_params("parallel", "arbitrary"),
        name="dil",
    )(q, k, v, toep)


def _merge_kernel(x_ref, xn_ref, oa_ref, ob_ref, oc_ref, od_ref, wg_ref, bg_ref, wb_ref, wo_ref, xo_ref):
    branches = (oa_ref, ob_ref, oc_ref, od_ref)
    xn = xn_ref[...]
    total = jnp.zeros(x_ref.shape, F32)
    for n in range(N_BRANCH):
        cols = slice(n * D_MODEL, (n + 1) * D_MODEL)
        gate = _sigmoid(_dot(xn, wg_ref[:, cols]) + bg_ref[:, cols])
        total = total + gate * _dot(branches[n][...], wb_ref[n])
    xo_ref[...] = x_ref[...] + _dot(total.astype(BF16), wo_ref[...])


def _merge(x, xn, outs, wg, bg, wb, wo):
    n = x.shape[0]
    rows = lambda width: pl.BlockSpec((ROW_TILE, width), lambda i: (i, 0))
    return pl.pallas_call(
        _merge_kernel,
        grid=(n // ROW_TILE,),
        in_specs=[rows(D_MODEL), rows(D_MODEL)] + [rows(256)] * N_BRANCH
                 + [_full(wg.shape), _full(bg.shape), _full(wb.shape), _full(wo.shape)],
        out_specs=rows(D_MODEL),
        out_shape=jax.ShapeDtypeStruct((n, D_MODEL), F32),
        compiler_params=_params("parallel"),
        name="merge",
    )(x, xn, *outs, wg, bg, wb, wo)


def _bucket_of_distance():
    d = np.arange(REL_MAX_DIST + 1)
    exact = REL_BUCKETS // 2
    nf = np.maximum(d, 1).astype(np.float32)
    log_b = exact + (np.log(nf / exact) / math.log(REL_MAX_DIST / exact) * (REL_BUCKETS - exact)).astype(np.int32)
    return np.where(d < exact, d, np.minimum(log_b, REL_BUCKETS - 1))


def _toeplitz_tiles(table, n_off, dil):
    span = 2 * LANE
    m = np.arange(span)
    off = np.arange(n_off)[:, None]
    dist = np.clip((off * LANE + LANE - 1 - m[None, :]) * dil, 0, REL_MAX_DIST)
    v = table[:, _bucket_of_distance()[dist]]
    flat = jnp.tile(v, (1, 1, LANE))[..., :LANE * (span - 1)]
    return flat.reshape(table.shape[0], n_off, LANE, span - 1)[..., LANE - 1:]


def _same_head(width, head):
    idx = np.arange(width) // head
    return jnp.asarray(idx[:, None] == idx[None, :], BF16)


def _rope_tables(seq):
    half = MLA_ROPE // 2
    freqs = ROPE_THETA ** (-np.arange(half, dtype=np.float64) / half)
    ang = np.arange(seq, dtype=np.float64)[:, None] * freqs[None, :]
    pad = np.zeros((seq, MLA_SLOT - MLA_QK))
    cos_h = np.concatenate([np.ones((seq, MLA_NOPE)), np.cos(ang), np.cos(ang), pad], axis=1)
    sin_h = np.concatenate([np.zeros((seq, MLA_NOPE)), np.sin(ang), np.sin(ang), pad], axis=1)
    return (jnp.asarray(np.tile(cos_h, (1, N_HEADS)), F32), jnp.asarray(np.tile(sin_h, (1, N_HEADS)), F32))


def _rot_half_cols():
    half = MLA_ROPE // 2
    src = np.arange(MLA_SLOT)
    sign = np.zeros(MLA_SLOT, np.float32)
    src[MLA_NOPE:MLA_NOPE + half] = np.arange(MLA_NOPE + half, MLA_QK)
    sign[MLA_NOPE:MLA_NOPE + half] = -1.0
    src[MLA_NOPE + half:MLA_QK] = np.arange(MLA_NOPE, MLA_NOPE + half)
    sign[MLA_NOPE + half:MLA_QK] = 1.0
    return src, sign


def _mixer(x, xn, seq, w_in, b_gate, qk_a, qk_b, qk_c, qk_d, mla_nq, w_uq, mla_nkv, w_ukv,
           w_branch, w_out, toeps, consts):
    n = x.shape[0]
    b = n // seq
    e64, e96, tri, cos, sin = consts
    toep_a, toep_b, toep_d = toeps
    row = lambda a: a.reshape(1, -1)
    tile4 = lambda g: jnp.tile(g, N_HEADS).reshape(1, -1)
    r3 = lambda a: a.reshape(b, seq, a.shape[-1])

    off_iq = OFF_A + 6 * HEAD_DIM
    off_ik = off_iq + N_HEADS * IDX_DIM
    iq_rep = jnp.concatenate([jnp.tile(w_in[:, off_iq + h * IDX_DIM:off_iq + (h + 1) * IDX_DIM], (1, 4))
                              for h in range(N_HEADS)], axis=1)
    ik_rep = jnp.tile(w_in[:, off_ik:off_ik + IDX_DIM], (1, 4))
    iw_pad = jnp.pad(w_in[:, off_ik + IDX_DIM:OFF_A + W_A], ((0, 0), (0, LANE - N_HEADS)))
    wa = jnp.concatenate([w_in[:, OFF_A:off_iq], iq_rep, ik_rep, iw_pad], axis=1).astype(BF16)
    qa, ka, va, iq, ik, iw = _proj_a(xn, wa, e64, tile4(qk_a[0]), row(qk_a[1]))
    out_a = _dsa(r3(qa), r3(ka), r3(va), r3(iq), r3(ik), r3(iw), toep_a, tri).reshape(n, 256)

    wb_in = w_in[:, OFF_B:OFF_B + W_B].astype(BF16)
    qb, kb, vb, kmean = _proj_b(xn, wb_in, e64, tile4(qk_b[0]), tile4(qk_b[1]))
    out_b = _moba(r3(qb), r3(kb), r3(vb), kmean.reshape(b, seq // MOBA_BLOCK, BRANCH_WIDTH),
                  toep_b).reshape(n, 256)

    src, sign = _rot_half_cols()
    src4 = np.concatenate([h * MLA_SLOT + src for h in range(N_HEADS)])
    sign4 = jnp.asarray(np.tile(sign, N_HEADS))
    slot_pad = MLA_SLOT - MLA_QK
    w_kr = w_in[:, OFF_C + MLA_Q_LORA + MLA_KV_LORA:OFF_C + W_C]
    kr_slots = jnp.pad(w_kr, ((0, 0), (MLA_NOPE, slot_pad)))
    kr_slots = jnp.tile(kr_slots, (1, N_HEADS))
    wc = jnp.concatenate([w_in[:, OFF_C:OFF_C + MLA_Q_LORA + MLA_KV_LORA], kr_slots,
                          kr_slots[:, src4] * sign4], axis=1).astype(BF16)
    slots = lambda w: jnp.pad(w.reshape(w.shape[0], N_HEADS, -1),
                              ((0, 0), (0, 0), (0, MLA_SLOT - w.shape[1] // N_HEADS))).reshape(w.shape[0], -1)
    wuq = slots(w_uq)
    ukv = w_ukv.reshape(MLA_KV_LORA, N_HEADS, MLA_NOPE + MLA_V)
    wuk = slots(ukv[:, :, :MLA_NOPE].reshape(MLA_KV_LORA, -1))
    wuv = ukv[:, :, MLA_NOPE:].reshape(MLA_KV_LORA, -1)
    gq = jnp.tile(jnp.pad(qk_c[0], (0, slot_pad)), N_HEADS)
    gk = jnp.tile(jnp.pad(qk_c[1], (0, slot_pad)), N_HEADS)
    qc, kc, vc = _proj_c(xn, wc, e96, row(mla_nq), row(mla_nkv), wuq.astype(BF16),
                         (wuq[:, src4] * sign4).astype(BF16), wuk.astype(BF16), wuv.astype(BF16),
                         row(gq), row(gq[src4]), row(gk), row(gk[src4]), cos, sin, seq)
    out_c = _mla(r3(qc), r3(kc), r3(vc)).reshape(n, 256)

    wd = w_in[:, OFF_D:OFF_D + W_D].astype(BF16)
    qd, kd, vd = _proj_d(xn, wd, e64, tile4(qk_d[0]), tile4(qk_d[1]))
    out_d = _dil(r3(qd), r3(kd), r3(vd), toep_d).reshape(n, 256)

    wg = w_in[:, OFF_G:OFF_G + W_G].astype(BF16)
    return _merge(x, xn, (out_a, out_b, out_c, out_d), wg, b_gate.reshape(1, -1),
                  w_branch.astype(BF16), w_out.astype(BF16))


def kernel(x, norm_gain, w_in, b_gate, qk_gain_a, qk_gain_b, qk_gain_c, qk_gain_d, mla_norm_q,
           w_mla_uq, mla_norm_kv, w_mla_ukv, w_branch, w_out, rel_bias, w_ffn_in, w_ffn_out):
    b, seq, d = x.shape
    depth = norm_gain.shape[0]
    toep_a = _toeplitz_tiles(rel_bias[0:4], seq // LANE, 1)
    toep_b = _toeplitz_tiles(rel_bias[4:8], seq // LANE, 1)
    toep_d = jnp.stack([_toeplitz_tiles(rel_bias[8 + 4 * g:12 + 4 * g], 2, dil)
                        for g, (_, dil) in enumerate(DIL_PATTERNS)])
    tri = jnp.asarray(np.arange(LANE)[:, None] < np.arange(LANE)[None, :], BF16)
    consts = (_same_head(256, HEAD_DIM), _same_head(N_HEADS * MLA_SLOT, MLA_SLOT), tri) + _rope_tables(seq)

    x = x.reshape(b * seq, d)
    for l in range(depth):
        g = norm_gain[l]
        x, xn = _ffn(x, g[0:1], g[1:2], w_ffn_in[l, 0].astype(BF16), w_ffn_out[l, 0].astype(BF16))
        x = _mixer(x, xn, seq, w_in[l], b_gate[l], qk_gain_a[l], qk_gain_b[l], qk_gain_c[l],
                   qk_gain_d[l], mla_norm_q[l], w_mla_uq[l], mla_norm_kv[l], w_mla_ukv[l],
                   w_branch[l], w_out[l], (toep_a, toep_b, toep_d), consts)
        x, _ = _ffn(x, g[2:3], g[2:3], w_ffn_in[l, 1].astype(BF16), w_ffn_out[l, 1].astype(BF16))
    return x.reshape(b, seq, d)
```

```python
import functools
import math

import numpy as np
import jax
import jax.numpy as jnp
from jax import lax
from jax.experimental import pallas as pl
from jax.experimental.pallas import tpu as pltpu

D_MODEL = 1024
HEAD_DIM = 64
N_HEADS = 4
BRANCH_WIDTH = N_HEADS * HEAD_DIM
IDX_DIM = 32
DSA_TOPK = 256
MOBA_BLOCK = 256
MOBA_TOPK = 3
MLA_Q_LORA = 384
MLA_KV_LORA = 256
MLA_NOPE = 64
MLA_ROPE = 32
MLA_QK = MLA_NOPE + MLA_ROPE
MLA_V = 64
MLA_SLOT = 128
ROPE_THETA = 10000.0
DIL_PATTERNS = ((128, 1), (512, 4), (2048, 16))
DIL_GROUPS = 3
N_BRANCH = 4
D_FF = 2816
REL_BUCKETS = 32
REL_MAX_DIST = 2048
RMS_EPS = 1e-6

OFF_A = 0
W_A = 4 * HEAD_DIM + 2 * HEAD_DIM + 4 * IDX_DIM + IDX_DIM + 4
OFF_B = OFF_A + W_A
W_B = 3 * BRANCH_WIDTH
OFF_C = OFF_B + W_B
W_C = MLA_Q_LORA + MLA_KV_LORA + MLA_ROPE
OFF_D = OFF_C + W_C
W_D = 3 * DIL_GROUPS * BRANCH_WIDTH
OFF_G = OFF_D + W_D
W_G = N_BRANCH * D_MODEL

LANE = 128
ROW_TILE = 512
FF_CHUNK = 1408
WIDTH_CLASSES = 4
DSA_ROWS = 256
MASKED = -1e30
VMEM_LIMIT = 56 * 1024 * 1024

F32 = jnp.float32
BF16 = jnp.bfloat16
NEG_INF = float("-inf")
INT_MIN = -2 ** 31

_CONTRACT_LAST = (((1,), (1,)), ((), ()))


def _dot(a, b, precision=None):
    return jnp.dot(a, b, preferred_element_type=F32, precision=precision)


def _dot_t(a, b, precision=None):
    return lax.dot_general(a, b, _CONTRACT_LAST, preferred_element_type=F32, precision=precision)


def _rms(x, gain):
    return x * lax.rsqrt(jnp.mean(x * x, axis=-1, keepdims=True) + RMS_EPS) * gain


def _sigmoid(x):
    return 1.0 / (1.0 + jnp.exp(-x))


def _head_sumsq(y, e_ref):
    y2 = y * y
    hi = y2.astype(BF16)
    lo = (y2 - hi.astype(F32)).astype(BF16)
    e = e_ref[...]
    return _dot(hi, e) + _dot(lo, e)


def _head_norm(y, e_ref, gain, width):
    return y * lax.rsqrt(_head_sumsq(y, e_ref) * (1.0 / width) + RMS_EPS) * gain


def _params(*sem):
    return pltpu.CompilerParams(dimension_semantics=sem, vmem_limit_bytes=VMEM_LIMIT)


def _full(shape):
    return pl.BlockSpec(shape, lambda *_: (0,) * len(shape))


def _ffn_kernel(x_ref, g_ref, gn_ref, wg_ref, wu_ref, wo_ref, xo_ref, xn_ref, xb_ref, acc_ref):
    j = pl.program_id(1)

    @pl.when(j == 0)
    def _():
        xb_ref[...] = _rms(x_ref[...], g_ref[...]).astype(BF16)
        acc_ref[...] = jnp.zeros_like(acc_ref)

    xb = xb_ref[...]
    gate = _dot(xb, wg_ref[...])
    up = _dot(xb, wu_ref[...])
    h = (gate * _sigmoid(gate) * up).astype(BF16)
    acc_ref[...] += _dot(h, wo_ref[...])

    @pl.when(j == pl.num_programs(1) - 1)
    def _():
        xo = x_ref[...] + 0.5 * acc_ref[...]
        xo_ref[...] = xo
        xn_ref[...] = _rms(xo, gn_ref[...]).astype(BF16)


def _ffn(x, gain, gain_next, w_in, w_out):
    n = x.shape[0]
    nchunk = D_FF // FF_CHUNK
    return pl.pallas_call(
        _ffn_kernel,
        grid=(n // ROW_TILE, nchunk),
        in_specs=[
            pl.BlockSpec((ROW_TILE, D_MODEL), lambda i, j: (i, 0)),
            _full((1, D_MODEL)),
            _full((1, D_MODEL)),
            pl.BlockSpec((D_MODEL, FF_CHUNK), lambda i, j: (0, j)),
            pl.BlockSpec((D_MODEL, FF_CHUNK), lambda i, j: (0, j + nchunk)),
            pl.BlockSpec((FF_CHUNK, D_MODEL), lambda i, j: (j, 0)),
        ],
        out_specs=[
            pl.BlockSpec((ROW_TILE, D_MODEL), lambda i, j: (i, 0)),
            pl.BlockSpec((ROW_TILE, D_MODEL), lambda i, j: (i, 0)),
        ],
        out_shape=[jax.ShapeDtypeStruct((n, D_MODEL), F32), jax.ShapeDtypeStruct((n, D_MODEL), BF16)],
        scratch_shapes=[pltpu.VMEM((ROW_TILE, D_MODEL), BF16), pltpu.VMEM((ROW_TILE, D_MODEL), F32)],
        compiler_params=_params("parallel", "arbitrary"),
        name="ffn",
    )(x, gain, gain_next, w_in, w_in, w_out)


def _split_terms(rep, hi_lanes):
    hi = rep.astype(BF16).astype(F32)
    return jnp.where(hi_lanes, hi, rep - hi).astype(BF16)


def _proj_a_kernel(xn_ref, w_ref, e_ref, gq_ref, gk_ref, q_ref, k_ref, v_ref, iq_ref, ik_ref, iw_ref):
    p = _dot(xn_ref[...], w_ref[...])
    q = _head_norm(p[:, :256], e_ref, gq_ref[...], HEAD_DIM) * HEAD_DIM ** -0.5
    q_ref[...] = q.astype(BF16)
    k_ref[...] = _rms(p[:, 256:320], gk_ref[...]).astype(BF16)
    v_ref[...] = p[:, 320:384].astype(BF16)
    lane = lax.broadcasted_iota(jnp.int32, (1, 4 * LANE), 1)
    copy = lax.shift_right_logical(lane, int(math.log2(IDX_DIM))) & 3
    iq_ref[...] = _split_terms(p[:, 384:896], copy < 2)
    ik_ref[...] = _split_terms(p[:, 896:1024], (copy[:, :LANE] & 1) == 0)
    iw_ref[...] = p[:, 1024:1152]


def _proj_a(xn, w, e64, gq, gk):
    n = xn.shape[0]
    rows = lambda width: pl.BlockSpec((ROW_TILE, width), lambda i: (i, 0))
    return pl.pallas_call(
        _proj_a_kernel,
        grid=(n // ROW_TILE,),
        in_specs=[rows(D_MODEL), _full(w.shape), _full(e64.shape), _full(gq.shape), _full(gk.shape)],
        out_specs=[rows(256), rows(64), rows(64), rows(512), rows(128), rows(128)],
        out_shape=[jax.ShapeDtypeStruct((n, 256), BF16), jax.ShapeDtypeStruct((n, 64), BF16),
                   jax.ShapeDtypeStruct((n, 64), BF16), jax.ShapeDtypeStruct((n, 512), BF16),
                   jax.ShapeDtypeStruct((n, 128), BF16), jax.ShapeDtypeStruct((n, 128), F32)],
        compiler_params=_params("parallel"),
        name="proj_a",
    )(xn, w, e64, gq, gk)


def _proj_b_kernel(xn_ref, w_ref, e_ref, gq_ref, gk_ref, q_ref, k_ref, v_ref, km_ref):
    p = _dot(xn_ref[...], w_ref[...])
    q_ref[...] = _head_norm(p[:, :256], e_ref, gq_ref[...], HEAD_DIM)
    k = _head_norm(p[:, 256:512], e_ref, gk_ref[...], HEAD_DIM)
    k_ref[...] = k.astype(BF16)
    v_ref[...] = p[:, 512:768].astype(BF16)
    km = jnp.mean(k.reshape(ROW_TILE // MOBA_BLOCK, MOBA_BLOCK, BRANCH_WIDTH), axis=1)
    for j in range(ROW_TILE // MOBA_BLOCK):
        km_ref[j] = km[j:j + 1]


def _proj_b(xn, w, e64, gq, gk):
    n = xn.shape[0]
    rows = lambda width: pl.BlockSpec((ROW_TILE, width), lambda i: (i, 0))
    per_tile = ROW_TILE // MOBA_BLOCK
    return pl.pallas_call(
        _proj_b_kernel,
        grid=(n // ROW_TILE,),
        in_specs=[rows(D_MODEL), _full(w.shape), _full(e64.shape), _full(gq.shape), _full(gk.shape)],
        out_specs=[rows(256), rows(256), rows(256),
                   pl.BlockSpec((per_tile, 1, BRANCH_WIDTH), lambda i: (i, 0, 0))],
        out_shape=[jax.ShapeDtypeStruct((n, 256), F32), jax.ShapeDtypeStruct((n, 256), BF16),
                   jax.ShapeDtypeStruct((n, 256), BF16),
                   jax.ShapeDtypeStruct((n // MOBA_BLOCK, 1, BRANCH_WIDTH), F32)],
        compiler_params=_params("parallel"),
        name="proj_b",
    )(xn, w, e64, gq, gk)


def _proj_c_kernel(xn_ref, w_ref, e_ref, nq_ref, nkv_ref, wuq_ref, wuqs_ref, wuk_ref, wuv_ref,
                   gq_ref, gqs_ref, gk_ref, gks_ref, cos_ref, sin_ref, q_ref, k_ref, v_ref):
    p = _dot(xn_ref[...], w_ref[...])
    cos = cos_ref[...]
    sin = sin_ref[...]
    xq = _rms(p[:, :384], nq_ref[...]).astype(BF16)
    qa = _dot(xq, wuq_ref[...])
    qs = _dot(xq, wuqs_ref[...])
    rq = lax.rsqrt(_head_sumsq(qa, e_ref) * (1.0 / MLA_QK) + RMS_EPS)
    q_ref[...] = (rq * (qa * gq_ref[...] * cos + qs * gqs_ref[...] * sin)).astype(BF16)
    xkv = _rms(p[:, 384:640], nkv_ref[...]).astype(BF16)
    ka = _dot(xkv, wuk_ref[...]) + p[:, 640:1152]
    ks = p[:, 1152:1664]
    rk = lax.rsqrt(_head_sumsq(ka, e_ref) * (1.0 / MLA_QK) + RMS_EPS)
    k_ref[...] = (rk * (ka * gk_ref[...] * cos + ks * gks_ref[...] * sin)).astype(BF16)
    v_ref[...] = _dot(xkv, wuv_ref[...]).astype(BF16)


def _proj_c(xn, w, e96, nq, nkv, wuq, wuqs, wuk, wuv, gq, gqs, gk, gks, cos, sin, seq):
    n = xn.shape[0]
    rows = lambda width: pl.BlockSpec((ROW_TILE, width), lambda i: (i, 0))
    pos = pl.BlockSpec((ROW_TILE, N_HEADS * MLA_SLOT), lambda i: (i % (seq // ROW_TILE), 0))
    consts = (w, e96, nq, nkv, wuq, wuqs, wuk, wuv, gq, gqs, gk, gks)
    return pl.pallas_call(
        _proj_c_kernel,
        grid=(n // ROW_TILE,),
        in_specs=[rows(D_MODEL)] + [_full(c.shape) for c in consts] + [pos, pos],
        out_specs=[rows(N_HEADS * MLA_SLOT), rows(N_HEADS * MLA_SLOT), rows(256)],
        out_shape=[jax.ShapeDtypeStruct((n, N_HEADS * MLA_SLOT), BF16),
                   jax.ShapeDtypeStruct((n, N_HEADS * MLA_SLOT), BF16),
                   jax.ShapeDtypeStruct((n, 256), BF16)],
        compiler_params=_params("parallel"),
        name="proj_c",
    )(xn, *consts, cos, sin)


def _proj_d_kernel(xn_ref, w_ref, e_ref, gq_ref, gk_ref, q_ref, k_ref, v_ref):
    xn = xn_ref[...]
    for c in range(DIL_GROUPS):
        lo, hi = c * 256, (c + 1) * 256
        pq = _dot(xn, w_ref[:, lo:hi])
        q_ref[:, lo:hi] = (_head_norm(pq, e_ref, gq_ref[...], HEAD_DIM) * HEAD_DIM ** -0.5).astype(BF16)
        pk = _dot(xn, w_ref[:, 768 + lo:768 + hi])
        k_ref[:, lo:hi] = _head_norm(pk, e_ref, gk_ref[...], HEAD_DIM).astype(BF16)
        v_ref[:, lo:hi] = _dot(xn, w_ref[:, 1536 + lo:1536 + hi]).astype(BF16)


def _proj_d(xn, w, e64, gq, gk):
    n = xn.shape[0]
    rows = lambda width: pl.BlockSpec((ROW_TILE, width), lambda i: (i, 0))
    return pl.pallas_call(
        _proj_d_kernel,
        grid=(n // ROW_TILE,),
        in_specs=[rows(D_MODEL), _full(w.shape), _full(e64.shape), _full(gq.shape), _full(gk.shape)],
        out_specs=[rows(768), rows(768), rows(768)],
        out_shape=[jax.ShapeDtypeStruct((n, 768), BF16)] * 3,
        compiler_params=_params("parallel"),
        name="proj_d",
    )(xn, w, e64, gq, gk)


def _for_causal_width(qi, n_qtiles, tile_rows, body):
    n_cls = min(WIDTH_CLASSES, n_qtiles)
    per = n_qtiles // n_cls
    for c in range(n_cls):
        pl.when((qi >= c * per) & (qi < (c + 1) * per))(functools.partial(body, (c + 1) * per * tile_rows))


def _bias_rows(toep_ref, h, qblk, nchunk):
    return jnp.concatenate([toep_ref[h, jnp.maximum(qblk - c + 1, 0)] for c in range(nchunk)], axis=1)


def _bias_tile(toep_ref, h, qi, sub_blocks, nchunk):
    return jnp.concatenate([_bias_rows(toep_ref, h, qi * sub_blocks + j, nchunk) for j in range(sub_blocks)],
                           axis=0)


def _softmax_pv(logits, v):
    m = jnp.max(logits, axis=1, keepdims=True)
    p = jnp.exp(logits - m)
    l = jnp.sum(p, axis=1, keepdims=True)
    return _dot(p.astype(BF16), v) / l


def _dsa_body(width, qi, q_ref, k_ref, v_ref, iq_ref, ik_ref, iw_ref, toep_ref, tri_ref, o_ref, sel_ref,
              n_sel):
    rows = DSA_ROWS
    nchunk = width // LANE
    ik = ik_ref[0, :width, :]
    iw = iw_ref[0][:, :N_HEADS] * (N_HEADS ** -0.5 * IDX_DIM ** -0.5)

    score = jnp.zeros((rows, width), F32)
    for h in range(N_HEADS):
        r = _dot_t(iq_ref[0, :, h * LANE:(h + 1) * LANE], ik)
        score = score + jnp.maximum(r, 0.0) * iw[:, h:h + 1]

    t = qi * rows + lax.broadcasted_iota(jnp.int32, (rows, width), 0)
    s = lax.broadcasted_iota(jnp.int32, (rows, width), 1)

    bits = pltpu.bitcast(score, jnp.int32)
    key = jnp.where(bits < 0, bits ^ jnp.int32(0x7FFFFFFF), bits)
    key = jnp.where(s <= t, key, jnp.int32(INT_MIN))

    half_min = -2 ** 15
    upper = lax.shift_right_arithmetic(key, 16).astype(jnp.int16)
    lower = ((key & jnp.int32(0xFFFF)) + half_min).astype(jnp.int16)

    def count(hit):
        ones = jnp.where(hit, jnp.int16(1), jnp.int16(0))
        acc = ones[:, :LANE]
        for c in range(1, nchunk):
            acc = acc + ones[:, c * LANE:(c + 1) * LANE]
        return jnp.sum(acc.astype(F32), axis=1, keepdims=True)

    def search(half, need):
        def step(i, ans):
            cand = ans | lax.shift_left(jnp.int32(1), 15 - i)
            cnt = count(half >= (cand + half_min).astype(jnp.int16))
            return jnp.where(cnt >= need, cand, ans)
        return lax.fori_loop(0, 16, step, jnp.zeros((rows, 1), jnp.int32))

    thr_upper = search(upper, n_sel) + half_min
    thr_upper16 = thr_upper.astype(jnp.int16)
    n_over = count(upper > thr_upper16)
    candidates = jnp.where(upper == thr_upper16, lower, jnp.int16(half_min))
    thr = thr_upper * 65536 + search(candidates, n_sel - n_over)

    above = key > thr
    tied = key == thr
    n_above = jnp.sum(jnp.where(above, 1.0, 0.0), axis=1, keepdims=True)
    n_tied = jnp.sum(jnp.where(tied, 1.0, 0.0), axis=1, keepdims=True)
    room = n_sel - n_above
    sel_ref[:, :width] = jnp.where(key >= jnp.maximum(thr, jnp.int32(INT_MIN + 1)), 1.0, 0.0)

    overflow = (n_tied > room) & (thr != jnp.int32(INT_MIN))

    @pl.when(jnp.max(jnp.where(overflow, 1.0, 0.0)) > 0.0)
    def _():
        before = jnp.zeros((rows, 1), F32)
        for c in range(nchunk):
            tc = jnp.where(tied[:, c * LANE:(c + 1) * LANE], 1.0, 0.0)
            rank = before + _dot(tc.astype(BF16), tri_ref[...])
            keep = above[:, c * LANE:(c + 1) * LANE] | ((tc > 0.0) & (rank < room))
            sel_ref[:, c * LANE:(c + 1) * LANE] = jnp.where(keep, 1.0, 0.0)
            before = before + jnp.sum(tc, axis=1, keepdims=True)

    unselected = (sel_ref[:, :width] - 1.0) * -MASKED
    k = k_ref[0, :width, :]
    v = v_ref[0, :width, :]
    for h in range(N_HEADS):
        hs = slice(h * HEAD_DIM, (h + 1) * HEAD_DIM)
        logits = _dot_t(q_ref[0, :, hs], k) + _bias_tile(toep_ref, h, qi, rows // LANE, nchunk) + unselected
        o_ref[0, :, hs] = _softmax_pv(logits, v).astype(o_ref.dtype)


def _dsa_kernel(*refs, seq, n_sel):
    qi = pl.program_id(1)
    _for_causal_width(qi, seq // DSA_ROWS, DSA_ROWS,
                      functools.partial(_dsa_body, qi=qi, n_sel=n_sel, **_named(refs)))


def _named(refs):
    names = ("q_ref", "k_ref", "v_ref", "iq_ref", "ik_ref", "iw_ref", "toep_ref", "tri_ref", "o_ref", "sel_ref")
    return dict(zip(names, refs, strict=True))


def _dsa(q, k, v, iq, ik, iw, toep, tri):
    b, seq, _ = q.shape
    n_sel = min(DSA_TOPK, seq // 4)
    qblock = lambda width: pl.BlockSpec((1, DSA_ROWS, width), lambda bi, qi: (bi, qi, 0))
    whole = lambda width: pl.BlockSpec((1, seq, width), lambda bi, qi: (bi, 0, 0))
    return pl.pallas_call(
        functools.partial(_dsa_kernel, seq=seq, n_sel=n_sel),
        grid=(b, seq // DSA_ROWS),
        in_specs=[qblock(256), whole(64), whole(64), qblock(512), whole(128), qblock(128),
                  _full(toep.shape), _full(tri.shape)],
        out_specs=qblock(256),
        out_shape=jax.ShapeDtypeStruct((b, seq, 256), BF16),
        scratch_shapes=[pltpu.VMEM((DSA_ROWS, seq), F32)],
        compiler_params=_params("parallel", "parallel"),
        name="dsa",
    )(q, k, v, iq, ik, iw, toep, tri)


def _moba_body(width, qi, q_ref, k_ref, v_ref, km_ref, toep_ref, spread_ref, o_ref, nblk):
    nchunk = width // LANE
    n_sel = min(MOBA_TOPK, nblk - 1)
    own = lax.shift_right_logical(qi * LANE, int(math.log2(MOBA_BLOCK)))
    q = q_ref[0]
    qb = (q * HEAD_DIM ** -0.5).astype(BF16)

    km = km_ref[0]
    head_of_col = lax.shift_right_logical(lax.broadcasted_iota(jnp.int32, (nblk, BRANCH_WIDTH), 1),
                                          int(math.log2(HEAD_DIM)))
    per_head = [jnp.where(head_of_col == h, km, 0.0) for h in range(N_HEADS)]
    pad = jnp.zeros((LANE - N_HEADS * nblk, BRANCH_WIDTH), F32)
    gate = _dot_t(q, jnp.concatenate(per_head + [pad], axis=0), precision=lax.Precision.HIGHEST)

    lane = lax.broadcasted_iota(jnp.int32, (LANE, LANE), 1)
    blk = lane & (nblk - 1)
    past = (blk < own) & (lane < N_HEADS * nblk)
    gate = jnp.where(past, gate, NEG_INF)
    rank = jnp.zeros((LANE, LANE), F32)
    for d in range(1, nblk):
        before = pltpu.roll(gate, d, axis=1)
        rank = rank + jnp.where(blk >= d, jnp.where(before >= gate, 1.0, 0.0), 0.0)
        after = pltpu.roll(gate, LANE - d, axis=1)
        rank = rank + jnp.where(blk < nblk - d, jnp.where(after > gate, 1.0, 0.0), 0.0)
    picked = jnp.where(past, jnp.where(rank < n_sel, 1.0, 0.0), jnp.where(blk == own, 1.0, 0.0)).astype(BF16)

    for h in range(N_HEADS):
        hs = slice(h * HEAD_DIM, (h + 1) * HEAD_DIM)
        unpicked = (_dot(picked, spread_ref[h, :, :width]) - 1.0) * -MASKED
        logits = _dot_t(qb[:, hs], k_ref[0, :width, hs]) + _bias_rows(toep_ref, h, qi, nchunk) + unpicked
        o_ref[0, :, hs] = _softmax_pv(logits, v_ref[0, :width, hs]).astype(o_ref.dtype)


def _moba_kernel(q_ref, k_ref, v_ref, km_ref, toep_ref, spread_ref, o_ref, *, seq):
    qi = pl.program_id(1)
    body = functools.partial(_moba_body, qi=qi, q_ref=q_ref, k_ref=k_ref, v_ref=v_ref, km_ref=km_ref,
                             toep_ref=toep_ref, spread_ref=spread_ref, o_ref=o_ref, nblk=seq // MOBA_BLOCK)
    _for_causal_width(qi, seq // LANE, LANE, body)


def _moba_spread(seq):
    nblk = seq // MOBA_BLOCK
    row = np.arange(LANE)[None, :, None]
    h = np.arange(N_HEADS)[:, None, None]
    blk_of_key = (np.arange(seq) // MOBA_BLOCK)[None, None, :]
    return jnp.asarray(row == h * nblk + blk_of_key, BF16)


def _moba(q, k, v, k_mean, toep):
    b, seq, _ = q.shape
    nblk = seq // MOBA_BLOCK
    if nblk & (nblk - 1) or N_HEADS * nblk > LANE:
        raise ValueError("MoBA lane layout needs a power-of-two block count with heads*blocks <= 128")
    spread = _moba_spread(seq)
    qblock = pl.BlockSpec((1, LANE, 256), lambda bi, qi: (bi, qi, 0))
    whole = pl.BlockSpec((1, seq, 256), lambda bi, qi: (bi, 0, 0))
    means = pl.BlockSpec((1, nblk, 256), lambda bi, qi: (bi, 0, 0))
    return pl.pallas_call(
        functools.partial(_moba_kernel, seq=seq),
        grid=(b, seq // LANE),
        in_specs=[qblock, whole, whole, means, _full(toep.shape), _full(spread.shape)],
        out_specs=qblock,
        out_shape=jax.ShapeDtypeStruct((b, seq, 256), BF16),
        compiler_params=_params("parallel", "parallel"),
        name="moba",
    )(q, k, v, k_mean, toep, spread)


def _mla_body(width, qi, q_ref, k_ref, v_ref, causal_ref, o_ref):
    nchunk = width // LANE
    causal = jnp.concatenate([causal_ref[0, jnp.clip(qi - c + 1, 0, 2)] for c in range(nchunk)], axis=1)
    for h in range(N_HEADS):
        qs = slice(h * MLA_SLOT, (h + 1) * MLA_SLOT)
        vs = slice(h * MLA_V, (h + 1) * MLA_V)
        logits = _dot_t(q_ref[0, :, qs], k_ref[0, :width, qs]) * MLA_QK ** -0.5 + causal
        o_ref[0, :, vs] = _softmax_pv(logits, v_ref[0, :width, vs]).astype(o_ref.dtype)


def _mla_kernel(q_ref, k_ref, v_ref, causal_ref, o_ref, *, seq):
    qi = pl.program_id(1)
    body = functools.partial(_mla_body, qi=qi, q_ref=q_ref, k_ref=k_ref, v_ref=v_ref, causal_ref=causal_ref,
                             o_ref=o_ref)
    _for_causal_width(qi, seq // LANE, LANE, body)


def _mla(q, k, v):
    b, seq, _ = q.shape
    tri = np.where(np.arange(LANE)[:, None] >= np.arange(LANE)[None, :], 0.0, NEG_INF)
    causal = jnp.asarray(np.stack([np.full((LANE, LANE), NEG_INF), tri, np.zeros((LANE, LANE))])[None], F32)
    return pl.pallas_call(
        functools.partial(_mla_kernel, seq=seq),
        grid=(b, seq // LANE),
        in_specs=[pl.BlockSpec((1, LANE, N_HEADS * MLA_SLOT), lambda bi, qi: (bi, qi, 0)),
                  pl.BlockSpec((1, seq, N_HEADS * MLA_SLOT), lambda bi, qi: (bi, 0, 0)),
                  pl.BlockSpec((1, seq, 256), lambda bi, qi: (bi, 0, 0)),
                  _full(causal.shape)],
        out_specs=pl.BlockSpec((1, LANE, 256), lambda bi, qi: (bi, qi, 0)),
        out_shape=jax.ShapeDtypeStruct((b, seq, 256), BF16),
        compiler_params=_params("parallel", "parallel"),
        name="mla",
    )(q, k, v, causal)


def _dil_group(dil, seq, toep_ref, g, qf_ref, kf_ref, vf_ref, m_ref, l_ref, acc_ref):
    n_band = seq // (dil * LANE)

    def unit(u, carry):
        r = lax.div(u, n_band)
        n = lax.rem(u, n_band)
        if dil == 1:
            cur = pl.ds(pl.multiple_of(n * LANE, LANE), LANE)
            prev = pl.ds(pl.multiple_of(jnp.maximum(n - 1, 0) * LANE, LANE), LANE)
        else:
            cur = pl.ds(n * (LANE * dil) + r, LANE, stride=dil)
            prev = pl.ds(jnp.maximum(n - 1, 0) * (LANE * dil) + r, LANE, stride=dil)
        prev_tile = jnp.where(n > 0, 2, 0)
        q = _get_rows(qf_ref, cur).astype(BF16)
        kk = jnp.concatenate([_get_rows(kf_ref, prev), _get_rows(kf_ref, cur)], axis=0).astype(BF16)
        vv = jnp.concatenate([_get_rows(vf_ref, prev), _get_rows(vf_ref, cur)], axis=0).astype(BF16)
        ms, ls, pvs = [], [], []
        for h in range(N_HEADS):
            hs = slice(h * HEAD_DIM, (h + 1) * HEAD_DIM)
            bias = jnp.concatenate([toep_ref[g, h, prev_tile], toep_ref[g, h, 1]], axis=1)
            logits = _dot_t(q[:, hs], kk[:, hs]) + bias
            m = jnp.max(logits, axis=1, keepdims=True)
            p = jnp.exp(logits - m)
            ms.append(jnp.broadcast_to(m, (LANE, HEAD_DIM)))
            ls.append(jnp.broadcast_to(jnp.sum(p, axis=1, keepdims=True), (LANE, HEAD_DIM)))
            pvs.append(_dot(p.astype(BF16), vv[:, hs]))
        m_new, l_new, pv = (jnp.concatenate(z, axis=1) for z in (ms, ls, pvs))
        m_old = _get_rows(m_ref, cur)
        m_tot = jnp.maximum(m_old, m_new)
        a_old = jnp.exp(m_old - m_tot)
        a_new = jnp.exp(m_new - m_tot)
        _set_rows(m_ref, cur, m_tot)
        _set_rows(l_ref, cur, a_old * _get_rows(l_ref, cur) + a_new * l_new)
        _set_rows(acc_ref, cur, a_old * _get_rows(acc_ref, cur) + a_new * pv)
        return carry

    lax.fori_loop(0, dil * n_band, unit, 0)


def _get_rows(ref, rows):
    return jnp.concatenate([ref[0, rows, :], ref[1, rows, :]], axis=1)


def _set_rows(ref, rows, val):
    ref[0, rows, :] = val[:, :LANE]
    ref[1, rows, :] = val[:, LANE:]


def _dil_kernel(q_ref, k_ref, v_ref, toep_ref, o_ref, qf_ref, kf_ref, vf_ref, m_ref, l_ref, acc_ref, *, seq):
    g = pl.program_id(1)
    everything = pl.ds(0, seq)
    _set_rows(qf_ref, everything, q_ref[0].astype(F32))
    _set_rows(kf_ref, everything, k_ref[0].astype(F32))
    _set_rows(vf_ref, everything, v_ref[0].astype(F32))

    @pl.when(g == 0)
    def _():
        m_ref[...] = jnp.full(m_ref.shape, NEG_INF, F32)
        l_ref[...] = jnp.zeros_like(l_ref)
        acc_ref[...] = jnp.zeros_like(acc_ref)

    for gi, (_, dil) in enumerate(DIL_PATTERNS):
        pl.when(g == gi)(functools.partial(_dil_group, dil, seq, toep_ref, gi, qf_ref, kf_ref, vf_ref,
                                           m_ref, l_ref, acc_ref))

    @pl.when(g == DIL_GROUPS - 1)
    def _():
        o_ref[0] = (_get_rows(acc_ref, everything) / _get_rows(l_ref, everything)).astype(o_ref.dtype)


def _dil(q, k, v, toep):
    b, seq, _ = q.shape
    group = pl.BlockSpec((1, seq, 256), lambda bi, g: (bi, 0, g))
    state = pltpu.VMEM((2, seq, LANE), F32)
    return pl.pallas_call(
        functools.partial(_dil_kernel, seq=seq),
        grid=(b, DIL_GROUPS),
        in_specs=[group, group, group, _full(toep.shape)],
        out_specs=pl.BlockSpec((1, seq, 256), lambda bi, g: (bi, 0, 0)),
        out_shape=jax.ShapeDtypeStruct((b, seq, 256), BF16),
        scratch_shapes=[state] * 6,
        compiler_params=_params("parallel", "arbitrary"),
        name="dil",
    )(q, k, v, toep)


def _merge_kernel(x_ref, xn_ref, oa_ref, ob_ref, oc_ref, od_ref, wg_ref, bg_ref, wb_ref, wo_ref, xo_ref):
    branches = (oa_ref, ob_ref, oc_ref, od_ref)
    xn = xn_ref[...]
    total = jnp.zeros(x_ref.shape, F32)
    for n in range(N_BRANCH):
        cols = slice(n * D_MODEL, (n + 1) * D_MODEL)
        gate = _sigmoid(_dot(xn, wg_ref[:, cols]) + bg_ref[:, cols])
        total = total + gate * _dot(branches[n][...], wb_ref[n])
    xo_ref[...] = x_ref[...] + _dot(total.astype(BF16), wo_ref[...])


def _merge(x, xn, outs, wg, bg, wb, wo):
    n = x.shape[0]
    rows = lambda width: pl.BlockSpec((ROW_TILE, width), lambda i: (i, 0))
    return pl.pallas_call(
        _merge_kernel,
        grid=(n // ROW_TILE,),
        in_specs=[rows(D_MODEL), rows(D_MODEL)] + [rows(256)] * N_BRANCH
                 + [_full(wg.shape), _full(bg.shape), _full(wb.shape), _full(wo.shape)],
        out_specs=rows(D_MODEL),
        out_shape=jax.ShapeDtypeStruct((n, D_MODEL), F32),
        compiler_params=_params("parallel"),
        name="merge",
    )(x, xn, *outs, wg, bg, wb, wo)


def _bucket_of_distance():
    d = np.arange(REL_MAX_DIST + 1)
    exact = REL_BUCKETS // 2
    nf = np.maximum(d, 1).astype(np.float32)
    log_b = exact + (np.log(nf / exact) / math.log(REL_MAX_DIST / exact) * (REL_BUCKETS - exact)).astype(np.int32)
    return np.where(d < exact, d, np.minimum(log_b, REL_BUCKETS - 1))


def _toeplitz_tiles(table, n_off, dil, max_steps=None):
    span = 2 * LANE
    m = np.arange(span)
    off = np.arange(n_off)[:, None]
    steps = off * LANE + LANE - 1 - m[None, :]
    live = (steps >= 0) if max_steps is None else (steps >= 0) & (steps <= max_steps)
    v = table[:, _bucket_of_distance()[np.clip(steps * dil, 0, REL_MAX_DIST)]]
    v = jnp.where(live[None], v, NEG_INF)
    flat = jnp.tile(v, (1, 1, LANE))[..., :LANE * (span - 1)]
    tiles = flat.reshape(table.shape[0], n_off, LANE, span - 1)[..., LANE - 1:]
    return jnp.concatenate([jnp.full((table.shape[0], 1, LANE, LANE), NEG_INF, F32), tiles], axis=1)


def _same_head(width, head):
    idx = np.arange(width) // head
    return jnp.asarray(idx[:, None] == idx[None, :], BF16)


def _rope_tables(seq):
    half = MLA_ROPE // 2
    freqs = ROPE_THETA ** (-np.arange(half, dtype=np.float64) / half)
    ang = np.arange(seq, dtype=np.float64)[:, None] * freqs[None, :]
    pad = np.zeros((seq, MLA_SLOT - MLA_QK))
    cos_h = np.concatenate([np.ones((seq, MLA_NOPE)), np.cos(ang), np.cos(ang), pad], axis=1)
    sin_h = np.concatenate([np.zeros((seq, MLA_NOPE)), np.sin(ang), np.sin(ang), pad], axis=1)
    return (jnp.asarray(np.tile(cos_h, (1, N_HEADS)), F32), jnp.asarray(np.tile(sin_h, (1, N_HEADS)), F32))


def _rot_half_cols():
    half = MLA_ROPE // 2
    src = np.arange(MLA_SLOT)
    sign = np.zeros(MLA_SLOT, np.float32)
    src[MLA_NOPE:MLA_NOPE + half] = np.arange(MLA_NOPE + half, MLA_QK)
    sign[MLA_NOPE:MLA_NOPE + half] = -1.0
    src[MLA_NOPE + half:MLA_QK] = np.arange(MLA_NOPE, MLA_NOPE + half)
    sign[MLA_NOPE + half:MLA_QK] = 1.0
    return src, sign


def _mixer(x, xn, seq, w_in, b_gate, qk_a, qk_b, qk_c, qk_d, mla_nq, w_uq, mla_nkv, w_ukv,
           w_branch, w_out, toeps, consts):
    n = x.shape[0]
    b = n // seq
    e64, e96, tri, cos, sin = consts
    toep_a, toep_b, toep_d = toeps
    row = lambda a: a.reshape(1, -1)
    tile4 = lambda g: jnp.tile(g, N_HEADS).reshape(1, -1)
    r3 = lambda a: a.reshape(b, seq, a.shape[-1])

    off_iq = OFF_A + 6 * HEAD_DIM
    off_ik = off_iq + N_HEADS * IDX_DIM
    iq_rep = jnp.concatenate([jnp.tile(w_in[:, off_iq + h * IDX_DIM:off_iq + (h + 1) * IDX_DIM], (1, 4))
                              for h in range(N_HEADS)], axis=1)
    ik_rep = jnp.tile(w_in[:, off_ik:off_ik + IDX_DIM], (1, 4))
    iw_pad = jnp.pad(w_in[:, off_ik + IDX_DIM:OFF_A + W_A], ((0, 0), (0, LANE - N_HEADS)))
    wa = jnp.concatenate([w_in[:, OFF_A:off_iq], iq_rep, ik_rep, iw_pad], axis=1).astype(BF16)
    qa, ka, va, iq, ik, iw = _proj_a(xn, wa, e64, tile4(qk_a[0]), row(qk_a[1]))
    out_a = _dsa(r3(qa), r3(ka), r3(va), r3(iq), r3(ik), r3(iw), toep_a, tri).reshape(n, 256)

    wb_in = w_in[:, OFF_B:OFF_B + W_B].astype(BF16)
    qb, kb, vb, kmean = _proj_b(xn, wb_in, e64, tile4(qk_b[0]), tile4(qk_b[1]))
    out_b = _moba(r3(qb), r3(kb), r3(vb), kmean.reshape(b, seq // MOBA_BLOCK, BRANCH_WIDTH),
                  toep_b).reshape(n, 256)

    src, sign = _rot_half_cols()
    src4 = np.concatenate([h * MLA_SLOT + src for h in range(N_HEADS)])
    sign4 = jnp.asarray(np.tile(sign, N_HEADS))
    slot_pad = MLA_SLOT - MLA_QK
    w_kr = w_in[:, OFF_C + MLA_Q_LORA + MLA_KV_LORA:OFF_C + W_C]
    kr_slots = jnp.pad(w_kr, ((0, 0), (MLA_NOPE, slot_pad)))
    kr_slots = jnp.tile(kr_slots, (1, N_HEADS))
    wc = jnp.concatenate([w_in[:, OFF_C:OFF_C + MLA_Q_LORA + MLA_KV_LORA], kr_slots,
                          kr_slots[:, src4] * sign4], axis=1).astype(BF16)
    slots = lambda w: jnp.pad(w.reshape(w.shape[0], N_HEADS, -1),
                              ((0, 0), (0, 0), (0, MLA_SLOT - w.shape[1] // N_HEADS))).reshape(w.shape[0], -1)
    wuq = slots(w_uq)
    ukv = w_ukv.reshape(MLA_KV_LORA, N_HEADS, MLA_NOPE + MLA_V)
    wuk = slots(ukv[:, :, :MLA_NOPE].reshape(MLA_KV_LORA, -1))
    wuv = ukv[:, :, MLA_NOPE:].reshape(MLA_KV_LORA, -1)
    gq = jnp.tile(jnp.pad(qk_c[0], (0, slot_pad)), N_HEADS)
    gk = jnp.tile(jnp.pad(qk_c[1], (0, slot_pad)), N_HEADS)
    qc, kc, vc = _proj_c(xn, wc, e96, row(mla_nq), row(mla_nkv), wuq.astype(BF16),
                         (wuq[:, src4] * sign4).astype(BF16), wuk.astype(BF16), wuv.astype(BF16),
                         row(gq), row(gq[src4]), row(gk), row(gk[src4]), cos, sin, seq)
    out_c = _mla(r3(qc), r3(kc), r3(vc)).reshape(n, 256)

    wd = w_in[:, OFF_D:OFF_D + W_D].astype(BF16)
    qd, kd, vd = _proj_d(xn, wd, e64, tile4(qk_d[0]), tile4(qk_d[1]))
    out_d = _dil(r3(qd), r3(kd), r3(vd), toep_d).reshape(n, 256)

    wg = w_in[:, OFF_G:OFF_G + W_G].astype(BF16)
    return _merge(x, xn, (out_a, out_b, out_c, out_d), wg, b_gate.reshape(1, -1),
                  w_branch.astype(BF16), w_out.astype(BF16))


def kernel(x, norm_gain, w_in, b_gate, qk_gain_a, qk_gain_b, qk_gain_c, qk_gain_d, mla_norm_q,
           w_mla_uq, mla_norm_kv, w_mla_ukv, w_branch, w_out, rel_bias, w_ffn_in, w_ffn_out):
    b, seq, d = x.shape
    depth = norm_gain.shape[0]
    toep_a = _toeplitz_tiles(rel_bias[0:4], seq // LANE, 1)
    toep_b = _toeplitz_tiles(rel_bias[4:8], seq // LANE, 1)
    toep_d = jnp.stack([_toeplitz_tiles(rel_bias[8 + 4 * g:12 + 4 * g], 2, dil, max_steps=window // dil)
                        for g, (window, dil) in enumerate(DIL_PATTERNS)])
    tri = jnp.asarray(np.arange(LANE)[:, None] < np.arange(LANE)[None, :], BF16)
    consts = (_same_head(256, HEAD_DIM), _same_head(N_HEADS * MLA_SLOT, MLA_SLOT), tri) + _rope_tables(seq)

    x = x.reshape(b * seq, d)
    for l in range(depth):
        g = norm_gain[l]
        x, xn = _ffn(x, g[0:1], g[1:2], w_ffn_in[l, 0].astype(BF16), w_ffn_out[l, 0].astype(BF16))
        x = _mixer(x, xn, seq, w_in[l], b_gate[l], qk_gain_a[l], qk_gain_b[l], qk_gain_c[l],
                   qk_gain_d[l], mla_norm_q[l], w_mla_uq[l], mla_norm_kv[l], w_mla_ukv[l],
                   w_branch[l], w_out[l], (toep_a, toep_b, toep_d), consts)
        x, _ = _ffn(x, g[2:3], g[2:3], w_ffn_in[l, 1].astype(BF16), w_ffn_out[l, 1].astype(BF16))
    return x.reshape(b, seq, d)
```

```python
import functools
import math

import numpy as np
import jax
import jax.numpy as jnp
from jax import lax
from jax.experimental import pallas as pl
from jax.experimental.pallas import tpu as pltpu

D_MODEL = 1024
HEAD_DIM = 64
N_HEADS = 4
BRANCH_WIDTH = N_HEADS * HEAD_DIM
IDX_DIM = 32
DSA_TOPK = 256
MOBA_BLOCK = 256
MOBA_TOPK = 3
MLA_Q_LORA = 384
MLA_KV_LORA = 256
MLA_NOPE = 64
MLA_ROPE = 32
MLA_QK = MLA_NOPE + MLA_ROPE
MLA_V = 64
MLA_SLOT = 128
ROPE_THETA = 10000.0
DIL_PATTERNS = ((128, 1), (512, 4), (2048, 16))
DIL_GROUPS = 3
N_BRANCH = 4
D_FF = 2816
REL_BUCKETS = 32
REL_MAX_DIST = 2048
RMS_EPS = 1e-6

OFF_A = 0
W_A = 4 * HEAD_DIM + 2 * HEAD_DIM + 4 * IDX_DIM + IDX_DIM + 4
OFF_B = OFF_A + W_A
W_B = 3 * BRANCH_WIDTH
OFF_C = OFF_B + W_B
W_C = MLA_Q_LORA + MLA_KV_LORA + MLA_ROPE
OFF_D = OFF_C + W_C
W_D = 3 * DIL_GROUPS * BRANCH_WIDTH
OFF_G = OFF_D + W_D
W_G = N_BRANCH * D_MODEL

LANE = 128
ROW_TILE = 512
FF_CHUNK = 1408
WIDTH_CLASSES = 8
DSA_ROWS = 256
MASKED = -1e30
VMEM_LIMIT = 56 * 1024 * 1024

F32 = jnp.float32
BF16 = jnp.bfloat16
NEG_INF = float("-inf")
INT_MIN = -2 ** 31

_CONTRACT_LAST = (((1,), (1,)), ((), ()))


def _dot(a, b, precision=None):
    return jnp.dot(a, b, preferred_element_type=F32, precision=precision)


def _dot_t(a, b, precision=None):
    return lax.dot_general(a, b, _CONTRACT_LAST, preferred_element_type=F32, precision=precision)


def _rms(x, gain):
    return x * lax.rsqrt(jnp.mean(x * x, axis=-1, keepdims=True) + RMS_EPS) * gain


def _sigmoid(x):
    return 1.0 / (1.0 + jnp.exp(-x))


def _head_sumsq(y, e_ref):
    y2 = y * y
    hi = y2.astype(BF16)
    lo = (y2 - hi.astype(F32)).astype(BF16)
    e = e_ref[...]
    return _dot(hi, e) + _dot(lo, e)


def _head_norm(y, e_ref, gain, width):
    return y * lax.rsqrt(_head_sumsq(y, e_ref) * (1.0 / width) + RMS_EPS) * gain


def _params(*sem):
    return pltpu.CompilerParams(dimension_semantics=sem, vmem_limit_bytes=VMEM_LIMIT)


def _full(shape):
    return pl.BlockSpec(shape, lambda *_: (0,) * len(shape))


def _ffn_kernel(x_ref, g_ref, gn_ref, wg_ref, wu_ref, wo_ref, xo_ref, xn_ref, xb_ref, acc_ref):
    j = pl.program_id(1)

    @pl.when(j == 0)
    def _():
        xb_ref[...] = _rms(x_ref[...], g_ref[...]).astype(BF16)
        acc_ref[...] = jnp.zeros_like(acc_ref)

    xb = xb_ref[...]
    gate = _dot(xb, wg_ref[...])
    up = _dot(xb, wu_ref[...])
    h = (gate * _sigmoid(gate) * up).astype(BF16)
    acc_ref[...] += _dot(h, wo_ref[...])

    @pl.when(j == pl.num_programs(1) - 1)
    def _():
        xo = x_ref[...] + 0.5 * acc_ref[...]
        xo_ref[...] = xo
        xn_ref[...] = _rms(xo, gn_ref[...]).astype(BF16)


def _ffn(x, gain, gain_next, w_in, w_out):
    n = x.shape[0]
    nchunk = D_FF // FF_CHUNK
    return pl.pallas_call(
        _ffn_kernel,
        grid=(n // ROW_TILE, nchunk),
        in_specs=[
            pl.BlockSpec((ROW_TILE, D_MODEL), lambda i, j: (i, 0)),
            _full((1, D_MODEL)),
            _full((1, D_MODEL)),
            pl.BlockSpec((D_MODEL, FF_CHUNK), lambda i, j: (0, j)),
            pl.BlockSpec((D_MODEL, FF_CHUNK), lambda i, j: (0, j + nchunk)),
            pl.BlockSpec((FF_CHUNK, D_MODEL), lambda i, j: (j, 0)),
        ],
        out_specs=[
            pl.BlockSpec((ROW_TILE, D_MODEL), lambda i, j: (i, 0)),
            pl.BlockSpec((ROW_TILE, D_MODEL), lambda i, j: (i, 0)),
        ],
        out_shape=[jax.ShapeDtypeStruct((n, D_MODEL), F32), jax.ShapeDtypeStruct((n, D_MODEL), BF16)],
        scratch_shapes=[pltpu.VMEM((ROW_TILE, D_MODEL), BF16), pltpu.VMEM((ROW_TILE, D_MODEL), F32)],
        compiler_params=_params("parallel", "arbitrary"),
        name="ffn",
    )(x, gain, gain_next, w_in, w_in, w_out)


def _split_terms(rep, hi_lanes):
    hi = rep.astype(BF16).astype(F32)
    return jnp.where(hi_lanes, hi, rep - hi).astype(BF16)


def _proj_a_kernel(xn_ref, w_ref, e_ref, gq_ref, gk_ref, q_ref, k_ref, v_ref, iq_ref, ik_ref, iw_ref):
    p = _dot(xn_ref[...], w_ref[...])
    q = _head_norm(p[:, :512], e_ref, gq_ref[...], HEAD_DIM) * HEAD_DIM ** -0.5
    q_ref[...] = q.astype(BF16)
    k = p[:, 512:640]
    k_ms = jnp.sum(k * k, axis=-1, keepdims=True) * (1.0 / HEAD_DIM)
    k_ref[...] = (k * lax.rsqrt(k_ms + RMS_EPS) * gk_ref[...]).astype(BF16)
    v_ref[...] = p[:, 640:768].astype(BF16)
    lane = lax.broadcasted_iota(jnp.int32, (1, 4 * LANE), 1)
    copy = lax.shift_right_logical(lane, int(math.log2(IDX_DIM))) & 3
    iq_ref[...] = _split_terms(p[:, 768:1280], copy < 2)
    ik_ref[...] = _split_terms(p[:, 1280:1408], (copy[:, :LANE] & 1) == 0)
    iw_ref[...] = p[:, 1408:1536]


def _proj_a(xn, w, e_slots, gq, gk):
    n = xn.shape[0]
    rows = lambda width: pl.BlockSpec((ROW_TILE, width), lambda i: (i, 0))
    return pl.pallas_call(
        _proj_a_kernel,
        grid=(n // ROW_TILE,),
        in_specs=[rows(D_MODEL), _full(w.shape), _full(e_slots.shape), _full(gq.shape), _full(gk.shape)],
        out_specs=[rows(512), rows(128), rows(128), rows(512), rows(128), rows(128)],
        out_shape=[jax.ShapeDtypeStruct((n, 512), BF16), jax.ShapeDtypeStruct((n, 128), BF16),
                   jax.ShapeDtypeStruct((n, 128), BF16), jax.ShapeDtypeStruct((n, 512), BF16),
                   jax.ShapeDtypeStruct((n, 128), BF16), jax.ShapeDtypeStruct((n, 128), F32)],
        compiler_params=_params("parallel"),
        name="proj_a",
    )(xn, w, e_slots, gq, gk)


def _proj_b_kernel(xn_ref, w_ref, e_ref, gq_ref, gk_ref, q_ref, k_ref, v_ref, km_ref):
    p = _dot(xn_ref[...], w_ref[...])
    q_ref[...] = _head_norm(p[:, :256], e_ref, gq_ref[...], HEAD_DIM)
    k = _head_norm(p[:, 256:512], e_ref, gk_ref[...], HEAD_DIM)
    k_ref[...] = k.astype(BF16)
    v_ref[...] = p[:, 512:768].astype(BF16)
    km = jnp.mean(k.reshape(ROW_TILE // MOBA_BLOCK, MOBA_BLOCK, BRANCH_WIDTH), axis=1)
    for j in range(ROW_TILE // MOBA_BLOCK):
        km_ref[j] = km[j:j + 1]


def _proj_b(xn, w, e64, gq, gk):
    n = xn.shape[0]
    rows = lambda width: pl.BlockSpec((ROW_TILE, width), lambda i: (i, 0))
    per_tile = ROW_TILE // MOBA_BLOCK
    return pl.pallas_call(
        _proj_b_kernel,
        grid=(n // ROW_TILE,),
        in_specs=[rows(D_MODEL), _full(w.shape), _full(e64.shape), _full(gq.shape), _full(gk.shape)],
        out_specs=[rows(256), rows(256), rows(256),
                   pl.BlockSpec((per_tile, 1, BRANCH_WIDTH), lambda i: (i, 0, 0))],
        out_shape=[jax.ShapeDtypeStruct((n, 256), F32), jax.ShapeDtypeStruct((n, 256), BF16),
                   jax.ShapeDtypeStruct((n, 256), BF16),
                   jax.ShapeDtypeStruct((n // MOBA_BLOCK, 1, BRANCH_WIDTH), F32)],
        compiler_params=_params("parallel"),
        name="proj_b",
    )(xn, w, e64, gq, gk)


def _proj_c_kernel(xn_ref, w_ref, e_ref, nq_ref, nkv_ref, wuq_ref, wuqs_ref, wuk_ref, wuv_ref,
                   gq_ref, gqs_ref, gk_ref, gks_ref, cos_ref, sin_ref, q_ref, k_ref, v_ref):
    p = _dot(xn_ref[...], w_ref[...])
    cos = cos_ref[...]
    sin = sin_ref[...]
    xq = _rms(p[:, :384], nq_ref[...]).astype(BF16)
    qa = _dot(xq, wuq_ref[...])
    qs = _dot(xq, wuqs_ref[...])
    rq = lax.rsqrt(_head_sumsq(qa, e_ref) * (1.0 / MLA_QK) + RMS_EPS)
    q_ref[...] = (rq * (qa * gq_ref[...] * cos + qs * gqs_ref[...] * sin)).astype(BF16)
    xkv = _rms(p[:, 384:640], nkv_ref[...]).astype(BF16)
    ka = _dot(xkv, wuk_ref[...]) + p[:, 640:1152]
    ks = p[:, 1152:1664]
    rk = lax.rsqrt(_head_sumsq(ka, e_ref) * (1.0 / MLA_QK) + RMS_EPS)
    k_ref[...] = (rk * (ka * gk_ref[...] * cos + ks * gks_ref[...] * sin)).astype(BF16)
    v_ref[...] = _dot(xkv, wuv_ref[...]).astype(BF16)


def _proj_c(xn, w, e96, nq, nkv, wuq, wuqs, wuk, wuv, gq, gqs, gk, gks, cos, sin, seq):
    n = xn.shape[0]
    rows = lambda width: pl.BlockSpec((ROW_TILE, width), lambda i: (i, 0))
    pos = pl.BlockSpec((ROW_TILE, N_HEADS * MLA_SLOT), lambda i: (i % (seq // ROW_TILE), 0))
    consts = (w, e96, nq, nkv, wuq, wuqs, wuk, wuv, gq, gqs, gk, gks)
    return pl.pallas_call(
        _proj_c_kernel,
        grid=(n // ROW_TILE,),
        in_specs=[rows(D_MODEL)] + [_full(c.shape) for c in consts] + [pos, pos],
        out_specs=[rows(N_HEADS * MLA_SLOT), rows(N_HEADS * MLA_SLOT), rows(256)],
        out_shape=[jax.ShapeDtypeStruct((n, N_HEADS * MLA_SLOT), BF16),
                   jax.ShapeDtypeStruct((n, N_HEADS * MLA_SLOT), BF16),
                   jax.ShapeDtypeStruct((n, 256), BF16)],
        compiler_params=_params("parallel"),
        name="proj_c",
    )(xn, *consts, cos, sin)


def _proj_d_kernel(xn_ref, w_ref, e_ref, gq_ref, gk_ref, q_ref, k_ref, v_ref):
    xn = xn_ref[...]
    for c in range(DIL_GROUPS):
        lo, hi = c * 256, (c + 1) * 256
        pq = _dot(xn, w_ref[:, lo:hi])
        q_ref[:, lo:hi] = (_head_norm(pq, e_ref, gq_ref[...], HEAD_DIM) * HEAD_DIM ** -0.5).astype(BF16)
        pk = _dot(xn, w_ref[:, 768 + lo:768 + hi])
        k_ref[:, lo:hi] = _head_norm(pk, e_ref, gk_ref[...], HEAD_DIM).astype(BF16)
        v_ref[:, lo:hi] = _dot(xn, w_ref[:, 1536 + lo:1536 + hi]).astype(BF16)


def _proj_d(xn, w, e64, gq, gk):
    n = xn.shape[0]
    rows = lambda width: pl.BlockSpec((ROW_TILE, width), lambda i: (i, 0))
    return pl.pallas_call(
        _proj_d_kernel,
        grid=(n // ROW_TILE,),
        in_specs=[rows(D_MODEL), _full(w.shape), _full(e64.shape), _full(gq.shape), _full(gk.shape)],
        out_specs=[rows(768), rows(768), rows(768)],
        out_shape=[jax.ShapeDtypeStruct((n, 768), BF16)] * 3,
        compiler_params=_params("parallel"),
        name="proj_d",
    )(xn, w, e64, gq, gk)


def _for_causal_width(qi, n_qtiles, tile_rows, body):
    n_cls = min(WIDTH_CLASSES, n_qtiles)
    per = n_qtiles // n_cls
    for c in range(n_cls):
        pl.when((qi >= c * per) & (qi < (c + 1) * per))(functools.partial(body, (c + 1) * per * tile_rows))


def _bias_rows(toep_ref, h, qblk, nchunk):
    return jnp.concatenate([toep_ref[h, jnp.maximum(qblk - c + 1, 0)] for c in range(nchunk)], axis=1)


def _bias_tile(toep_ref, h, qi, sub_blocks, nchunk):
    return jnp.concatenate([_bias_rows(toep_ref, h, qi * sub_blocks + j, nchunk) for j in range(sub_blocks)],
                           axis=0)


def _softmax_pv(logits, v):
    m = jnp.max(logits, axis=1, keepdims=True)
    p = jnp.exp(logits - m)
    l = jnp.sum(p, axis=1, keepdims=True)
    return _dot(p.astype(BF16), v) / l


def _dsa_body(width, qi, q_ref, k_ref, v_ref, iq_ref, ik_ref, iw_ref, toep_ref, tri_ref, o_ref, sel_ref,
              n_sel):
    rows = DSA_ROWS
    nchunk = width // LANE
    ik = ik_ref[0, :width, :]
    iw = iw_ref[0][:, :N_HEADS] * (N_HEADS ** -0.5 * IDX_DIM ** -0.5)

    score = jnp.zeros((rows, width), F32)
    for h in range(N_HEADS):
        r = _dot_t(iq_ref[0, :, h * LANE:(h + 1) * LANE], ik)
        score = score + jnp.maximum(r, 0.0) * iw[:, h:h + 1]

    t = qi * rows + lax.broadcasted_iota(jnp.int32, (rows, width), 0)
    s = lax.broadcasted_iota(jnp.int32, (rows, width), 1)

    bits = pltpu.bitcast(score, jnp.int32)
    key = jnp.where(bits < 0, bits ^ jnp.int32(0x7FFFFFFF), bits)
    key = jnp.where(s <= t, key, jnp.int32(INT_MIN))

    half_min = -2 ** 15
    upper = lax.shift_right_arithmetic(key, 16).astype(jnp.int16)
    lower = ((key & jnp.int32(0xFFFF)) + half_min).astype(jnp.int16)

    def count(hit):
        ones = jnp.where(hit, jnp.int16(1), jnp.int16(0))
        acc = ones[:, :LANE]
        for c in range(1, nchunk):
            acc = acc + ones[:, c * LANE:(c + 1) * LANE]
        return jnp.sum(acc.astype(F32), axis=1, keepdims=True)

    def search(half, need):
        def step(i, ans):
            cand = ans | lax.shift_left(jnp.int32(1), 15 - i)
            cnt = count(half >= (cand + half_min).astype(jnp.int16))
            return jnp.where(cnt >= need, cand, ans)
        return lax.fori_loop(0, 16, step, jnp.zeros((rows, 1), jnp.int32))

    thr_upper = search(upper, n_sel) + half_min
    thr_upper16 = thr_upper.astype(jnp.int16)
    n_over = count(upper > thr_upper16)
    candidates = jnp.where(upper == thr_upper16, lower, jnp.int16(half_min))
    thr = thr_upper * 65536 + search(candidates, n_sel - n_over)

    above = key > thr
    tied = key == thr
    n_above = jnp.sum(jnp.where(above, 1.0, 0.0), axis=1, keepdims=True)
    n_tied = jnp.sum(jnp.where(tied, 1.0, 0.0), axis=1, keepdims=True)
    room = n_sel - n_above
    sel_ref[:, :width] = jnp.where(key >= jnp.maximum(thr, jnp.int32(INT_MIN + 1)), 1.0, 0.0)

    overflow = (n_tied > room) & (thr != jnp.int32(INT_MIN))

    @pl.when(jnp.max(jnp.where(overflow, 1.0, 0.0)) > 0.0)
    def _():
        before = jnp.zeros((rows, 1), F32)
        for c in range(nchunk):
            tc = jnp.where(tied[:, c * LANE:(c + 1) * LANE], 1.0, 0.0)
            rank = before + _dot(tc.astype(BF16), tri_ref[...])
            keep = above[:, c * LANE:(c + 1) * LANE] | ((tc > 0.0) & (rank < room))
            sel_ref[:, c * LANE:(c + 1) * LANE] = jnp.where(keep, 1.0, 0.0)
            before = before + jnp.sum(tc, axis=1, keepdims=True)

    unselected = (sel_ref[:, :width] - 1.0) * -MASKED
    k = k_ref[0, :width, :]
    v = v_ref[0, :width, :]
    first = lax.broadcasted_iota(jnp.int32, (rows, LANE), 1) < HEAD_DIM
    for pair in range(N_HEADS // 2):
        outs = []
        for h in (2 * pair, 2 * pair + 1):
            logits = (_dot_t(q_ref[0, :, h * LANE:(h + 1) * LANE], k)
                      + _bias_tile(toep_ref, h, qi, rows // LANE, nchunk) + unselected)
            outs.append(_softmax_pv(logits, v))
        o_ref[0, :, pair * LANE:(pair + 1) * LANE] = jnp.where(first, outs[0], outs[1]).astype(o_ref.dtype)


def _dsa_kernel(*refs, seq, n_sel):
    qi = pl.program_id(1)
    _for_causal_width(qi, seq // DSA_ROWS, DSA_ROWS,
                      functools.partial(_dsa_body, qi=qi, n_sel=n_sel, **_named(refs)))


def _named(refs):
    names = ("q_ref", "k_ref", "v_ref", "iq_ref", "ik_ref", "iw_ref", "toep_ref", "tri_ref", "o_ref", "sel_ref")
    return dict(zip(names, refs, strict=True))


def _dsa(q, k, v, iq, ik, iw, toep, tri):
    b, seq, _ = q.shape
    n_sel = min(DSA_TOPK, seq // 4)
    qblock = lambda width: pl.BlockSpec((1, DSA_ROWS, width), lambda bi, qi: (bi, qi, 0))
    whole = lambda width: pl.BlockSpec((1, seq, width), lambda bi, qi: (bi, 0, 0))
    return pl.pallas_call(
        functools.partial(_dsa_kernel, seq=seq, n_sel=n_sel),
        grid=(b, seq // DSA_ROWS),
        in_specs=[qblock(512), whole(128), whole(128), qblock(512), whole(128), qblock(128),
                  _full(toep.shape), _full(tri.shape)],
        out_specs=qblock(256),
        out_shape=jax.ShapeDtypeStruct((b, seq, 256), BF16),
        scratch_shapes=[pltpu.VMEM((DSA_ROWS, seq), F32)],
        compiler_params=_params("parallel", "parallel"),
        name="dsa",
    )(q, k, v, iq, ik, iw, toep, tri)


def _moba_body(width, qi, q_ref, k_ref, v_ref, km_ref, toep_ref, spread_ref, o_ref, nblk):
    nchunk = width // LANE
    n_sel = min(MOBA_TOPK, nblk - 1)
    own = lax.shift_right_logical(qi * LANE, int(math.log2(MOBA_BLOCK)))
    q = q_ref[0]

    km = km_ref[0]
    head_of_col = lax.shift_right_logical(lax.broadcasted_iota(jnp.int32, (nblk, BRANCH_WIDTH), 1),
                                          int(math.log2(HEAD_DIM)))
    per_head = [jnp.where(head_of_col == h, km, 0.0) for h in range(N_HEADS)]
    pad = jnp.zeros((LANE - N_HEADS * nblk, BRANCH_WIDTH), F32)
    gate = _dot_t(q, jnp.concatenate(per_head + [pad], axis=0), precision=lax.Precision.HIGHEST)

    lane = lax.broadcasted_iota(jnp.int32, (LANE, LANE), 1)
    blk = lane & (nblk - 1)
    past = (blk < own) & (lane < N_HEADS * nblk)
    gate = jnp.where(past, gate, NEG_INF)
    rank = jnp.zeros((LANE, LANE), F32)
    for d in range(1, nblk):
        before = pltpu.roll(gate, d, axis=1)
        rank = rank + jnp.where(blk >= d, jnp.where(before >= gate, 1.0, 0.0), 0.0)
        after = pltpu.roll(gate, LANE - d, axis=1)
        rank = rank + jnp.where(blk < nblk - d, jnp.where(after > gate, 1.0, 0.0), 0.0)
    picked = jnp.where(past, jnp.where(rank < n_sel, 1.0, 0.0), jnp.where(blk == own, 1.0, 0.0)).astype(BF16)

    first = lane < HEAD_DIM
    for pair in range(N_HEADS // 2):
        ps = slice(pair * LANE, (pair + 1) * LANE)
        qp = q[:, ps] * HEAD_DIM ** -0.5
        outs = []
        for mine in (first, ~first):
            h = 2 * pair + len(outs)
            unpicked = (_dot(picked, spread_ref[h, :, :width]) - 1.0) * -MASKED
            logits = (_dot_t(jnp.where(mine, qp, 0.0).astype(BF16), k_ref[0, :width, ps])
                      + _bias_rows(toep_ref, h, qi, nchunk) + unpicked)
            outs.append(_softmax_pv(logits, v_ref[0, :width, ps]))
        o_ref[0, :, ps] = jnp.where(first, outs[0], outs[1]).astype(o_ref.dtype)


def _moba_kernel(q_ref, k_ref, v_ref, km_ref, toep_ref, spread_ref, o_ref, *, seq):
    qi = pl.program_id(1)
    body = functools.partial(_moba_body, qi=qi, q_ref=q_ref, k_ref=k_ref, v_ref=v_ref, km_ref=km_ref,
                             toep_ref=toep_ref, spread_ref=spread_ref, o_ref=o_ref, nblk=seq // MOBA_BLOCK)
    _for_causal_width(qi, seq // LANE, LANE, body)


def _moba_spread(seq):
    nblk = seq // MOBA_BLOCK
    row = np.arange(LANE)[None, :, None]
    h = np.arange(N_HEADS)[:, None, None]
    blk_of_key = (np.arange(seq) // MOBA_BLOCK)[None, None, :]
    return jnp.asarray(row == h * nblk + blk_of_key, BF16)


def _moba(q, k, v, k_mean, toep):
    b, seq, _ = q.shape
    nblk = seq // MOBA_BLOCK
    if nblk & (nblk - 1) or N_HEADS * nblk > LANE:
        raise ValueError("MoBA lane layout needs a power-of-two block count with heads*blocks <= 128")
    spread = _moba_spread(seq)
    qblock = pl.BlockSpec((1, LANE, 256), lambda bi, qi: (bi, qi, 0))
    whole = pl.BlockSpec((1, seq, 256), lambda bi, qi: (bi, 0, 0))
    means = pl.BlockSpec((1, nblk, 256), lambda bi, qi: (bi, 0, 0))
    return pl.pallas_call(
        functools.partial(_moba_kernel, seq=seq),
        grid=(b, seq // LANE),
        in_specs=[qblock, whole, whole, means, _full(toep.shape), _full(spread.shape)],
        out_specs=qblock,
        out_shape=jax.ShapeDtypeStruct((b, seq, 256), BF16),
        compiler_params=_params("parallel", "parallel"),
        name="moba",
    )(q, k, v, k_mean, toep, spread)


def _mla_body(width, qi, q_ref, k_ref, v_ref, causal_ref, o_ref):
    nchunk = width // LANE
    causal = jnp.concatenate([causal_ref[0, jnp.clip(qi - c + 1, 0, 2)] for c in range(nchunk)], axis=1)
    for h in range(N_HEADS):
        qs = slice(h * MLA_SLOT, (h + 1) * MLA_SLOT)
        vs = slice(h * MLA_V, (h + 1) * MLA_V)
        logits = _dot_t(q_ref[0, :, qs], k_ref[0, :width, qs]) * MLA_QK ** -0.5 + causal
        o_ref[0, :, vs] = _softmax_pv(logits, v_ref[0, :width, vs]).astype(o_ref.dtype)


def _mla_kernel(q_ref, k_ref, v_ref, causal_ref, o_ref, *, seq):
    qi = pl.program_id(1)
    body = functools.partial(_mla_body, qi=qi, q_ref=q_ref, k_ref=k_ref, v_ref=v_ref, causal_ref=causal_ref,
                             o_ref=o_ref)
    _for_causal_width(qi, seq // LANE, LANE, body)


def _mla(q, k, v):
    b, seq, _ = q.shape
    tri = np.where(np.arange(LANE)[:, None] >= np.arange(LANE)[None, :], 0.0, NEG_INF)
    causal = jnp.asarray(np.stack([np.full((LANE, LANE), NEG_INF), tri, np.zeros((LANE, LANE))])[None], F32)
    return pl.pallas_call(
        functools.partial(_mla_kernel, seq=seq),
        grid=(b, seq // LANE),
        in_specs=[pl.BlockSpec((1, LANE, N_HEADS * MLA_SLOT), lambda bi, qi: (bi, qi, 0)),
                  pl.BlockSpec((1, seq, N_HEADS * MLA_SLOT), lambda bi, qi: (bi, 0, 0)),
                  pl.BlockSpec((1, seq, 256), lambda bi, qi: (bi, 0, 0)),
                  _full(causal.shape)],
        out_specs=pl.BlockSpec((1, LANE, 256), lambda bi, qi: (bi, qi, 0)),
        out_shape=jax.ShapeDtypeStruct((b, seq, 256), BF16),
        compiler_params=_params("parallel", "parallel"),
        name="mla",
    )(q, k, v, causal)


def _dil_group(dil, seq, toep_ref, g, qf_ref, kf_ref, vf_ref, m_ref, l_ref, acc_ref):
    n_band = seq // (dil * LANE)

    def unit(u, carry):
        r = lax.div(u, n_band)
        n = lax.rem(u, n_band)
        if dil == 1:
            cur = pl.ds(pl.multiple_of(n * LANE, LANE), LANE)
            prev = pl.ds(pl.multiple_of(jnp.maximum(n - 1, 0) * LANE, LANE), LANE)
        else:
            cur = pl.ds(n * (LANE * dil) + r, LANE, stride=dil)
            prev = pl.ds(jnp.maximum(n - 1, 0) * (LANE * dil) + r, LANE, stride=dil)
        prev_tile = jnp.where(n > 0, 2, 0)
        first = lax.broadcasted_iota(jnp.int32, (LANE, LANE), 1) < HEAD_DIM
        for pair in range(N_HEADS // 2):
            qp = qf_ref[pair, cur, :]
            kk = jnp.concatenate([kf_ref[pair, prev, :], kf_ref[pair, cur, :]], axis=0).astype(BF16)
            vv = jnp.concatenate([vf_ref[pair, prev, :], vf_ref[pair, cur, :]], axis=0).astype(BF16)
            stats = []
            for mine in (first, ~first):
                h = 2 * pair + len(stats)
                bias = jnp.concatenate([toep_ref[g, h, prev_tile], toep_ref[g, h, 1]], axis=1)
                logits = _dot_t(jnp.where(mine, qp, 0.0).astype(BF16), kk) + bias
                m = jnp.max(logits, axis=1, keepdims=True)
                p = jnp.exp(logits - m)
                stats.append((m, jnp.sum(p, axis=1, keepdims=True), _dot(p.astype(BF16), vv)))
            m_new, l_new, pv = (jnp.where(first, a, b) for a, b in zip(*stats))
            m_old = m_ref[pair, cur, :]
            m_tot = jnp.maximum(m_old, m_new)
            a_old = jnp.exp(m_old - m_tot)
            a_new = jnp.exp(m_new - m_tot)
            m_ref[pair, cur, :] = m_tot
            l_ref[pair, cur, :] = a_old * l_ref[pair, cur, :] + a_new * l_new
            acc_ref[pair, cur, :] = a_old * acc_ref[pair, cur, :] + a_new * pv
        return carry

    lax.fori_loop(0, dil * n_band, unit, 0, unroll=2)


def _get_rows(ref, rows):
    return jnp.concatenate([ref[0, rows, :], ref[1, rows, :]], axis=1)


def _set_rows(ref, rows, val):
    ref[0, rows, :] = val[:, :LANE]
    ref[1, rows, :] = val[:, LANE:]


def _dil_kernel(q_ref, k_ref, v_ref, toep_ref, o_ref, qf_ref, kf_ref, vf_ref, m_ref, l_ref, acc_ref, *, seq):
    g = pl.program_id(1)
    everything = pl.ds(0, seq)
    _set_rows(qf_ref, everything, q_ref[0].astype(F32))
    _set_rows(kf_ref, everything, k_ref[0].astype(F32))
    _set_rows(vf_ref, everything, v_ref[0].astype(F32))

    @pl.when(g == 0)
    def _():
        m_ref[...] = jnp.full(m_ref.shape, NEG_INF, F32)
        l_ref[...] = jnp.zeros_like(l_ref)
        acc_ref[...] = jnp.zeros_like(acc_ref)

    for gi, (_, dil) in enumerate(DIL_PATTERNS):
        pl.when(g == gi)(functools.partial(_dil_group, dil, seq, toep_ref, gi, qf_ref, kf_ref, vf_ref,
                                           m_ref, l_ref, acc_ref))

    @pl.when(g == DIL_GROUPS - 1)
    def _():
        o_ref[0] = (_get_rows(acc_ref, everything) / _get_rows(l_ref, everything)).astype(o_ref.dtype)


def _dil(q, k, v, toep):
    b, seq, _ = q.shape
    group = pl.BlockSpec((1, seq, 256), lambda bi, g: (bi, 0, g))
    state = pltpu.VMEM((2, seq, LANE), F32)
    return pl.pallas_call(
        functools.partial(_dil_kernel, seq=seq),
        grid=(b, DIL_GROUPS),
        in_specs=[group, group, group, _full(toep.shape)],
        out_specs=pl.BlockSpec((1, seq, 256), lambda bi, g: (bi, 0, 0)),
        out_shape=jax.ShapeDtypeStruct((b, seq, 256), BF16),
        scratch_shapes=[state] * 6,
        compiler_params=_params("parallel", "arbitrary"),
        name="dil",
    )(q, k, v, toep)


def _merge_kernel(x_ref, xn_ref, oa_ref, ob_ref, oc_ref, od_ref, wg_ref, bg_ref, wb_ref, wo_ref, xo_ref):
    branches = (oa_ref, ob_ref, oc_ref, od_ref)
    xn = xn_ref[...]
    total = jnp.zeros(x_ref.shape, F32)
    for n in range(N_BRANCH):
        cols = slice(n * D_MODEL, (n + 1) * D_MODEL)
        gate = _sigmoid(_dot(xn, wg_ref[:, cols]) + bg_ref[:, cols])
        total = total + gate * _dot(branches[n][...], wb_ref[n])
    xo_ref[...] = x_ref[...] + _dot(total.astype(BF16), wo_ref[...])


def _merge(x, xn, outs, wg, bg, wb, wo):
    n = x.shape[0]
    rows = lambda width: pl.BlockSpec((ROW_TILE, width), lambda i: (i, 0))
    return pl.pallas_call(
        _merge_kernel,
        grid=(n // ROW_TILE,),
        in_specs=[rows(D_MODEL), rows(D_MODEL)] + [rows(256)] * N_BRANCH
                 + [_full(wg.shape), _full(bg.shape), _full(wb.shape), _full(wo.shape)],
        out_specs=rows(D_MODEL),
        out_shape=jax.ShapeDtypeStruct((n, D_MODEL), F32),
        compiler_params=_params("parallel"),
        name="merge",
    )(x, xn, *outs, wg, bg, wb, wo)


def _bucket_of_distance():
    d = np.arange(REL_MAX_DIST + 1)
    exact = REL_BUCKETS // 2
    nf = np.maximum(d, 1).astype(np.float32)
    log_b = exact + (np.log(nf / exact) / math.log(REL_MAX_DIST / exact) * (REL_BUCKETS - exact)).astype(np.int32)
    return np.where(d < exact, d, np.minimum(log_b, REL_BUCKETS - 1))


def _toeplitz_tiles(table, n_off, dil, max_steps=None):
    span = 2 * LANE
    m = np.arange(span)
    off = np.arange(n_off)[:, None]
    steps = off * LANE + LANE - 1 - m[None, :]
    live = (steps >= 0) if max_steps is None else (steps >= 0) & (steps <= max_steps)
    v = table[:, _bucket_of_distance()[np.clip(steps * dil, 0, REL_MAX_DIST)]]
    v = jnp.where(live[None], v, NEG_INF)
    flat = jnp.tile(v, (1, 1, LANE))[..., :LANE * (span - 1)]
    tiles = flat.reshape(table.shape[0], n_off, LANE, span - 1)[..., LANE - 1:]
    return jnp.concatenate([jnp.full((table.shape[0], 1, LANE, LANE), NEG_INF, F32), tiles], axis=1)


def _same_head(width, head):
    idx = np.arange(width) // head
    return jnp.asarray(idx[:, None] == idx[None, :], BF16)


def _rope_tables(seq):
    half = MLA_ROPE // 2
    freqs = ROPE_THETA ** (-np.arange(half, dtype=np.float64) / half)
    ang = np.arange(seq, dtype=np.float64)[:, None] * freqs[None, :]
    pad = np.zeros((seq, MLA_SLOT - MLA_QK))
    cos_h = np.concatenate([np.ones((seq, MLA_NOPE)), np.cos(ang), np.cos(ang), pad], axis=1)
    sin_h = np.concatenate([np.zeros((seq, MLA_NOPE)), np.sin(ang), np.sin(ang), pad], axis=1)
    return (jnp.asarray(np.tile(cos_h, (1, N_HEADS)), F32), jnp.asarray(np.tile(sin_h, (1, N_HEADS)), F32))


def _rot_half_cols():
    half = MLA_ROPE // 2
    src = np.arange(MLA_SLOT)
    sign = np.zeros(MLA_SLOT, np.float32)
    src[MLA_NOPE:MLA_NOPE + half] = np.arange(MLA_NOPE + half, MLA_QK)
    sign[MLA_NOPE:MLA_NOPE + half] = -1.0
    src[MLA_NOPE + half:MLA_QK] = np.arange(MLA_NOPE, MLA_NOPE + half)
    sign[MLA_NOPE + half:MLA_QK] = 1.0
    return src, sign


def _mixer(x, xn, seq, w_in, b_gate, qk_a, qk_b, qk_c, qk_d, mla_nq, w_uq, mla_nkv, w_ukv,
           w_branch, w_out, toeps, consts):
    n = x.shape[0]
    b = n // seq
    e64, e96, tri, cos, sin = consts
    toep_a, toep_b, toep_d = toeps
    row = lambda a: a.reshape(1, -1)
    tile4 = lambda g: jnp.tile(g, N_HEADS).reshape(1, -1)
    r3 = lambda a: a.reshape(b, seq, a.shape[-1])

    off_iq = OFF_A + 6 * HEAD_DIM
    off_ik = off_iq + N_HEADS * IDX_DIM
    iq_rep = jnp.concatenate([jnp.tile(w_in[:, off_iq + h * IDX_DIM:off_iq + (h + 1) * IDX_DIM], (1, 4))
                              for h in range(N_HEADS)], axis=1)
    ik_rep = jnp.tile(w_in[:, off_ik:off_ik + IDX_DIM], (1, 4))
    iw_pad = jnp.pad(w_in[:, off_ik + IDX_DIM:OFF_A + W_A], ((0, 0), (0, LANE - N_HEADS)))
    half_pad = ((0, 0), (0, LANE - HEAD_DIM))
    q_slots = jnp.concatenate([jnp.pad(w_in[:, OFF_A + h * HEAD_DIM:OFF_A + (h + 1) * HEAD_DIM], half_pad)
                               for h in range(N_HEADS)], axis=1)
    off_k = OFF_A + N_HEADS * HEAD_DIM
    k_slot = jnp.pad(w_in[:, off_k:off_k + HEAD_DIM], half_pad)
    v_twice = jnp.tile(w_in[:, off_k + HEAD_DIM:off_iq], (1, 2))
    wa = jnp.concatenate([q_slots, k_slot, v_twice, iq_rep, ik_rep, iw_pad], axis=1).astype(BF16)
    gq_slots = jnp.tile(jnp.pad(qk_a[0], (0, LANE - HEAD_DIM)), N_HEADS).reshape(1, -1)
    gk_slot = jnp.pad(qk_a[1], (0, LANE - HEAD_DIM)).reshape(1, -1)
    qa, ka, va, iq, ik, iw = _proj_a(xn, wa, e96, gq_slots, gk_slot)
    out_a = _dsa(r3(qa), r3(ka), r3(va), r3(iq), r3(ik), r3(iw), toep_a, tri).reshape(n, 256)

    wb_in = w_in[:, OFF_B:OFF_B + W_B].astype(BF16)
    qb, kb, vb, kmean = _proj_b(xn, wb_in, e64, tile4(qk_b[0]), tile4(qk_b[1]))
    out_b = _moba(r3(qb), r3(kb), r3(vb), kmean.reshape(b, seq // MOBA_BLOCK, BRANCH_WIDTH),
                  toep_b).reshape(n, 256)

    src, sign = _rot_half_cols()
    src4 = np.concatenate([h * MLA_SLOT + src for h in range(N_HEADS)])
    sign4 = jnp.asarray(np.tile(sign, N_HEADS))
    slot_pad = MLA_SLOT - MLA_QK
    w_kr = w_in[:, OFF_C + MLA_Q_LORA + MLA_KV_LORA:OFF_C + W_C]
    kr_slots = jnp.pad(w_kr, ((0, 0), (MLA_NOPE, slot_pad)))
    kr_slots = jnp.tile(kr_slots, (1, N_HEADS))
    wc = jnp.concatenate([w_in[:, OFF_C:OFF_C + MLA_Q_LORA + MLA_KV_LORA], kr_slots,
                          kr_slots[:, src4] * sign4], axis=1).astype(BF16)
    slots = lambda w: jnp.pad(w.reshape(w.shape[0], N_HEADS, -1),
                              ((0, 0), (0, 0), (0, MLA_SLOT - w.shape[1] // N_HEADS))).reshape(w.shape[0], -1)
    wuq = slots(w_uq)
    ukv = w_ukv.reshape(MLA_KV_LORA, N_HEADS, MLA_NOPE + MLA_V)
    wuk = slots(ukv[:, :, :MLA_NOPE].reshape(MLA_KV_LORA, -1))
    wuv = ukv[:, :, MLA_NOPE:].reshape(MLA_KV_LORA, -1)
    gq = jnp.tile(jnp.pad(qk_c[0], (0, slot_pad)), N_HEADS)
    gk = jnp.tile(jnp.pad(qk_c[1], (0, slot_pad)), N_HEADS)
    qc, kc, vc = _proj_c(xn, wc, e96, row(mla_nq), row(mla_nkv), wuq.astype(BF16),
                         (wuq[:, src4] * sign4).astype(BF16), wuk.astype(BF16), wuv.astype(BF16),
                         row(gq), row(gq[src4]), row(gk), row(gk[src4]), cos, sin, seq)
    out_c = _mla(r3(qc), r3(kc), r3(vc)).reshape(n, 256)

    wd = w_in[:, OFF_D:OFF_D + W_D].astype(BF16)
    qd, kd, vd = _proj_d(xn, wd, e64, tile4(qk_d[0]), tile4(qk_d[1]))
    out_d = _dil(r3(qd), r3(kd), r3(vd), toep_d).reshape(n, 256)

    wg = w_in[:, OFF_G:OFF_G + W_G].astype(BF16)
    return _merge(x, xn, (out_a, out_b, out_c, out_d), wg, b_gate.reshape(1, -1),
                  w_branch.astype(BF16), w_out.astype(BF16))


def kernel(x, norm_gain, w_in, b_gate, qk_gain_a, qk_gain_b, qk_gain_c, qk_gain_d, mla_norm_q,
           w_mla_uq, mla_norm_kv, w_mla_ukv, w_branch, w_out, rel_bias, w_ffn_in, w_ffn_out):
    b, seq, d = x.shape
    depth = norm_gain.shape[0]
    toep_a = _toeplitz_tiles(rel_bias[0:4], seq // LANE, 1)
    toep_b = _toeplitz_tiles(rel_bias[4:8], seq // LANE, 1)
    toep_d = jnp.stack([_toeplitz_tiles(rel_bias[8 + 4 * g:12 + 4 * g], 2, dil, max_steps=window // dil)
                        for g, (window, dil) in enumerate(DIL_PATTERNS)])
    tri = jnp.asarray(np.arange(LANE)[:, None] < np.arange(LANE)[None, :], BF16)
    consts = (_same_head(256, HEAD_DIM), _same_head(N_HEADS * MLA_SLOT, MLA_SLOT), tri) + _rope_tables(seq)

    x = x.reshape(b * seq, d)
    for l in range(depth):
        g = norm_gain[l]
        x, xn = _ffn(x, g[0:1], g[1:2], w_ffn_in[l, 0].astype(BF16), w_ffn_out[l, 0].astype(BF16))
        x = _mixer(x, xn, seq, w_in[l], b_gate[l], qk_gain_a[l], qk_gain_b[l], qk_gain_c[l],
                   qk_gain_d[l], mla_norm_q[l], w_mla_uq[l], mla_norm_kv[l], w_mla_ukv[l],
                   w_branch[l], w_out[l], (toep_a, toep_b, toep_d), consts)
        x, _ = _ffn(x, g[2:3], g[2:3], w_ffn_in[l, 1].astype(BF16), w_ffn_out[l, 1].astype(BF16))
    return x.reshape(b, seq, d)
```

```python
import functools
import math

import numpy as np
import jax
import jax.numpy as jnp
from jax import lax
from jax.experimental import pallas as pl
from jax.experimental.pallas import tpu as pltpu

D_MODEL = 1024
HEAD_DIM = 64
N_HEADS = 4
BRANCH_WIDTH = N_HEADS * HEAD_DIM
IDX_DIM = 32
DSA_TOPK = 256
MOBA_BLOCK = 256
MOBA_TOPK = 3
MLA_Q_LORA = 384
MLA_KV_LORA = 256
MLA_NOPE = 64
MLA_ROPE = 32
MLA_QK = MLA_NOPE + MLA_ROPE
MLA_V = 64
MLA_SLOT = 128
ROPE_THETA = 10000.0
DIL_PATTERNS = ((128, 1), (512, 4), (2048, 16))
DIL_GROUPS = 3
N_BRANCH = 4
D_FF = 2816
REL_BUCKETS = 32
REL_MAX_DIST = 2048
RMS_EPS = 1e-6

OFF_A = 0
W_A = 4 * HEAD_DIM + 2 * HEAD_DIM + 4 * IDX_DIM + IDX_DIM + 4
OFF_B = OFF_A + W_A
W_B = 3 * BRANCH_WIDTH
OFF_C = OFF_B + W_B
W_C = MLA_Q_LORA + MLA_KV_LORA + MLA_ROPE
OFF_D = OFF_C + W_C
W_D = 3 * DIL_GROUPS * BRANCH_WIDTH
OFF_G = OFF_D + W_D
W_G = N_BRANCH * D_MODEL

LANE = 128
ROW_TILE = 512
FF_CHUNK = 1408
WIDTH_CLASSES = 8
DSA_WIDTH_CLASSES = 4
DSA_ROWS = 256
MASKED = -1e30
VMEM_LIMIT = 56 * 1024 * 1024

F32 = jnp.float32
BF16 = jnp.bfloat16
NEG_INF = float("-inf")
INT_MIN = -2 ** 31

_CONTRACT_LAST = (((1,), (1,)), ((), ()))


def _dot(a, b, precision=None):
    return jnp.dot(a, b, preferred_element_type=F32, precision=precision)


def _dot_t(a, b, precision=None):
    return lax.dot_general(a, b, _CONTRACT_LAST, preferred_element_type=F32, precision=precision)


def _rms(x, gain):
    return x * lax.rsqrt(jnp.mean(x * x, axis=-1, keepdims=True) + RMS_EPS) * gain


def _sigmoid(x):
    return 1.0 / (1.0 + jnp.exp(-x))


def _head_sumsq(y, e_ref):
    y2 = y * y
    hi = y2.astype(BF16)
    lo = (y2 - hi.astype(F32)).astype(BF16)
    e = e_ref[...]
    return _dot(hi, e) + _dot(lo, e)


def _head_norm(y, e_ref, gain, width):
    return y * lax.rsqrt(_head_sumsq(y, e_ref) * (1.0 / width) + RMS_EPS) * gain


def _params(*sem):
    return pltpu.CompilerParams(dimension_semantics=sem, vmem_limit_bytes=VMEM_LIMIT)


def _full(shape):
    return pl.BlockSpec(shape, lambda *_: (0,) * len(shape))


def _ffn_kernel(x_ref, g_ref, gn_ref, wg_ref, wu_ref, wo_ref, xo_ref, xn_ref, xb_ref, acc_ref):
    j = pl.program_id(1)

    @pl.when(j == 0)
    def _():
        xb_ref[...] = _rms(x_ref[...], g_ref[...]).astype(BF16)
        acc_ref[...] = jnp.zeros_like(acc_ref)

    xb = xb_ref[...]
    gate = _dot(xb, wg_ref[...])
    up = _dot(xb, wu_ref[...])
    h = (gate * _sigmoid(gate) * up).astype(BF16)
    acc_ref[...] += _dot(h, wo_ref[...])

    @pl.when(j == pl.num_programs(1) - 1)
    def _():
        xo = x_ref[...] + 0.5 * acc_ref[...]
        xo_ref[...] = xo
        xn_ref[...] = _rms(xo, gn_ref[...]).astype(BF16)


def _ffn(x, gain, gain_next, w_in, w_out):
    n = x.shape[0]
    nchunk = D_FF // FF_CHUNK
    return pl.pallas_call(
        _ffn_kernel,
        grid=(n // ROW_TILE, nchunk),
        in_specs=[
            pl.BlockSpec((ROW_TILE, D_MODEL), lambda i, j: (i, 0)),
            _full((1, D_MODEL)),
            _full((1, D_MODEL)),
            pl.BlockSpec((D_MODEL, FF_CHUNK), lambda i, j: (0, j)),
            pl.BlockSpec((D_MODEL, FF_CHUNK), lambda i, j: (0, j + nchunk)),
            pl.BlockSpec((FF_CHUNK, D_MODEL), lambda i, j: (j, 0)),
        ],
        out_specs=[
            pl.BlockSpec((ROW_TILE, D_MODEL), lambda i, j: (i, 0)),
            pl.BlockSpec((ROW_TILE, D_MODEL), lambda i, j: (i, 0)),
        ],
        out_shape=[jax.ShapeDtypeStruct((n, D_MODEL), F32), jax.ShapeDtypeStruct((n, D_MODEL), BF16)],
        scratch_shapes=[pltpu.VMEM((ROW_TILE, D_MODEL), BF16), pltpu.VMEM((ROW_TILE, D_MODEL), F32)],
        compiler_params=_params("parallel", "arbitrary"),
        name="ffn",
    )(x, gain, gain_next, w_in, w_in, w_out)


def _split_terms(rep, hi_lanes):
    hi = rep.astype(BF16).astype(F32)
    return jnp.where(hi_lanes, hi, rep - hi).astype(BF16)


def _proj_a_kernel(xn_ref, w_ref, e_ref, gq_ref, gk_ref, q_ref, k_ref, v_ref, iq_ref, ik_ref, iw_ref):
    p = _dot(xn_ref[...], w_ref[...])
    q = _head_norm(p[:, :512], e_ref, gq_ref[...], HEAD_DIM) * HEAD_DIM ** -0.5
    q_ref[...] = q.astype(BF16)
    k = p[:, 512:640]
    k_ms = jnp.sum(k * k, axis=-1, keepdims=True) * (1.0 / HEAD_DIM)
    k_ref[...] = (k * lax.rsqrt(k_ms + RMS_EPS) * gk_ref[...]).astype(BF16)
    v_ref[...] = p[:, 640:768].astype(BF16)
    lane = lax.broadcasted_iota(jnp.int32, (1, 4 * LANE), 1)
    copy = lax.shift_right_logical(lane, int(math.log2(IDX_DIM))) & 3
    iq_ref[...] = _split_terms(p[:, 768:1280], copy < 2)
    ik_ref[...] = _split_terms(p[:, 1280:1408], (copy[:, :LANE] & 1) == 0)
    iw_ref[...] = p[:, 1408:1536]


def _proj_a(xn, w, e_slots, gq, gk):
    n = xn.shape[0]
    rows = lambda width: pl.BlockSpec((ROW_TILE, width), lambda i: (i, 0))
    return pl.pallas_call(
        _proj_a_kernel,
        grid=(n // ROW_TILE,),
        in_specs=[rows(D_MODEL), _full(w.shape), _full(e_slots.shape), _full(gq.shape), _full(gk.shape)],
        out_specs=[rows(512), rows(128), rows(128), rows(512), rows(128), rows(128)],
        out_shape=[jax.ShapeDtypeStruct((n, 512), BF16), jax.ShapeDtypeStruct((n, 128), BF16),
                   jax.ShapeDtypeStruct((n, 128), BF16), jax.ShapeDtypeStruct((n, 512), BF16),
                   jax.ShapeDtypeStruct((n, 128), BF16), jax.ShapeDtypeStruct((n, 128), F32)],
        compiler_params=_params("parallel"),
        name="proj_a",
    )(xn, w, e_slots, gq, gk)


def _proj_b_kernel(xn_ref, w_ref, e_ref, gq_ref, gk_ref, q_ref, k_ref, v_ref, km_ref):
    p = _dot(xn_ref[...], w_ref[...])
    q_ref[...] = _head_norm(p[:, :256], e_ref, gq_ref[...], HEAD_DIM)
    k = _head_norm(p[:, 256:512], e_ref, gk_ref[...], HEAD_DIM)
    k_ref[...] = k.astype(BF16)
    v_ref[...] = p[:, 512:768].astype(BF16)
    km = jnp.mean(k.reshape(ROW_TILE // MOBA_BLOCK, MOBA_BLOCK, BRANCH_WIDTH), axis=1)
    for j in range(ROW_TILE // MOBA_BLOCK):
        km_ref[j] = km[j:j + 1]


def _proj_b(xn, w, e64, gq, gk):
    n = xn.shape[0]
    rows = lambda width: pl.BlockSpec((ROW_TILE, width), lambda i: (i, 0))
    per_tile = ROW_TILE // MOBA_BLOCK
    return pl.pallas_call(
        _proj_b_kernel,
        grid=(n // ROW_TILE,),
        in_specs=[rows(D_MODEL), _full(w.shape), _full(e64.shape), _full(gq.shape), _full(gk.shape)],
        out_specs=[rows(256), rows(256), rows(256),
                   pl.BlockSpec((per_tile, 1, BRANCH_WIDTH), lambda i: (i, 0, 0))],
        out_shape=[jax.ShapeDtypeStruct((n, 256), F32), jax.ShapeDtypeStruct((n, 256), BF16),
                   jax.ShapeDtypeStruct((n, 256), BF16),
                   jax.ShapeDtypeStruct((n // MOBA_BLOCK, 1, BRANCH_WIDTH), F32)],
        compiler_params=_params("parallel"),
        name="proj_b",
    )(xn, w, e64, gq, gk)


def _proj_c_kernel(xn_ref, w_ref, e_ref, nq_ref, nkv_ref, wuq_ref, wuqs_ref, wuk_ref, wuv_ref,
                   gq_ref, gqs_ref, gk_ref, gks_ref, cos_ref, sin_ref, q_ref, k_ref, v_ref):
    p = _dot(xn_ref[...], w_ref[...])
    cos = cos_ref[...]
    sin = sin_ref[...]
    xq = _rms(p[:, :384], nq_ref[...]).astype(BF16)
    qa = _dot(xq, wuq_ref[...])
    qs = _dot(xq, wuqs_ref[...])
    rq = lax.rsqrt(_head_sumsq(qa, e_ref) * (1.0 / MLA_QK) + RMS_EPS)
    q_ref[...] = (rq * (qa * gq_ref[...] * cos + qs * gqs_ref[...] * sin)).astype(BF16)
    xkv = _rms(p[:, 384:640], nkv_ref[...]).astype(BF16)
    ka = _dot(xkv, wuk_ref[...]) + p[:, 640:1152]
    ks = p[:, 1152:1664]
    rk = lax.rsqrt(_head_sumsq(ka, e_ref) * (1.0 / MLA_QK) + RMS_EPS)
    k_ref[...] = (rk * (ka * gk_ref[...] * cos + ks * gks_ref[...] * sin)).astype(BF16)
    v_ref[...] = _dot(xkv, wuv_ref[...]).astype(BF16)


def _proj_c(xn, w, e96, nq, nkv, wuq, wuqs, wuk, wuv, gq, gqs, gk, gks, cos, sin, seq):
    n = xn.shape[0]
    rows = lambda width: pl.BlockSpec((ROW_TILE, width), lambda i: (i, 0))
    pos = pl.BlockSpec((ROW_TILE, N_HEADS * MLA_SLOT), lambda i: (i % (seq // ROW_TILE), 0))
    consts = (w, e96, nq, nkv, wuq, wuqs, wuk, wuv, gq, gqs, gk, gks)
    return pl.pallas_call(
        _proj_c_kernel,
        grid=(n // ROW_TILE,),
        in_specs=[rows(D_MODEL)] + [_full(c.shape) for c in consts] + [pos, pos],
        out_specs=[rows(N_HEADS * MLA_SLOT), rows(N_HEADS * MLA_SLOT), rows(256)],
        out_shape=[jax.ShapeDtypeStruct((n, N_HEADS * MLA_SLOT), BF16),
                   jax.ShapeDtypeStruct((n, N_HEADS * MLA_SLOT), BF16),
                   jax.ShapeDtypeStruct((n, 256), BF16)],
        compiler_params=_params("parallel"),
        name="proj_c",
    )(xn, *consts, cos, sin)


def _proj_d_kernel(xn_ref, w_ref, e_ref, gq_ref, gk_ref, q_ref, k_ref, v_ref):
    xn = xn_ref[...]
    for c in range(DIL_GROUPS):
        lo, hi = c * 256, (c + 1) * 256
        pq = _dot(xn, w_ref[:, lo:hi])
        q_ref[:, lo:hi] = (_head_norm(pq, e_ref, gq_ref[...], HEAD_DIM) * HEAD_DIM ** -0.5).astype(BF16)
        pk = _dot(xn, w_ref[:, 768 + lo:768 + hi])
        k_ref[:, lo:hi] = _head_norm(pk, e_ref, gk_ref[...], HEAD_DIM).astype(BF16)
        v_ref[:, lo:hi] = _dot(xn, w_ref[:, 1536 + lo:1536 + hi]).astype(BF16)


def _proj_d(xn, w, e64, gq, gk):
    n = xn.shape[0]
    rows = lambda width: pl.BlockSpec((ROW_TILE, width), lambda i: (i, 0))
    return pl.pallas_call(
        _proj_d_kernel,
        grid=(n // ROW_TILE,),
        in_specs=[rows(D_MODEL), _full(w.shape), _full(e64.shape), _full(gq.shape), _full(gk.shape)],
        out_specs=[rows(768), rows(768), rows(768)],
        out_shape=[jax.ShapeDtypeStruct((n, 768), BF16)] * 3,
        compiler_params=_params("parallel"),
        name="proj_d",
    )(xn, w, e64, gq, gk)


def _for_causal_width(qi, n_qtiles, tile_rows, body, classes=WIDTH_CLASSES):
    n_cls = min(classes, n_qtiles)
    per = n_qtiles // n_cls
    for c in range(n_cls):
        pl.when((qi >= c * per) & (qi < (c + 1) * per))(functools.partial(body, (c + 1) * per * tile_rows))


def _bias_rows(toep_ref, h, qblk, nchunk):
    return jnp.concatenate([toep_ref[h, jnp.maximum(qblk - c + 1, 0)] for c in range(nchunk)], axis=1)


def _bias_tile(toep_ref, h, qi, sub_blocks, nchunk):
    return jnp.concatenate([_bias_rows(toep_ref, h, qi * sub_blocks + j, nchunk) for j in range(sub_blocks)],
                           axis=0)


def _softmax_pv(logits, v):
    m = jnp.max(logits, axis=1, keepdims=True)
    p = jnp.exp(logits - m)
    l = jnp.sum(p, axis=1, keepdims=True)
    return _dot(p.astype(BF16), v) / l


def _dsa_body(width, qi, q_ref, k_ref, v_ref, iq_ref, ik_ref, iw_ref, toep_ref, tri_ref, o_ref, sel_ref,
              n_sel):
    rows = DSA_ROWS
    nchunk = width // LANE
    ik = ik_ref[0, :width, :]
    iw = iw_ref[0][:, :N_HEADS] * (N_HEADS ** -0.5 * IDX_DIM ** -0.5)

    score = jnp.zeros((rows, width), F32)
    for h in range(N_HEADS):
        r = _dot_t(iq_ref[0, :, h * LANE:(h + 1) * LANE], ik)
        score = score + jnp.maximum(r, 0.0) * iw[:, h:h + 1]

    t = qi * rows + lax.broadcasted_iota(jnp.int32, (rows, width), 0)
    s = lax.broadcasted_iota(jnp.int32, (rows, width), 1)

    bits = pltpu.bitcast(score, jnp.int32)
    key = jnp.where(bits < 0, bits ^ jnp.int32(0x7FFFFFFF), bits)
    key = jnp.where(s <= t, key, jnp.int32(INT_MIN))

    half_min = -2 ** 15
    upper = lax.shift_right_arithmetic(key, 16).astype(jnp.int16)
    lower = ((key & jnp.int32(0xFFFF)) + half_min).astype(jnp.int16)

    def count(hit):
        ones = jnp.where(hit, jnp.int16(1), jnp.int16(0))
        acc = ones[:, :LANE]
        for c in range(1, nchunk):
            acc = acc + ones[:, c * LANE:(c + 1) * LANE]
        return jnp.sum(acc.astype(F32), axis=1, keepdims=True)

    def search(half, need):
        def step(i, ans):
            cand = ans | lax.shift_left(jnp.int32(1), 15 - i)
            cnt = count(half >= (cand + half_min).astype(jnp.int16))
            return jnp.where(cnt >= need, cand, ans)
        return lax.fori_loop(0, 16, step, jnp.zeros((rows, 1), jnp.int32))

    thr_upper = search(upper, n_sel) + half_min
    thr_upper16 = thr_upper.astype(jnp.int16)
    n_over = count(upper > thr_upper16)
    candidates = jnp.where(upper == thr_upper16, lower, jnp.int16(half_min))
    thr = thr_upper * 65536 + search(candidates, n_sel - n_over)

    above = key > thr
    tied = key == thr
    n_above = jnp.sum(jnp.where(above, 1.0, 0.0), axis=1, keepdims=True)
    n_tied = jnp.sum(jnp.where(tied, 1.0, 0.0), axis=1, keepdims=True)
    room = n_sel - n_above
    sel_ref[:, :width] = jnp.where(key >= jnp.maximum(thr, jnp.int32(INT_MIN + 1)), 1.0, 0.0)

    overflow = (n_tied > room) & (thr != jnp.int32(INT_MIN))

    @pl.when(jnp.max(jnp.where(overflow, 1.0, 0.0)) > 0.0)
    def _():
        before = jnp.zeros((rows, 1), F32)
        for c in range(nchunk):
            tc = jnp.where(tied[:, c * LANE:(c + 1) * LANE], 1.0, 0.0)
            rank = before + _dot(tc.astype(BF16), tri_ref[...])
            keep = above[:, c * LANE:(c + 1) * LANE] | ((tc > 0.0) & (rank < room))
            sel_ref[:, c * LANE:(c + 1) * LANE] = jnp.where(keep, 1.0, 0.0)
            before = before + jnp.sum(tc, axis=1, keepdims=True)

    unselected = (sel_ref[:, :width] - 1.0) * -MASKED
    k = k_ref[0, :width, :]
    v = v_ref[0, :width, :]
    first = lax.broadcasted_iota(jnp.int32, (rows, LANE), 1) < HEAD_DIM
    for pair in range(N_HEADS // 2):
        outs = []
        for h in (2 * pair, 2 * pair + 1):
            logits = (_dot_t(q_ref[0, :, h * LANE:(h + 1) * LANE], k)
                      + _bias_tile(toep_ref, h, qi, rows // LANE, nchunk) + unselected)
            outs.append(_softmax_pv(logits, v))
        o_ref[0, :, pair * LANE:(pair + 1) * LANE] = jnp.where(first, outs[0], outs[1]).astype(o_ref.dtype)


def _dsa_kernel(*refs, seq, n_sel):
    qi = pl.program_id(1)
    _for_causal_width(qi, seq // DSA_ROWS, DSA_ROWS,
                      functools.partial(_dsa_body, qi=qi, n_sel=n_sel, **_named(refs)), classes=DSA_WIDTH_CLASSES)


def _named(refs):
    names = ("q_ref", "k_ref", "v_ref", "iq_ref", "ik_ref", "iw_ref", "toep_ref", "tri_ref", "o_ref", "sel_ref")
    return dict(zip(names, refs, strict=True))


def _dsa(q, k, v, iq, ik, iw, toep, tri):
    b, seq, _ = q.shape
    n_sel = min(DSA_TOPK, seq // 4)
    qblock = lambda width: pl.BlockSpec((1, DSA_ROWS, width), lambda bi, qi: (bi, qi, 0))
    whole = lambda width: pl.BlockSpec((1, seq, width), lambda bi, qi: (bi, 0, 0))
    return pl.pallas_call(
        functools.partial(_dsa_kernel, seq=seq, n_sel=n_sel),
        grid=(b, seq // DSA_ROWS),
        in_specs=[qblock(512), whole(128), whole(128), qblock(512), whole(128), qblock(128),
                  _full(toep.shape), _full(tri.shape)],
        out_specs=qblock(256),
        out_shape=jax.ShapeDtypeStruct((b, seq, 256), BF16),
        scratch_shapes=[pltpu.VMEM((DSA_ROWS, seq), F32)],
        compiler_params=_params("parallel", "parallel"),
        name="dsa",
    )(q, k, v, iq, ik, iw, toep, tri)


def _moba_body(width, qi, q_ref, k_ref, v_ref, km_ref, toep_ref, spread_ref, o_ref, nblk):
    nchunk = width // LANE
    n_sel = min(MOBA_TOPK, nblk - 1)
    own = lax.shift_right_logical(qi * LANE, int(math.log2(MOBA_BLOCK)))
    q = q_ref[0]

    km = km_ref[0]
    head_of_col = lax.shift_right_logical(lax.broadcasted_iota(jnp.int32, (nblk, BRANCH_WIDTH), 1),
                                          int(math.log2(HEAD_DIM)))
    per_head = [jnp.where(head_of_col == h, km, 0.0) for h in range(N_HEADS)]
    pad = jnp.zeros((LANE - N_HEADS * nblk, BRANCH_WIDTH), F32)
    gate = _dot_t(q, jnp.concatenate(per_head + [pad], axis=0), precision=lax.Precision.HIGHEST)

    lane = lax.broadcasted_iota(jnp.int32, (LANE, LANE), 1)
    blk = lane & (nblk - 1)
    past = (blk < own) & (lane < N_HEADS * nblk)
    gate = jnp.where(past, gate, NEG_INF)
    rank = jnp.zeros((LANE, LANE), F32)
    for d in range(1, nblk):
        before = pltpu.roll(gate, d, axis=1)
        rank = rank + jnp.where(blk >= d, jnp.where(before >= gate, 1.0, 0.0), 0.0)
        after = pltpu.roll(gate, LANE - d, axis=1)
        rank = rank + jnp.where(blk < nblk - d, jnp.where(after > gate, 1.0, 0.0), 0.0)
    picked = jnp.where(past, jnp.where(rank < n_sel, 1.0, 0.0), jnp.where(blk == own, 1.0, 0.0)).astype(BF16)

    first = lane < HEAD_DIM
    for pair in range(N_HEADS // 2):
        ps = slice(pair * LANE, (pair + 1) * LANE)
        qp = q[:, ps] * HEAD_DIM ** -0.5
        outs = []
        for mine in (first, ~first):
            h = 2 * pair + len(outs)
            unpicked = (_dot(picked, spread_ref[h, :, :width]) - 1.0) * -MASKED
            logits = (_dot_t(jnp.where(mine, qp, 0.0).astype(BF16), k_ref[0, :width, ps])
                      + _bias_rows(toep_ref, h, qi, nchunk) + unpicked)
            outs.append(_softmax_pv(logits, v_ref[0, :width, ps]))
        o_ref[0, :, ps] = jnp.where(first, outs[0], outs[1]).astype(o_ref.dtype)


def _moba_kernel(q_ref, k_ref, v_ref, km_ref, toep_ref, spread_ref, o_ref, *, seq):
    qi = pl.program_id(1)
    body = functools.partial(_moba_body, qi=qi, q_ref=q_ref, k_ref=k_ref, v_ref=v_ref, km_ref=km_ref,
                             toep_ref=toep_ref, spread_ref=spread_ref, o_ref=o_ref, nblk=seq // MOBA_BLOCK)
    _for_causal_width(qi, seq // LANE, LANE, body)


def _moba_spread(seq):
    nblk = seq // MOBA_BLOCK
    row = np.arange(LANE)[None, :, None]
    h = np.arange(N_HEADS)[:, None, None]
    blk_of_key = (np.arange(seq) // MOBA_BLOCK)[None, None, :]
    return jnp.asarray(row == h * nblk + blk_of_key, BF16)


def _moba(q, k, v, k_mean, toep):
    b, seq, _ = q.shape
    nblk = seq // MOBA_BLOCK
    if nblk & (nblk - 1) or N_HEADS * nblk > LANE:
        raise ValueError("MoBA lane layout needs a power-of-two block count with heads*blocks <= 128")
    spread = _moba_spread(seq)
    qblock = pl.BlockSpec((1, LANE, 256), lambda bi, qi: (bi, qi, 0))
    whole = pl.BlockSpec((1, seq, 256), lambda bi, qi: (bi, 0, 0))
    means = pl.BlockSpec((1, nblk, 256), lambda bi, qi: (bi, 0, 0))
    return pl.pallas_call(
        functools.partial(_moba_kernel, seq=seq),
        grid=(b, seq // LANE),
        in_specs=[qblock, whole, whole, means, _full(toep.shape), _full(spread.shape)],
        out_specs=qblock,
        out_shape=jax.ShapeDtypeStruct((b, seq, 256), BF16),
        compiler_params=_params("parallel", "parallel"),
        name="moba",
    )(q, k, v, k_mean, toep, spread)


def _mla_body(width, qi, q_ref, k_ref, v_ref, causal_ref, o_ref):
    nchunk = width // LANE
    causal = jnp.concatenate([causal_ref[0, jnp.clip(qi - c + 1, 0, 2)] for c in range(nchunk)], axis=1)
    for h in range(N_HEADS):
        qs = slice(h * MLA_SLOT, (h + 1) * MLA_SLOT)
        vs = slice(h * MLA_V, (h + 1) * MLA_V)
        logits = _dot_t(q_ref[0, :, qs], k_ref[0, :width, qs]) * MLA_QK ** -0.5 + causal
        o_ref[0, :, vs] = _softmax_pv(logits, v_ref[0, :width, vs]).astype(o_ref.dtype)


def _mla_kernel(q_ref, k_ref, v_ref, causal_ref, o_ref, *, seq):
    qi = pl.program_id(1)
    body = functools.partial(_mla_body, qi=qi, q_ref=q_ref, k_ref=k_ref, v_ref=v_ref, causal_ref=causal_ref,
                             o_ref=o_ref)
    _for_causal_width(qi, seq // LANE, LANE, body)


def _mla(q, k, v):
    b, seq, _ = q.shape
    tri = np.where(np.arange(LANE)[:, None] >= np.arange(LANE)[None, :], 0.0, NEG_INF)
    causal = jnp.asarray(np.stack([np.full((LANE, LANE), NEG_INF), tri, np.zeros((LANE, LANE))])[None], F32)
    return pl.pallas_call(
        functools.partial(_mla_kernel, seq=seq),
        grid=(b, seq // LANE),
        in_specs=[pl.BlockSpec((1, LANE, N_HEADS * MLA_SLOT), lambda bi, qi: (bi, qi, 0)),
                  pl.BlockSpec((1, seq, N_HEADS * MLA_SLOT), lambda bi, qi: (bi, 0, 0)),
                  pl.BlockSpec((1, seq, 256), lambda bi, qi: (bi, 0, 0)),
                  _full(causal.shape)],
        out_specs=pl.BlockSpec((1, LANE, 256), lambda bi, qi: (bi, qi, 0)),
        out_shape=jax.ShapeDtypeStruct((b, seq, 256), BF16),
        compiler_params=_params("parallel", "parallel"),
        name="mla",
    )(q, k, v, causal)


def _dil_group(dil, seq, toep_ref, g, qf_ref, kf_ref, vf_ref, m_ref, l_ref, acc_ref):
    n_band = seq // (dil * LANE)

    def unit(u, carry):
        r = lax.div(u, n_band)
        n = lax.rem(u, n_band)
        if dil == 1:
            cur = pl.ds(pl.multiple_of(n * LANE, LANE), LANE)
            prev = pl.ds(pl.multiple_of(jnp.maximum(n - 1, 0) * LANE, LANE), LANE)
        else:
            cur = pl.ds(n * (LANE * dil) + r, LANE, stride=dil)
            prev = pl.ds(jnp.maximum(n - 1, 0) * (LANE * dil) + r, LANE, stride=dil)
        prev_tile = jnp.where(n > 0, 2, 0)
        first = lax.broadcasted_iota(jnp.int32, (LANE, LANE), 1) < HEAD_DIM
        for pair in range(N_HEADS // 2):
            qp = qf_ref[pair, cur, :]
            kk = jnp.concatenate([kf_ref[pair, prev, :], kf_ref[pair, cur, :]], axis=0).astype(BF16)
            vv = jnp.concatenate([vf_ref[pair, prev, :], vf_ref[pair, cur, :]], axis=0).astype(BF16)
            stats = []
            for mine in (first, ~first):
                h = 2 * pair + len(stats)
                bias = jnp.concatenate([toep_ref[g, h, prev_tile], toep_ref[g, h, 1]], axis=1)
                logits = _dot_t(jnp.where(mine, qp, 0.0).astype(BF16), kk) + bias
                m = jnp.max(logits, axis=1, keepdims=True)
                p = jnp.exp(logits - m)
                stats.append((m, jnp.sum(p, axis=1, keepdims=True), _dot(p.astype(BF16), vv)))
            m_new, l_new, pv = (jnp.where(first, a, b) for a, b in zip(*stats))
            m_old = m_ref[pair, cur, :]
            m_tot = jnp.maximum(m_old, m_new)
            a_old = jnp.exp(m_old - m_tot)
            a_new = jnp.exp(m_new - m_tot)
            m_ref[pair, cur, :] = m_tot
            l_ref[pair, cur, :] = a_old * l_ref[pair, cur, :] + a_new * l_new
            acc_ref[pair, cur, :] = a_old * acc_ref[pair, cur, :] + a_new * pv
        return carry

    lax.fori_loop(0, dil * n_band, unit, 0, unroll=2)


def _get_rows(ref, rows):
    return jnp.concatenate([ref[0, rows, :], ref[1, rows, :]], axis=1)


def _set_rows(ref, rows, val):
    ref[0, rows, :] = val[:, :LANE]
    ref[1, rows, :] = val[:, LANE:]


def _dil_kernel(q_ref, k_ref, v_ref, toep_ref, o_ref, qf_ref, kf_ref, vf_ref, m_ref, l_ref, acc_ref, *, seq):
    g = pl.program_id(1)
    everything = pl.ds(0, seq)
    _set_rows(qf_ref, everything, q_ref[0].astype(F32))
    _set_rows(kf_ref, everything, k_ref[0].astype(F32))
    _set_rows(vf_ref, everything, v_ref[0].astype(F32))

    @pl.when(g == 0)
    def _():
        m_ref[...] = jnp.full(m_ref.shape, NEG_INF, F32)
        l_ref[...] = jnp.zeros_like(l_ref)
        acc_ref[...] = jnp.zeros_like(acc_ref)

    for gi, (_, dil) in enumerate(DIL_PATTERNS):
        pl.when(g == gi)(functools.partial(_dil_group, dil, seq, toep_ref, gi, qf_ref, kf_ref, vf_ref,
                                           m_ref, l_ref, acc_ref))

    @pl.when(g == DIL_GROUPS - 1)
    def _():
        o_ref[0] = (_get_rows(acc_ref, everything) / _get_rows(l_ref, everything)).astype(o_ref.dtype)


def _dil(q, k, v, toep):
    b, seq, _ = q.shape
    group = pl.BlockSpec((1, seq, 256), lambda bi, g: (bi, 0, g))
    state = pltpu.VMEM((2, seq, LANE), F32)
    return pl.pallas_call(
        functools.partial(_dil_kernel, seq=seq),
        grid=(b, DIL_GROUPS),
        in_specs=[group, group, group, _full(toep.shape)],
        out_specs=pl.BlockSpec((1, seq, 256), lambda bi, g: (bi, 0, 0)),
        out_shape=jax.ShapeDtypeStruct((b, seq, 256), BF16),
        scratch_shapes=[state] * 6,
        compiler_params=_params("parallel", "arbitrary"),
        name="dil",
    )(q, k, v, toep)


def _merge_kernel(x_ref, xn_ref, oa_ref, ob_ref, oc_ref, od_ref, wg_ref, bg_ref, wb_ref, wo_ref, xo_ref):
    branches = (oa_ref, ob_ref, oc_ref, od_ref)
    xn = xn_ref[...]
    total = jnp.zeros(x_ref.shape, F32)
    for n in range(N_BRANCH):
        cols = slice(n * D_MODEL, (n + 1) * D_MODEL)
        gate = _sigmoid(_dot(xn, wg_ref[:, cols]) + bg_ref[:, cols])
        total = total + gate * _dot(branches[n][...], wb_ref[n])
    xo_ref[...] = x_ref[...] + _dot(total.astype(BF16), wo_ref[...])


def _merge(x, xn, outs, wg, bg, wb, wo):
    n = x.shape[0]
    rows = lambda width: pl.BlockSpec((ROW_TILE, width), lambda i: (i, 0))
    return pl.pallas_call(
        _merge_kernel,
        grid=(n // ROW_TILE,),
        in_specs=[rows(D_MODEL), rows(D_MODEL)] + [rows(256)] * N_BRANCH
                 + [_full(wg.shape), _full(bg.shape), _full(wb.shape), _full(wo.shape)],
        out_specs=rows(D_MODEL),
        out_shape=jax.ShapeDtypeStruct((n, D_MODEL), F32),
        compiler_params=_params("parallel"),
        name="merge",
    )(x, xn, *outs, wg, bg, wb, wo)


def _bucket_of_distance():
    d = np.arange(REL_MAX_DIST + 1)
    exact = REL_BUCKETS // 2
    nf = np.maximum(d, 1).astype(np.float32)
    log_b = exact + (np.log(nf / exact) / math.log(REL_MAX_DIST / exact) * (REL_BUCKETS - exact)).astype(np.int32)
    return np.where(d < exact, d, np.minimum(log_b, REL_BUCKETS - 1))


def _toeplitz_tiles(table, n_off, dil, max_steps=None):
    span = 2 * LANE
    m = np.arange(span)
    off = np.arange(n_off)[:, None]
    steps = off * LANE + LANE - 1 - m[None, :]
    live = (steps >= 0) if max_steps is None else (steps >= 0) & (steps <= max_steps)
    v = table[:, _bucket_of_distance()[np.clip(steps * dil, 0, REL_MAX_DIST)]]
    v = jnp.where(live[None], v, NEG_INF)
    flat = jnp.tile(v, (1, 1, LANE))[..., :LANE * (span - 1)]
    tiles = flat.reshape(table.shape[0], n_off, LANE, span - 1)[..., LANE - 1:]
    return jnp.concatenate([jnp.full((table.shape[0], 1, LANE, LANE), NEG_INF, F32), tiles], axis=1)


def _same_head(width, head):
    idx = np.arange(width) // head
    return jnp.asarray(idx[:, None] == idx[None, :], BF16)


def _rope_tables(seq):
    half = MLA_ROPE // 2
    freqs = ROPE_THETA ** (-np.arange(half, dtype=np.float64) / half)
    ang = np.arange(seq, dtype=np.float64)[:, None] * freqs[None, :]
    pad = np.zeros((seq, MLA_SLOT - MLA_QK))
    cos_h = np.concatenate([np.ones((seq, MLA_NOPE)), np.cos(ang), np.cos(ang), pad], axis=1)
    sin_h = np.concatenate([np.zeros((seq, MLA_NOPE)), np.sin(ang), np.sin(ang), pad], axis=1)
    return (jnp.asarray(np.tile(cos_h, (1, N_HEADS)), F32), jnp.asarray(np.tile(sin_h, (1, N_HEADS)), F32))


def _rot_half_cols():
    half = MLA_ROPE // 2
    src = np.arange(MLA_SLOT)
    sign = np.zeros(MLA_SLOT, np.float32)
    src[MLA_NOPE:MLA_NOPE + half] = np.arange(MLA_NOPE + half, MLA_QK)
    sign[MLA_NOPE:MLA_NOPE + half] = -1.0
    src[MLA_NOPE + half:MLA_QK] = np.arange(MLA_NOPE, MLA_NOPE + half)
    sign[MLA_NOPE + half:MLA_QK] = 1.0
    return src, sign


def _mixer(x, xn, seq, w_in, b_gate, qk_a, qk_b, qk_c, qk_d, mla_nq, w_uq, mla_nkv, w_ukv,
           w_branch, w_out, toeps, consts):
    n = x.shape[0]
    b = n // seq
    e64, e96, tri, cos, sin = consts
    toep_a, toep_b, toep_d = toeps
    row = lambda a: a.reshape(1, -1)
    tile4 = lambda g: jnp.tile(g, N_HEADS).reshape(1, -1)
    r3 = lambda a: a.reshape(b, seq, a.shape[-1])

    off_iq = OFF_A + 6 * HEAD_DIM
    off_ik = off_iq + N_HEADS * IDX_DIM
    iq_rep = jnp.concatenate([jnp.tile(w_in[:, off_iq + h * IDX_DIM:off_iq + (h + 1) * IDX_DIM], (1, 4))
                              for h in range(N_HEADS)], axis=1)
    ik_rep = jnp.tile(w_in[:, off_ik:off_ik + IDX_DIM], (1, 4))
    iw_pad = jnp.pad(w_in[:, off_ik + IDX_DIM:OFF_A + W_A], ((0, 0), (0, LANE - N_HEADS)))
    half_pad = ((0, 0), (0, LANE - HEAD_DIM))
    q_slots = jnp.concatenate([jnp.pad(w_in[:, OFF_A + h * HEAD_DIM:OFF_A + (h + 1) * HEAD_DIM], half_pad)
                               for h in range(N_HEADS)], axis=1)
    off_k = OFF_A + N_HEADS * HEAD_DIM
    k_slot = jnp.pad(w_in[:, off_k:off_k + HEAD_DIM], half_pad)
    v_twice = jnp.tile(w_in[:, off_k + HEAD_DIM:off_iq], (1, 2))
    wa = jnp.concatenate([q_slots, k_slot, v_twice, iq_rep, ik_rep, iw_pad], axis=1).astype(BF16)
    gq_slots = jnp.tile(jnp.pad(qk_a[0], (0, LANE - HEAD_DIM)), N_HEADS).reshape(1, -1)
    gk_slot = jnp.pad(qk_a[1], (0, LANE - HEAD_DIM)).reshape(1, -1)
    qa, ka, va, iq, ik, iw = _proj_a(xn, wa, e96, gq_slots, gk_slot)
    out_a = _dsa(r3(qa), r3(ka), r3(va), r3(iq), r3(ik), r3(iw), toep_a, tri).reshape(n, 256)

    wb_in = w_in[:, OFF_B:OFF_B + W_B].astype(BF16)
    qb, kb, vb, kmean = _proj_b(xn, wb_in, e64, tile4(qk_b[0]), tile4(qk_b[1]))
    out_b = _moba(r3(qb), r3(kb), r3(vb), kmean.reshape(b, seq // MOBA_BLOCK, BRANCH_WIDTH),
                  toep_b).reshape(n, 256)

    src, sign = _rot_half_cols()
    src4 = np.concatenate([h * MLA_SLOT + src for h in range(N_HEADS)])
    sign4 = jnp.asarray(np.tile(sign, N_HEADS))
    slot_pad = MLA_SLOT - MLA_QK
    w_kr = w_in[:, OFF_C + MLA_Q_LORA + MLA_KV_LORA:OFF_C + W_C]
    kr_slots = jnp.pad(w_kr, ((0, 0), (MLA_NOPE, slot_pad)))
    kr_slots = jnp.tile(kr_slots, (1, N_HEADS))
    wc = jnp.concatenate([w_in[:, OFF_C:OFF_C + MLA_Q_LORA + MLA_KV_LORA], kr_slots,
                          kr_slots[:, src4] * sign4], axis=1).astype(BF16)
    slots = lambda w: jnp.pad(w.reshape(w.shape[0], N_HEADS, -1),
                              ((0, 0), (0, 0), (0, MLA_SLOT - w.shape[1] // N_HEADS))).reshape(w.shape[0], -1)
    wuq = slots(w_uq)
    ukv = w_ukv.reshape(MLA_KV_LORA, N_HEADS, MLA_NOPE + MLA_V)
    wuk = slots(ukv[:, :, :MLA_NOPE].reshape(MLA_KV_LORA, -1))
    wuv = ukv[:, :, MLA_NOPE:].reshape(MLA_KV_LORA, -1)
    gq = jnp.tile(jnp.pad(qk_c[0], (0, slot_pad)), N_HEADS)
    gk = jnp.tile(jnp.pad(qk_c[1], (0, slot_pad)), N_HEADS)
    qc, kc, vc = _proj_c(xn, wc, e96, row(mla_nq), row(mla_nkv), wuq.astype(BF16),
                         (wuq[:, src4] * sign4).astype(BF16), wuk.astype(BF16), wuv.astype(BF16),
                         row(gq), row(gq[src4]), row(gk), row(gk[src4]), cos, sin, seq)
    out_c = _mla(r3(qc), r3(kc), r3(vc)).reshape(n, 256)

    wd = w_in[:, OFF_D:OFF_D + W_D].astype(BF16)
    qd, kd, vd = _proj_d(xn, wd, e64, tile4(qk_d[0]), tile4(qk_d[1]))
    out_d = _dil(r3(qd), r3(kd), r3(vd), toep_d).reshape(n, 256)

    wg = w_in[:, OFF_G:OFF_G + W_G].astype(BF16)
    return _merge(x, xn, (out_a, out_b, out_c, out_d), wg, b_gate.reshape(1, -1),
                  w_branch.astype(BF16), w_out.astype(BF16))


def kernel(x, norm_gain, w_in, b_gate, qk_gain_a, qk_gain_b, qk_gain_c, qk_gain_d, mla_norm_q,
           w_mla_uq, mla_norm_kv, w_mla_ukv, w_branch, w_out, rel_bias, w_ffn_in, w_ffn_out):
    b, seq, d = x.shape
    depth = norm_gain.shape[0]
    toep_a = _toeplitz_tiles(rel_bias[0:4], seq // LANE, 1)
    toep_b = _toeplitz_tiles(rel_bias[4:8], seq // LANE, 1)
    toep_d = jnp.stack([_toeplitz_tiles(rel_bias[8 + 4 * g:12 + 4 * g], 2, dil, max_steps=window // dil)
                        for g, (window, dil) in enumerate(DIL_PATTERNS)])
    tri = jnp.asarray(np.arange(LANE)[:, None] < np.arange(LANE)[None, :], BF16)
    consts = (_same_head(256, HEAD_DIM), _same_head(N_HEADS * MLA_SLOT, MLA_SLOT), tri) + _rope_tables(seq)

    x = x.reshape(b * seq, d)
    for l in range(depth):
        g = norm_gain[l]
        x, xn = _ffn(x, g[0:1], g[1:2], w_ffn_in[l, 0].astype(BF16), w_ffn_out[l, 0].astype(BF16))
        x = _mixer(x, xn, seq, w_in[l], b_gate[l], qk_gain_a[l], qk_gain_b[l], qk_gain_c[l],
                   qk_gain_d[l], mla_norm_q[l], w_mla_uq[l], mla_norm_kv[l], w_mla_ukv[l],
                   w_branch[l], w_out[l], (toep_a, toep_b, toep_d), consts)
        x, _ = _ffn(x, g[2:3], g[2:3], w_ffn_in[l, 1].astype(BF16), w_ffn_out[l, 1].astype(BF16))
    return x.reshape(b, seq, d)
```

```python
import functools
import math

import numpy as np
import jax
import jax.numpy as jnp
from jax import lax
from jax.experimental import pallas as pl
from jax.experimental.pallas import tpu as pltpu

D_MODEL = 1024
HEAD_DIM = 64
N_HEADS = 4
BRANCH_WIDTH = N_HEADS * HEAD_DIM
IDX_DIM = 32
DSA_TOPK = 256
MOBA_BLOCK = 256
MOBA_TOPK = 3
MLA_Q_LORA = 384
MLA_KV_LORA = 256
MLA_NOPE = 64
MLA_ROPE = 32
MLA_QK = MLA_NOPE + MLA_ROPE
MLA_V = 64
MLA_SLOT = 128
ROPE_THETA = 10000.0
DIL_PATTERNS = ((128, 1), (512, 4), (2048, 16))
DIL_GROUPS = 3
N_BRANCH = 4
D_FF = 2816
REL_BUCKETS = 32
REL_MAX_DIST = 2048
RMS_EPS = 1e-6

OFF_A = 0
W_A = 4 * HEAD_DIM + 2 * HEAD_DIM + 4 * IDX_DIM + IDX_DIM + 4
OFF_B = OFF_A + W_A
W_B = 3 * BRANCH_WIDTH
OFF_C = OFF_B + W_B
W_C = MLA_Q_LORA + MLA_KV_LORA + MLA_ROPE
OFF_D = OFF_C + W_C
W_D = 3 * DIL_GROUPS * BRANCH_WIDTH
OFF_G = OFF_D + W_D
W_G = N_BRANCH * D_MODEL

LANE = 128
ROW_TILE = 512
FF_CHUNK = 1408
WIDTH_CLASSES = 8
ATT_ROWS = 256
ATT_WIDTH_CLASSES = 4
DSA_WIDTH_CLASSES = 4
DSA_ROWS = 256
MASKED = -1e30
VMEM_LIMIT = 56 * 1024 * 1024

F32 = jnp.float32
BF16 = jnp.bfloat16
NEG_INF = float("-inf")
INT_MIN = -2 ** 31

_CONTRACT_LAST = (((1,), (1,)), ((), ()))


def _dot(a, b, precision=None):
    return jnp.dot(a, b, preferred_element_type=F32, precision=precision)


def _dot_t(a, b, precision=None):
    return lax.dot_general(a, b, _CONTRACT_LAST, preferred_element_type=F32, precision=precision)


def _rms(x, gain):
    return x * lax.rsqrt(jnp.mean(x * x, axis=-1, keepdims=True) + RMS_EPS) * gain


def _sigmoid(x):
    return 1.0 / (1.0 + jnp.exp(-x))


def _head_sumsq(y, e_ref):
    y2 = y * y
    hi = y2.astype(BF16)
    lo = (y2 - hi.astype(F32)).astype(BF16)
    e = e_ref[...]
    return _dot(hi, e) + _dot(lo, e)


def _head_norm(y, e_ref, gain, width):
    return y * lax.rsqrt(_head_sumsq(y, e_ref) * (1.0 / width) + RMS_EPS) * gain


def _params(*sem):
    return pltpu.CompilerParams(dimension_semantics=sem, vmem_limit_bytes=VMEM_LIMIT)


def _full(shape):
    return pl.BlockSpec(shape, lambda *_: (0,) * len(shape))


def _resident(shape):
    return pl.BlockSpec(shape, lambda *_: (0,) * len(shape), pipeline_mode=pl.Buffered(1))


def _ffn_kernel(x_ref, g_ref, gn_ref, wi_ref, wo_ref, xo_ref, xn_ref):
    half = ROW_TILE // 2
    for s in range(2):
        rows = slice(s * half, (s + 1) * half)
        x = x_ref[rows, :]
        xb = _rms(x, g_ref[...]).astype(BF16)
        acc = None
        for j in range(D_FF // FF_CHUNK):
            gate = _dot(xb, wi_ref[:, j * FF_CHUNK:(j + 1) * FF_CHUNK])
            up = _dot(xb, wi_ref[:, D_FF + j * FF_CHUNK:D_FF + (j + 1) * FF_CHUNK])
            h = (gate * _sigmoid(gate) * up).astype(BF16)
            part = _dot(h, wo_ref[j * FF_CHUNK:(j + 1) * FF_CHUNK, :])
            acc = part if acc is None else acc + part
        xo = x + 0.5 * acc
        xo_ref[rows, :] = xo
        xn_ref[rows, :] = _rms(xo, gn_ref[...]).astype(BF16)


def _ffn(x, gain, gain_next, w_in, w_out):
    n = x.shape[0]
    rows = pl.BlockSpec((ROW_TILE, D_MODEL), lambda i: (i, 0))
    return pl.pallas_call(
        _ffn_kernel,
        grid=(n // ROW_TILE,),
        in_specs=[rows, _full((1, D_MODEL)), _full((1, D_MODEL)), _resident(w_in.shape), _resident(w_out.shape)],
        out_specs=[rows, rows],
        out_shape=[jax.ShapeDtypeStruct((n, D_MODEL), F32), jax.ShapeDtypeStruct((n, D_MODEL), BF16)],
        compiler_params=_params("parallel"),
        name="ffn",
    )(x, gain, gain_next, w_in, w_out)


def _split_terms(rep, hi_lanes):
    hi = rep.astype(BF16).astype(F32)
    return jnp.where(hi_lanes, hi, rep - hi).astype(BF16)


def _proj_a_kernel(xn_ref, w_ref, e_ref, gq_ref, gk_ref, q_ref, k_ref, v_ref, iq_ref, ik_ref, iw_ref):
    p = _dot(xn_ref[...], w_ref[...])
    q = _head_norm(p[:, :512], e_ref, gq_ref[...], HEAD_DIM) * HEAD_DIM ** -0.5
    q_ref[...] = q.astype(BF16)
    k = p[:, 512:640]
    k_ms = jnp.sum(k * k, axis=-1, keepdims=True) * (1.0 / HEAD_DIM)
    k_ref[...] = (k * lax.rsqrt(k_ms + RMS_EPS) * gk_ref[...]).astype(BF16)
    v_ref[...] = p[:, 640:768].astype(BF16)
    lane = lax.broadcasted_iota(jnp.int32, (1, 4 * LANE), 1)
    copy = lax.shift_right_logical(lane, int(math.log2(IDX_DIM))) & 3
    iq_ref[...] = _split_terms(p[:, 768:1280], copy < 2)
    ik_ref[...] = _split_terms(p[:, 1280:1408], (copy[:, :LANE] & 1) == 0)
    iw_ref[...] = p[:, 1408:1536]


def _proj_a(xn, w, e_slots, gq, gk):
    n = xn.shape[0]
    rows = lambda width: pl.BlockSpec((ROW_TILE, width), lambda i: (i, 0))
    return pl.pallas_call(
        _proj_a_kernel,
        grid=(n // ROW_TILE,),
        in_specs=[rows(D_MODEL), _full(w.shape), _full(e_slots.shape), _full(gq.shape), _full(gk.shape)],
        out_specs=[rows(512), rows(128), rows(128), rows(512), rows(128), rows(128)],
        out_shape=[jax.ShapeDtypeStruct((n, 512), BF16), jax.ShapeDtypeStruct((n, 128), BF16),
                   jax.ShapeDtypeStruct((n, 128), BF16), jax.ShapeDtypeStruct((n, 512), BF16),
                   jax.ShapeDtypeStruct((n, 128), BF16), jax.ShapeDtypeStruct((n, 128), F32)],
        compiler_params=_params("parallel"),
        name="proj_a",
    )(xn, w, e_slots, gq, gk)


def _proj_b_kernel(xn_ref, w_ref, e_ref, gq_ref, gk_ref, q_ref, k_ref, v_ref, km_ref):
    p = _dot(xn_ref[...], w_ref[...])
    q_ref[...] = _head_norm(p[:, :256], e_ref, gq_ref[...], HEAD_DIM)
    k = _head_norm(p[:, 256:512], e_ref, gk_ref[...], HEAD_DIM)
    k_ref[...] = k.astype(BF16)
    v_ref[...] = p[:, 512:768].astype(BF16)
    km = jnp.mean(k.reshape(ROW_TILE // MOBA_BLOCK, MOBA_BLOCK, BRANCH_WIDTH), axis=1)
    for j in range(ROW_TILE // MOBA_BLOCK):
        km_ref[j] = km[j:j + 1]


def _proj_b(xn, w, e64, gq, gk):
    n = xn.shape[0]
    rows = lambda width: pl.BlockSpec((ROW_TILE, width), lambda i: (i, 0))
    per_tile = ROW_TILE // MOBA_BLOCK
    return pl.pallas_call(
        _proj_b_kernel,
        grid=(n // ROW_TILE,),
        in_specs=[rows(D_MODEL), _full(w.shape), _full(e64.shape), _full(gq.shape), _full(gk.shape)],
        out_specs=[rows(256), rows(256), rows(256),
                   pl.BlockSpec((per_tile, 1, BRANCH_WIDTH), lambda i: (i, 0, 0))],
        out_shape=[jax.ShapeDtypeStruct((n, 256), F32), jax.ShapeDtypeStruct((n, 256), BF16),
                   jax.ShapeDtypeStruct((n, 256), BF16),
                   jax.ShapeDtypeStruct((n // MOBA_BLOCK, 1, BRANCH_WIDTH), F32)],
        compiler_params=_params("parallel"),
        name="proj_b",
    )(xn, w, e64, gq, gk)


def _proj_c_kernel(xn_ref, w_ref, e_ref, nq_ref, nkv_ref, wuq_ref, wuqs_ref, wuk_ref, wuv_ref,
                   gq_ref, gqs_ref, gk_ref, gks_ref, cos_ref, sin_ref, q_ref, k_ref, v_ref):
    p = _dot(xn_ref[...], w_ref[...])
    cos = cos_ref[...]
    sin = sin_ref[...]
    xq = _rms(p[:, :384], nq_ref[...]).astype(BF16)
    qa = _dot(xq, wuq_ref[...])
    qs = _dot(xq, wuqs_ref[...])
    rq = lax.rsqrt(_head_sumsq(qa, e_ref) * (1.0 / MLA_QK) + RMS_EPS)
    q_ref[...] = (rq * (qa * gq_ref[...] * cos + qs * gqs_ref[...] * sin)).astype(BF16)
    xkv = _rms(p[:, 384:640], nkv_ref[...]).astype(BF16)
    ka = _dot(xkv, wuk_ref[...]) + p[:, 640:1152]
    ks = p[:, 1152:1664]
    rk = lax.rsqrt(_head_sumsq(ka, e_ref) * (1.0 / MLA_QK) + RMS_EPS)
    k_ref[...] = (rk * (ka * gk_ref[...] * cos + ks * gks_ref[...] * sin)).astype(BF16)
    v_ref[...] = _dot(xkv, wuv_ref[...]).astype(BF16)


def _proj_c(xn, w, e96, nq, nkv, wuq, wuqs, wuk, wuv, gq, gqs, gk, gks, cos, sin, seq):
    n = xn.shape[0]
    rows = lambda width: pl.BlockSpec((ROW_TILE, width), lambda i: (i, 0))
    pos = pl.BlockSpec((ROW_TILE, N_HEADS * MLA_SLOT), lambda i: (i % (seq // ROW_TILE), 0))
    consts = (w, e96, nq, nkv, wuq, wuqs, wuk, wuv, gq, gqs, gk, gks)
    return pl.pallas_call(
        _proj_c_kernel,
        grid=(n // ROW_TILE,),
        in_specs=[rows(D_MODEL)] + [_full(c.shape) for c in consts] + [pos, pos],
        out_specs=[rows(N_HEADS * MLA_SLOT), rows(N_HEADS * MLA_SLOT), rows(256)],
        out_shape=[jax.ShapeDtypeStruct((n, N_HEADS * MLA_SLOT), BF16),
                   jax.ShapeDtypeStruct((n, N_HEADS * MLA_SLOT), BF16),
                   jax.ShapeDtypeStruct((n, 256), BF16)],
        compiler_params=_params("parallel"),
        name="proj_c",
    )(xn, *consts, cos, sin)


def _proj_d_kernel(xn_ref, w_ref, e_ref, gq_ref, gk_ref, q_ref, k_ref, v_ref):
    xn = xn_ref[...]
    for c in range(DIL_GROUPS):
        lo, hi = c * 256, (c + 1) * 256
        pq = _dot(xn, w_ref[:, lo:hi])
        q_ref[:, lo:hi] = (_head_norm(pq, e_ref, gq_ref[...], HEAD_DIM) * HEAD_DIM ** -0.5).astype(BF16)
        pk = _dot(xn, w_ref[:, 768 + lo:768 + hi])
        k_ref[:, lo:hi] = _head_norm(pk, e_ref, gk_ref[...], HEAD_DIM).astype(BF16)
        v_ref[:, lo:hi] = _dot(xn, w_ref[:, 1536 + lo:1536 + hi]).astype(BF16)


def _proj_d(xn, w, e64, gq, gk):
    n = xn.shape[0]
    rows = lambda width: pl.BlockSpec((ROW_TILE, width), lambda i: (i, 0))
    return pl.pallas_call(
        _proj_d_kernel,
        grid=(n // ROW_TILE,),
        in_specs=[rows(D_MODEL), _full(w.shape), _full(e64.shape), _full(gq.shape), _full(gk.shape)],
        out_specs=[rows(768), rows(768), rows(768)],
        out_shape=[jax.ShapeDtypeStruct((n, 768), BF16)] * 3,
        compiler_params=_params("parallel"),
        name="proj_d",
    )(xn, w, e64, gq, gk)


def _for_causal_width(qi, n_qtiles, tile_rows, body, classes=WIDTH_CLASSES):
    n_cls = min(classes, n_qtiles)
    per = n_qtiles // n_cls
    for c in range(n_cls):
        pl.when((qi >= c * per) & (qi < (c + 1) * per))(functools.partial(body, (c + 1) * per * tile_rows))


def _bias_rows(toep_ref, h, qblk, nchunk):
    return jnp.concatenate([toep_ref[h, jnp.maximum(qblk - c + 1, 0)] for c in range(nchunk)], axis=1)


def _bias_tile(toep_ref, h, qi, sub_blocks, nchunk):
    return jnp.concatenate([_bias_rows(toep_ref, h, qi * sub_blocks + j, nchunk) for j in range(sub_blocks)],
                           axis=0)


def _softmax_pv(logits, v):
    m = jnp.max(logits, axis=1, keepdims=True)
    p = jnp.exp(logits - m)
    l = jnp.sum(p, axis=1, keepdims=True)
    return _dot(p.astype(BF16), v) / l


def _dsa_body(width, qi, q_ref, k_ref, v_ref, iq_ref, ik_ref, iw_ref, toep_ref, tri_ref, o_ref, sel_ref,
              n_sel):
    rows = DSA_ROWS
    nchunk = width // LANE
    ik = ik_ref[0, :width, :]
    iw = iw_ref[0][:, :N_HEADS] * (N_HEADS ** -0.5 * IDX_DIM ** -0.5)

    score = jnp.zeros((rows, width), F32)
    for h in range(N_HEADS):
        r = _dot_t(iq_ref[0, :, h * LANE:(h + 1) * LANE], ik)
        score = score + jnp.maximum(r, 0.0) * iw[:, h:h + 1]

    t = qi * rows + lax.broadcasted_iota(jnp.int32, (rows, width), 0)
    s = lax.broadcasted_iota(jnp.int32, (rows, width), 1)

    bits = pltpu.bitcast(score, jnp.int32)
    key = jnp.where(bits < 0, bits ^ jnp.int32(0x7FFFFFFF), bits)
    key = jnp.where(s <= t, key, jnp.int32(INT_MIN))

    half_min = -2 ** 15
    upper = lax.shift_right_arithmetic(key, 16).astype(jnp.int16)
    lower = ((key & jnp.int32(0xFFFF)) + half_min).astype(jnp.int16)

    def count(hit):
        ones = jnp.where(hit, jnp.int16(1), jnp.int16(0))
        acc = ones[:, :LANE]
        for c in range(1, nchunk):
            acc = acc + ones[:, c * LANE:(c + 1) * LANE]
        return jnp.sum(acc.astype(F32), axis=1, keepdims=True)

    def search(half, need):
        def step(i, ans):
            cand = ans | lax.shift_left(jnp.int32(1), 15 - i)
            cnt = count(half >= (cand + half_min).astype(jnp.int16))
            return jnp.where(cnt >= need, cand, ans)
        return lax.fori_loop(0, 16, step, jnp.zeros((rows, 1), jnp.int32))

    thr_upper = search(upper, n_sel) + half_min
    thr_upper16 = thr_upper.astype(jnp.int16)
    n_over = count(upper > thr_upper16)
    candidates = jnp.where(upper == thr_upper16, lower, jnp.int16(half_min))
    thr = thr_upper * 65536 + search(candidates, n_sel - n_over)

    above = key > thr
    tied = key == thr
    n_above = jnp.sum(jnp.where(above, 1.0, 0.0), axis=1, keepdims=True)
    n_tied = jnp.sum(jnp.where(tied, 1.0, 0.0), axis=1, keepdims=True)
    room = n_sel - n_above
    sel_ref[:, :width] = jnp.where(key >= jnp.maximum(thr, jnp.int32(INT_MIN + 1)), 1.0, 0.0)

    overflow = (n_tied > room) & (thr != jnp.int32(INT_MIN))

    @pl.when(jnp.max(jnp.where(overflow, 1.0, 0.0)) > 0.0)
    def _():
        before = jnp.zeros((rows, 1), F32)
        for c in range(nchunk):
            tc = jnp.where(tied[:, c * LANE:(c + 1) * LANE], 1.0, 0.0)
            rank = before + _dot(tc.astype(BF16), tri_ref[...])
            keep = above[:, c * LANE:(c + 1) * LANE] | ((tc > 0.0) & (rank < room))
            sel_ref[:, c * LANE:(c + 1) * LANE] = jnp.where(keep, 1.0, 0.0)
            before = before + jnp.sum(tc, axis=1, keepdims=True)

    unselected = (sel_ref[:, :width] - 1.0) * -MASKED
    k = k_ref[0, :width, :]
    v = v_ref[0, :width, :]
    first = lax.broadcasted_iota(jnp.int32, (rows, LANE), 1) < HEAD_DIM
    for pair in range(N_HEADS // 2):
        outs = []
        for h in (2 * pair, 2 * pair + 1):
            logits = (_dot_t(q_ref[0, :, h * LANE:(h + 1) * LANE], k)
                      + _bias_tile(toep_ref, h, qi, rows // LANE, nchunk) + unselected)
            outs.append(_softmax_pv(logits, v))
        o_ref[0, :, pair * LANE:(pair + 1) * LANE] = jnp.where(first, outs[0], outs[1]).astype(o_ref.dtype)


def _dsa_kernel(*refs, seq, n_sel):
    qi = pl.program_id(1)
    _for_causal_width(qi, seq // DSA_ROWS, DSA_ROWS,
                      functools.partial(_dsa_body, qi=qi, n_sel=n_sel, **_named(refs)), classes=DSA_WIDTH_CLASSES)


def _named(refs):
    names = ("q_ref", "k_ref", "v_ref", "iq_ref", "ik_ref", "iw_ref", "toep_ref", "tri_ref", "o_ref", "sel_ref")
    return dict(zip(names, refs, strict=True))


def _dsa(q, k, v, iq, ik, iw, toep, tri):
    b, seq, _ = q.shape
    n_sel = min(DSA_TOPK, seq // 4)
    qblock = lambda width: pl.BlockSpec((1, DSA_ROWS, width), lambda bi, qi: (bi, qi, 0))
    whole = lambda width: pl.BlockSpec((1, seq, width), lambda bi, qi: (bi, 0, 0))
    return pl.pallas_call(
        functools.partial(_dsa_kernel, seq=seq, n_sel=n_sel),
        grid=(b, seq // DSA_ROWS),
        in_specs=[qblock(512), whole(128), whole(128), qblock(512), whole(128), qblock(128),
                  _full(toep.shape), _full(tri.shape)],
        out_specs=qblock(256),
        out_shape=jax.ShapeDtypeStruct((b, seq, 256), BF16),
        scratch_shapes=[pltpu.VMEM((DSA_ROWS, seq), F32)],
        compiler_params=_params("parallel", "parallel"),
        name="dsa",
    )(q, k, v, iq, ik, iw, toep, tri)


def _moba_body(width, qi, q_ref, k_ref, v_ref, km_ref, toep_ref, spread_ref, o_ref, nblk):
    nchunk = width // LANE
    n_sel = min(MOBA_TOPK, nblk - 1)
    rows = ATT_ROWS
    own = lax.shift_right_logical(qi * rows, int(math.log2(MOBA_BLOCK)))
    q = q_ref[0]

    km = km_ref[0]
    head_of_col = lax.shift_right_logical(lax.broadcasted_iota(jnp.int32, (nblk, BRANCH_WIDTH), 1),
                                          int(math.log2(HEAD_DIM)))
    per_head = [jnp.where(head_of_col == h, km, 0.0) for h in range(N_HEADS)]
    pad = jnp.zeros((LANE - N_HEADS * nblk, BRANCH_WIDTH), F32)
    gate = _dot_t(q, jnp.concatenate(per_head + [pad], axis=0), precision=lax.Precision.HIGHEST)

    lane = lax.broadcasted_iota(jnp.int32, (rows, LANE), 1)
    blk = lane & (nblk - 1)
    past = (blk < own) & (lane < N_HEADS * nblk)
    gate = jnp.where(past, gate, NEG_INF)
    rank = jnp.zeros((rows, LANE), F32)
    for d in range(1, nblk):
        before = pltpu.roll(gate, d, axis=1)
        rank = rank + jnp.where(blk >= d, jnp.where(before >= gate, 1.0, 0.0), 0.0)
        after = pltpu.roll(gate, LANE - d, axis=1)
        rank = rank + jnp.where(blk < nblk - d, jnp.where(after > gate, 1.0, 0.0), 0.0)
    picked = jnp.where(past, jnp.where(rank < n_sel, 1.0, 0.0), jnp.where(blk == own, 1.0, 0.0)).astype(BF16)

    first = lane < HEAD_DIM
    for pair in range(N_HEADS // 2):
        ps = slice(pair * LANE, (pair + 1) * LANE)
        qp = q[:, ps] * HEAD_DIM ** -0.5
        outs = []
        for mine in (first, ~first):
            h = 2 * pair + len(outs)
            unpicked = (_dot(picked, spread_ref[h, :, :width]) - 1.0) * -MASKED
            logits = (_dot_t(jnp.where(mine, qp, 0.0).astype(BF16), k_ref[0, :width, ps])
                      + _bias_tile(toep_ref, h, qi, rows // LANE, nchunk) + unpicked)
            outs.append(_softmax_pv(logits, v_ref[0, :width, ps]))
        o_ref[0, :, ps] = jnp.where(first, outs[0], outs[1]).astype(o_ref.dtype)


def _moba_kernel(q_ref, k_ref, v_ref, km_ref, toep_ref, spread_ref, o_ref, *, seq):
    qi = pl.program_id(1)
    body = functools.partial(_moba_body, qi=qi, q_ref=q_ref, k_ref=k_ref, v_ref=v_ref, km_ref=km_ref,
                             toep_ref=toep_ref, spread_ref=spread_ref, o_ref=o_ref, nblk=seq // MOBA_BLOCK)
    _for_causal_width(qi, seq // ATT_ROWS, ATT_ROWS, body, classes=ATT_WIDTH_CLASSES)


def _moba_spread(seq):
    nblk = seq // MOBA_BLOCK
    row = np.arange(LANE)[None, :, None]
    h = np.arange(N_HEADS)[:, None, None]
    blk_of_key = (np.arange(seq) // MOBA_BLOCK)[None, None, :]
    return jnp.asarray(row == h * nblk + blk_of_key, BF16)


def _moba(q, k, v, k_mean, toep):
    b, seq, _ = q.shape
    nblk = seq // MOBA_BLOCK
    if nblk & (nblk - 1) or N_HEADS * nblk > LANE:
        raise ValueError("MoBA lane layout needs a power-of-two block count with heads*blocks <= 128")
    spread = _moba_spread(seq)
    qblock = pl.BlockSpec((1, ATT_ROWS, 256), lambda bi, qi: (bi, qi, 0))
    whole = pl.BlockSpec((1, seq, 256), lambda bi, qi: (bi, 0, 0))
    means = pl.BlockSpec((1, nblk, 256), lambda bi, qi: (bi, 0, 0))
    return pl.pallas_call(
        functools.partial(_moba_kernel, seq=seq),
        grid=(b, seq // ATT_ROWS),
        in_specs=[qblock, whole, whole, means, _full(toep.shape), _full(spread.shape)],
        out_specs=qblock,
        out_shape=jax.ShapeDtypeStruct((b, seq, 256), BF16),
        compiler_params=_params("parallel", "parallel"),
        name="moba",
    )(q, k, v, k_mean, toep, spread)


def _mla_body(width, qi, q_ref, k_ref, v_ref, causal_ref, o_ref):
    nchunk = width // LANE
    sub = ATT_ROWS // LANE
    causal = jnp.concatenate(
        [jnp.concatenate([causal_ref[0, jnp.clip(qi * sub + j - c + 1, 0, 2)] for c in range(nchunk)], axis=1)
         for j in range(sub)], axis=0)
    for h in range(N_HEADS):
        qs = slice(h * MLA_SLOT, (h + 1) * MLA_SLOT)
        vs = slice(h * MLA_V, (h + 1) * MLA_V)
        logits = _dot_t(q_ref[0, :, qs], k_ref[0, :width, qs]) * MLA_QK ** -0.5 + causal
        o_ref[0, :, vs] = _softmax_pv(logits, v_ref[0, :width, vs]).astype(o_ref.dtype)


def _mla_kernel(q_ref, k_ref, v_ref, causal_ref, o_ref, *, seq):
    qi = pl.program_id(1)
    body = functools.partial(_mla_body, qi=qi, q_ref=q_ref, k_ref=k_ref, v_ref=v_ref, causal_ref=causal_ref,
                             o_ref=o_ref)
    _for_causal_width(qi, seq // ATT_ROWS, ATT_ROWS, body, classes=ATT_WIDTH_CLASSES)


def _mla(q, k, v):
    b, seq, _ = q.shape
    tri = np.where(np.arange(LANE)[:, None] >= np.arange(LANE)[None, :], 0.0, NEG_INF)
    causal = jnp.asarray(np.stack([np.full((LANE, LANE), NEG_INF), tri, np.zeros((LANE, LANE))])[None], F32)
    return pl.pallas_call(
        functools.partial(_mla_kernel, seq=seq),
        grid=(b, seq // ATT_ROWS),
        in_specs=[pl.BlockSpec((1, ATT_ROWS, N_HEADS * MLA_SLOT), lambda bi, qi: (bi, qi, 0)),
                  pl.BlockSpec((1, seq, N_HEADS * MLA_SLOT), lambda bi, qi: (bi, 0, 0)),
                  pl.BlockSpec((1, seq, 256), lambda bi, qi: (bi, 0, 0)),
                  _full(causal.shape)],
        out_specs=pl.BlockSpec((1, ATT_ROWS, 256), lambda bi, qi: (bi, qi, 0)),
        out_shape=jax.ShapeDtypeStruct((b, seq, 256), BF16),
        compiler_params=_params("parallel", "parallel"),
        name="mla",
    )(q, k, v, causal)


def _dil_group(dil, seq, toep_ref, g, qf_ref, kf_ref, vf_ref, m_ref, l_ref, acc_ref):
    n_band = seq // (dil * LANE)

    def unit(u, carry):
        r = lax.div(u, n_band)
        n = lax.rem(u, n_band)
        if dil == 1:
            cur = pl.ds(pl.multiple_of(n * LANE, LANE), LANE)
            prev = pl.ds(pl.multiple_of(jnp.maximum(n - 1, 0) * LANE, LANE), LANE)
        else:
            cur = pl.ds(n * (LANE * dil) + r, LANE, stride=dil)
            prev = pl.ds(jnp.maximum(n - 1, 0) * (LANE * dil) + r, LANE, stride=dil)
        prev_tile = jnp.where(n > 0, 2, 0)
        first = lax.broadcasted_iota(jnp.int32, (LANE, LANE), 1) < HEAD_DIM
        for pair in range(N_HEADS // 2):
            qp = qf_ref[pair, cur, :]
            kk = jnp.concatenate([kf_ref[pair, prev, :], kf_ref[pair, cur, :]], axis=0).astype(BF16)
            vv = jnp.concatenate([vf_ref[pair, prev, :], vf_ref[pair, cur, :]], axis=0).astype(BF16)
            stats = []
            for mine in (first, ~first):
                h = 2 * pair + len(stats)
                bias = jnp.concatenate([toep_ref[g, h, prev_tile], toep_ref[g, h, 1]], axis=1)
                logits = _dot_t(jnp.where(mine, qp, 0.0).astype(BF16), kk) + bias
                m = jnp.max(logits, axis=1, keepdims=True)
                p = jnp.exp(logits - m)
                stats.append((m, jnp.sum(p, axis=1, keepdims=True), _dot(p.astype(BF16), vv)))
            m_new, l_new, pv = (jnp.where(first, a, b) for a, b in zip(*stats))
            m_old = m_ref[pair, cur, :]
            m_tot = jnp.maximum(m_old, m_new)
            a_old = jnp.exp(m_old - m_tot)
            a_new = jnp.exp(m_new - m_tot)
            m_ref[pair, cur, :] = m_tot
            l_ref[pair, cur, :] = a_old * l_ref[pair, cur, :] + a_new * l_new
            acc_ref[pair, cur, :] = a_old * acc_ref[pair, cur, :] + a_new * pv
        return carry

    lax.fori_loop(0, dil * n_band, unit, 0, unroll=2)


def _get_rows(ref, rows):
    return jnp.concatenate([ref[0, rows, :], ref[1, rows, :]], axis=1)


def _set_rows(ref, rows, val):
    ref[0, rows, :] = val[:, :LANE]
    ref[1, rows, :] = val[:, LANE:]


def _dil_kernel(q_ref, k_ref, v_ref, toep_ref, o_ref, qf_ref, kf_ref, vf_ref, m_ref, l_ref, acc_ref, *, seq):
    g = pl.program_id(1)
    everything = pl.ds(0, seq)
    _set_rows(qf_ref, everything, q_ref[0].astype(F32))
    _set_rows(kf_ref, everything, k_ref[0].astype(F32))
    _set_rows(vf_ref, everything, v_ref[0].astype(F32))

    @pl.when(g == 0)
    def _():
        m_ref[...] = jnp.full(m_ref.shape, NEG_INF, F32)
        l_ref[...] = jnp.zeros_like(l_ref)
        acc_ref[...] = jnp.zeros_like(acc_ref)

    for gi, (_, dil) in enumerate(DIL_PATTERNS):
        pl.when(g == gi)(functools.partial(_dil_group, dil, seq, toep_ref, gi, qf_ref, kf_ref, vf_ref,
                                           m_ref, l_ref, acc_ref))

    @pl.when(g == DIL_GROUPS - 1)
    def _():
        o_ref[0] = (_get_rows(acc_ref, everything) / _get_rows(l_ref, everything)).astype(o_ref.dtype)


def _dil(q, k, v, toep):
    b, seq, _ = q.shape
    group = pl.BlockSpec((1, seq, 256), lambda bi, g: (bi, 0, g))
    state = pltpu.VMEM((2, seq, LANE), F32)
    return pl.pallas_call(
        functools.partial(_dil_kernel, seq=seq),
        grid=(b, DIL_GROUPS),
        in_specs=[group, group, group, _full(toep.shape)],
        out_specs=pl.BlockSpec((1, seq, 256), lambda bi, g: (bi, 0, 0)),
        out_shape=jax.ShapeDtypeStruct((b, seq, 256), BF16),
        scratch_shapes=[state] * 6,
        compiler_params=_params("parallel", "arbitrary"),
        name="dil",
    )(q, k, v, toep)


def _merge_kernel(x_ref, xn_ref, oa_ref, ob_ref, oc_ref, od_ref, wg_ref, bg_ref, wb_ref, wo_ref, xo_ref):
    branches = (oa_ref, ob_ref, oc_ref, od_ref)
    xn = xn_ref[...]
    total = jnp.zeros(x_ref.shape, F32)
    for n in range(N_BRANCH):
        cols = slice(n * D_MODEL, (n + 1) * D_MODEL)
        gate = _sigmoid(_dot(xn, wg_ref[:, cols]) + bg_ref[:, cols])
        total = total + gate * _dot(branches[n][...], wb_ref[n])
    xo_ref[...] = x_ref[...] + _dot(total.astype(BF16), wo_ref[...])


def _merge(x, xn, outs, wg, bg, wb, wo):
    n = x.shape[0]
    rows = lambda width: pl.BlockSpec((ROW_TILE, width), lambda i: (i, 0))
    return pl.pallas_call(
        _merge_kernel,
        grid=(n // ROW_TILE,),
        in_specs=[rows(D_MODEL), rows(D_MODEL)] + [rows(256)] * N_BRANCH
                 + [_full(wg.shape), _full(bg.shape), _full(wb.shape), _full(wo.shape)],
        out_specs=rows(D_MODEL),
        out_shape=jax.ShapeDtypeStruct((n, D_MODEL), F32),
        compiler_params=_params("parallel"),
        name="merge",
    )(x, xn, *outs, wg, bg, wb, wo)


def _bucket_of_distance():
    d = np.arange(REL_MAX_DIST + 1)
    exact = REL_BUCKETS // 2
    nf = np.maximum(d, 1).astype(np.float32)
    log_b = exact + (np.log(nf / exact) / math.log(REL_MAX_DIST / exact) * (REL_BUCKETS - exact)).astype(np.int32)
    return np.where(d < exact, d, np.minimum(log_b, REL_BUCKETS - 1))


def _toeplitz_tiles(table, n_off, dil, max_steps=None):
    span = 2 * LANE
    m = np.arange(span)
    off = np.arange(n_off)[:, None]
    steps = off * LANE + LANE - 1 - m[None, :]
    live = (steps >= 0) if max_steps is None else (steps >= 0) & (steps <= max_steps)
    v = table[:, _bucket_of_distance()[np.clip(steps * dil, 0, REL_MAX_DIST)]]
    v = jnp.where(live[None], v, NEG_INF)
    flat = jnp.tile(v, (1, 1, LANE))[..., :LANE * (span - 1)]
    tiles = flat.reshape(table.shape[0], n_off, LANE, span - 1)[..., LANE - 1:]
    return jnp.concatenate([jnp.full((table.shape[0], 1, LANE, LANE), NEG_INF, F32), tiles], axis=1)


def _same_head(width, head):
    idx = np.arange(width) // head
    return jnp.asarray(idx[:, None] == idx[None, :], BF16)


def _rope_tables(seq):
    half = MLA_ROPE // 2
    freqs = ROPE_THETA ** (-np.arange(half, dtype=np.float64) / half)
    ang = np.arange(seq, dtype=np.float64)[:, None] * freqs[None, :]
    pad = np.zeros((seq, MLA_SLOT - MLA_QK))
    cos_h = np.concatenate([np.ones((seq, MLA_NOPE)), np.cos(ang), np.cos(ang), pad], axis=1)
    sin_h = np.concatenate([np.zeros((seq, MLA_NOPE)), np.sin(ang), np.sin(ang), pad], axis=1)
    return (jnp.asarray(np.tile(cos_h, (1, N_HEADS)), F32), jnp.asarray(np.tile(sin_h, (1, N_HEADS)), F32))


def _rot_half_cols():
    half = MLA_ROPE // 2
    src = np.arange(MLA_SLOT)
    sign = np.zeros(MLA_SLOT, np.float32)
    src[MLA_NOPE:MLA_NOPE + half] = np.arange(MLA_NOPE + half, MLA_QK)
    sign[MLA_NOPE:MLA_NOPE + half] = -1.0
    src[MLA_NOPE + half:MLA_QK] = np.arange(MLA_NOPE, MLA_NOPE + half)
    sign[MLA_NOPE + half:MLA_QK] = 1.0
    return src, sign


def _mixer(x, xn, seq, w_in, b_gate, qk_a, qk_b, qk_c, qk_d, mla_nq, w_uq, mla_nkv, w_ukv,
           w_branch, w_out, toeps, consts):
    n = x.shape[0]
    b = n // seq
    e64, e96, tri, cos, sin = consts
    toep_a, toep_b, toep_d = toeps
    row = lambda a: a.reshape(1, -1)
    tile4 = lambda g: jnp.tile(g, N_HEADS).reshape(1, -1)
    r3 = lambda a: a.reshape(b, seq, a.shape[-1])

    off_iq = OFF_A + 6 * HEAD_DIM
    off_ik = off_iq + N_HEADS * IDX_DIM
    iq_rep = jnp.concatenate([jnp.tile(w_in[:, off_iq + h * IDX_DIM:off_iq + (h + 1) * IDX_DIM], (1, 4))
                              for h in range(N_HEADS)], axis=1)
    ik_rep = jnp.tile(w_in[:, off_ik:off_ik + IDX_DIM], (1, 4))
    iw_pad = jnp.pad(w_in[:, off_ik + IDX_DIM:OFF_A + W_A], ((0, 0), (0, LANE - N_HEADS)))
    half_pad = ((0, 0), (0, LANE - HEAD_DIM))
    q_slots = jnp.concatenate([jnp.pad(w_in[:, OFF_A + h * HEAD_DIM:OFF_A + (h + 1) * HEAD_DIM], half_pad)
                               for h in range(N_HEADS)], axis=1)
    off_k = OFF_A + N_HEADS * HEAD_DIM
    k_slot = jnp.pad(w_in[:, off_k:off_k + HEAD_DIM], half_pad)
    v_twice = jnp.tile(w_in[:, off_k + HEAD_DIM:off_iq], (1, 2))
    wa = jnp.concatenate([q_slots, k_slot, v_twice, iq_rep, ik_rep, iw_pad], axis=1).astype(BF16)
    gq_slots = jnp.tile(jnp.pad(qk_a[0], (0, LANE - HEAD_DIM)), N_HEADS).reshape(1, -1)
    gk_slot = jnp.pad(qk_a[1], (0, LANE - HEAD_DIM)).reshape(1, -1)
    qa, ka, va, iq, ik, iw = _proj_a(xn, wa, e96, gq_slots, gk_slot)
    out_a = _dsa(r3(qa), r3(ka), r3(va), r3(iq), r3(ik), r3(iw), toep_a, tri).reshape(n, 256)

    wb_in = w_in[:, OFF_B:OFF_B + W_B].astype(BF16)
    qb, kb, vb, kmean = _proj_b(xn, wb_in, e64, tile4(qk_b[0]), tile4(qk_b[1]))
    out_b = _moba(r3(qb), r3(kb), r3(vb), kmean.reshape(b, seq // MOBA_BLOCK, BRANCH_WIDTH),
                  toep_b).reshape(n, 256)

    src, sign = _rot_half_cols()
    src4 = np.concatenate([h * MLA_SLOT + src for h in range(N_HEADS)])
    sign4 = jnp.asarray(np.tile(sign, N_HEADS))
    slot_pad = MLA_SLOT - MLA_QK
    w_kr = w_in[:, OFF_C + MLA_Q_LORA + MLA_KV_LORA:OFF_C + W_C]
    kr_slots = jnp.pad(w_kr, ((0, 0), (MLA_NOPE, slot_pad)))
    kr_slots = jnp.tile(kr_slots, (1, N_HEADS))
    wc = jnp.concatenate([w_in[:, OFF_C:OFF_C + MLA_Q_LORA + MLA_KV_LORA], kr_slots,
                          kr_slots[:, src4] * sign4], axis=1).astype(BF16)
    slots = lambda w: jnp.pad(w.reshape(w.shape[0], N_HEADS, -1),
                              ((0, 0), (0, 0), (0, MLA_SLOT - w.shape[1] // N_HEADS))).reshape(w.shape[0], -1)
    wuq = slots(w_uq)
    ukv = w_ukv.reshape(MLA_KV_LORA, N_HEADS, MLA_NOPE + MLA_V)
    wuk = slots(ukv[:, :, :MLA_NOPE].reshape(MLA_KV_LORA, -1))
    wuv = ukv[:, :, MLA_NOPE:].reshape(MLA_KV_LORA, -1)
    gq = jnp.tile(jnp.pad(qk_c[0], (0, slot_pad)), N_HEADS)
    gk = jnp.tile(jnp.pad(qk_c[1], (0, slot_pad)), N_HEADS)
    qc, kc, vc = _proj_c(xn, wc, e96, row(mla_nq), row(mla_nkv), wuq.astype(BF16),
                         (wuq[:, src4] * sign4).astype(BF16), wuk.astype(BF16), wuv.astype(BF16),
                         row(gq), row(gq[src4]), row(gk), row(gk[src4]), cos, sin, seq)
    out_c = _mla(r3(qc), r3(kc), r3(vc)).reshape(n, 256)

    wd = w_in[:, OFF_D:OFF_D + W_D].astype(BF16)
    qd, kd, vd = _proj_d(xn, wd, e64, tile4(qk_d[0]), tile4(qk_d[1]))
    out_d = _dil(r3(qd), r3(kd), r3(vd), toep_d).reshape(n, 256)

    wg = w_in[:, OFF_G:OFF_G + W_G].astype(BF16)
    return _merge(x, xn, (out_a, out_b, out_c, out_d), wg, b_gate.reshape(1, -1),
                  w_branch.astype(BF16), w_out.astype(BF16))


def kernel(x, norm_gain, w_in, b_gate, qk_gain_a, qk_gain_b, qk_gain_c, qk_gain_d, mla_norm_q,
           w_mla_uq, mla_norm_kv, w_mla_ukv, w_branch, w_out, rel_bias, w_ffn_in, w_ffn_out):
    b, seq, d = x.shape
    depth = norm_gain.shape[0]
    toep_a = _toeplitz_tiles(rel_bias[0:4], seq // LANE, 1)
    toep_b = _toeplitz_tiles(rel_bias[4:8], seq // LANE, 1)
    toep_d = jnp.stack([_toeplitz_tiles(rel_bias[8 + 4 * g:12 + 4 * g], 2, dil, max_steps=window // dil)
                        for g, (window, dil) in enumerate(DIL_PATTERNS)])
    tri = jnp.asarray(np.arange(LANE)[:, None] < np.arange(LANE)[None, :], BF16)
    consts = (_same_head(256, HEAD_DIM), _same_head(N_HEADS * MLA_SLOT, MLA_SLOT), tri) + _rope_tables(seq)

    x = x.reshape(b * seq, d)
    for l in range(depth):
        g = norm_gain[l]
        x, xn = _ffn(x, g[0:1], g[1:2], w_ffn_in[l, 0].astype(BF16), w_ffn_out[l, 0].astype(BF16))
        x = _mixer(x, xn, seq, w_in[l], b_gate[l], qk_gain_a[l], qk_gain_b[l], qk_gain_c[l],
                   qk_gain_d[l], mla_norm_q[l], w_mla_uq[l], mla_norm_kv[l], w_mla_ukv[l],
                   w_branch[l], w_out[l], (toep_a, toep_b, toep_d), consts)
        x, _ = _ffn(x, g[2:3], g[2:3], w_ffn_in[l, 1].astype(BF16), w_ffn_out[l, 1].astype(BF16))
    return x.reshape(b, seq, d)
```

```python
import functools
import math

import numpy as np
import jax
import jax.numpy as jnp
from jax import lax
from jax.experimental import pallas as pl
from jax.experimental.pallas import tpu as pltpu

D_MODEL = 1024
HEAD_DIM = 64
N_HEADS = 4
BRANCH_WIDTH = N_HEADS * HEAD_DIM
IDX_DIM = 32
DSA_TOPK = 256
MOBA_BLOCK = 256
MOBA_TOPK = 3
MLA_Q_LORA = 384
MLA_KV_LORA = 256
MLA_NOPE = 64
MLA_ROPE = 32
MLA_QK = MLA_NOPE + MLA_ROPE
MLA_V = 64
MLA_SLOT = 128
ROPE_THETA = 10000.0
DIL_PATTERNS = ((128, 1), (512, 4), (2048, 16))
DIL_GROUPS = 3
N_BRANCH = 4
D_FF = 2816
REL_BUCKETS = 32
REL_MAX_DIST = 2048
RMS_EPS = 1e-6

OFF_A = 0
W_A = 4 * HEAD_DIM + 2 * HEAD_DIM + 4 * IDX_DIM + IDX_DIM + 4
OFF_B = OFF_A + W_A
W_B = 3 * BRANCH_WIDTH
OFF_C = OFF_B + W_B
W_C = MLA_Q_LORA + MLA_KV_LORA + MLA_ROPE
OFF_D = OFF_C + W_C
W_D = 3 * DIL_GROUPS * BRANCH_WIDTH
OFF_G = OFF_D + W_D
W_G = N_BRANCH * D_MODEL

LANE = 128
ROW_TILE = 512
FF_CHUNK = 1408
WIDTH_CLASSES = 8
ATT_ROWS = 256
ATT_WIDTH_CLASSES = 4
DSA_WIDTH_CLASSES = 4
DSA_ROWS = 256
MASKED = -1e30
VMEM_LIMIT = 56 * 1024 * 1024

F32 = jnp.float32
BF16 = jnp.bfloat16
NEG_INF = float("-inf")
INT_MIN = -2 ** 31

_CONTRACT_LAST = (((1,), (1,)), ((), ()))


def _dot(a, b, precision=None):
    return jnp.dot(a, b, preferred_element_type=F32, precision=precision)


def _dot_t(a, b, precision=None):
    return lax.dot_general(a, b, _CONTRACT_LAST, preferred_element_type=F32, precision=precision)


def _rms(x, gain):
    return x * lax.rsqrt(jnp.mean(x * x, axis=-1, keepdims=True) + RMS_EPS) * gain


def _sigmoid(x):
    return 1.0 / (1.0 + jnp.exp(-x))


def _head_sumsq(y, e_ref):
    y2 = y * y
    hi = y2.astype(BF16)
    lo = (y2 - hi.astype(F32)).astype(BF16)
    e = e_ref[...]
    return _dot(hi, e) + _dot(lo, e)


def _head_norm(y, e_ref, gain, width):
    return y * lax.rsqrt(_head_sumsq(y, e_ref) * (1.0 / width) + RMS_EPS) * gain


def _params(*sem):
    return pltpu.CompilerParams(dimension_semantics=sem, vmem_limit_bytes=VMEM_LIMIT)


def _full(shape):
    return pl.BlockSpec(shape, lambda *_: (0,) * len(shape))


def _resident(shape):
    return pl.BlockSpec(shape, lambda *_: (0,) * len(shape), pipeline_mode=pl.Buffered(1))


def _ffn_kernel(x_ref, g_ref, gn_ref, wi_ref, wo_ref, xo_ref, xn_ref):
    half = ROW_TILE // 2
    for s in range(2):
        rows = slice(s * half, (s + 1) * half)
        x = x_ref[rows, :]
        xb = _rms(x, g_ref[...]).astype(BF16)
        acc = None
        for j in range(D_FF // FF_CHUNK):
            gate = _dot(xb, wi_ref[:, j * FF_CHUNK:(j + 1) * FF_CHUNK])
            up = _dot(xb, wi_ref[:, D_FF + j * FF_CHUNK:D_FF + (j + 1) * FF_CHUNK])
            h = (gate * _sigmoid(gate) * up).astype(BF16)
            part = _dot(h, wo_ref[j * FF_CHUNK:(j + 1) * FF_CHUNK, :])
            acc = part if acc is None else acc + part
        xo = x + 0.5 * acc
        xo_ref[rows, :] = xo
        xn_ref[rows, :] = _rms(xo, gn_ref[...]).astype(BF16)


def _ffn(x, gain, gain_next, w_in, w_out):
    n = x.shape[0]
    rows = pl.BlockSpec((ROW_TILE, D_MODEL), lambda i: (i, 0))
    return pl.pallas_call(
        _ffn_kernel,
        grid=(n // ROW_TILE,),
        in_specs=[rows, _full((1, D_MODEL)), _full((1, D_MODEL)), _resident(w_in.shape), _resident(w_out.shape)],
        out_specs=[rows, rows],
        out_shape=[jax.ShapeDtypeStruct((n, D_MODEL), F32), jax.ShapeDtypeStruct((n, D_MODEL), BF16)],
        compiler_params=_params("parallel"),
        name="ffn",
    )(x, gain, gain_next, w_in, w_out)


def _split_terms(rep, hi_lanes):
    hi = rep.astype(BF16).astype(F32)
    return jnp.where(hi_lanes, hi, rep - hi).astype(BF16)


def _with_ones(v):
    first = (lax.broadcasted_iota(jnp.int32, (1, v.shape[1]), 1) & (LANE - 1)) < HEAD_DIM
    return jnp.where(first, v, 1.0).astype(BF16), jnp.where(first, 1.0, v).astype(BF16)


def _proj_a_kernel(xn_ref, w_ref, e_ref, gq_ref, gk_ref, q_ref, k_ref, va_ref, vb_ref, iq_ref, ik_ref, iw_ref):
    p = _dot(xn_ref[...], w_ref[...])
    q = _head_norm(p[:, :512], e_ref, gq_ref[...], HEAD_DIM) * HEAD_DIM ** -0.5
    q_ref[...] = q.astype(BF16)
    k = p[:, 512:640]
    k_ms = jnp.sum(k * k, axis=-1, keepdims=True) * (1.0 / HEAD_DIM)
    k_ref[...] = (k * lax.rsqrt(k_ms + RMS_EPS) * gk_ref[...]).astype(BF16)
    va_ref[...], vb_ref[...] = _with_ones(p[:, 640:768])
    lane = lax.broadcasted_iota(jnp.int32, (1, 4 * LANE), 1)
    copy = lax.shift_right_logical(lane, int(math.log2(IDX_DIM))) & 3
    iq_ref[...] = _split_terms(p[:, 768:1280], copy < 2)
    ik_ref[...] = _split_terms(p[:, 1280:1408], (copy[:, :LANE] & 1) == 0)
    iw_ref[...] = p[:, 1408:1536]


def _proj_a(xn, w, e_slots, gq, gk):
    n = xn.shape[0]
    rows = lambda width: pl.BlockSpec((ROW_TILE, width), lambda i: (i, 0))
    return pl.pallas_call(
        _proj_a_kernel,
        grid=(n // ROW_TILE,),
        in_specs=[rows(D_MODEL), _full(w.shape), _full(e_slots.shape), _full(gq.shape), _full(gk.shape)],
        out_specs=[rows(512), rows(128), rows(128), rows(128), rows(512), rows(128), rows(128)],
        out_shape=[jax.ShapeDtypeStruct((n, 512), BF16), jax.ShapeDtypeStruct((n, 128), BF16),
                   jax.ShapeDtypeStruct((n, 128), BF16), jax.ShapeDtypeStruct((n, 128), BF16),
                   jax.ShapeDtypeStruct((n, 512), BF16),
                   jax.ShapeDtypeStruct((n, 128), BF16), jax.ShapeDtypeStruct((n, 128), F32)],
        compiler_params=_params("parallel"),
        name="proj_a",
    )(xn, w, e_slots, gq, gk)


def _proj_b_kernel(xn_ref, w_ref, e_ref, gq_ref, gk_ref, q_ref, k_ref, va_ref, vb_ref, km_ref):
    p = _dot(xn_ref[...], w_ref[...])
    q_ref[...] = _head_norm(p[:, :256], e_ref, gq_ref[...], HEAD_DIM)
    k = _head_norm(p[:, 256:512], e_ref, gk_ref[...], HEAD_DIM)
    k_ref[...] = k.astype(BF16)
    va_ref[...], vb_ref[...] = _with_ones(p[:, 512:768])
    km = jnp.mean(k.reshape(ROW_TILE // MOBA_BLOCK, MOBA_BLOCK, BRANCH_WIDTH), axis=1)
    for j in range(ROW_TILE // MOBA_BLOCK):
        km_ref[j] = km[j:j + 1]


def _proj_b(xn, w, e64, gq, gk):
    n = xn.shape[0]
    rows = lambda width: pl.BlockSpec((ROW_TILE, width), lambda i: (i, 0))
    per_tile = ROW_TILE // MOBA_BLOCK
    return pl.pallas_call(
        _proj_b_kernel,
        grid=(n // ROW_TILE,),
        in_specs=[rows(D_MODEL), _full(w.shape), _full(e64.shape), _full(gq.shape), _full(gk.shape)],
        out_specs=[rows(256), rows(256), rows(256), rows(256),
                   pl.BlockSpec((per_tile, 1, BRANCH_WIDTH), lambda i: (i, 0, 0))],
        out_shape=[jax.ShapeDtypeStruct((n, 256), F32), jax.ShapeDtypeStruct((n, 256), BF16),
                   jax.ShapeDtypeStruct((n, 256), BF16), jax.ShapeDtypeStruct((n, 256), BF16),
                   jax.ShapeDtypeStruct((n // MOBA_BLOCK, 1, BRANCH_WIDTH), F32)],
        compiler_params=_params("parallel"),
        name="proj_b",
    )(xn, w, e64, gq, gk)


def _proj_c_kernel(xn_ref, w_ref, e_ref, nq_ref, nkv_ref, wuq_ref, wuqs_ref, wuk_ref, wuv_ref,
                   gq_ref, gqs_ref, gk_ref, gks_ref, cos_ref, sin_ref, q_ref, k_ref, va_ref, vb_ref):
    p = _dot(xn_ref[...], w_ref[...])
    cos = cos_ref[...]
    sin = sin_ref[...]
    xq = _rms(p[:, :384], nq_ref[...]).astype(BF16)
    qa = _dot(xq, wuq_ref[...])
    qs = _dot(xq, wuqs_ref[...])
    rq = lax.rsqrt(_head_sumsq(qa, e_ref) * (1.0 / MLA_QK) + RMS_EPS)
    q_ref[...] = (rq * (qa * gq_ref[...] * cos + qs * gqs_ref[...] * sin)).astype(BF16)
    xkv = _rms(p[:, 384:640], nkv_ref[...]).astype(BF16)
    ka = _dot(xkv, wuk_ref[...]) + p[:, 640:1152]
    ks = p[:, 1152:1664]
    rk = lax.rsqrt(_head_sumsq(ka, e_ref) * (1.0 / MLA_QK) + RMS_EPS)
    k_ref[...] = (rk * (ka * gk_ref[...] * cos + ks * gks_ref[...] * sin)).astype(BF16)
    va_ref[...], vb_ref[...] = _with_ones(_dot(xkv, wuv_ref[...]))


def _proj_c(xn, w, e96, nq, nkv, wuq, wuqs, wuk, wuv, gq, gqs, gk, gks, cos, sin, seq):
    n = xn.shape[0]
    rows = lambda width: pl.BlockSpec((ROW_TILE, width), lambda i: (i, 0))
    pos = pl.BlockSpec((ROW_TILE, N_HEADS * MLA_SLOT), lambda i: (i % (seq // ROW_TILE), 0))
    consts = (w, e96, nq, nkv, wuq, wuqs, wuk, wuv, gq, gqs, gk, gks)
    return pl.pallas_call(
        _proj_c_kernel,
        grid=(n // ROW_TILE,),
        in_specs=[rows(D_MODEL)] + [_full(c.shape) for c in consts] + [pos, pos],
        out_specs=[rows(N_HEADS * MLA_SLOT), rows(N_HEADS * MLA_SLOT), rows(256), rows(256)],
        out_shape=[jax.ShapeDtypeStruct((n, N_HEADS * MLA_SLOT), BF16),
                   jax.ShapeDtypeStruct((n, N_HEADS * MLA_SLOT), BF16),
                   jax.ShapeDtypeStruct((n, 256), BF16), jax.ShapeDtypeStruct((n, 256), BF16)],
        compiler_params=_params("parallel"),
        name="proj_c",
    )(xn, *consts, cos, sin)


def _proj_d_kernel(xn_ref, w_ref, e_ref, gq_ref, gk_ref, q_ref, k_ref, v_ref):
    xn = xn_ref[...]
    for c in range(DIL_GROUPS):
        lo, hi = c * 256, (c + 1) * 256
        pq = _dot(xn, w_ref[:, lo:hi])
        q_ref[:, lo:hi] = (_head_norm(pq, e_ref, gq_ref[...], HEAD_DIM) * HEAD_DIM ** -0.5).astype(BF16)
        pk = _dot(xn, w_ref[:, 768 + lo:768 + hi])
        k_ref[:, lo:hi] = _head_norm(pk, e_ref, gk_ref[...], HEAD_DIM).astype(BF16)
        v_ref[:, lo:hi] = _dot(xn, w_ref[:, 1536 + lo:1536 + hi]).astype(BF16)


def _proj_d(xn, w, e64, gq, gk):
    n = xn.shape[0]
    rows = lambda width: pl.BlockSpec((ROW_TILE, width), lambda i: (i, 0))
    return pl.pallas_call(
        _proj_d_kernel,
        grid=(n // ROW_TILE,),
        in_specs=[rows(D_MODEL), _full(w.shape), _full(e64.shape), _full(gq.shape), _full(gk.shape)],
        out_specs=[rows(768), rows(768), rows(768)],
        out_shape=[jax.ShapeDtypeStruct((n, 768), BF16)] * 3,
        compiler_params=_params("parallel"),
        name="proj_d",
    )(xn, w, e64, gq, gk)


def _for_causal_width(qi, n_qtiles, tile_rows, body, classes=WIDTH_CLASSES):
    n_cls = min(classes, n_qtiles)
    per = n_qtiles // n_cls
    for c in range(n_cls):
        pl.when((qi >= c * per) & (qi < (c + 1) * per))(functools.partial(body, (c + 1) * per * tile_rows))


def _bias_rows(toep_ref, h, qblk, nchunk):
    return jnp.concatenate([toep_ref[h, jnp.maximum(qblk - c + 1, 0)] for c in range(nchunk)], axis=1)


def _bias_tile(toep_ref, h, qi, sub_blocks, nchunk):
    return jnp.concatenate([_bias_rows(toep_ref, h, qi * sub_blocks + j, nchunk) for j in range(sub_blocks)],
                           axis=0)


def _softmax_pv(logits, v):
    m = jnp.max(logits, axis=1, keepdims=True)
    p = jnp.exp(logits - m)
    l = jnp.sum(p, axis=1, keepdims=True)
    return _dot(p.astype(BF16), v) / l


def _exp_pv(logits, v_ones):
    m = jnp.max(logits, axis=1, keepdims=True)
    return _dot(jnp.exp(logits - m).astype(BF16), v_ones)


def _pair_output(first_head, second_head):
    first = lax.broadcasted_iota(jnp.int32, first_head.shape, 1) < HEAD_DIM
    numer = jnp.where(first, first_head, second_head)
    denom = jnp.where(first, pltpu.roll(first_head, HEAD_DIM, axis=1), pltpu.roll(second_head, HEAD_DIM, axis=1))
    return numer / denom


def _dsa_body(width, qi, q_ref, k_ref, va_ref, vb_ref, iq_ref, ik_ref, iw_ref, toep_ref, tri_ref, o_ref,
              sel_ref, n_sel):
    rows = DSA_ROWS
    nchunk = width // LANE
    ik = ik_ref[0, :width, :]
    iw = iw_ref[0][:, :N_HEADS] * (N_HEADS ** -0.5 * IDX_DIM ** -0.5)

    score = None
    for h in range(N_HEADS):
        r = _dot_t(iq_ref[0, :, h * LANE:(h + 1) * LANE], ik)
        term = jnp.maximum(r, 0.0) * iw[:, h:h + 1]
        score = term if score is None else score + term

    t = qi * rows + lax.broadcasted_iota(jnp.int32, (rows, width), 0)
    s = lax.broadcasted_iota(jnp.int32, (rows, width), 1)

    bits = pltpu.bitcast(score, jnp.int32)
    key = jnp.where(bits < 0, jnp.int32(INT_MIN) - bits, bits)
    key = jnp.where(s <= t, key, jnp.int32(INT_MIN))

    half_min = -2 ** 15
    upper = lax.shift_right_arithmetic(key, 16).astype(jnp.int16)
    lower = ((key & jnp.int32(0xFFFF)) + half_min).astype(jnp.int16)

    def count(hit):
        ones = jnp.where(hit, jnp.int16(1), jnp.int16(0))
        acc = ones[:, :LANE]
        for c in range(1, nchunk):
            acc = acc + ones[:, c * LANE:(c + 1) * LANE]
        return jnp.sum(acc.astype(F32), axis=1, keepdims=True)

    def search(half, need):
        def step(i, ans):
            cand = ans | lax.shift_left(jnp.int32(1), 15 - i)
            cnt = count(half >= (cand + half_min).astype(jnp.int16))
            return jnp.where(cnt >= need, cand, ans)
        return lax.fori_loop(0, 16, step, jnp.zeros((rows, 1), jnp.int32))

    thr_upper = search(upper, n_sel) + half_min
    thr_upper16 = thr_upper.astype(jnp.int16)
    n_over = count(upper > thr_upper16)
    candidates = jnp.where(upper == thr_upper16, lower, jnp.int16(half_min))
    thr = thr_upper * 65536 + search(candidates, n_sel - n_over)

    sel = jnp.where(key >= jnp.maximum(thr, jnp.int32(INT_MIN + 1)), 1.0, 0.0)
    sel_ref[:, :width] = sel

    overflow = jnp.sum(sel, axis=1, keepdims=True) > n_sel

    @pl.when(jnp.max(jnp.where(overflow, 1.0, 0.0)) > 0.0)
    def _():
        above = key > thr
        room = n_sel - jnp.sum(jnp.where(above, 1.0, 0.0), axis=1, keepdims=True)
        before = jnp.zeros((rows, 1), F32)
        for c in range(nchunk):
            cols = slice(c * LANE, (c + 1) * LANE)
            tc = jnp.where(key[:, cols] == thr, 1.0, 0.0)
            rank = before + _dot(tc.astype(BF16), tri_ref[...])
            keep = above[:, cols] | ((tc > 0.0) & (rank < room))
            sel_ref[:, cols] = jnp.where(keep, 1.0, 0.0)
            before = before + jnp.sum(tc, axis=1, keepdims=True)

    unselected = (sel_ref[:, :width] - 1.0) * -MASKED
    k = k_ref[0, :width, :]
    for pair in range(N_HEADS // 2):
        outs = []
        for h, v_ref in ((2 * pair, va_ref), (2 * pair + 1, vb_ref)):
            logits = (_dot_t(q_ref[0, :, h * LANE:(h + 1) * LANE], k)
                      + _bias_tile(toep_ref, h, qi, rows // LANE, nchunk) + unselected)
            outs.append(_exp_pv(logits, v_ref[0, :width, :]))
        o_ref[0, :, pair * LANE:(pair + 1) * LANE] = _pair_output(*outs).astype(o_ref.dtype)


def _dsa_kernel(*refs, seq, n_sel):
    qi = pl.program_id(1)
    _for_causal_width(qi, seq // DSA_ROWS, DSA_ROWS,
                      functools.partial(_dsa_body, qi=qi, n_sel=n_sel, **_named(refs)), classes=DSA_WIDTH_CLASSES)


def _named(refs):
    names = ("q_ref", "k_ref", "va_ref", "vb_ref", "iq_ref", "ik_ref", "iw_ref", "toep_ref", "tri_ref", "o_ref",
             "sel_ref")
    return dict(zip(names, refs, strict=True))


def _dsa(q, k, va, vb, iq, ik, iw, toep, tri):
    b, seq, _ = q.shape
    n_sel = min(DSA_TOPK, seq // 4)
    qblock = lambda width: pl.BlockSpec((1, DSA_ROWS, width), lambda bi, qi: (bi, qi, 0))
    whole = lambda width: pl.BlockSpec((1, seq, width), lambda bi, qi: (bi, 0, 0))
    return pl.pallas_call(
        functools.partial(_dsa_kernel, seq=seq, n_sel=n_sel),
        grid=(b, seq // DSA_ROWS),
        in_specs=[qblock(512), whole(128), whole(128), whole(128), qblock(512), whole(128), qblock(128),
                  _full(toep.shape), _full(tri.shape)],
        out_specs=qblock(256),
        out_shape=jax.ShapeDtypeStruct((b, seq, 256), BF16),
        scratch_shapes=[pltpu.VMEM((DSA_ROWS, seq), F32)],
        compiler_params=_params("parallel", "parallel"),
        name="dsa",
    )(q, k, va, vb, iq, ik, iw, toep, tri)


def _moba_body(width, qi, q_ref, k_ref, va_ref, vb_ref, km_ref, toep_ref, spread_ref, o_ref, nblk):
    nchunk = width // LANE
    n_sel = min(MOBA_TOPK, nblk - 1)
    rows = ATT_ROWS
    own = lax.shift_right_logical(qi * rows, int(math.log2(MOBA_BLOCK)))
    q = q_ref[0]

    km = km_ref[0]
    head_of_col = lax.shift_right_logical(lax.broadcasted_iota(jnp.int32, (nblk, BRANCH_WIDTH), 1),
                                          int(math.log2(HEAD_DIM)))
    per_head = [jnp.where(head_of_col == h, km, 0.0) for h in range(N_HEADS)]
    pad = jnp.zeros((LANE - N_HEADS * nblk, BRANCH_WIDTH), F32)
    gate = _dot_t(q, jnp.concatenate(per_head + [pad], axis=0), precision=lax.Precision.HIGHEST)

    lane = lax.broadcasted_iota(jnp.int32, (rows, LANE), 1)
    blk = lane & (nblk - 1)
    past = (blk < own) & (lane < N_HEADS * nblk)
    gate = jnp.where(past, gate, NEG_INF)
    rank = jnp.zeros((rows, LANE), F32)
    for d in range(1, nblk):
        before = pltpu.roll(gate, d, axis=1)
        rank = rank + jnp.where(blk >= d, jnp.where(before >= gate, 1.0, 0.0), 0.0)
        after = pltpu.roll(gate, LANE - d, axis=1)
        rank = rank + jnp.where(blk < nblk - d, jnp.where(after > gate, 1.0, 0.0), 0.0)
    picked = jnp.where(past, jnp.where(rank < n_sel, 1.0, 0.0), jnp.where(blk == own, 1.0, 0.0)).astype(BF16)

    first = lane < HEAD_DIM
    for pair in range(N_HEADS // 2):
        ps = slice(pair * LANE, (pair + 1) * LANE)
        qp = q[:, ps] * HEAD_DIM ** -0.5
        outs = []
        for mine, v_ref in ((first, va_ref), (~first, vb_ref)):
            h = 2 * pair + len(outs)
            unpicked = (_dot(picked, spread_ref[h, :, :width]) - 1.0) * -MASKED
            logits = (_dot_t(jnp.where(mine, qp, 0.0).astype(BF16), k_ref[0, :width, ps])
                      + _bias_tile(toep_ref, h, qi, rows // LANE, nchunk) + unpicked)
            outs.append(_exp_pv(logits, v_ref[0, :width, ps]))
        o_ref[0, :, ps] = _pair_output(*outs).astype(o_ref.dtype)


def _moba_kernel(q_ref, k_ref, va_ref, vb_ref, km_ref, toep_ref, spread_ref, o_ref, *, seq):
    qi = pl.program_id(1)
    body = functools.partial(_moba_body, qi=qi, q_ref=q_ref, k_ref=k_ref, va_ref=va_ref, vb_ref=vb_ref,
                             km_ref=km_ref, toep_ref=toep_ref, spread_ref=spread_ref, o_ref=o_ref,
                             nblk=seq // MOBA_BLOCK)
    _for_causal_width(qi, seq // ATT_ROWS, ATT_ROWS, body, classes=ATT_WIDTH_CLASSES)


def _moba_spread(seq):
    nblk = seq // MOBA_BLOCK
    row = np.arange(LANE)[None, :, None]
    h = np.arange(N_HEADS)[:, None, None]
    blk_of_key = (np.arange(seq) // MOBA_BLOCK)[None, None, :]
    return jnp.asarray(row == h * nblk + blk_of_key, BF16)


def _moba(q, k, va, vb, k_mean, toep):
    b, seq, _ = q.shape
    nblk = seq // MOBA_BLOCK
    if nblk & (nblk - 1) or N_HEADS * nblk > LANE:
        raise ValueError("MoBA lane layout needs a power-of-two block count with heads*blocks <= 128")
    spread = _moba_spread(seq)
    qblock = pl.BlockSpec((1, ATT_ROWS, 256), lambda bi, qi: (bi, qi, 0))
    whole = pl.BlockSpec((1, seq, 256), lambda bi, qi: (bi, 0, 0))
    means = pl.BlockSpec((1, nblk, 256), lambda bi, qi: (bi, 0, 0))
    return pl.pallas_call(
        functools.partial(_moba_kernel, seq=seq),
        grid=(b, seq // ATT_ROWS),
        in_specs=[qblock, whole, whole, whole, means, _full(toep.shape), _full(spread.shape)],
        out_specs=qblock,
        out_shape=jax.ShapeDtypeStruct((b, seq, 256), BF16),
        compiler_params=_params("parallel", "parallel"),
        name="moba",
    )(q, k, va, vb, k_mean, toep, spread)


def _mla_body(width, qi, q_ref, k_ref, va_ref, vb_ref, causal_ref, o_ref):
    nchunk = width // LANE
    sub = ATT_ROWS // LANE
    causal = jnp.concatenate(
        [jnp.concatenate([causal_ref[0, jnp.clip(qi * sub + j - c + 1, 0, 2)] for c in range(nchunk)], axis=1)
         for j in range(sub)], axis=0)
    for pair in range(N_HEADS // 2):
        ps = slice(pair * LANE, (pair + 1) * LANE)
        outs = []
        for h, v_ref in ((2 * pair, va_ref), (2 * pair + 1, vb_ref)):
            qs = slice(h * MLA_SLOT, (h + 1) * MLA_SLOT)
            logits = _dot_t(q_ref[0, :, qs], k_ref[0, :width, qs]) * MLA_QK ** -0.5 + causal
            outs.append(_exp_pv(logits, v_ref[0, :width, ps]))
        o_ref[0, :, ps] = _pair_output(*outs).astype(o_ref.dtype)


def _mla_kernel(q_ref, k_ref, va_ref, vb_ref, causal_ref, o_ref, *, seq):
    qi = pl.program_id(1)
    body = functools.partial(_mla_body, qi=qi, q_ref=q_ref, k_ref=k_ref, va_ref=va_ref, vb_ref=vb_ref,
                             causal_ref=causal_ref, o_ref=o_ref)
    _for_causal_width(qi, seq // ATT_ROWS, ATT_ROWS, body, classes=ATT_WIDTH_CLASSES)


def _mla(q, k, va, vb):
    b, seq, _ = q.shape
    tri = np.where(np.arange(LANE)[:, None] >= np.arange(LANE)[None, :], 0.0, NEG_INF)
    causal = jnp.asarray(np.stack([np.full((LANE, LANE), NEG_INF), tri, np.zeros((LANE, LANE))])[None], F32)
    return pl.pallas_call(
        functools.partial(_mla_kernel, seq=seq),
        grid=(b, seq // ATT_ROWS),
        in_specs=[pl.BlockSpec((1, ATT_ROWS, N_HEADS * MLA_SLOT), lambda bi, qi: (bi, qi, 0)),
                  pl.BlockSpec((1, seq, N_HEADS * MLA_SLOT), lambda bi, qi: (bi, 0, 0)),
                  pl.BlockSpec((1, seq, 256), lambda bi, qi: (bi, 0, 0)),
                  pl.BlockSpec((1, seq, 256), lambda bi, qi: (bi, 0, 0)),
                  _full(causal.shape)],
        out_specs=pl.BlockSpec((1, ATT_ROWS, 256), lambda bi, qi: (bi, qi, 0)),
        out_shape=jax.ShapeDtypeStruct((b, seq, 256), BF16),
        compiler_params=_params("parallel", "parallel"),
        name="mla",
    )(q, k, va, vb, causal)


def _dil_group(dil, seq, toep_ref, g, qf_ref, kf_ref, vf_ref, m_ref, l_ref, acc_ref):
    n_band = seq // (dil * LANE)

    def unit(u, carry):
        r = lax.div(u, n_band)
        n = lax.rem(u, n_band)
        if dil == 1:
            cur = pl.ds(pl.multiple_of(n * LANE, LANE), LANE)
            prev = pl.ds(pl.multiple_of(jnp.maximum(n - 1, 0) * LANE, LANE), LANE)
        else:
            cur = pl.ds(n * (LANE * dil) + r, LANE, stride=dil)
            prev = pl.ds(jnp.maximum(n - 1, 0) * (LANE * dil) + r, LANE, stride=dil)
        prev_tile = jnp.where(n > 0, 2, 0)
        first = lax.broadcasted_iota(jnp.int32, (LANE, LANE), 1) < HEAD_DIM
        for pair in range(N_HEADS // 2):
            qp = qf_ref[pair, cur, :]
            kk = jnp.concatenate([kf_ref[pair, prev, :], kf_ref[pair, cur, :]], axis=0).astype(BF16)
            vv = jnp.concatenate([vf_ref[pair, prev, :], vf_ref[pair, cur, :]], axis=0).astype(BF16)
            stats = []
            for mine in (first, ~first):
                h = 2 * pair + len(stats)
                bias = jnp.concatenate([toep_ref[g, h, prev_tile], toep_ref[g, h, 1]], axis=1)
                logits = _dot_t(jnp.where(mine, qp, 0.0).astype(BF16), kk) + bias
                m = jnp.max(logits, axis=1, keepdims=True)
                p = jnp.exp(logits - m)
                stats.append((m, jnp.sum(p, axis=1, keepdims=True), _dot(p.astype(BF16), vv)))
            m_new, l_new, pv = (jnp.where(first, a, b) for a, b in zip(*stats))
            m_old = m_ref[pair, cur, :]
            m_tot = jnp.maximum(m_old, m_new)
            a_old = jnp.exp(m_old - m_tot)
            a_new = jnp.exp(m_new - m_tot)
            m_ref[pair, cur, :] = m_tot
            l_ref[pair, cur, :] = a_old * l_ref[pair, cur, :] + a_new * l_new
            acc_ref[pair, cur, :] = a_old * acc_ref[pair, cur, :] + a_new * pv
        return carry

    lax.fori_loop(0, dil * n_band, unit, 0, unroll=2)


def _get_rows(ref, rows):
    return jnp.concatenate([ref[0, rows, :], ref[1, rows, :]], axis=1)


def _set_rows(ref, rows, val):
    ref[0, rows, :] = val[:, :LANE]
    ref[1, rows, :] = val[:, LANE:]


def _dil_kernel(q_ref, k_ref, v_ref, toep_ref, o_ref, qf_ref, kf_ref, vf_ref, m_ref, l_ref, acc_ref, *, seq):
    g = pl.program_id(1)
    everything = pl.ds(0, seq)
    _set_rows(qf_ref, everything, q_ref[0].astype(F32))
    _set_rows(kf_ref, everything, k_ref[0].astype(F32))
    _set_rows(vf_ref, everything, v_ref[0].astype(F32))

    @pl.when(g == 0)
    def _():
        m_ref[...] = jnp.full(m_ref.shape, NEG_INF, F32)
        l_ref[...] = jnp.zeros_like(l_ref)
        acc_ref[...] = jnp.zeros_like(acc_ref)

    for gi, (_, dil) in enumerate(DIL_PATTERNS):
        pl.when(g == gi)(functools.partial(_dil_group, dil, seq, toep_ref, gi, qf_ref, kf_ref, vf_ref,
                                           m_ref, l_ref, acc_ref))

    @pl.when(g == DIL_GROUPS - 1)
    def _():
        o_ref[0] = (_get_rows(acc_ref, everything) / _get_rows(l_ref, everything)).astype(o_ref.dtype)


def _dil(q, k, v, toep):
    b, seq, _ = q.shape
    group = pl.BlockSpec((1, seq, 256), lambda bi, g: (bi, 0, g))
    state = pltpu.VMEM((2, seq, LANE), F32)
    return pl.pallas_call(
        functools.partial(_dil_kernel, seq=seq),
        grid=(b, DIL_GROUPS),
        in_specs=[group, group, group, _full(toep.shape)],
        out_specs=pl.BlockSpec((1, seq, 256), lambda bi, g: (bi, 0, 0)),
        out_shape=jax.ShapeDtypeStruct((b, seq, 256), BF16),
        scratch_shapes=[state] * 6,
        compiler_params=_params("parallel", "arbitrary"),
        name="dil",
    )(q, k, v, toep)


def _merge_kernel(x_ref, xn_ref, oa_ref, ob_ref, oc_ref, od_ref, wg_ref, bg_ref, wb_ref, wo_ref, xo_ref):
    branches = (oa_ref, ob_ref, oc_ref, od_ref)
    xn = xn_ref[...]
    total = jnp.zeros(x_ref.shape, F32)
    for n in range(N_BRANCH):
        cols = slice(n * D_MODEL, (n + 1) * D_MODEL)
        gate = _sigmoid(_dot(xn, wg_ref[:, cols]) + bg_ref[:, cols])
        total = total + gate * _dot(branches[n][...], wb_ref[n])
    xo_ref[...] = x_ref[...] + _dot(total.astype(BF16), wo_ref[...])


def _merge(x, xn, outs, wg, bg, wb, wo):
    n = x.shape[0]
    rows = lambda width: pl.BlockSpec((ROW_TILE, width), lambda i: (i, 0))
    return pl.pallas_call(
        _merge_kernel,
        grid=(n // ROW_TILE,),
        in_specs=[rows(D_MODEL), rows(D_MODEL)] + [rows(256)] * N_BRANCH
                 + [_full(wg.shape), _full(bg.shape), _full(wb.shape), _full(wo.shape)],
        out_specs=rows(D_MODEL),
        out_shape=jax.ShapeDtypeStruct((n, D_MODEL), F32),
        compiler_params=_params("parallel"),
        name="merge",
    )(x, xn, *outs, wg, bg, wb, wo)


def _bucket_of_distance():
    d = np.arange(REL_MAX_DIST + 1)
    exact = REL_BUCKETS // 2
    nf = np.maximum(d, 1).astype(np.float32)
    log_b = exact + (np.log(nf / exact) / math.log(REL_MAX_DIST / exact) * (REL_BUCKETS - exact)).astype(np.int32)
    return np.where(d < exact, d, np.minimum(log_b, REL_BUCKETS - 1))


def _toeplitz_tiles(table, n_off, dil, max_steps=None):
    span = 2 * LANE
    m = np.arange(span)
    off = np.arange(n_off)[:, None]
    steps = off * LANE + LANE - 1 - m[None, :]
    live = (steps >= 0) if max_steps is None else (steps >= 0) & (steps <= max_steps)
    v = table[:, _bucket_of_distance()[np.clip(steps * dil, 0, REL_MAX_DIST)]]
    v = jnp.where(live[None], v, NEG_INF)
    flat = jnp.tile(v, (1, 1, LANE))[..., :LANE * (span - 1)]
    tiles = flat.reshape(table.shape[0], n_off, LANE, span - 1)[..., LANE - 1:]
    return jnp.concatenate([jnp.full((table.shape[0], 1, LANE, LANE), NEG_INF, F32), tiles], axis=1)


def _same_head(width, head):
    idx = np.arange(width) // head
    return jnp.asarray(idx[:, None] == idx[None, :], BF16)


def _rope_tables(seq):
    half = MLA_ROPE // 2
    freqs = ROPE_THETA ** (-np.arange(half, dtype=np.float64) / half)
    ang = np.arange(seq, dtype=np.float64)[:, None] * freqs[None, :]
    pad = np.zeros((seq, MLA_SLOT - MLA_QK))
    cos_h = np.concatenate([np.ones((seq, MLA_NOPE)), np.cos(ang), np.cos(ang), pad], axis=1)
    sin_h = np.concatenate([np.zeros((seq, MLA_NOPE)), np.sin(ang), np.sin(ang), pad], axis=1)
    return (jnp.asarray(np.tile(cos_h, (1, N_HEADS)), F32), jnp.asarray(np.tile(sin_h, (1, N_HEADS)), F32))


def _rot_half_cols():
    half = MLA_ROPE // 2
    src = np.arange(MLA_SLOT)
    sign = np.zeros(MLA_SLOT, np.float32)
    src[MLA_NOPE:MLA_NOPE + half] = np.arange(MLA_NOPE + half, MLA_QK)
    sign[MLA_NOPE:MLA_NOPE + half] = -1.0
    src[MLA_NOPE + half:MLA_QK] = np.arange(MLA_NOPE, MLA_NOPE + half)
    sign[MLA_NOPE + half:MLA_QK] = 1.0
    return src, sign


def _mixer(x, xn, seq, w_in, b_gate, qk_a, qk_b, qk_c, qk_d, mla_nq, w_uq, mla_nkv, w_ukv,
           w_branch, w_out, toeps, consts):
    n = x.shape[0]
    b = n // seq
    e64, e96, tri, cos, sin = consts
    toep_a, toep_b, toep_d = toeps
    row = lambda a: a.reshape(1, -1)
    tile4 = lambda g: jnp.tile(g, N_HEADS).reshape(1, -1)
    r3 = lambda a: a.reshape(b, seq, a.shape[-1])

    off_iq = OFF_A + 6 * HEAD_DIM
    off_ik = off_iq + N_HEADS * IDX_DIM
    iq_rep = jnp.concatenate([jnp.tile(w_in[:, off_iq + h * IDX_DIM:off_iq + (h + 1) * IDX_DIM], (1, 4))
                              for h in range(N_HEADS)], axis=1)
    ik_rep = jnp.tile(w_in[:, off_ik:off_ik + IDX_DIM], (1, 4))
    iw_pad = jnp.pad(w_in[:, off_ik + IDX_DIM:OFF_A + W_A], ((0, 0), (0, LANE - N_HEADS)))
    half_pad = ((0, 0), (0, LANE - HEAD_DIM))
    q_slots = jnp.concatenate([jnp.pad(w_in[:, OFF_A + h * HEAD_DIM:OFF_A + (h + 1) * HEAD_DIM], half_pad)
                               for h in range(N_HEADS)], axis=1)
    off_k = OFF_A + N_HEADS * HEAD_DIM
    k_slot = jnp.pad(w_in[:, off_k:off_k + HEAD_DIM], half_pad)
    v_twice = jnp.tile(w_in[:, off_k + HEAD_DIM:off_iq], (1, 2))
    wa = jnp.concatenate([q_slots, k_slot, v_twice, iq_rep, ik_rep, iw_pad], axis=1).astype(BF16)
    gq_slots = jnp.tile(jnp.pad(qk_a[0], (0, LANE - HEAD_DIM)), N_HEADS).reshape(1, -1)
    gk_slot = jnp.pad(qk_a[1], (0, LANE - HEAD_DIM)).reshape(1, -1)
    qa, ka, va, vb, iq, ik, iw = _proj_a(xn, wa, e96, gq_slots, gk_slot)
    out_a = _dsa(r3(qa), r3(ka), r3(va), r3(vb), r3(iq), r3(ik), r3(iw), toep_a, tri).reshape(n, 256)

    wb_in = w_in[:, OFF_B:OFF_B + W_B].astype(BF16)
    qb, kb, vb_first, vb_second, kmean = _proj_b(xn, wb_in, e64, tile4(qk_b[0]), tile4(qk_b[1]))
    out_b = _moba(r3(qb), r3(kb), r3(vb_first), r3(vb_second),
                  kmean.reshape(b, seq // MOBA_BLOCK, BRANCH_WIDTH), toep_b).reshape(n, 256)

    src, sign = _rot_half_cols()
    src4 = np.concatenate([h * MLA_SLOT + src for h in range(N_HEADS)])
    sign4 = jnp.asarray(np.tile(sign, N_HEADS))
    slot_pad = MLA_SLOT - MLA_QK
    w_kr = w_in[:, OFF_C + MLA_Q_LORA + MLA_KV_LORA:OFF_C + W_C]
    kr_slots = jnp.pad(w_kr, ((0, 0), (MLA_NOPE, slot_pad)))
    kr_slots = jnp.tile(kr_slots, (1, N_HEADS))
    wc = jnp.concatenate([w_in[:, OFF_C:OFF_C + MLA_Q_LORA + MLA_KV_LORA], kr_slots,
                          kr_slots[:, src4] * sign4], axis=1).astype(BF16)
    slots = lambda w: jnp.pad(w.reshape(w.shape[0], N_HEADS, -1),
                              ((0, 0), (0, 0), (0, MLA_SLOT - w.shape[1] // N_HEADS))).reshape(w.shape[0], -1)
    wuq = slots(w_uq)
    ukv = w_ukv.reshape(MLA_KV_LORA, N_HEADS, MLA_NOPE + MLA_V)
    wuk = slots(ukv[:, :, :MLA_NOPE].reshape(MLA_KV_LORA, -1))
    wuv = ukv[:, :, MLA_NOPE:].reshape(MLA_KV_LORA, -1)
    gq = jnp.tile(jnp.pad(qk_c[0], (0, slot_pad)), N_HEADS)
    gk = jnp.tile(jnp.pad(qk_c[1], (0, slot_pad)), N_HEADS)
    qc, kc, vc_first, vc_second = _proj_c(
        xn, wc, e96, row(mla_nq), row(mla_nkv), wuq.astype(BF16), (wuq[:, src4] * sign4).astype(BF16),
        wuk.astype(BF16), wuv.astype(BF16), row(gq), row(gq[src4]), row(gk), row(gk[src4]), cos, sin, seq)
    out_c = _mla(r3(qc), r3(kc), r3(vc_first), r3(vc_second)).reshape(n, 256)

    wd = w_in[:, OFF_D:OFF_D + W_D].astype(BF16)
    qd, kd, vd = _proj_d(xn, wd, e64, tile4(qk_d[0]), tile4(qk_d[1]))
    out_d = _dil(r3(qd), r3(kd), r3(vd), toep_d).reshape(n, 256)

    wg = w_in[:, OFF_G:OFF_G + W_G].astype(BF16)
    return _merge(x, xn, (out_a, out_b, out_c, out_d), wg, b_gate.reshape(1, -1),
                  w_branch.astype(BF16), w_out.astype(BF16))


def kernel(x, norm_gain, w_in, b_gate, qk_gain_a, qk_gain_b, qk_gain_c, qk_gain_d, mla_norm_q,
           w_mla_uq, mla_norm_kv, w_mla_ukv, w_branch, w_out, rel_bias, w_ffn_in, w_ffn_out):
    b, seq, d = x.shape
    depth = norm_gain.shape[0]
    toep_a = _toeplitz_tiles(rel_bias[0:4], seq // LANE, 1)
    toep_b = _toeplitz_tiles(rel_bias[4:8], seq // LANE, 1)
    toep_d = jnp.stack([_toeplitz_tiles(rel_bias[8 + 4 * g:12 + 4 * g], 2, dil, max_steps=window // dil)
                        for g, (window, dil) in enumerate(DIL_PATTERNS)])
    tri = jnp.asarray(np.arange(LANE)[:, None] < np.arange(LANE)[None, :], BF16)
    consts = (_same_head(256, HEAD_DIM), _same_head(N_HEADS * MLA_SLOT, MLA_SLOT), tri) + _rope_tables(seq)

    x = x.reshape(b * seq, d)
    for l in range(depth):
        g = norm_gain[l]
        x, xn = _ffn(x, g[0:1], g[1:2], w_ffn_in[l, 0].astype(BF16), w_ffn_out[l, 0].astype(BF16))
        x = _mixer(x, xn, seq, w_in[l], b_gate[l], qk_gain_a[l], qk_gain_b[l], qk_gain_c[l],
                   qk_gain_d[l], mla_norm_q[l], w_mla_uq[l], mla_norm_kv[l], w_mla_ukv[l],
                   w_branch[l], w_out[l], (toep_a, toep_b, toep_d), consts)
        x, _ = _ffn(x, g[2:3], g[2:3], w_ffn_in[l, 1].astype(BF16), w_ffn_out[l, 1].astype(BF16))
    return x.reshape(b, seq, d)
```

```python
import functools
import math

import numpy as np
import jax
import jax.numpy as jnp
from jax import lax
from jax.experimental import pallas as pl
from jax.experimental.pallas import tpu as pltpu

D_MODEL = 1024
HEAD_DIM = 64
N_HEADS = 4
BRANCH_WIDTH = N_HEADS * HEAD_DIM
IDX_DIM = 32
DSA_TOPK = 256
MOBA_BLOCK = 256
MOBA_TOPK = 3
MLA_Q_LORA = 384
MLA_KV_LORA = 256
MLA_NOPE = 64
MLA_ROPE = 32
MLA_QK = MLA_NOPE + MLA_ROPE
MLA_V = 64
MLA_SLOT = 128
ROPE_THETA = 10000.0
DIL_PATTERNS = ((128, 1), (512, 4), (2048, 16))
DIL_GROUPS = 3
N_BRANCH = 4
D_FF = 2816
REL_BUCKETS = 32
REL_MAX_DIST = 2048
RMS_EPS = 1e-6

OFF_A = 0
W_A = 4 * HEAD_DIM + 2 * HEAD_DIM + 4 * IDX_DIM + IDX_DIM + 4
OFF_B = OFF_A + W_A
W_B = 3 * BRANCH_WIDTH
OFF_C = OFF_B + W_B
W_C = MLA_Q_LORA + MLA_KV_LORA + MLA_ROPE
OFF_D = OFF_C + W_C
W_D = 3 * DIL_GROUPS * BRANCH_WIDTH
OFF_G = OFF_D + W_D
W_G = N_BRANCH * D_MODEL

LANE = 128
ROW_TILE = 512
FF_CHUNK = 1408
WIDTH_CLASSES = 8
ATT_ROWS = 256
MLA_ROWS = 256
ATT_WIDTH_CLASSES = 4
DSA_WIDTH_CLASSES = 4
DSA_ROWS = 256
MASKED = -1e30
VMEM_LIMIT = 56 * 1024 * 1024

F32 = jnp.float32
BF16 = jnp.bfloat16
NEG_INF = float("-inf")
INT_MIN = -2 ** 31

_CONTRACT_LAST = (((1,), (1,)), ((), ()))


def _dot(a, b, precision=None):
    return jnp.dot(a, b, preferred_element_type=F32, precision=precision)


def _dot_t(a, b, precision=None):
    return lax.dot_general(a, b, _CONTRACT_LAST, preferred_element_type=F32, precision=precision)


def _rms(x, gain):
    return x * lax.rsqrt(jnp.mean(x * x, axis=-1, keepdims=True) + RMS_EPS) * gain


def _sigmoid(x):
    return 1.0 / (1.0 + jnp.exp(-x))


def _head_sumsq(y, e_ref):
    y2 = y * y
    hi = y2.astype(BF16)
    lo = (y2 - hi.astype(F32)).astype(BF16)
    e = e_ref[...]
    return _dot(hi, e) + _dot(lo, e)


def _head_norm(y, e_ref, gain, width):
    return y * lax.rsqrt(_head_sumsq(y, e_ref) * (1.0 / width) + RMS_EPS) * gain


def _params(*sem):
    return pltpu.CompilerParams(dimension_semantics=sem, vmem_limit_bytes=VMEM_LIMIT)


def _full(shape):
    return pl.BlockSpec(shape, lambda *_: (0,) * len(shape))


def _resident(shape):
    return pl.BlockSpec(shape, lambda *_: (0,) * len(shape), pipeline_mode=pl.Buffered(1))


def _ffn_kernel(x_ref, g_ref, gn_ref, wi_ref, wo_ref, xo_ref, xn_ref):
    half = ROW_TILE // 2
    for s in range(2):
        rows = slice(s * half, (s + 1) * half)
        x = x_ref[rows, :]
        xb = _rms(x, g_ref[...]).astype(BF16)
        acc = None
        for j in range(D_FF // FF_CHUNK):
            gate = _dot(xb, wi_ref[:, j * FF_CHUNK:(j + 1) * FF_CHUNK])
            up = _dot(xb, wi_ref[:, D_FF + j * FF_CHUNK:D_FF + (j + 1) * FF_CHUNK])
            h = (gate * _sigmoid(gate) * up).astype(BF16)
            part = _dot(h, wo_ref[j * FF_CHUNK:(j + 1) * FF_CHUNK, :])
            acc = part if acc is None else acc + part
        xo = x + 0.5 * acc
        xo_ref[rows, :] = xo
        xn_ref[rows, :] = _rms(xo, gn_ref[...]).astype(BF16)


def _ffn(x, gain, gain_next, w_in, w_out):
    n = x.shape[0]
    rows = pl.BlockSpec((ROW_TILE, D_MODEL), lambda i: (i, 0))
    return pl.pallas_call(
        _ffn_kernel,
        grid=(n // ROW_TILE,),
        in_specs=[rows, _full((1, D_MODEL)), _full((1, D_MODEL)), _resident(w_in.shape), _resident(w_out.shape)],
        out_specs=[rows, rows],
        out_shape=[jax.ShapeDtypeStruct((n, D_MODEL), F32), jax.ShapeDtypeStruct((n, D_MODEL), BF16)],
        compiler_params=_params("parallel"),
        name="ffn",
    )(x, gain, gain_next, w_in, w_out)


def _split_terms(rep, hi_lanes):
    hi = rep.astype(BF16).astype(F32)
    return jnp.where(hi_lanes, hi, rep - hi).astype(BF16)


def _with_ones(v):
    first = (lax.broadcasted_iota(jnp.int32, (1, v.shape[1]), 1) & (LANE - 1)) < HEAD_DIM
    return jnp.where(first, v, 1.0).astype(BF16), jnp.where(first, 1.0, v).astype(BF16)


def _proj_a_kernel(xn_ref, w_ref, e_ref, gq_ref, gk_ref, q_ref, k_ref, va_ref, vb_ref, iq_ref, ik_ref, iw_ref):
    p = _dot(xn_ref[...], w_ref[...])
    q = _head_norm(p[:, :512], e_ref, gq_ref[...], HEAD_DIM) * HEAD_DIM ** -0.5
    q_ref[...] = q.astype(BF16)
    k = p[:, 512:640]
    k_ms = jnp.sum(k * k, axis=-1, keepdims=True) * (1.0 / HEAD_DIM)
    k_ref[...] = (k * lax.rsqrt(k_ms + RMS_EPS) * gk_ref[...]).astype(BF16)
    va_ref[...], vb_ref[...] = _with_ones(p[:, 640:768])
    lane = lax.broadcasted_iota(jnp.int32, (1, 4 * LANE), 1)
    copy = lax.shift_right_logical(lane, int(math.log2(IDX_DIM))) & 3
    iq_ref[...] = _split_terms(p[:, 768:1280], copy < 2)
    ik_ref[...] = _split_terms(p[:, 1280:1408], (copy[:, :LANE] & 1) == 0)
    iw_ref[...] = p[:, 1408:1536]


def _proj_a(xn, w, e_slots, gq, gk):
    n = xn.shape[0]
    rows = lambda width: pl.BlockSpec((ROW_TILE, width), lambda i: (i, 0))
    return pl.pallas_call(
        _proj_a_kernel,
        grid=(n // ROW_TILE,),
        in_specs=[rows(D_MODEL), _full(w.shape), _full(e_slots.shape), _full(gq.shape), _full(gk.shape)],
        out_specs=[rows(512), rows(128), rows(128), rows(128), rows(512), rows(128), rows(128)],
        out_shape=[jax.ShapeDtypeStruct((n, 512), BF16), jax.ShapeDtypeStruct((n, 128), BF16),
                   jax.ShapeDtypeStruct((n, 128), BF16), jax.ShapeDtypeStruct((n, 128), BF16),
                   jax.ShapeDtypeStruct((n, 512), BF16),
                   jax.ShapeDtypeStruct((n, 128), BF16), jax.ShapeDtypeStruct((n, 128), F32)],
        compiler_params=_params("parallel"),
        name="proj_a",
    )(xn, w, e_slots, gq, gk)


def _proj_b_kernel(xn_ref, w_ref, e_ref, gq_ref, gk_ref, q_ref, k_ref, va_ref, vb_ref, km_ref):
    p = _dot(xn_ref[...], w_ref[...])
    q_ref[...] = _head_norm(p[:, :256], e_ref, gq_ref[...], HEAD_DIM)
    k = _head_norm(p[:, 256:512], e_ref, gk_ref[...], HEAD_DIM)
    k_ref[...] = k.astype(BF16)
    va_ref[...], vb_ref[...] = _with_ones(p[:, 512:768])
    km = jnp.mean(k.reshape(ROW_TILE // MOBA_BLOCK, MOBA_BLOCK, BRANCH_WIDTH), axis=1)
    for j in range(ROW_TILE // MOBA_BLOCK):
        km_ref[j] = km[j:j + 1]


def _proj_b(xn, w, e64, gq, gk):
    n = xn.shape[0]
    rows = lambda width: pl.BlockSpec((ROW_TILE, width), lambda i: (i, 0))
    per_tile = ROW_TILE // MOBA_BLOCK
    return pl.pallas_call(
        _proj_b_kernel,
        grid=(n // ROW_TILE,),
        in_specs=[rows(D_MODEL), _full(w.shape), _full(e64.shape), _full(gq.shape), _full(gk.shape)],
        out_specs=[rows(256), rows(256), rows(256), rows(256),
                   pl.BlockSpec((per_tile, 1, BRANCH_WIDTH), lambda i: (i, 0, 0))],
        out_shape=[jax.ShapeDtypeStruct((n, 256), F32), jax.ShapeDtypeStruct((n, 256), BF16),
                   jax.ShapeDtypeStruct((n, 256), BF16), jax.ShapeDtypeStruct((n, 256), BF16),
                   jax.ShapeDtypeStruct((n // MOBA_BLOCK, 1, BRANCH_WIDTH), F32)],
        compiler_params=_params("parallel"),
        name="proj_b",
    )(xn, w, e64, gq, gk)


def _proj_c_kernel(xn_ref, w_ref, e_ref, nq_ref, nkv_ref, wuq_ref, wuqs_ref, wuk_ref, wuv_ref,
                   gq_ref, gqs_ref, gk_ref, gks_ref, cos_ref, sin_ref, q_ref, k_ref, va_ref, vb_ref):
    p = _dot(xn_ref[...], w_ref[...])
    cos = cos_ref[...]
    sin = sin_ref[...]
    xq = _rms(p[:, :384], nq_ref[...]).astype(BF16)
    qa = _dot(xq, wuq_ref[...])
    qs = _dot(xq, wuqs_ref[...])
    rq = lax.rsqrt(_head_sumsq(qa, e_ref) * (1.0 / MLA_QK) + RMS_EPS)
    q_ref[...] = (rq * (qa * gq_ref[...] * cos + qs * gqs_ref[...] * sin)).astype(BF16)
    xkv = _rms(p[:, 384:640], nkv_ref[...]).astype(BF16)
    ka = _dot(xkv, wuk_ref[...]) + p[:, 640:1152]
    ks = p[:, 1152:1664]
    rk = lax.rsqrt(_head_sumsq(ka, e_ref) * (1.0 / MLA_QK) + RMS_EPS)
    k_ref[...] = (rk * (ka * gk_ref[...] * cos + ks * gks_ref[...] * sin)).astype(BF16)
    va_ref[...], vb_ref[...] = _with_ones(_dot(xkv, wuv_ref[...]))


def _proj_c(xn, w, e96, nq, nkv, wuq, wuqs, wuk, wuv, gq, gqs, gk, gks, cos, sin, seq):
    n = xn.shape[0]
    rows = lambda width: pl.BlockSpec((ROW_TILE, width), lambda i: (i, 0))
    pos = pl.BlockSpec((ROW_TILE, N_HEADS * MLA_SLOT), lambda i: (i % (seq // ROW_TILE), 0))
    consts = (w, e96, nq, nkv, wuq, wuqs, wuk, wuv, gq, gqs, gk, gks)
    return pl.pallas_call(
        _proj_c_kernel,
        grid=(n // ROW_TILE,),
        in_specs=[rows(D_MODEL)] + [_full(c.shape) for c in consts] + [pos, pos],
        out_specs=[rows(N_HEADS * MLA_SLOT), rows(N_HEADS * MLA_SLOT), rows(256), rows(256)],
        out_shape=[jax.ShapeDtypeStruct((n, N_HEADS * MLA_SLOT), BF16),
                   jax.ShapeDtypeStruct((n, N_HEADS * MLA_SLOT), BF16),
                   jax.ShapeDtypeStruct((n, 256), BF16), jax.ShapeDtypeStruct((n, 256), BF16)],
        compiler_params=_params("parallel"),
        name="proj_c",
    )(xn, *consts, cos, sin)


def _proj_d_kernel(xn_ref, w_ref, e_ref, gq_ref, gk_ref, q_ref, k_ref, v_ref):
    xn = xn_ref[...]
    for c in range(DIL_GROUPS):
        lo, hi = c * 256, (c + 1) * 256
        pq = _dot(xn, w_ref[:, lo:hi])
        q_ref[:, lo:hi] = (_head_norm(pq, e_ref, gq_ref[...], HEAD_DIM) * HEAD_DIM ** -0.5).astype(BF16)
        pk = _dot(xn, w_ref[:, 768 + lo:768 + hi])
        k_ref[:, lo:hi] = _head_norm(pk, e_ref, gk_ref[...], HEAD_DIM).astype(BF16)
        v_ref[:, lo:hi] = _dot(xn, w_ref[:, 1536 + lo:1536 + hi]).astype(BF16)


def _proj_d(xn, w, e64, gq, gk):
    n = xn.shape[0]
    rows = lambda width: pl.BlockSpec((ROW_TILE, width), lambda i: (i, 0))
    return pl.pallas_call(
        _proj_d_kernel,
        grid=(n // ROW_TILE,),
        in_specs=[rows(D_MODEL), _full(w.shape), _full(e64.shape), _full(gq.shape), _full(gk.shape)],
        out_specs=[rows(768), rows(768), rows(768)],
        out_shape=[jax.ShapeDtypeStruct((n, 768), BF16)] * 3,
        compiler_params=_params("parallel"),
        name="proj_d",
    )(xn, w, e64, gq, gk)


def _for_causal_width(qi, n_qtiles, tile_rows, body, classes=WIDTH_CLASSES):
    n_cls = min(classes, n_qtiles)
    per = n_qtiles // n_cls
    for c in range(n_cls):
        pl.when((qi >= c * per) & (qi < (c + 1) * per))(functools.partial(body, (c + 1) * per * tile_rows))


def _bias_rows(toep_ref, h, qblk, nchunk):
    return jnp.concatenate([toep_ref[h, jnp.maximum(qblk - c + 1, 0)] for c in range(nchunk)], axis=1)


def _bias_tile(toep_ref, h, qi, sub_blocks, nchunk):
    return jnp.concatenate([_bias_rows(toep_ref, h, qi * sub_blocks + j, nchunk) for j in range(sub_blocks)],
                           axis=0)


def _softmax_pv(logits, v):
    m = jnp.max(logits, axis=1, keepdims=True)
    p = jnp.exp(logits - m)
    l = jnp.sum(p, axis=1, keepdims=True)
    return _dot(p.astype(BF16), v) / l


def _exp_pv(logits, v_ones):
    return _dot(_exp_weights(logits), v_ones)


def _exp_weights(logits):
    m = jnp.max(logits, axis=1, keepdims=True)
    return jnp.exp(logits - m).astype(BF16)


def _pair_output(first_head, second_head):
    first = lax.broadcasted_iota(jnp.int32, first_head.shape, 1) < HEAD_DIM
    numer = jnp.where(first, first_head, second_head)
    denom = jnp.where(first, pltpu.roll(first_head, HEAD_DIM, axis=1), pltpu.roll(second_head, HEAD_DIM, axis=1))
    return numer / denom


def _dsa_body(width, qi, q_ref, k_ref, va_ref, vb_ref, iq_ref, ik_ref, iw_ref, toep_ref, tri_ref, o_ref,
              sel_ref, n_sel):
    rows = DSA_ROWS
    nchunk = width // LANE
    ik = ik_ref[0, :width, :]
    iw = iw_ref[0][:, :N_HEADS] * (N_HEADS ** -0.5 * IDX_DIM ** -0.5)

    r = _dot_t(jnp.concatenate([iq_ref[0, :, h * LANE:(h + 1) * LANE] for h in range(N_HEADS)], axis=0), ik)
    score = None
    for h in range(N_HEADS):
        term = jnp.maximum(r[h * rows:(h + 1) * rows], 0.0) * iw[:, h:h + 1]
        score = term if score is None else score + term

    t = qi * rows + lax.broadcasted_iota(jnp.int32, (rows, width), 0)
    s = lax.broadcasted_iota(jnp.int32, (rows, width), 1)

    bits = pltpu.bitcast(score, jnp.int32)
    key = jnp.where(bits < 0, jnp.int32(INT_MIN) - bits, bits)
    key = jnp.where(s <= t, key, jnp.int32(INT_MIN))

    half_min = -2 ** 15
    upper = lax.shift_right_arithmetic(key, 16).astype(jnp.int16)
    lower = ((key & jnp.int32(0xFFFF)) + half_min).astype(jnp.int16)

    def count(hit):
        ones = jnp.where(hit, jnp.int16(1), jnp.int16(0))
        acc = ones[:, :LANE]
        for c in range(1, nchunk):
            acc = acc + ones[:, c * LANE:(c + 1) * LANE]
        return jnp.sum(acc.astype(F32), axis=1, keepdims=True)

    def search(half, need):
        def step(i, ans):
            cand = ans | lax.shift_left(jnp.int32(1), 15 - i)
            cnt = count(half >= (cand + half_min).astype(jnp.int16))
            return jnp.where(cnt >= need, cand, ans)
        return lax.fori_loop(0, 16, step, jnp.zeros((rows, 1), jnp.int32))

    thr_upper = search(upper, n_sel) + half_min
    thr_upper16 = thr_upper.astype(jnp.int16)
    n_over = count(upper > thr_upper16)
    candidates = jnp.where(upper == thr_upper16, lower, jnp.int16(half_min))
    thr = thr_upper * 65536 + search(candidates, n_sel - n_over)

    sel = jnp.where(key >= jnp.maximum(thr, jnp.int32(INT_MIN + 1)), 1.0, 0.0)
    sel_ref[:, :width] = sel

    overflow = jnp.sum(sel, axis=1, keepdims=True) > n_sel

    @pl.when(jnp.max(jnp.where(overflow, 1.0, 0.0)) > 0.0)
    def _():
        above = key > thr
        room = n_sel - jnp.sum(jnp.where(above, 1.0, 0.0), axis=1, keepdims=True)
        before = jnp.zeros((rows, 1), F32)
        for c in range(nchunk):
            cols = slice(c * LANE, (c + 1) * LANE)
            tc = jnp.where(key[:, cols] == thr, 1.0, 0.0)
            rank = before + _dot(tc.astype(BF16), tri_ref[...])
            keep = above[:, cols] | ((tc > 0.0) & (rank < room))
            sel_ref[:, cols] = jnp.where(keep, 1.0, 0.0)
            before = before + jnp.sum(tc, axis=1, keepdims=True)

    unselected = (sel_ref[:, :width] - 1.0) * -MASKED
    k = k_ref[0, :width, :]
    qk = _dot_t(jnp.concatenate([q_ref[0, :, h * LANE:(h + 1) * LANE] for h in range(N_HEADS)], axis=0), k)
    logits = [qk[h * rows:(h + 1) * rows] + _bias_tile(toep_ref, h, qi, rows // LANE, nchunk) + unselected
              for h in range(N_HEADS)]
    weights = [_exp_weights(x) for x in logits]
    for pair in range(N_HEADS // 2):
        outs = [_dot(weights[2 * pair], va_ref[0, :width, :]), _dot(weights[2 * pair + 1], vb_ref[0, :width, :])]
        o_ref[0, :, pair * LANE:(pair + 1) * LANE] = _pair_output(*outs).astype(o_ref.dtype)


def _dsa_kernel(*refs, seq, n_sel):
    qi = pl.program_id(1)
    _for_causal_width(qi, seq // DSA_ROWS, DSA_ROWS,
                      functools.partial(_dsa_body, qi=qi, n_sel=n_sel, **_named(refs)), classes=DSA_WIDTH_CLASSES)


def _named(refs):
    names = ("q_ref", "k_ref", "va_ref", "vb_ref", "iq_ref", "ik_ref", "iw_ref", "toep_ref", "tri_ref", "o_ref",
             "sel_ref")
    return dict(zip(names, refs, strict=True))


def _dsa(q, k, va, vb, iq, ik, iw, toep, tri):
    b, seq, _ = q.shape
    n_sel = min(DSA_TOPK, seq // 4)
    qblock = lambda width: pl.BlockSpec((1, DSA_ROWS, width), lambda bi, qi: (bi, qi, 0))
    whole = lambda width: pl.BlockSpec((1, seq, width), lambda bi, qi: (bi, 0, 0))
    return pl.pallas_call(
        functools.partial(_dsa_kernel, seq=seq, n_sel=n_sel),
        grid=(b, seq // DSA_ROWS),
        in_specs=[qblock(512), whole(128), whole(128), whole(128), qblock(512), whole(128), qblock(128),
                  _full(toep.shape), _full(tri.shape)],
        out_specs=qblock(256),
        out_shape=jax.ShapeDtypeStruct((b, seq, 256), BF16),
        scratch_shapes=[pltpu.VMEM((DSA_ROWS, seq), F32)],
        compiler_params=_params("parallel", "parallel"),
        name="dsa",
    )(q, k, va, vb, iq, ik, iw, toep, tri)


def _moba_body(width, qi, q_ref, k_ref, va_ref, vb_ref, km_ref, toep_ref, spread_ref, o_ref, nblk):
    nchunk = width // LANE
    n_sel = min(MOBA_TOPK, nblk - 1)
    rows = ATT_ROWS
    own = lax.shift_right_logical(qi * rows, int(math.log2(MOBA_BLOCK)))
    q = q_ref[0]

    km = km_ref[0]
    head_of_col = lax.shift_right_logical(lax.broadcasted_iota(jnp.int32, (nblk, BRANCH_WIDTH), 1),
                                          int(math.log2(HEAD_DIM)))
    per_head = [jnp.where(head_of_col == h, km, 0.0) for h in range(N_HEADS)]
    pad = jnp.zeros((LANE - N_HEADS * nblk, BRANCH_WIDTH), F32)
    gate = _dot_t(q, jnp.concatenate(per_head + [pad], axis=0), precision=lax.Precision.HIGHEST)

    lane = lax.broadcasted_iota(jnp.int32, (rows, LANE), 1)
    blk = lane & (nblk - 1)
    past = (blk < own) & (lane < N_HEADS * nblk)
    gate = jnp.where(past, gate, NEG_INF)
    rank = jnp.zeros((rows, LANE), F32)
    for d in range(1, nblk):
        before = pltpu.roll(gate, d, axis=1)
        rank = rank + jnp.where(blk >= d, jnp.where(before >= gate, 1.0, 0.0), 0.0)
        after = pltpu.roll(gate, LANE - d, axis=1)
        rank = rank + jnp.where(blk < nblk - d, jnp.where(after > gate, 1.0, 0.0), 0.0)
    picked = jnp.where(past, jnp.where(rank < n_sel, 1.0, 0.0), jnp.where(blk == own, 1.0, 0.0)).astype(BF16)

    first = lane < HEAD_DIM
    logits = []
    for h in range(N_HEADS):
        ps = slice(h // 2 * LANE, (h // 2 + 1) * LANE)
        mine = first if h % 2 == 0 else ~first
        qh = jnp.where(mine, q[:, ps] * HEAD_DIM ** -0.5, 0.0).astype(BF16)
        unpicked = (_dot(picked, spread_ref[h, :, :width]) - 1.0) * -MASKED
        logits.append(_dot_t(qh, k_ref[0, :width, ps]) + _bias_tile(toep_ref, h, qi, rows // LANE, nchunk)
                      + unpicked)
    weights = [_exp_weights(x) for x in logits]
    for pair in range(N_HEADS // 2):
        ps = slice(pair * LANE, (pair + 1) * LANE)
        outs = [_dot(weights[2 * pair], va_ref[0, :width, ps]), _dot(weights[2 * pair + 1], vb_ref[0, :width, ps])]
        o_ref[0, :, ps] = _pair_output(*outs).astype(o_ref.dtype)


def _moba_kernel(q_ref, k_ref, va_ref, vb_ref, km_ref, toep_ref, spread_ref, o_ref, *, seq):
    qi = pl.program_id(1)
    body = functools.partial(_moba_body, qi=qi, q_ref=q_ref, k_ref=k_ref, va_ref=va_ref, vb_ref=vb_ref,
                             km_ref=km_ref, toep_ref=toep_ref, spread_ref=spread_ref, o_ref=o_ref,
                             nblk=seq // MOBA_BLOCK)
    _for_causal_width(qi, seq // ATT_ROWS, ATT_ROWS, body, classes=ATT_WIDTH_CLASSES)


def _moba_spread(seq):
    nblk = seq // MOBA_BLOCK
    row = np.arange(LANE)[None, :, None]
    h = np.arange(N_HEADS)[:, None, None]
    blk_of_key = (np.arange(seq) // MOBA_BLOCK)[None, None, :]
    return jnp.asarray(row == h * nblk + blk_of_key, BF16)


def _moba(q, k, va, vb, k_mean, toep):
    b, seq, _ = q.shape
    nblk = seq // MOBA_BLOCK
    if nblk & (nblk - 1) or N_HEADS * nblk > LANE:
        raise ValueError("MoBA lane layout needs a power-of-two block count with heads*blocks <= 128")
    spread = _moba_spread(seq)
    qblock = pl.BlockSpec((1, ATT_ROWS, 256), lambda bi, qi: (bi, qi, 0))
    whole = pl.BlockSpec((1, seq, 256), lambda bi, qi: (bi, 0, 0))
    means = pl.BlockSpec((1, nblk, 256), lambda bi, qi: (bi, 0, 0))
    return pl.pallas_call(
        functools.partial(_moba_kernel, seq=seq),
        grid=(b, seq // ATT_ROWS),
        in_specs=[qblock, whole, whole, whole, means, _full(toep.shape), _full(spread.shape)],
        out_specs=qblock,
        out_shape=jax.ShapeDtypeStruct((b, seq, 256), BF16),
        compiler_params=_params("parallel", "parallel"),
        name="moba",
    )(q, k, va, vb, k_mean, toep, spread)


def _mla_body(width, qi, q_ref, k_ref, va_ref, vb_ref, causal_ref, o_ref):
    nchunk = width // LANE
    sub = MLA_ROWS // LANE
    causal = jnp.concatenate(
        [jnp.concatenate([causal_ref[0, jnp.clip(qi * sub + j - c + 1, 0, 2)] for c in range(nchunk)], axis=1)
         for j in range(sub)], axis=0)
    logits = []
    for h in range(N_HEADS):
        qs = slice(h * MLA_SLOT, (h + 1) * MLA_SLOT)
        logits.append(_dot_t(q_ref[0, :, qs], k_ref[0, :width, qs]) * MLA_QK ** -0.5 + causal)
    weights = [_exp_weights(x) for x in logits]
    for pair in range(N_HEADS // 2):
        ps = slice(pair * LANE, (pair + 1) * LANE)
        outs = [_dot(weights[2 * pair], va_ref[0, :width, ps]), _dot(weights[2 * pair + 1], vb_ref[0, :width, ps])]
        o_ref[0, :, ps] = _pair_output(*outs).astype(o_ref.dtype)


def _mla_kernel(q_ref, k_ref, va_ref, vb_ref, causal_ref, o_ref, *, seq):
    qi = pl.program_id(1)
    body = functools.partial(_mla_body, qi=qi, q_ref=q_ref, k_ref=k_ref, va_ref=va_ref, vb_ref=vb_ref,
                             causal_ref=causal_ref, o_ref=o_ref)
    _for_causal_width(qi, seq // MLA_ROWS, MLA_ROWS, body, classes=ATT_WIDTH_CLASSES)


def _mla(q, k, va, vb):
    b, seq, _ = q.shape
    tri = np.where(np.arange(LANE)[:, None] >= np.arange(LANE)[None, :], 0.0, NEG_INF)
    causal = jnp.asarray(np.stack([np.full((LANE, LANE), NEG_INF), tri, np.zeros((LANE, LANE))])[None], F32)
    return pl.pallas_call(
        functools.partial(_mla_kernel, seq=seq),
        grid=(b, seq // MLA_ROWS),
        in_specs=[pl.BlockSpec((1, MLA_ROWS, N_HEADS * MLA_SLOT), lambda bi, qi: (bi, qi, 0)),
                  pl.BlockSpec((1, seq, N_HEADS * MLA_SLOT), lambda bi, qi: (bi, 0, 0)),
                  pl.BlockSpec((1, seq, 256), lambda bi, qi: (bi, 0, 0)),
                  pl.BlockSpec((1, seq, 256), lambda bi, qi: (bi, 0, 0)),
                  _full(causal.shape)],
        out_specs=pl.BlockSpec((1, MLA_ROWS, 256), lambda bi, qi: (bi, qi, 0)),
        out_shape=jax.ShapeDtypeStruct((b, seq, 256), BF16),
        compiler_params=_params("parallel", "parallel"),
        name="mla",
    )(q, k, va, vb, causal)


def _dil_group(dil, seq, toep_ref, g, qf_ref, kf_ref, vf_ref, m_ref, l_ref, acc_ref):
    n_band = seq // (dil * LANE)

    def unit(u, carry):
        r = lax.div(u, n_band)
        n = lax.rem(u, n_band)
        if dil == 1:
            cur = pl.ds(pl.multiple_of(n * LANE, LANE), LANE)
            prev = pl.ds(pl.multiple_of(jnp.maximum(n - 1, 0) * LANE, LANE), LANE)
        else:
            cur = pl.ds(n * (LANE * dil) + r, LANE, stride=dil)
            prev = pl.ds(jnp.maximum(n - 1, 0) * (LANE * dil) + r, LANE, stride=dil)
        prev_tile = jnp.where(n > 0, 2, 0)
        first = lax.broadcasted_iota(jnp.int32, (LANE, LANE), 1) < HEAD_DIM
        for pair in range(N_HEADS // 2):
            qp = qf_ref[pair, cur, :]
            kk = jnp.concatenate([kf_ref[pair, prev, :], kf_ref[pair, cur, :]], axis=0).astype(BF16)
            vv = jnp.concatenate([vf_ref[pair, prev, :], vf_ref[pair, cur, :]], axis=0).astype(BF16)
            stats = []
            for mine in (first, ~first):
                h = 2 * pair + len(stats)
                bias = jnp.concatenate([toep_ref[g, h, prev_tile], toep_ref[g, h, 1]], axis=1)
                logits = _dot_t(jnp.where(mine, qp, 0.0).astype(BF16), kk) + bias
                m = jnp.max(logits, axis=1, keepdims=True)
                p = jnp.exp(logits - m)
                stats.append((m, jnp.sum(p, axis=1, keepdims=True), _dot(p.astype(BF16), vv)))
            m_new, l_new, pv = (jnp.where(first, a, b) for a, b in zip(*stats))
            m_old = m_ref[pair, cur, :]
            m_tot = jnp.maximum(m_old, m_new)
            a_old = jnp.exp(m_old - m_tot)
            a_new = jnp.exp(m_new - m_tot)
            m_ref[pair, cur, :] = m_tot
            l_ref[pair, cur, :] = a_old * l_ref[pair, cur, :] + a_new * l_new
            acc_ref[pair, cur, :] = a_old * acc_ref[pair, cur, :] + a_new * pv
        return carry

    lax.fori_loop(0, dil * n_band, unit, 0, unroll=2)


def _get_rows(ref, rows):
    return jnp.concatenate([ref[0, rows, :], ref[1, rows, :]], axis=1)


def _set_rows(ref, rows, val):
    ref[0, rows, :] = val[:, :LANE]
    ref[1, rows, :] = val[:, LANE:]


def _dil_kernel(q_ref, k_ref, v_ref, toep_ref, o_ref, qf_ref, kf_ref, vf_ref, m_ref, l_ref, acc_ref, *, seq):
    g = pl.program_id(1)
    everything = pl.ds(0, seq)
    _set_rows(qf_ref, everything, q_ref[0].astype(F32))
    _set_rows(kf_ref, everything, k_ref[0].astype(F32))
    _set_rows(vf_ref, everything, v_ref[0].astype(F32))

    @pl.when(g == 0)
    def _():
        m_ref[...] = jnp.full(m_ref.shape, NEG_INF, F32)
        l_ref[...] = jnp.zeros_like(l_ref)
        acc_ref[...] = jnp.zeros_like(acc_ref)

    for gi, (_, dil) in enumerate(DIL_PATTERNS):
        pl.when(g == gi)(functools.partial(_dil_group, dil, seq, toep_ref, gi, qf_ref, kf_ref, vf_ref,
                                           m_ref, l_ref, acc_ref))

    @pl.when(g == DIL_GROUPS - 1)
    def _():
        o_ref[0] = (_get_rows(acc_ref, everything) / _get_rows(l_ref, everything)).astype(o_ref.dtype)


def _dil(q, k, v, toep):
    b, seq, _ = q.shape
    group = pl.BlockSpec((1, seq, 256), lambda bi, g: (bi, 0, g))
    state = pltpu.VMEM((2, seq, LANE), F32)
    return pl.pallas_call(
        functools.partial(_dil_kernel, seq=seq),
        grid=(b, DIL_GROUPS),
        in_specs=[group, group, group, _full(toep.shape)],
        out_specs=pl.BlockSpec((1, seq, 256), lambda bi, g: (bi, 0, 0)),
        out_shape=jax.ShapeDtypeStruct((b, seq, 256), BF16),
        scratch_shapes=[state] * 6,
        compiler_params=_params("parallel", "arbitrary"),
        name="dil",
    )(q, k, v, toep)


def _merge_kernel(x_ref, xn_ref, oa_ref, ob_ref, oc_ref, od_ref, wg_ref, bg_ref, wb_ref, wo_ref, xo_ref):
    branches = (oa_ref, ob_ref, oc_ref, od_ref)
    xn = xn_ref[...]
    total = jnp.zeros(x_ref.shape, F32)
    for n in range(N_BRANCH):
        cols = slice(n * D_MODEL, (n + 1) * D_MODEL)
        gate = _sigmoid(_dot(xn, wg_ref[:, cols]) + bg_ref[:, cols])
        total = total + gate * _dot(branches[n][...], wb_ref[n])
    xo_ref[...] = x_ref[...] + _dot(total.astype(BF16), wo_ref[...])


def _merge(x, xn, outs, wg, bg, wb, wo):
    n = x.shape[0]
    rows = lambda width: pl.BlockSpec((ROW_TILE, width), lambda i: (i, 0))
    return pl.pallas_call(
        _merge_kernel,
        grid=(n // ROW_TILE,),
        in_specs=[rows(D_MODEL), rows(D_MODEL)] + [rows(256)] * N_BRANCH
                 + [_full(wg.shape), _full(bg.shape), _full(wb.shape), _full(wo.shape)],
        out_specs=rows(D_MODEL),
        out_shape=jax.ShapeDtypeStruct((n, D_MODEL), F32),
        compiler_params=_params("parallel"),
        name="merge",
    )(x, xn, *outs, wg, bg, wb, wo)


def _bucket_of_distance():
    d = np.arange(REL_MAX_DIST + 1)
    exact = REL_BUCKETS // 2
    nf = np.maximum(d, 1).astype(np.float32)
    log_b = exact + (np.log(nf / exact) / math.log(REL_MAX_DIST / exact) * (REL_BUCKETS - exact)).astype(np.int32)
    return np.where(d < exact, d, np.minimum(log_b, REL_BUCKETS - 1))


def _toeplitz_tiles(table, n_off, dil, max_steps=None):
    span = 2 * LANE
    m = np.arange(span)
    off = np.arange(n_off)[:, None]
    steps = off * LANE + LANE - 1 - m[None, :]
    live = (steps >= 0) if max_steps is None else (steps >= 0) & (steps <= max_steps)
    v = table[:, _bucket_of_distance()[np.clip(steps * dil, 0, REL_MAX_DIST)]]
    v = jnp.where(live[None], v, NEG_INF)
    flat = jnp.tile(v, (1, 1, LANE))[..., :LANE * (span - 1)]
    tiles = flat.reshape(table.shape[0], n_off, LANE, span - 1)[..., LANE - 1:]
    return jnp.concatenate([jnp.full((table.shape[0], 1, LANE, LANE), NEG_INF, F32), tiles], axis=1)


def _same_head(width, head):
    idx = np.arange(width) // head
    return jnp.asarray(idx[:, None] == idx[None, :], BF16)


def _rope_tables(seq):
    half = MLA_ROPE // 2
    freqs = ROPE_THETA ** (-np.arange(half, dtype=np.float64) / half)
    ang = np.arange(seq, dtype=np.float64)[:, None] * freqs[None, :]
    pad = np.zeros((seq, MLA_SLOT - MLA_QK))
    cos_h = np.concatenate([np.ones((seq, MLA_NOPE)), np.cos(ang), np.cos(ang), pad], axis=1)
    sin_h = np.concatenate([np.zeros((seq, MLA_NOPE)), np.sin(ang), np.sin(ang), pad], axis=1)
    return (jnp.asarray(np.tile(cos_h, (1, N_HEADS)), F32), jnp.asarray(np.tile(sin_h, (1, N_HEADS)), F32))


def _rot_half_cols():
    half = MLA_ROPE // 2
    src = np.arange(MLA_SLOT)
    sign = np.zeros(MLA_SLOT, np.float32)
    src[MLA_NOPE:MLA_NOPE + half] = np.arange(MLA_NOPE + half, MLA_QK)
    sign[MLA_NOPE:MLA_NOPE + half] = -1.0
    src[MLA_NOPE + half:MLA_QK] = np.arange(MLA_NOPE, MLA_NOPE + half)
    sign[MLA_NOPE + half:MLA_QK] = 1.0
    return src, sign


def _mixer(x, xn, seq, w_in, b_gate, qk_a, qk_b, qk_c, qk_d, mla_nq, w_uq, mla_nkv, w_ukv,
           w_branch, w_out, toeps, consts):
    n = x.shape[0]
    b = n // seq
    e64, e96, tri, cos, sin = consts
    toep_a, toep_b, toep_d = toeps
    row = lambda a: a.reshape(1, -1)
    tile4 = lambda g: jnp.tile(g, N_HEADS).reshape(1, -1)
    r3 = lambda a: a.reshape(b, seq, a.shape[-1])

    off_iq = OFF_A + 6 * HEAD_DIM
    off_ik = off_iq + N_HEADS * IDX_DIM
    iq_rep = jnp.concatenate([jnp.tile(w_in[:, off_iq + h * IDX_DIM:off_iq + (h + 1) * IDX_DIM], (1, 4))
                              for h in range(N_HEADS)], axis=1)
    ik_rep = jnp.tile(w_in[:, off_ik:off_ik + IDX_DIM], (1, 4))
    iw_pad = jnp.pad(w_in[:, off_ik + IDX_DIM:OFF_A + W_A], ((0, 0), (0, LANE - N_HEADS)))
    half_pad = ((0, 0), (0, LANE - HEAD_DIM))
    q_slots = jnp.concatenate([jnp.pad(w_in[:, OFF_A + h * HEAD_DIM:OFF_A + (h + 1) * HEAD_DIM], half_pad)
                               for h in range(N_HEADS)], axis=1)
    off_k = OFF_A + N_HEADS * HEAD_DIM
    k_slot = jnp.pad(w_in[:, off_k:off_k + HEAD_DIM], half_pad)
    v_twice = jnp.tile(w_in[:, off_k + HEAD_DIM:off_iq], (1, 2))
    wa = jnp.concatenate([q_slots, k_slot, v_twice, iq_rep, ik_rep, iw_pad], axis=1).astype(BF16)
    gq_slots = jnp.tile(jnp.pad(qk_a[0], (0, LANE - HEAD_DIM)), N_HEADS).reshape(1, -1)
    gk_slot = jnp.pad(qk_a[1], (0, LANE - HEAD_DIM)).reshape(1, -1)
    qa, ka, va, vb, iq, ik, iw = _proj_a(xn, wa, e96, gq_slots, gk_slot)
    out_a = _dsa(r3(qa), r3(ka), r3(va), r3(vb), r3(iq), r3(ik), r3(iw), toep_a, tri).reshape(n, 256)

    wb_in = w_in[:, OFF_B:OFF_B + W_B].astype(BF16)
    qb, kb, vb_first, vb_second, kmean = _proj_b(xn, wb_in, e64, tile4(qk_b[0]), tile4(qk_b[1]))
    out_b = _moba(r3(qb), r3(kb), r3(vb_first), r3(vb_second),
                  kmean.reshape(b, seq // MOBA_BLOCK, BRANCH_WIDTH), toep_b).reshape(n, 256)

    src, sign = _rot_half_cols()
    src4 = np.concatenate([h * MLA_SLOT + src for h in range(N_HEADS)])
    sign4 = jnp.asarray(np.tile(sign, N_HEADS))
    slot_pad = MLA_SLOT - MLA_QK
    w_kr = w_in[:, OFF_C + MLA_Q_LORA + MLA_KV_LORA:OFF_C + W_C]
    kr_slots = jnp.pad(w_kr, ((0, 0), (MLA_NOPE, slot_pad)))
    kr_slots = jnp.tile(kr_slots, (1, N_HEADS))
    wc = jnp.concatenate([w_in[:, OFF_C:OFF_C + MLA_Q_LORA + MLA_KV_LORA], kr_slots,
                          kr_slots[:, src4] * sign4], axis=1).astype(BF16)
    slots = lambda w: jnp.pad(w.reshape(w.shape[0], N_HEADS, -1),
                              ((0, 0), (0, 0), (0, MLA_SLOT - w.shape[1] // N_HEADS))).reshape(w.shape[0], -1)
    wuq = slots(w_uq)
    ukv = w_ukv.reshape(MLA_KV_LORA, N_HEADS, MLA_NOPE + MLA_V)
    wuk = slots(ukv[:, :, :MLA_NOPE].reshape(MLA_KV_LORA, -1))
    wuv = ukv[:, :, MLA_NOPE:].reshape(MLA_KV_LORA, -1)
    gq = jnp.tile(jnp.pad(qk_c[0], (0, slot_pad)), N_HEADS)
    gk = jnp.tile(jnp.pad(qk_c[1], (0, slot_pad)), N_HEADS)
    qc, kc, vc_first, vc_second = _proj_c(
        xn, wc, e96, row(mla_nq), row(mla_nkv), wuq.astype(BF16), (wuq[:, src4] * sign4).astype(BF16),
        wuk.astype(BF16), wuv.astype(BF16), row(gq), row(gq[src4]), row(gk), row(gk[src4]), cos, sin, seq)
    out_c = _mla(r3(qc), r3(kc), r3(vc_first), r3(vc_second)).reshape(n, 256)

    wd = w_in[:, OFF_D:OFF_D + W_D].astype(BF16)
    qd, kd, vd = _proj_d(xn, wd, e64, tile4(qk_d[0]), tile4(qk_d[1]))
    out_d = _dil(r3(qd), r3(kd), r3(vd), toep_d).reshape(n, 256)

    wg = w_in[:, OFF_G:OFF_G + W_G].astype(BF16)
    return _merge(x, xn, (out_a, out_b, out_c, out_d), wg, b_gate.reshape(1, -1),
                  w_branch.astype(BF16), w_out.astype(BF16))


def kernel(x, norm_gain, w_in, b_gate, qk_gain_a, qk_gain_b, qk_gain_c, qk_gain_d, mla_norm_q,
           w_mla_uq, mla_norm_kv, w_mla_ukv, w_branch, w_out, rel_bias, w_ffn_in, w_ffn_out):
    b, seq, d = x.shape
    depth = norm_gain.shape[0]
    toep_a = _toeplitz_tiles(rel_bias[0:4], seq // LANE, 1)
    toep_b = _toeplitz_tiles(rel_bias[4:8], seq // LANE, 1)
    toep_d = jnp.stack([_toeplitz_tiles(rel_bias[8 + 4 * g:12 + 4 * g], 2, dil, max_steps=window // dil)
                        for g, (window, dil) in enumerate(DIL_PATTERNS)])
    tri = jnp.asarray(np.arange(LANE)[:, None] < np.arange(LANE)[None, :], BF16)
    consts = (_same_head(256, HEAD_DIM), _same_head(N_HEADS * MLA_SLOT, MLA_SLOT), tri) + _rope_tables(seq)

    x = x.reshape(b * seq, d)
    for l in range(depth):
        g = norm_gain[l]
        x, xn = _ffn(x, g[0:1], g[1:2], w_ffn_in[l, 0].astype(BF16), w_ffn_out[l, 0].astype(BF16))
        x = _mixer(x, xn, seq, w_in[l], b_gate[l], qk_gain_a[l], qk_gain_b[l], qk_gain_c[l],
                   qk_gain_d[l], mla_norm_q[l], w_mla_uq[l], mla_norm_kv[l], w_mla_ukv[l],
                   w_branch[l], w_out[l], (toep_a, toep_b, toep_d), consts)
        x, _ = _ffn(x, g[2:3], g[2:3], w_ffn_in[l, 1].astype(BF16), w_ffn_out[l, 1].astype(BF16))
    return x.reshape(b, seq, d)
```

```python
import functools
import math

import numpy as np
import jax
import jax.numpy as jnp
from jax import lax
from jax.experimental import pallas as pl
from jax.experimental.pallas import tpu as pltpu

D_MODEL = 1024
HEAD_DIM = 64
N_HEADS = 4
BRANCH_WIDTH = N_HEADS * HEAD_DIM
IDX_DIM = 32
DSA_TOPK = 256
MOBA_BLOCK = 256
MOBA_TOPK = 3
MLA_Q_LORA = 384
MLA_KV_LORA = 256
MLA_NOPE = 64
MLA_ROPE = 32
MLA_QK = MLA_NOPE + MLA_ROPE
MLA_V = 64
MLA_SLOT = 128
ROPE_THETA = 10000.0
DIL_PATTERNS = ((128, 1), (512, 4), (2048, 16))
DIL_GROUPS = 3
N_BRANCH = 4
D_FF = 2816
REL_BUCKETS = 32
REL_MAX_DIST = 2048
RMS_EPS = 1e-6

OFF_A = 0
W_A = 4 * HEAD_DIM + 2 * HEAD_DIM + 4 * IDX_DIM + IDX_DIM + 4
OFF_B = OFF_A + W_A
W_B = 3 * BRANCH_WIDTH
OFF_C = OFF_B + W_B
W_C = MLA_Q_LORA + MLA_KV_LORA + MLA_ROPE
OFF_D = OFF_C + W_C
W_D = 3 * DIL_GROUPS * BRANCH_WIDTH
OFF_G = OFF_D + W_D
W_G = N_BRANCH * D_MODEL

LANE = 128
ROW_TILE = 512
FFN_ROWS = 512
FF_CHUNK = 1408
WIDTH_CLASSES = 8
ATT_ROWS = 256
MLA_ROWS = 256
ATT_WIDTH_CLASSES = 4
DSA_WIDTH_CLASSES = 4
DSA_ROWS = 256
MASKED = -1e30
VMEM_LIMIT = 56 * 1024 * 1024

F32 = jnp.float32
BF16 = jnp.bfloat16
NEG_INF = float("-inf")
INT_MIN = -2 ** 31

_CONTRACT_LAST = (((1,), (1,)), ((), ()))


def _dot(a, b, precision=None):
    return jnp.dot(a, b, preferred_element_type=F32, precision=precision)


def _dot_t(a, b, precision=None):
    return lax.dot_general(a, b, _CONTRACT_LAST, preferred_element_type=F32, precision=precision)


def _rms(x, gain):
    return x * lax.rsqrt(jnp.mean(x * x, axis=-1, keepdims=True) + RMS_EPS) * gain


def _sigmoid(x):
    return 1.0 / (1.0 + jnp.exp(-x))


def _head_sumsq(y, e_ref):
    y2 = y * y
    hi = y2.astype(BF16)
    lo = (y2 - hi.astype(F32)).astype(BF16)
    e = e_ref[...]
    step = e.shape[0]
    blocks = [_dot(hi[:, c:c + step], e) + _dot(lo[:, c:c + step], e) for c in range(0, y.shape[1], step)]
    return blocks[0] if len(blocks) == 1 else jnp.concatenate(blocks, axis=1)


def _head_norm(y, e_ref, gain, width):
    return y * lax.rsqrt(_head_sumsq(y, e_ref) * (1.0 / width) + RMS_EPS) * gain


def _params(*sem):
    return pltpu.CompilerParams(dimension_semantics=sem, vmem_limit_bytes=VMEM_LIMIT)


def _full(shape):
    return pl.BlockSpec(shape, lambda *_: (0,) * len(shape))


def _resident(shape):
    return pl.BlockSpec(shape, lambda *_: (0,) * len(shape), pipeline_mode=pl.Buffered(1))


def _ffn_kernel(x_ref, g_ref, gn_ref, wi_ref, wo_ref, xo_ref, xn_ref):
    half = FFN_ROWS // 2
    for s in range(2):
        rows = slice(s * half, (s + 1) * half)
        x = x_ref[rows, :]
        xb = _rms(x, g_ref[...]).astype(BF16)
        acc = None
        for j in range(D_FF // FF_CHUNK):
            gate = _dot(xb, wi_ref[:, j * FF_CHUNK:(j + 1) * FF_CHUNK])
            up = _dot(xb, wi_ref[:, D_FF + j * FF_CHUNK:D_FF + (j + 1) * FF_CHUNK])
            h = (gate * _sigmoid(gate) * up).astype(BF16)
            part = _dot(h, wo_ref[j * FF_CHUNK:(j + 1) * FF_CHUNK, :])
            acc = part if acc is None else acc + part
        xo = x + 0.5 * acc
        xo_ref[rows, :] = xo
        xn_ref[rows, :] = _rms(xo, gn_ref[...]).astype(BF16)


def _ffn(x, gain, gain_next, w_in, w_out):
    n = x.shape[0]
    rows = pl.BlockSpec((FFN_ROWS, D_MODEL), lambda i: (i, 0))
    return pl.pallas_call(
        _ffn_kernel,
        grid=(n // FFN_ROWS,),
        in_specs=[rows, _full((1, D_MODEL)), _full((1, D_MODEL)), _resident(w_in.shape), _resident(w_out.shape)],
        out_specs=[rows, rows],
        out_shape=[jax.ShapeDtypeStruct((n, D_MODEL), F32), jax.ShapeDtypeStruct((n, D_MODEL), BF16)],
        compiler_params=_params("parallel"),
        name="ffn",
    )(x, gain, gain_next, w_in, w_out)


def _split_terms(rep, hi_lanes):
    hi = rep.astype(BF16).astype(F32)
    return jnp.where(hi_lanes, hi, rep - hi).astype(BF16)


def _with_ones(v):
    first = (lax.broadcasted_iota(jnp.int32, (1, v.shape[1]), 1) & (LANE - 1)) < HEAD_DIM
    return jnp.where(first, v, 1.0).astype(BF16), jnp.where(first, 1.0, v).astype(BF16)


def _proj_a_kernel(xn_ref, w_ref, e_ref, gq_ref, gk_ref, q_ref, k_ref, va_ref, vb_ref, iq_ref, ik_ref, iw_ref):
    p = _dot(xn_ref[...], w_ref[...])
    q = _head_norm(p[:, :512], e_ref, gq_ref[...], HEAD_DIM) * HEAD_DIM ** -0.5
    q_ref[...] = q.astype(BF16)
    k = p[:, 512:640]
    k_ms = jnp.sum(k * k, axis=-1, keepdims=True) * (1.0 / HEAD_DIM)
    k_ref[...] = (k * lax.rsqrt(k_ms + RMS_EPS) * gk_ref[...]).astype(BF16)
    va_ref[...], vb_ref[...] = _with_ones(p[:, 640:768])
    lane = lax.broadcasted_iota(jnp.int32, (1, 4 * LANE), 1)
    copy = lax.shift_right_logical(lane, int(math.log2(IDX_DIM))) & 3
    iq_ref[...] = _split_terms(p[:, 768:1280], copy < 2)
    ik_ref[...] = _split_terms(p[:, 1280:1408], (copy[:, :LANE] & 1) == 0)
    iw_ref[...] = p[:, 1408:1536]


def _proj_a(xn, w, e_slots, gq, gk):
    n = xn.shape[0]
    rows = lambda width: pl.BlockSpec((ROW_TILE, width), lambda i: (i, 0))
    return pl.pallas_call(
        _proj_a_kernel,
        grid=(n // ROW_TILE,),
        in_specs=[rows(D_MODEL), _full(w.shape), _full(e_slots.shape), _full(gq.shape), _full(gk.shape)],
        out_specs=[rows(512), rows(128), rows(128), rows(128), rows(512), rows(128), rows(128)],
        out_shape=[jax.ShapeDtypeStruct((n, 512), BF16), jax.ShapeDtypeStruct((n, 128), BF16),
                   jax.ShapeDtypeStruct((n, 128), BF16), jax.ShapeDtypeStruct((n, 128), BF16),
                   jax.ShapeDtypeStruct((n, 512), BF16),
                   jax.ShapeDtypeStruct((n, 128), BF16), jax.ShapeDtypeStruct((n, 128), F32)],
        compiler_params=_params("parallel"),
        name="proj_a",
    )(xn, w, e_slots, gq, gk)


def _proj_b_kernel(xn_ref, w_ref, e_ref, gq_ref, gk_ref, q_ref, k_ref, va_ref, vb_ref, km_ref):
    p = _dot(xn_ref[...], w_ref[...])
    q_ref[...] = _head_norm(p[:, :256], e_ref, gq_ref[...], HEAD_DIM)
    k = _head_norm(p[:, 256:512], e_ref, gk_ref[...], HEAD_DIM)
    k_ref[...] = k.astype(BF16)
    va_ref[...], vb_ref[...] = _with_ones(p[:, 512:768])
    km = jnp.mean(k.reshape(ROW_TILE // MOBA_BLOCK, MOBA_BLOCK, BRANCH_WIDTH), axis=1)
    for j in range(ROW_TILE // MOBA_BLOCK):
        km_ref[j] = km[j:j + 1]


def _proj_b(xn, w, e64, gq, gk):
    n = xn.shape[0]
    rows = lambda width: pl.BlockSpec((ROW_TILE, width), lambda i: (i, 0))
    per_tile = ROW_TILE // MOBA_BLOCK
    return pl.pallas_call(
        _proj_b_kernel,
        grid=(n // ROW_TILE,),
        in_specs=[rows(D_MODEL), _full(w.shape), _full(e64.shape), _full(gq.shape), _full(gk.shape)],
        out_specs=[rows(256), rows(256), rows(256), rows(256),
                   pl.BlockSpec((per_tile, 1, BRANCH_WIDTH), lambda i: (i, 0, 0))],
        out_shape=[jax.ShapeDtypeStruct((n, 256), F32), jax.ShapeDtypeStruct((n, 256), BF16),
                   jax.ShapeDtypeStruct((n, 256), BF16), jax.ShapeDtypeStruct((n, 256), BF16),
                   jax.ShapeDtypeStruct((n // MOBA_BLOCK, 1, BRANCH_WIDTH), F32)],
        compiler_params=_params("parallel"),
        name="proj_b",
    )(xn, w, e64, gq, gk)


def _proj_c_kernel(xn_ref, w_ref, e_ref, nq_ref, nkv_ref, wuq_ref, wuqs_ref, wuk_ref, wuv_ref,
                   gq_ref, gqs_ref, gk_ref, gks_ref, cos_ref, sin_ref, q_ref, k_ref, va_ref, vb_ref):
    p = _dot(xn_ref[...], w_ref[...])
    cos = cos_ref[...]
    sin = sin_ref[...]
    xq = _rms(p[:, :384], nq_ref[...]).astype(BF16)
    qa = _dot(xq, wuq_ref[...])
    qs = _dot(xq, wuqs_ref[...])
    rq = lax.rsqrt(_head_sumsq(qa, e_ref) * (1.0 / MLA_QK) + RMS_EPS)
    q_ref[...] = (rq * (qa * gq_ref[...] * cos + qs * gqs_ref[...] * sin)).astype(BF16)
    xkv = _rms(p[:, 384:640], nkv_ref[...]).astype(BF16)
    ka = _dot(xkv, wuk_ref[...]) + p[:, 640:1152]
    ks = p[:, 1152:1664]
    rk = lax.rsqrt(_head_sumsq(ka, e_ref) * (1.0 / MLA_QK) + RMS_EPS)
    k_ref[...] = (rk * (ka * gk_ref[...] * cos + ks * gks_ref[...] * sin)).astype(BF16)
    va_ref[...], vb_ref[...] = _with_ones(_dot(xkv, wuv_ref[...]))


def _proj_c(xn, w, e96, nq, nkv, wuq, wuqs, wuk, wuv, gq, gqs, gk, gks, cos, sin, seq):
    n = xn.shape[0]
    rows = lambda width: pl.BlockSpec((ROW_TILE, width), lambda i: (i, 0))
    pos = pl.BlockSpec((ROW_TILE, N_HEADS * MLA_SLOT), lambda i: (i % (seq // ROW_TILE), 0))
    consts = (w, e96, nq, nkv, wuq, wuqs, wuk, wuv, gq, gqs, gk, gks)
    return pl.pallas_call(
        _proj_c_kernel,
        grid=(n // ROW_TILE,),
        in_specs=[rows(D_MODEL)] + [_full(c.shape) for c in consts] + [pos, pos],
        out_specs=[rows(N_HEADS * MLA_SLOT), rows(N_HEADS * MLA_SLOT), rows(256), rows(256)],
        out_shape=[jax.ShapeDtypeStruct((n, N_HEADS * MLA_SLOT), BF16),
                   jax.ShapeDtypeStruct((n, N_HEADS * MLA_SLOT), BF16),
                   jax.ShapeDtypeStruct((n, 256), BF16), jax.ShapeDtypeStruct((n, 256), BF16)],
        compiler_params=_params("parallel"),
        name="proj_c",
    )(xn, *consts, cos, sin)


def _proj_d_kernel(xn_ref, w_ref, e_ref, gq_ref, gk_ref, q_ref, k_ref, v_ref):
    xn = xn_ref[...]
    for c in range(DIL_GROUPS):
        lo, hi = c * 256, (c + 1) * 256
        pq = _dot(xn, w_ref[:, lo:hi])
        q_ref[:, lo:hi] = (_head_norm(pq, e_ref, gq_ref[...], HEAD_DIM) * HEAD_DIM ** -0.5).astype(BF16)
        pk = _dot(xn, w_ref[:, 768 + lo:768 + hi])
        k_ref[:, lo:hi] = _head_norm(pk, e_ref, gk_ref[...], HEAD_DIM).astype(BF16)
        v_ref[:, lo:hi] = _dot(xn, w_ref[:, 1536 + lo:1536 + hi]).astype(BF16)


def _proj_d(xn, w, e64, gq, gk):
    n = xn.shape[0]
    rows = lambda width: pl.BlockSpec((ROW_TILE, width), lambda i: (i, 0))
    return pl.pallas_call(
        _proj_d_kernel,
        grid=(n // ROW_TILE,),
        in_specs=[rows(D_MODEL), _full(w.shape), _full(e64.shape), _full(gq.shape), _full(gk.shape)],
        out_specs=[rows(768), rows(768), rows(768)],
        out_shape=[jax.ShapeDtypeStruct((n, 768), BF16)] * 3,
        compiler_params=_params("parallel"),
        name="proj_d",
    )(xn, w, e64, gq, gk)


def _for_causal_width(qi, n_qtiles, tile_rows, body, classes=WIDTH_CLASSES):
    n_cls = min(classes, n_qtiles)
    per = n_qtiles // n_cls
    for c in range(n_cls):
        pl.when((qi >= c * per) & (qi < (c + 1) * per))(functools.partial(body, (c + 1) * per * tile_rows))


def _bias_rows(toep_ref, h, qblk, nchunk):
    return jnp.concatenate([toep_ref[h, jnp.maximum(qblk - c + 1, 0)] for c in range(nchunk)], axis=1)


def _bias_tile(toep_ref, h, qi, sub_blocks, nchunk):
    return jnp.concatenate([_bias_rows(toep_ref, h, qi * sub_blocks + j, nchunk) for j in range(sub_blocks)],
                           axis=0)


def _softmax_pv(logits, v):
    m = jnp.max(logits, axis=1, keepdims=True)
    p = jnp.exp(logits - m)
    l = jnp.sum(p, axis=1, keepdims=True)
    return _dot(p.astype(BF16), v) / l


def _exp_pv(logits, v_ones):
    return _dot(_exp_weights(logits), v_ones)


def _exp_weights(logits):
    m = jnp.max(logits, axis=1, keepdims=True)
    return jnp.exp(logits - m).astype(BF16)


def _pair_output(first_head, second_head):
    first = lax.broadcasted_iota(jnp.int32, first_head.shape, 1) < HEAD_DIM
    numer = jnp.where(first, first_head, second_head)
    denom = jnp.where(first, pltpu.roll(first_head, HEAD_DIM, axis=1), pltpu.roll(second_head, HEAD_DIM, axis=1))
    return numer / denom


def _dsa_body(width, qi, q_ref, k_ref, va_ref, vb_ref, iq_ref, ik_ref, iw_ref, toep_ref, tri_ref, o_ref,
              sel_ref, n_sel):
    rows = DSA_ROWS
    nchunk = width // LANE
    ik = ik_ref[0, :width, :]
    iw = iw_ref[0][:, :N_HEADS] * (N_HEADS ** -0.5 * IDX_DIM ** -0.5)

    r = _dot_t(jnp.concatenate([iq_ref[0, :, h * LANE:(h + 1) * LANE] for h in range(N_HEADS)], axis=0), ik)
    score = None
    for h in range(N_HEADS):
        term = jnp.maximum(r[h * rows:(h + 1) * rows], 0.0) * iw[:, h:h + 1]
        score = term if score is None else score + term

    t = qi * rows + lax.broadcasted_iota(jnp.int32, (rows, width), 0)
    s = lax.broadcasted_iota(jnp.int32, (rows, width), 1)

    bits = pltpu.bitcast(score, jnp.int32)
    key = jnp.where(bits < 0, jnp.int32(INT_MIN) - bits, bits)
    key = jnp.where(s <= t, key, jnp.int32(INT_MIN))

    half_min = -2 ** 15
    upper = lax.shift_right_arithmetic(key, 16).astype(jnp.int16)
    lower = ((key & jnp.int32(0xFFFF)) + half_min).astype(jnp.int16)

    def count(hit):
        ones = jnp.where(hit, jnp.int16(1), jnp.int16(0))
        acc = ones[:, :LANE]
        for c in range(1, nchunk):
            acc = acc + ones[:, c * LANE:(c + 1) * LANE]
        return jnp.sum(acc.astype(F32), axis=1, keepdims=True)

    def search(half, need):
        def step(i, ans):
            cand = ans | lax.shift_left(jnp.int32(1), 15 - i)
            cnt = count(half >= (cand + half_min).astype(jnp.int16))
            return jnp.where(cnt >= need, cand, ans)
        return lax.fori_loop(0, 16, step, jnp.zeros((rows, 1), jnp.int32))

    thr_upper = search(upper, n_sel) + half_min
    thr_upper16 = thr_upper.astype(jnp.int16)
    n_over = count(upper > thr_upper16)
    candidates = jnp.where(upper == thr_upper16, lower, jnp.int16(half_min))
    thr = thr_upper * 65536 + search(candidates, n_sel - n_over)

    sel = jnp.where(key >= jnp.maximum(thr, jnp.int32(INT_MIN + 1)), 1.0, 0.0)
    sel_ref[:, :width] = sel

    overflow = jnp.sum(sel, axis=1, keepdims=True) > n_sel

    @pl.when(jnp.max(jnp.where(overflow, 1.0, 0.0)) > 0.0)
    def _():
        above = key > thr
        room = n_sel - jnp.sum(jnp.where(above, 1.0, 0.0), axis=1, keepdims=True)
        before = jnp.zeros((rows, 1), F32)
        for c in range(nchunk):
            cols = slice(c * LANE, (c + 1) * LANE)
            tc = jnp.where(key[:, cols] == thr, 1.0, 0.0)
            rank = before + _dot(tc.astype(BF16), tri_ref[...])
            keep = above[:, cols] | ((tc > 0.0) & (rank < room))
            sel_ref[:, cols] = jnp.where(keep, 1.0, 0.0)
            before = before + jnp.sum(tc, axis=1, keepdims=True)

    unselected = (sel_ref[:, :width] - 1.0) * -MASKED
    k = k_ref[0, :width, :]
    qk = _dot_t(jnp.concatenate([q_ref[0, :, h * LANE:(h + 1) * LANE] for h in range(N_HEADS)], axis=0), k)
    logits = [qk[h * rows:(h + 1) * rows] + _bias_tile(toep_ref, h, qi, rows // LANE, nchunk) + unselected
              for h in range(N_HEADS)]
    weights = [_exp_weights(x) for x in logits]
    for pair in range(N_HEADS // 2):
        outs = [_dot(weights[2 * pair], va_ref[0, :width, :]), _dot(weights[2 * pair + 1], vb_ref[0, :width, :])]
        o_ref[0, :, pair * LANE:(pair + 1) * LANE] = _pair_output(*outs).astype(o_ref.dtype)


def _dsa_kernel(*refs, seq, n_sel):
    qi = pl.program_id(1)
    _for_causal_width(qi, seq // DSA_ROWS, DSA_ROWS,
                      functools.partial(_dsa_body, qi=qi, n_sel=n_sel, **_named(refs)), classes=DSA_WIDTH_CLASSES)


def _named(refs):
    names = ("q_ref", "k_ref", "va_ref", "vb_ref", "iq_ref", "ik_ref", "iw_ref", "toep_ref", "tri_ref", "o_ref",
             "sel_ref")
    return dict(zip(names, refs, strict=True))


def _dsa(q, k, va, vb, iq, ik, iw, toep, tri):
    b, seq, _ = q.shape
    n_sel = min(DSA_TOPK, seq // 4)
    qblock = lambda width: pl.BlockSpec((1, DSA_ROWS, width), lambda bi, qi: (bi, qi, 0))
    whole = lambda width: pl.BlockSpec((1, seq, width), lambda bi, qi: (bi, 0, 0))
    return pl.pallas_call(
        functools.partial(_dsa_kernel, seq=seq, n_sel=n_sel),
        grid=(b, seq // DSA_ROWS),
        in_specs=[qblock(512), whole(128), whole(128), whole(128), qblock(512), whole(128), qblock(128),
                  _full(toep.shape), _full(tri.shape)],
        out_specs=qblock(256),
        out_shape=jax.ShapeDtypeStruct((b, seq, 256), BF16),
        scratch_shapes=[pltpu.VMEM((DSA_ROWS, seq), F32)],
        compiler_params=_params("parallel", "parallel"),
        name="dsa",
    )(q, k, va, vb, iq, ik, iw, toep, tri)


def _moba_body(width, qi, q_ref, k_ref, va_ref, vb_ref, km_ref, toep_ref, spread_ref, o_ref, nblk):
    nchunk = width // LANE
    n_sel = min(MOBA_TOPK, nblk - 1)
    rows = ATT_ROWS
    own = lax.shift_right_logical(qi * rows, int(math.log2(MOBA_BLOCK)))
    q = q_ref[0]

    km = km_ref[0]
    head_of_col = lax.shift_right_logical(lax.broadcasted_iota(jnp.int32, (nblk, BRANCH_WIDTH), 1),
                                          int(math.log2(HEAD_DIM)))
    per_head = [jnp.where(head_of_col == h, km, 0.0) for h in range(N_HEADS)]
    pad = jnp.zeros((LANE - N_HEADS * nblk, BRANCH_WIDTH), F32)
    gate = _dot_t(q, jnp.concatenate(per_head + [pad], axis=0), precision=lax.Precision.HIGHEST)

    lane = lax.broadcasted_iota(jnp.int32, (rows, LANE), 1)
    blk = lane & (nblk - 1)
    past = (blk < own) & (lane < N_HEADS * nblk)
    gate = jnp.where(past, gate, NEG_INF)
    rank = jnp.zeros((rows, LANE), F32)
    for d in range(1, nblk):
        before = pltpu.roll(gate, d, axis=1)
        rank = rank + jnp.where(blk >= d, jnp.where(before >= gate, 1.0, 0.0), 0.0)
        after = pltpu.roll(gate, LANE - d, axis=1)
        rank = rank + jnp.where(blk < nblk - d, jnp.where(after > gate, 1.0, 0.0), 0.0)
    picked = jnp.where(past, jnp.where(rank < n_sel, 1.0, 0.0), jnp.where(blk == own, 1.0, 0.0)).astype(BF16)

    first = lane < HEAD_DIM
    logits = []
    for h in range(N_HEADS):
        ps = slice(h // 2 * LANE, (h // 2 + 1) * LANE)
        mine = first if h % 2 == 0 else ~first
        qh = jnp.where(mine, q[:, ps] * HEAD_DIM ** -0.5, 0.0).astype(BF16)
        unpicked = (_dot(picked, spread_ref[h, :, :width]) - 1.0) * -MASKED
        logits.append(_dot_t(qh, k_ref[0, :width, ps]) + _bias_tile(toep_ref, h, qi, rows // LANE, nchunk)
                      + unpicked)
    weights = [_exp_weights(x) for x in logits]
    for pair in range(N_HEADS // 2):
        ps = slice(pair * LANE, (pair + 1) * LANE)
        outs = [_dot(weights[2 * pair], va_ref[0, :width, ps]), _dot(weights[2 * pair + 1], vb_ref[0, :width, ps])]
        o_ref[0, :, ps] = _pair_output(*outs).astype(o_ref.dtype)


def _moba_kernel(q_ref, k_ref, va_ref, vb_ref, km_ref, toep_ref, spread_ref, o_ref, *, seq):
    qi = pl.program_id(1)
    body = functools.partial(_moba_body, qi=qi, q_ref=q_ref, k_ref=k_ref, va_ref=va_ref, vb_ref=vb_ref,
                             km_ref=km_ref, toep_ref=toep_ref, spread_ref=spread_ref, o_ref=o_ref,
                             nblk=seq // MOBA_BLOCK)
    _for_causal_width(qi, seq // ATT_ROWS, ATT_ROWS, body, classes=ATT_WIDTH_CLASSES)


def _moba_spread(seq):
    nblk = seq // MOBA_BLOCK
    row = np.arange(LANE)[None, :, None]
    h = np.arange(N_HEADS)[:, None, None]
    blk_of_key = (np.arange(seq) // MOBA_BLOCK)[None, None, :]
    return jnp.asarray(row == h * nblk + blk_of_key, BF16)


def _moba(q, k, va, vb, k_mean, toep):
    b, seq, _ = q.shape
    nblk = seq // MOBA_BLOCK
    if nblk & (nblk - 1) or N_HEADS * nblk > LANE:
        raise ValueError("MoBA lane layout needs a power-of-two block count with heads*blocks <= 128")
    spread = _moba_spread(seq)
    qblock = pl.BlockSpec((1, ATT_ROWS, 256), lambda bi, qi: (bi, qi, 0))
    whole = pl.BlockSpec((1, seq, 256), lambda bi, qi: (bi, 0, 0))
    means = pl.BlockSpec((1, nblk, 256), lambda bi, qi: (bi, 0, 0))
    return pl.pallas_call(
        functools.partial(_moba_kernel, seq=seq),
        grid=(b, seq // ATT_ROWS),
        in_specs=[qblock, whole, whole, whole, means, _full(toep.shape), _full(spread.shape)],
        out_specs=qblock,
        out_shape=jax.ShapeDtypeStruct((b, seq, 256), BF16),
        compiler_params=_params("parallel", "parallel"),
        name="moba",
    )(q, k, va, vb, k_mean, toep, spread)


def _mla_body(width, qi, q_ref, k_ref, va_ref, vb_ref, causal_ref, o_ref):
    nchunk = width // LANE
    sub = MLA_ROWS // LANE
    causal = jnp.concatenate(
        [jnp.concatenate([causal_ref[0, jnp.clip(qi * sub + j - c + 1, 0, 2)] for c in range(nchunk)], axis=1)
         for j in range(sub)], axis=0)
    logits = []
    for h in range(N_HEADS):
        qs = slice(h * MLA_SLOT, (h + 1) * MLA_SLOT)
        logits.append(_dot_t(q_ref[0, :, qs], k_ref[0, :width, qs]) * MLA_QK ** -0.5 + causal)
    weights = [_exp_weights(x) for x in logits]
    for pair in range(N_HEADS // 2):
        ps = slice(pair * LANE, (pair + 1) * LANE)
        outs = [_dot(weights[2 * pair], va_ref[0, :width, ps]), _dot(weights[2 * pair + 1], vb_ref[0, :width, ps])]
        o_ref[0, :, ps] = _pair_output(*outs).astype(o_ref.dtype)


def _mla_kernel(q_ref, k_ref, va_ref, vb_ref, causal_ref, o_ref, *, seq):
    qi = pl.program_id(1)
    body = functools.partial(_mla_body, qi=qi, q_ref=q_ref, k_ref=k_ref, va_ref=va_ref, vb_ref=vb_ref,
                             causal_ref=causal_ref, o_ref=o_ref)
    _for_causal_width(qi, seq // MLA_ROWS, MLA_ROWS, body, classes=ATT_WIDTH_CLASSES)


def _mla(q, k, va, vb):
    b, seq, _ = q.shape
    tri = np.where(np.arange(LANE)[:, None] >= np.arange(LANE)[None, :], 0.0, NEG_INF)
    causal = jnp.asarray(np.stack([np.full((LANE, LANE), NEG_INF), tri, np.zeros((LANE, LANE))])[None], F32)
    return pl.pallas_call(
        functools.partial(_mla_kernel, seq=seq),
        grid=(b, seq // MLA_ROWS),
        in_specs=[pl.BlockSpec((1, MLA_ROWS, N_HEADS * MLA_SLOT), lambda bi, qi: (bi, qi, 0)),
                  pl.BlockSpec((1, seq, N_HEADS * MLA_SLOT), lambda bi, qi: (bi, 0, 0)),
                  pl.BlockSpec((1, seq, 256), lambda bi, qi: (bi, 0, 0)),
                  pl.BlockSpec((1, seq, 256), lambda bi, qi: (bi, 0, 0)),
                  _full(causal.shape)],
        out_specs=pl.BlockSpec((1, MLA_ROWS, 256), lambda bi, qi: (bi, qi, 0)),
        out_shape=jax.ShapeDtypeStruct((b, seq, 256), BF16),
        compiler_params=_params("parallel", "parallel"),
        name="mla",
    )(q, k, va, vb, causal)


def _dil_group(dil, seq, toep_ref, g, qf_ref, kf_ref, vf_ref, m_ref, l_ref, acc_ref):
    n_band = seq // (dil * LANE)

    def unit(u, carry):
        r = lax.div(u, n_band)
        n = lax.rem(u, n_band)
        if dil == 1:
            cur = pl.ds(pl.multiple_of(n * LANE, LANE), LANE)
            prev = pl.ds(pl.multiple_of(jnp.maximum(n - 1, 0) * LANE, LANE), LANE)
        else:
            cur = pl.ds(n * (LANE * dil) + r, LANE, stride=dil)
            prev = pl.ds(jnp.maximum(n - 1, 0) * (LANE * dil) + r, LANE, stride=dil)
        prev_tile = jnp.where(n > 0, 2, 0)
        first = lax.broadcasted_iota(jnp.int32, (LANE, LANE), 1) < HEAD_DIM
        kk = [jnp.concatenate([kf_ref[pair, prev, :], kf_ref[pair, cur, :]], axis=0).astype(BF16)
              for pair in range(N_HEADS // 2)]
        vv = [jnp.concatenate([vf_ref[pair, prev, :], vf_ref[pair, cur, :]], axis=0).astype(BF16)
              for pair in range(N_HEADS // 2)]
        logits = []
        for h in range(N_HEADS):
            mine = first if h % 2 == 0 else ~first
            bias = jnp.concatenate([toep_ref[g, h, prev_tile], toep_ref[g, h, 1]], axis=1)
            logits.append(_dot_t(jnp.where(mine, qf_ref[h // 2, cur, :], 0.0).astype(BF16), kk[h // 2]) + bias)
        ms = [jnp.max(x, axis=1, keepdims=True) for x in logits]
        ps = [jnp.exp(x - m) for x, m in zip(logits, ms)]
        ls = [jnp.sum(p, axis=1, keepdims=True) for p in ps]
        pvs = [_dot(p.astype(BF16), vv[h // 2]) for h, p in enumerate(ps)]
        for pair in range(N_HEADS // 2):
            m_new, l_new, pv = (jnp.where(first, x[2 * pair], x[2 * pair + 1]) for x in (ms, ls, pvs))
            m_old = m_ref[pair, cur, :]
            m_tot = jnp.maximum(m_old, m_new)
            a_old = jnp.exp(m_old - m_tot)
            a_new = jnp.exp(m_new - m_tot)
            m_ref[pair, cur, :] = m_tot
            l_ref[pair, cur, :] = a_old * l_ref[pair, cur, :] + a_new * l_new
            acc_ref[pair, cur, :] = a_old * acc_ref[pair, cur, :] + a_new * pv
        return carry

    lax.fori_loop(0, dil * n_band, unit, 0, unroll=2)


def _get_rows(ref, rows):
    return jnp.concatenate([ref[0, rows, :], ref[1, rows, :]], axis=1)


def _set_rows(ref, rows, val):
    ref[0, rows, :] = val[:, :LANE]
    ref[1, rows, :] = val[:, LANE:]


def _dil_kernel(q_ref, k_ref, v_ref, toep_ref, o_ref, qf_ref, kf_ref, vf_ref, m_ref, l_ref, acc_ref, *, seq):
    g = pl.program_id(1)
    everything = pl.ds(0, seq)
    _set_rows(qf_ref, everything, q_ref[0].astype(F32))
    _set_rows(kf_ref, everything, k_ref[0].astype(F32))
    _set_rows(vf_ref, everything, v_ref[0].astype(F32))

    @pl.when(g == 0)
    def _():
        m_ref[...] = jnp.full(m_ref.shape, NEG_INF, F32)
        l_ref[...] = jnp.zeros_like(l_ref)
        acc_ref[...] = jnp.zeros_like(acc_ref)

    for gi, (_, dil) in enumerate(DIL_PATTERNS):
        pl.when(g == gi)(functools.partial(_dil_group, dil, seq, toep_ref, gi, qf_ref, kf_ref, vf_ref,
                                           m_ref, l_ref, acc_ref))

    @pl.when(g == DIL_GROUPS - 1)
    def _():
        o_ref[0] = (_get_rows(acc_ref, everything) / _get_rows(l_ref, everything)).astype(o_ref.dtype)


def _dil(q, k, v, toep):
    b, seq, _ = q.shape
    group = pl.BlockSpec((1, seq, 256), lambda bi, g: (bi, 0, g))
    state = pltpu.VMEM((2, seq, LANE), F32)
    return pl.pallas_call(
        functools.partial(_dil_kernel, seq=seq),
        grid=(b, DIL_GROUPS),
        in_specs=[group, group, group, _full(toep.shape)],
        out_specs=pl.BlockSpec((1, seq, 256), lambda bi, g: (bi, 0, 0)),
        out_shape=jax.ShapeDtypeStruct((b, seq, 256), BF16),
        scratch_shapes=[state] * 6,
        compiler_params=_params("parallel", "arbitrary"),
        name="dil",
    )(q, k, v, toep)


def _merge_kernel(x_ref, xn_ref, oa_ref, ob_ref, oc_ref, od_ref, wg_ref, bg_ref, wb_ref, wo_ref, xo_ref):
    branches = (oa_ref, ob_ref, oc_ref, od_ref)
    xn = xn_ref[...]
    total = jnp.zeros(x_ref.shape, F32)
    for n in range(N_BRANCH):
        cols = slice(n * D_MODEL, (n + 1) * D_MODEL)
        gate = _sigmoid(_dot(xn, wg_ref[:, cols]) + bg_ref[:, cols])
        total = total + gate * _dot(branches[n][...], wb_ref[n])
    xo_ref[...] = x_ref[...] + _dot(total.astype(BF16), wo_ref[...])


def _merge(x, xn, outs, wg, bg, wb, wo):
    n = x.shape[0]
    rows = lambda width: pl.BlockSpec((ROW_TILE, width), lambda i: (i, 0))
    return pl.pallas_call(
        _merge_kernel,
        grid=(n // ROW_TILE,),
        in_specs=[rows(D_MODEL), rows(D_MODEL)] + [rows(256)] * N_BRANCH
                 + [_full(wg.shape), _full(bg.shape), _full(wb.shape), _full(wo.shape)],
        out_specs=rows(D_MODEL),
        out_shape=jax.ShapeDtypeStruct((n, D_MODEL), F32),
        compiler_params=_params("parallel"),
        name="merge",
    )(x, xn, *outs, wg, bg, wb, wo)


def _bucket_of_distance():
    d = np.arange(REL_MAX_DIST + 1)
    exact = REL_BUCKETS // 2
    nf = np.maximum(d, 1).astype(np.float32)
    log_b = exact + (np.log(nf / exact) / math.log(REL_MAX_DIST / exact) * (REL_BUCKETS - exact)).astype(np.int32)
    return np.where(d < exact, d, np.minimum(log_b, REL_BUCKETS - 1))


def _toeplitz_tiles(table, n_off, dil, max_steps=None):
    span = 2 * LANE
    m = np.arange(span)
    off = np.arange(n_off)[:, None]
    steps = off * LANE + LANE - 1 - m[None, :]
    live = (steps >= 0) if max_steps is None else (steps >= 0) & (steps <= max_steps)
    v = table[:, _bucket_of_distance()[np.clip(steps * dil, 0, REL_MAX_DIST)]]
    v = jnp.where(live[None], v, NEG_INF)
    flat = jnp.tile(v, (1, 1, LANE))[..., :LANE * (span - 1)]
    tiles = flat.reshape(table.shape[0], n_off, LANE, span - 1)[..., LANE - 1:]
    return jnp.concatenate([jnp.full((table.shape[0], 1, LANE, LANE), NEG_INF, F32), tiles], axis=1)


def _same_head(width, head):
    idx = np.arange(width) // head
    return jnp.asarray(idx[:, None] == idx[None, :], BF16)


def _rope_tables(seq):
    half = MLA_ROPE // 2
    freqs = ROPE_THETA ** (-np.arange(half, dtype=np.float64) / half)
    ang = np.arange(seq, dtype=np.float64)[:, None] * freqs[None, :]
    pad = np.zeros((seq, MLA_SLOT - MLA_QK))
    cos_h = np.concatenate([np.ones((seq, MLA_NOPE)), np.cos(ang), np.cos(ang), pad], axis=1)
    sin_h = np.concatenate([np.zeros((seq, MLA_NOPE)), np.sin(ang), np.sin(ang), pad], axis=1)
    return (jnp.asarray(np.tile(cos_h, (1, N_HEADS)), F32), jnp.asarray(np.tile(sin_h, (1, N_HEADS)), F32))


def _rot_half_cols():
    half = MLA_ROPE // 2
    src = np.arange(MLA_SLOT)
    sign = np.zeros(MLA_SLOT, np.float32)
    src[MLA_NOPE:MLA_NOPE + half] = np.arange(MLA_NOPE + half, MLA_QK)
    sign[MLA_NOPE:MLA_NOPE + half] = -1.0
    src[MLA_NOPE + half:MLA_QK] = np.arange(MLA_NOPE, MLA_NOPE + half)
    sign[MLA_NOPE + half:MLA_QK] = 1.0
    return src, sign


def _mixer(x, xn, seq, w_in, b_gate, qk_a, qk_b, qk_c, qk_d, mla_nq, w_uq, mla_nkv, w_ukv,
           w_branch, w_out, toeps, consts):
    n = x.shape[0]
    b = n // seq
    e64, e96, tri, cos, sin = consts
    toep_a, toep_b, toep_d = toeps
    row = lambda a: a.reshape(1, -1)
    tile4 = lambda g: jnp.tile(g, N_HEADS).reshape(1, -1)
    r3 = lambda a: a.reshape(b, seq, a.shape[-1])

    off_iq = OFF_A + 6 * HEAD_DIM
    off_ik = off_iq + N_HEADS * IDX_DIM
    iq_rep = jnp.concatenate([jnp.tile(w_in[:, off_iq + h * IDX_DIM:off_iq + (h + 1) * IDX_DIM], (1, 4))
                              for h in range(N_HEADS)], axis=1)
    ik_rep = jnp.tile(w_in[:, off_ik:off_ik + IDX_DIM], (1, 4))
    iw_pad = jnp.pad(w_in[:, off_ik + IDX_DIM:OFF_A + W_A], ((0, 0), (0, LANE - N_HEADS)))
    half_pad = ((0, 0), (0, LANE - HEAD_DIM))
    q_slots = jnp.concatenate([jnp.pad(w_in[:, OFF_A + h * HEAD_DIM:OFF_A + (h + 1) * HEAD_DIM], half_pad)
                               for h in range(N_HEADS)], axis=1)
    off_k = OFF_A + N_HEADS * HEAD_DIM
    k_slot = jnp.pad(w_in[:, off_k:off_k + HEAD_DIM], half_pad)
    v_twice = jnp.tile(w_in[:, off_k + HEAD_DIM:off_iq], (1, 2))
    wa = jnp.concatenate([q_slots, k_slot, v_twice, iq_rep, ik_rep, iw_pad], axis=1).astype(BF16)
    gq_slots = jnp.tile(jnp.pad(qk_a[0], (0, LANE - HEAD_DIM)), N_HEADS).reshape(1, -1)
    gk_slot = jnp.pad(qk_a[1], (0, LANE - HEAD_DIM)).reshape(1, -1)
    qa, ka, va, vb, iq, ik, iw = _proj_a(xn, wa, e96, gq_slots, gk_slot)
    out_a = _dsa(r3(qa), r3(ka), r3(va), r3(vb), r3(iq), r3(ik), r3(iw), toep_a, tri).reshape(n, 256)

    wb_in = w_in[:, OFF_B:OFF_B + W_B].astype(BF16)
    qb, kb, vb_first, vb_second, kmean = _proj_b(xn, wb_in, e64, tile4(qk_b[0]), tile4(qk_b[1]))
    out_b = _moba(r3(qb), r3(kb), r3(vb_first), r3(vb_second),
                  kmean.reshape(b, seq // MOBA_BLOCK, BRANCH_WIDTH), toep_b).reshape(n, 256)

    src, sign = _rot_half_cols()
    src4 = np.concatenate([h * MLA_SLOT + src for h in range(N_HEADS)])
    sign4 = jnp.asarray(np.tile(sign, N_HEADS))
    slot_pad = MLA_SLOT - MLA_QK
    w_kr = w_in[:, OFF_C + MLA_Q_LORA + MLA_KV_LORA:OFF_C + W_C]
    kr_slots = jnp.pad(w_kr, ((0, 0), (MLA_NOPE, slot_pad)))
    kr_slots = jnp.tile(kr_slots, (1, N_HEADS))
    wc = jnp.concatenate([w_in[:, OFF_C:OFF_C + MLA_Q_LORA + MLA_KV_LORA], kr_slots,
                          kr_slots[:, src4] * sign4], axis=1).astype(BF16)
    slots = lambda w: jnp.pad(w.reshape(w.shape[0], N_HEADS, -1),
                              ((0, 0), (0, 0), (0, MLA_SLOT - w.shape[1] // N_HEADS))).reshape(w.shape[0], -1)
    wuq = slots(w_uq)
    ukv = w_ukv.reshape(MLA_KV_LORA, N_HEADS, MLA_NOPE + MLA_V)
    wuk = slots(ukv[:, :, :MLA_NOPE].reshape(MLA_KV_LORA, -1))
    wuv = ukv[:, :, MLA_NOPE:].reshape(MLA_KV_LORA, -1)
    gq = jnp.tile(jnp.pad(qk_c[0], (0, slot_pad)), N_HEADS)
    gk = jnp.tile(jnp.pad(qk_c[1], (0, slot_pad)), N_HEADS)
    qc, kc, vc_first, vc_second = _proj_c(
        xn, wc, e96, row(mla_nq), row(mla_nkv), wuq.astype(BF16), (wuq[:, src4] * sign4).astype(BF16),
        wuk.astype(BF16), wuv.astype(BF16), row(gq), row(gq[src4]), row(gk), row(gk[src4]), cos, sin, seq)
    out_c = _mla(r3(qc), r3(kc), r3(vc_first), r3(vc_second)).reshape(n, 256)

    wd = w_in[:, OFF_D:OFF_D + W_D].astype(BF16)
    qd, kd, vd = _proj_d(xn, wd, e64, tile4(qk_d[0]), tile4(qk_d[1]))
    out_d = _dil(r3(qd), r3(kd), r3(vd), toep_d).reshape(n, 256)

    wg = w_in[:, OFF_G:OFF_G + W_G].astype(BF16)
    return _merge(x, xn, (out_a, out_b, out_c, out_d), wg, b_gate.reshape(1, -1),
                  w_branch.astype(BF16), w_out.astype(BF16))


def kernel(x, norm_gain, w_in, b_gate, qk_gain_a, qk_gain_b, qk_gain_c, qk_gain_d, mla_norm_q,
           w_mla_uq, mla_norm_kv, w_mla_ukv, w_branch, w_out, rel_bias, w_ffn_in, w_ffn_out):
    b, seq, d = x.shape
    depth = norm_gain.shape[0]
    toep_a = _toeplitz_tiles(rel_bias[0:4], seq // LANE, 1)
    toep_b = _toeplitz_tiles(rel_bias[4:8], seq // LANE, 1)
    toep_d = jnp.stack([_toeplitz_tiles(rel_bias[8 + 4 * g:12 + 4 * g], 2, dil, max_steps=window // dil)
                        for g, (window, dil) in enumerate(DIL_PATTERNS)])
    tri = jnp.asarray(np.arange(LANE)[:, None] < np.arange(LANE)[None, :], BF16)
    consts = (_same_head(256, HEAD_DIM), _same_head(256, MLA_SLOT), tri) + _rope_tables(seq)

    x = x.reshape(b * seq, d)
    for l in range(depth):
        g = norm_gain[l]
        x, xn = _ffn(x, g[0:1], g[1:2], w_ffn_in[l, 0].astype(BF16), w_ffn_out[l, 0].astype(BF16))
        x = _mixer(x, xn, seq, w_in[l], b_gate[l], qk_gain_a[l], qk_gain_b[l], qk_gain_c[l],
                   qk_gain_d[l], mla_norm_q[l], w_mla_uq[l], mla_norm_kv[l], w_mla_ukv[l],
                   w_branch[l], w_out[l], (toep_a, toep_b, toep_d), consts)
        x, _ = _ffn(x, g[2:3], g[2:3], w_ffn_in[l, 1].astype(BF16), w_ffn_out[l, 1].astype(BF16))
    return x.reshape(b, seq, d)
```

```python
import functools
import math

import numpy as np
import jax
import jax.numpy as jnp
from jax import lax
from jax.experimental import pallas as pl
from jax.experimental.pallas import tpu as pltpu

D_MODEL = 1024
HEAD_DIM = 64
N_HEADS = 4
BRANCH_WIDTH = N_HEADS * HEAD_DIM
IDX_DIM = 32
DSA_TOPK = 256
MOBA_BLOCK = 256
MOBA_TOPK = 3
MLA_Q_LORA = 384
MLA_KV_LORA = 256
MLA_NOPE = 64
MLA_ROPE = 32
MLA_QK = MLA_NOPE + MLA_ROPE
MLA_V = 64
MLA_SLOT = 128
ROPE_THETA = 10000.0
DIL_PATTERNS = ((128, 1), (512, 4), (2048, 16))
DIL_GROUPS = 3
N_BRANCH = 4
D_FF = 2816
REL_BUCKETS = 32
REL_MAX_DIST = 2048
RMS_EPS = 1e-6

OFF_A = 0
W_A = 4 * HEAD_DIM + 2 * HEAD_DIM + 4 * IDX_DIM + IDX_DIM + 4
OFF_B = OFF_A + W_A
W_B = 3 * BRANCH_WIDTH
OFF_C = OFF_B + W_B
W_C = MLA_Q_LORA + MLA_KV_LORA + MLA_ROPE
OFF_D = OFF_C + W_C
W_D = 3 * DIL_GROUPS * BRANCH_WIDTH
OFF_G = OFF_D + W_D
W_G = N_BRANCH * D_MODEL

LANE = 128
ROW_TILE = 512
FFN_ROWS = 512
FF_CHUNK = 1408
WIDTH_CLASSES = 8
ATT_ROWS = 256
MLA_ROWS = 256
ATT_WIDTH_CLASSES = 4
DSA_WIDTH_CLASSES = 4
DSA_ROWS = 256
MASKED = -1e30
VMEM_LIMIT = 56 * 1024 * 1024

F32 = jnp.float32
BF16 = jnp.bfloat16
NEG_INF = float("-inf")
INT_MIN = -2 ** 31

_CONTRACT_LAST = (((1,), (1,)), ((), ()))


def _dot(a, b, precision=None):
    return jnp.dot(a, b, preferred_element_type=F32, precision=precision)


def _dot_t(a, b, precision=None):
    return lax.dot_general(a, b, _CONTRACT_LAST, preferred_element_type=F32, precision=precision)


def _rms(x, gain):
    return x * lax.rsqrt(jnp.mean(x * x, axis=-1, keepdims=True) + RMS_EPS) * gain


def _sigmoid(x):
    return 1.0 / (1.0 + jnp.exp(-x))


def _head_sumsq(y, e_ref):
    y2 = y * y
    hi = y2.astype(BF16)
    lo = (y2 - hi.astype(F32)).astype(BF16)
    e = e_ref[...]
    step = e.shape[0]
    blocks = [_dot(hi[:, c:c + step], e) + _dot(lo[:, c:c + step], e) for c in range(0, y.shape[1], step)]
    return blocks[0] if len(blocks) == 1 else jnp.concatenate(blocks, axis=1)


def _head_norm(y, e_ref, gain, width):
    return y * lax.rsqrt(_head_sumsq(y, e_ref) * (1.0 / width) + RMS_EPS) * gain


def _params(*sem):
    return pltpu.CompilerParams(dimension_semantics=sem, vmem_limit_bytes=VMEM_LIMIT)


def _full(shape):
    return pl.BlockSpec(shape, lambda *_: (0,) * len(shape))


def _resident(shape):
    return pl.BlockSpec(shape, lambda *_: (0,) * len(shape), pipeline_mode=pl.Buffered(1))


def _ffn_kernel(x_ref, g_ref, gn_ref, wi_ref, wo_ref, xo_ref, xn_ref):
    half = FFN_ROWS // 2
    for s in range(2):
        rows = slice(s * half, (s + 1) * half)
        x = x_ref[rows, :]
        xb = _rms(x, g_ref[...]).astype(BF16)
        acc = None
        for j in range(D_FF // FF_CHUNK):
            gate = _dot(xb, wi_ref[:, j * FF_CHUNK:(j + 1) * FF_CHUNK])
            up = _dot(xb, wi_ref[:, D_FF + j * FF_CHUNK:D_FF + (j + 1) * FF_CHUNK])
            h = (gate * _sigmoid(gate) * up).astype(BF16)
            part = _dot(h, wo_ref[j * FF_CHUNK:(j + 1) * FF_CHUNK, :])
            acc = part if acc is None else acc + part
        xo = x + 0.5 * acc
        xo_ref[rows, :] = xo
        xn_ref[rows, :] = _rms(xo, gn_ref[...]).astype(BF16)


def _ffn(x, gain, gain_next, w_in, w_out):
    n = x.shape[0]
    rows = pl.BlockSpec((FFN_ROWS, D_MODEL), lambda i: (i, 0))
    return pl.pallas_call(
        _ffn_kernel,
        grid=(n // FFN_ROWS,),
        in_specs=[rows, _full((1, D_MODEL)), _full((1, D_MODEL)), _resident(w_in.shape), _resident(w_out.shape)],
        out_specs=[rows, rows],
        out_shape=[jax.ShapeDtypeStruct((n, D_MODEL), F32), jax.ShapeDtypeStruct((n, D_MODEL), BF16)],
        compiler_params=_params("parallel"),
        name="ffn",
    )(x, gain, gain_next, w_in, w_out)


def _split_terms(rep, hi_lanes):
    hi = rep.astype(BF16).astype(F32)
    return jnp.where(hi_lanes, hi, rep - hi).astype(BF16)


def _with_ones(v):
    first = (lax.broadcasted_iota(jnp.int32, (1, v.shape[1]), 1) & (LANE - 1)) < HEAD_DIM
    return jnp.where(first, v, 1.0).astype(BF16), jnp.where(first, 1.0, v).astype(BF16)


def _proj_a_kernel(xn_ref, w_ref, e_ref, gq_ref, gk_ref, q_ref, k_ref, va_ref, vb_ref, iq_ref, ik_ref, iw_ref):
    p = _dot(xn_ref[...], w_ref[...])
    q = _head_norm(p[:, :512], e_ref, gq_ref[...], HEAD_DIM) * HEAD_DIM ** -0.5
    q_ref[...] = q.astype(BF16)
    k = p[:, 512:640]
    k_ms = jnp.sum(k * k, axis=-1, keepdims=True) * (1.0 / HEAD_DIM)
    k_ref[...] = (k * lax.rsqrt(k_ms + RMS_EPS) * gk_ref[...]).astype(BF16)
    va_ref[...], vb_ref[...] = _with_ones(p[:, 640:768])
    lane = lax.broadcasted_iota(jnp.int32, (1, 4 * LANE), 1)
    copy = lax.shift_right_logical(lane, int(math.log2(IDX_DIM))) & 3
    iq_ref[...] = _split_terms(p[:, 768:1280], copy < 2)
    ik_ref[...] = _split_terms(p[:, 1280:1408], (copy[:, :LANE] & 1) == 0)
    iw_ref[...] = p[:, 1408:1536]


def _proj_a(xn, w, e_slots, gq, gk):
    n = xn.shape[0]
    rows = lambda width: pl.BlockSpec((ROW_TILE, width), lambda i: (i, 0))
    return pl.pallas_call(
        _proj_a_kernel,
        grid=(n // ROW_TILE,),
        in_specs=[rows(D_MODEL), _full(w.shape), _full(e_slots.shape), _full(gq.shape), _full(gk.shape)],
        out_specs=[rows(512), rows(128), rows(128), rows(128), rows(512), rows(128), rows(128)],
        out_shape=[jax.ShapeDtypeStruct((n, 512), BF16), jax.ShapeDtypeStruct((n, 128), BF16),
                   jax.ShapeDtypeStruct((n, 128), BF16), jax.ShapeDtypeStruct((n, 128), BF16),
                   jax.ShapeDtypeStruct((n, 512), BF16),
                   jax.ShapeDtypeStruct((n, 128), BF16), jax.ShapeDtypeStruct((n, 128), F32)],
        compiler_params=_params("parallel"),
        name="proj_a",
    )(xn, w, e_slots, gq, gk)


def _proj_b_kernel(xn_ref, w_ref, e_ref, gq_ref, gk_ref, q_ref, k_ref, va_ref, vb_ref, km_ref):
    p = _dot(xn_ref[...], w_ref[...])
    q_ref[...] = _head_norm(p[:, :256], e_ref, gq_ref[...], HEAD_DIM)
    k = _head_norm(p[:, 256:512], e_ref, gk_ref[...], HEAD_DIM)
    k_ref[...] = k.astype(BF16)
    va_ref[...], vb_ref[...] = _with_ones(p[:, 512:768])
    km = jnp.mean(k.reshape(ROW_TILE // MOBA_BLOCK, MOBA_BLOCK, BRANCH_WIDTH), axis=1)
    for j in range(ROW_TILE // MOBA_BLOCK):
        km_ref[j] = km[j:j + 1]


def _proj_b(xn, w, e64, gq, gk):
    n = xn.shape[0]
    rows = lambda width: pl.BlockSpec((ROW_TILE, width), lambda i: (i, 0))
    per_tile = ROW_TILE // MOBA_BLOCK
    return pl.pallas_call(
        _proj_b_kernel,
        grid=(n // ROW_TILE,),
        in_specs=[rows(D_MODEL), _full(w.shape), _full(e64.shape), _full(gq.shape), _full(gk.shape)],
        out_specs=[rows(256), rows(256), rows(256), rows(256),
                   pl.BlockSpec((per_tile, 1, BRANCH_WIDTH), lambda i: (i, 0, 0))],
        out_shape=[jax.ShapeDtypeStruct((n, 256), F32), jax.ShapeDtypeStruct((n, 256), BF16),
                   jax.ShapeDtypeStruct((n, 256), BF16), jax.ShapeDtypeStruct((n, 256), BF16),
                   jax.ShapeDtypeStruct((n // MOBA_BLOCK, 1, BRANCH_WIDTH), F32)],
        compiler_params=_params("parallel"),
        name="proj_b",
    )(xn, w, e64, gq, gk)


def _proj_c_kernel(xn_ref, w_ref, e_ref, nq_ref, nkv_ref, wuq_ref, wuqs_ref, wuk_ref, wuv_ref,
                   gq_ref, gqs_ref, gk_ref, gks_ref, cos_ref, sin_ref, q_ref, k_ref, va_ref, vb_ref):
    p = _dot(xn_ref[...], w_ref[...])
    cos = cos_ref[...]
    sin = sin_ref[...]
    xq = _rms(p[:, :384], nq_ref[...]).astype(BF16)
    qa = _dot(xq, wuq_ref[...])
    qs = _dot(xq, wuqs_ref[...])
    rq = lax.rsqrt(_head_sumsq(qa, e_ref) * (1.0 / MLA_QK) + RMS_EPS)
    q_ref[...] = (rq * (qa * gq_ref[...] * cos + qs * gqs_ref[...] * sin)).astype(BF16)
    xkv = _rms(p[:, 384:640], nkv_ref[...]).astype(BF16)
    ka = _dot(xkv, wuk_ref[...]) + p[:, 640:1152]
    ks = p[:, 1152:1664]
    rk = lax.rsqrt(_head_sumsq(ka, e_ref) * (1.0 / MLA_QK) + RMS_EPS)
    k_ref[...] = (rk * (ka * gk_ref[...] * cos + ks * gks_ref[...] * sin)).astype(BF16)
    va_ref[...], vb_ref[...] = _with_ones(_dot(xkv, wuv_ref[...]))


def _proj_c(xn, w, e96, nq, nkv, wuq, wuqs, wuk, wuv, gq, gqs, gk, gks, cos, sin, seq):
    n = xn.shape[0]
    rows = lambda width: pl.BlockSpec((ROW_TILE, width), lambda i: (i, 0))
    pos = pl.BlockSpec((ROW_TILE, N_HEADS * MLA_SLOT), lambda i: (i % (seq // ROW_TILE), 0))
    consts = (w, e96, nq, nkv, wuq, wuqs, wuk, wuv, gq, gqs, gk, gks)
    return pl.pallas_call(
        _proj_c_kernel,
        grid=(n // ROW_TILE,),
        in_specs=[rows(D_MODEL)] + [_full(c.shape) for c in consts] + [pos, pos],
        out_specs=[rows(N_HEADS * MLA_SLOT), rows(N_HEADS * MLA_SLOT), rows(256), rows(256)],
        out_shape=[jax.ShapeDtypeStruct((n, N_HEADS * MLA_SLOT), BF16),
                   jax.ShapeDtypeStruct((n, N_HEADS * MLA_SLOT), BF16),
                   jax.ShapeDtypeStruct((n, 256), BF16), jax.ShapeDtypeStruct((n, 256), BF16)],
        compiler_params=_params("parallel"),
        name="proj_c",
    )(xn, *consts, cos, sin)


def _proj_d_kernel(xn_ref, w_ref, e_ref, gq_ref, gk_ref, q_ref, k_ref, v_ref):
    xn = xn_ref[...]
    for c in range(DIL_GROUPS):
        lo, hi = c * 256, (c + 1) * 256
        pq = _dot(xn, w_ref[:, lo:hi])
        q_ref[:, lo:hi] = (_head_norm(pq, e_ref, gq_ref[...], HEAD_DIM) * HEAD_DIM ** -0.5).astype(BF16)
        pk = _dot(xn, w_ref[:, 768 + lo:768 + hi])
        k_ref[:, lo:hi] = _head_norm(pk, e_ref, gk_ref[...], HEAD_DIM).astype(BF16)
        v_ref[:, lo:hi] = _dot(xn, w_ref[:, 1536 + lo:1536 + hi]).astype(BF16)


def _proj_d(xn, w, e64, gq, gk):
    n = xn.shape[0]
    rows = lambda width: pl.BlockSpec((ROW_TILE, width), lambda i: (i, 0))
    return pl.pallas_call(
        _proj_d_kernel,
        grid=(n // ROW_TILE,),
        in_specs=[rows(D_MODEL), _full(w.shape), _full(e64.shape), _full(gq.shape), _full(gk.shape)],
        out_specs=[rows(768), rows(768), rows(768)],
        out_shape=[jax.ShapeDtypeStruct((n, 768), BF16)] * 3,
        compiler_params=_params("parallel"),
        name="proj_d",
    )(xn, w, e64, gq, gk)


def _for_causal_width(qi, n_qtiles, tile_rows, body, classes=WIDTH_CLASSES):
    n_cls = min(classes, n_qtiles)
    per = n_qtiles // n_cls
    for c in range(n_cls):
        pl.when((qi >= c * per) & (qi < (c + 1) * per))(functools.partial(body, (c + 1) * per * tile_rows))


def _bias_rows(toep_ref, h, qblk, nchunk):
    return jnp.concatenate([toep_ref[h, jnp.maximum(qblk - c + 1, 0)] for c in range(nchunk)], axis=1)


def _bias_tile(toep_ref, h, qi, sub_blocks, nchunk):
    return jnp.concatenate([_bias_rows(toep_ref, h, qi * sub_blocks + j, nchunk) for j in range(sub_blocks)],
                           axis=0)


def _softmax_pv(logits, v):
    m = jnp.max(logits, axis=1, keepdims=True)
    p = jnp.exp(logits - m)
    l = jnp.sum(p, axis=1, keepdims=True)
    return _dot(p.astype(BF16), v) / l


def _exp_pv(logits, v_ones):
    return _dot(_exp_weights(logits), v_ones)


def _exp_weights(logits):
    m = jnp.max(logits, axis=1, keepdims=True)
    return jnp.exp(logits - m).astype(BF16)


def _pair_output(first_head, second_head):
    first = lax.broadcasted_iota(jnp.int32, first_head.shape, 1) < HEAD_DIM
    numer = jnp.where(first, first_head, second_head)
    denom = jnp.where(first, pltpu.roll(first_head, HEAD_DIM, axis=1), pltpu.roll(second_head, HEAD_DIM, axis=1))
    return numer / denom


def _dsa_body(width, qi, q_ref, k_ref, va_ref, vb_ref, iq_ref, ik_ref, iw_ref, toep_ref, tri_ref, o_ref,
              sel_ref, n_sel):
    rows = DSA_ROWS
    nchunk = width // LANE
    ik = ik_ref[0, :width, :]
    iw = iw_ref[0][:, :N_HEADS] * (N_HEADS ** -0.5 * IDX_DIM ** -0.5)

    r = _dot_t(jnp.concatenate([iq_ref[0, :, h * LANE:(h + 1) * LANE] for h in range(N_HEADS)], axis=0), ik)
    score = None
    for h in range(N_HEADS):
        term = jnp.maximum(r[h * rows:(h + 1) * rows], 0.0) * iw[:, h:h + 1]
        score = term if score is None else score + term

    t = qi * rows + lax.broadcasted_iota(jnp.int32, (rows, width), 0)
    s = lax.broadcasted_iota(jnp.int32, (rows, width), 1)

    bits = pltpu.bitcast(score, jnp.int32)
    key = jnp.where(bits < 0, jnp.int32(INT_MIN) - bits, bits)
    key = jnp.where(s <= t, key, jnp.int32(INT_MIN))

    half_min = -2 ** 15
    upper = lax.shift_right_arithmetic(key, 16).astype(jnp.int16)
    lower = ((key & jnp.int32(0xFFFF)) + half_min).astype(jnp.int16)

    def lane_counts(hit):
        ones = jnp.where(hit, jnp.int16(1), jnp.int16(0))
        acc = ones[:, :LANE]
        for c in range(1, nchunk):
            acc = acc + ones[:, c * LANE:(c + 1) * LANE]
        return acc

    def total(acc):
        return jnp.sum(acc.astype(F32), axis=1, keepdims=True)

    def count(hit):
        return total(lane_counts(hit))

    def search(half, need):
        top, bottom = slice(0, rows // 2), slice(rows // 2, rows)
        need_rows = jnp.broadcast_to(need, (rows, 1)).astype(F32)

        def bit(step):
            return jnp.where(step < 16, lax.shift_left(jnp.int32(1), jnp.maximum(15 - step, 0)), 0)

        def hits(part, cand):
            return lane_counts(half[part] >= (cand + half_min).astype(jnp.int16))

        def body(i, carry):
            ans_top, acc_top, ans_bottom = carry
            cand_top = ans_top | bit(i)
            ans_top = jnp.where(total(acc_top) >= need_rows[top], cand_top, ans_top)
            cand_bottom = ans_bottom | bit(i)
            ans_bottom = jnp.where(total(hits(bottom, cand_bottom)) >= need_rows[bottom], cand_bottom, ans_bottom)
            return ans_top, hits(top, ans_top | bit(i + 1)), ans_bottom

        zero = jnp.zeros((rows // 2, 1), jnp.int32)
        ans_top, _, ans_bottom = lax.fori_loop(0, 16, body, (zero, hits(top, zero | bit(0)), zero))
        return jnp.concatenate([ans_top, ans_bottom], axis=0)

    thr_upper = search(upper, n_sel) + half_min
    thr_upper16 = thr_upper.astype(jnp.int16)
    n_over = count(upper > thr_upper16)
    candidates = jnp.where(upper == thr_upper16, lower, jnp.int16(half_min))
    thr = thr_upper * 65536 + search(candidates, n_sel - n_over)

    sel = jnp.where(key >= jnp.maximum(thr, jnp.int32(INT_MIN + 1)), 1.0, 0.0)
    sel_ref[:, :width] = sel

    overflow = jnp.sum(sel, axis=1, keepdims=True) > n_sel

    @pl.when(jnp.max(jnp.where(overflow, 1.0, 0.0)) > 0.0)
    def _():
        above = key > thr
        room = n_sel - jnp.sum(jnp.where(above, 1.0, 0.0), axis=1, keepdims=True)
        before = jnp.zeros((rows, 1), F32)
        for c in range(nchunk):
            cols = slice(c * LANE, (c + 1) * LANE)
            tc = jnp.where(key[:, cols] == thr, 1.0, 0.0)
            rank = before + _dot(tc.astype(BF16), tri_ref[...])
            keep = above[:, cols] | ((tc > 0.0) & (rank < room))
            sel_ref[:, cols] = jnp.where(keep, 1.0, 0.0)
            before = before + jnp.sum(tc, axis=1, keepdims=True)

    unselected = (sel_ref[:, :width] - 1.0) * -MASKED
    k = k_ref[0, :width, :]
    qk = _dot_t(jnp.concatenate([q_ref[0, :, h * LANE:(h + 1) * LANE] for h in range(N_HEADS)], axis=0), k)
    logits = [qk[h * rows:(h + 1) * rows] + _bias_tile(toep_ref, h, qi, rows // LANE, nchunk) + unselected
              for h in range(N_HEADS)]
    weights = [_exp_weights(x) for x in logits]
    for pair in range(N_HEADS // 2):
        outs = [_dot(weights[2 * pair], va_ref[0, :width, :]), _dot(weights[2 * pair + 1], vb_ref[0, :width, :])]
        o_ref[0, :, pair * LANE:(pair + 1) * LANE] = _pair_output(*outs).astype(o_ref.dtype)


def _dsa_kernel(*refs, seq, n_sel):
    qi = pl.program_id(1)
    _for_causal_width(qi, seq // DSA_ROWS, DSA_ROWS,
                      functools.partial(_dsa_body, qi=qi, n_sel=n_sel, **_named(refs)), classes=DSA_WIDTH_CLASSES)


def _named(refs):
    names = ("q_ref", "k_ref", "va_ref", "vb_ref", "iq_ref", "ik_ref", "iw_ref", "toep_ref", "tri_ref", "o_ref",
             "sel_ref")
    return dict(zip(names, refs, strict=True))


def _dsa(q, k, va, vb, iq, ik, iw, toep, tri):
    b, seq, _ = q.shape
    n_sel = min(DSA_TOPK, seq // 4)
    qblock = lambda width: pl.BlockSpec((1, DSA_ROWS, width), lambda bi, qi: (bi, qi, 0))
    whole = lambda width: pl.BlockSpec((1, seq, width), lambda bi, qi: (bi, 0, 0))
    return pl.pallas_call(
        functools.partial(_dsa_kernel, seq=seq, n_sel=n_sel),
        grid=(b, seq // DSA_ROWS),
        in_specs=[qblock(512), whole(128), whole(128), whole(128), qblock(512), whole(128), qblock(128),
                  _full(toep.shape), _full(tri.shape)],
        out_specs=qblock(256),
        out_shape=jax.ShapeDtypeStruct((b, seq, 256), BF16),
        scratch_shapes=[pltpu.VMEM((DSA_ROWS, seq), F32)],
        compiler_params=_params("parallel", "parallel"),
        name="dsa",
    )(q, k, va, vb, iq, ik, iw, toep, tri)


def _moba_body(width, qi, q_ref, k_ref, va_ref, vb_ref, km_ref, toep_ref, spread_ref, o_ref, nblk):
    nchunk = width // LANE
    n_sel = min(MOBA_TOPK, nblk - 1)
    rows = ATT_ROWS
    own = lax.shift_right_logical(qi * rows, int(math.log2(MOBA_BLOCK)))
    q = q_ref[0]

    km = km_ref[0]
    head_of_col = lax.shift_right_logical(lax.broadcasted_iota(jnp.int32, (nblk, BRANCH_WIDTH), 1),
                                          int(math.log2(HEAD_DIM)))
    per_head = [jnp.where(head_of_col == h, km, 0.0) for h in range(N_HEADS)]
    pad = jnp.zeros((LANE - N_HEADS * nblk, BRANCH_WIDTH), F32)
    gate = _dot_t(q, jnp.concatenate(per_head + [pad], axis=0), precision=lax.Precision.HIGHEST)

    lane = lax.broadcasted_iota(jnp.int32, (rows, LANE), 1)
    blk = lane & (nblk - 1)
    past = (blk < own) & (lane < N_HEADS * nblk)
    gate = jnp.where(past, gate, NEG_INF)
    rank = jnp.zeros((rows, LANE), F32)
    for d in range(1, nblk):
        before = pltpu.roll(gate, d, axis=1)
        rank = rank + jnp.where(blk >= d, jnp.where(before >= gate, 1.0, 0.0), 0.0)
        after = pltpu.roll(gate, LANE - d, axis=1)
        rank = rank + jnp.where(blk < nblk - d, jnp.where(after > gate, 1.0, 0.0), 0.0)
    picked = jnp.where(past, jnp.where(rank < n_sel, 1.0, 0.0), jnp.where(blk == own, 1.0, 0.0)).astype(BF16)

    first = lane < HEAD_DIM
    logits = []
    for h in range(N_HEADS):
        ps = slice(h // 2 * LANE, (h // 2 + 1) * LANE)
        mine = first if h % 2 == 0 else ~first
        qh = jnp.where(mine, q[:, ps] * HEAD_DIM ** -0.5, 0.0).astype(BF16)
        unpicked = (_dot(picked, spread_ref[h, :, :width]) - 1.0) * -MASKED
        logits.append(_dot_t(qh, k_ref[0, :width, ps]) + _bias_tile(toep_ref, h, qi, rows // LANE, nchunk)
                      + unpicked)
    weights = [_exp_weights(x) for x in logits]
    for pair in range(N_HEADS // 2):
        ps = slice(pair * LANE, (pair + 1) * LANE)
        outs = [_dot(weights[2 * pair], va_ref[0, :width, ps]), _dot(weights[2 * pair + 1], vb_ref[0, :width, ps])]
        o_ref[0, :, ps] = _pair_output(*outs).astype(o_ref.dtype)


def _moba_kernel(q_ref, k_ref, va_ref, vb_ref, km_ref, toep_ref, spread_ref, o_ref, *, seq):
    qi = pl.program_id(1)
    body = functools.partial(_moba_body, qi=qi, q_ref=q_ref, k_ref=k_ref, va_ref=va_ref, vb_ref=vb_ref,
                             km_ref=km_ref, toep_ref=toep_ref, spread_ref=spread_ref, o_ref=o_ref,
                             nblk=seq // MOBA_BLOCK)
    _for_causal_width(qi, seq // ATT_ROWS, ATT_ROWS, body, classes=ATT_WIDTH_CLASSES)


def _moba_spread(seq):
    nblk = seq // MOBA_BLOCK
    row = np.arange(LANE)[None, :, None]
    h = np.arange(N_HEADS)[:, None, None]
    blk_of_key = (np.arange(seq) // MOBA_BLOCK)[None, None, :]
    return jnp.asarray(row == h * nblk + blk_of_key, BF16)


def _moba(q, k, va, vb, k_mean, toep):
    b, seq, _ = q.shape
    nblk = seq // MOBA_BLOCK
    if nblk & (nblk - 1) or N_HEADS * nblk > LANE:
        raise ValueError("MoBA lane layout needs a power-of-two block count with heads*blocks <= 128")
    spread = _moba_spread(seq)
    qblock = pl.BlockSpec((1, ATT_ROWS, 256), lambda bi, qi: (bi, qi, 0))
    whole = pl.BlockSpec((1, seq, 256), lambda bi, qi: (bi, 0, 0))
    means = pl.BlockSpec((1, nblk, 256), lambda bi, qi: (bi, 0, 0))
    return pl.pallas_call(
        functools.partial(_moba_kernel, seq=seq),
        grid=(b, seq // ATT_ROWS),
        in_specs=[qblock, whole, whole, whole, means, _full(toep.shape), _full(spread.shape)],
        out_specs=qblock,
        out_shape=jax.ShapeDtypeStruct((b, seq, 256), BF16),
        compiler_params=_params("parallel", "parallel"),
        name="moba",
    )(q, k, va, vb, k_mean, toep, spread)


def _mla_body(width, qi, q_ref, k_ref, va_ref, vb_ref, causal_ref, o_ref):
    nchunk = width // LANE
    sub = MLA_ROWS // LANE
    causal = jnp.concatenate(
        [jnp.concatenate([causal_ref[0, jnp.clip(qi * sub + j - c + 1, 0, 2)] for c in range(nchunk)], axis=1)
         for j in range(sub)], axis=0)
    logits = []
    for h in range(N_HEADS):
        qs = slice(h * MLA_SLOT, (h + 1) * MLA_SLOT)
        logits.append(_dot_t(q_ref[0, :, qs], k_ref[0, :width, qs]) * MLA_QK ** -0.5 + causal)
    weights = [_exp_weights(x) for x in logits]
    for pair in range(N_HEADS // 2):
        ps = slice(pair * LANE, (pair + 1) * LANE)
        outs = [_dot(weights[2 * pair], va_ref[0, :width, ps]), _dot(weights[2 * pair + 1], vb_ref[0, :width, ps])]
        o_ref[0, :, ps] = _pair_output(*outs).astype(o_ref.dtype)


def _mla_kernel(q_ref, k_ref, va_ref, vb_ref, causal_ref, o_ref, *, seq):
    qi = pl.program_id(1)
    body = functools.partial(_mla_body, qi=qi, q_ref=q_ref, k_ref=k_ref, va_ref=va_ref, vb_ref=vb_ref,
                             causal_ref=causal_ref, o_ref=o_ref)
    _for_causal_width(qi, seq // MLA_ROWS, MLA_ROWS, body, classes=ATT_WIDTH_CLASSES)


def _mla(q, k, va, vb):
    b, seq, _ = q.shape
    tri = np.where(np.arange(LANE)[:, None] >= np.arange(LANE)[None, :], 0.0, NEG_INF)
    causal = jnp.asarray(np.stack([np.full((LANE, LANE), NEG_INF), tri, np.zeros((LANE, LANE))])[None], F32)
    return pl.pallas_call(
        functools.partial(_mla_kernel, seq=seq),
        grid=(b, seq // MLA_ROWS),
        in_specs=[pl.BlockSpec((1, MLA_ROWS, N_HEADS * MLA_SLOT), lambda bi, qi: (bi, qi, 0)),
                  pl.BlockSpec((1, seq, N_HEADS * MLA_SLOT), lambda bi, qi: (bi, 0, 0)),
                  pl.BlockSpec((1, seq, 256), lambda bi, qi: (bi, 0, 0)),
                  pl.BlockSpec((1, seq, 256), lambda bi, qi: (bi, 0, 0)),
                  _full(causal.shape)],
        out_specs=pl.BlockSpec((1, MLA_ROWS, 256), lambda bi, qi: (bi, qi, 0)),
        out_shape=jax.ShapeDtypeStruct((b, seq, 256), BF16),
        compiler_params=_params("parallel", "parallel"),
        name="mla",
    )(q, k, va, vb, causal)


def _dil_group(dil, seq, toep_ref, g, qf_ref, kf_ref, vf_ref, m_ref, l_ref, acc_ref):
    n_band = seq // (dil * LANE)

    def unit(u, carry):
        r = lax.div(u, n_band)
        n = lax.rem(u, n_band)
        if dil == 1:
            cur = pl.ds(pl.multiple_of(n * LANE, LANE), LANE)
            prev = pl.ds(pl.multiple_of(jnp.maximum(n - 1, 0) * LANE, LANE), LANE)
        else:
            cur = pl.ds(n * (LANE * dil) + r, LANE, stride=dil)
            prev = pl.ds(jnp.maximum(n - 1, 0) * (LANE * dil) + r, LANE, stride=dil)
        prev_tile = jnp.where(n > 0, 2, 0)
        first = lax.broadcasted_iota(jnp.int32, (LANE, LANE), 1) < HEAD_DIM
        kk = [jnp.concatenate([kf_ref[pair, prev, :], kf_ref[pair, cur, :]], axis=0).astype(BF16)
              for pair in range(N_HEADS // 2)]
        vv = [jnp.concatenate([vf_ref[pair, prev, :], vf_ref[pair, cur, :]], axis=0).astype(BF16)
              for pair in range(N_HEADS // 2)]
        logits = []
        for h in range(N_HEADS):
            mine = first if h % 2 == 0 else ~first
            bias = jnp.concatenate([toep_ref[g, h, prev_tile], toep_ref[g, h, 1]], axis=1)
            logits.append(_dot_t(jnp.where(mine, qf_ref[h // 2, cur, :], 0.0).astype(BF16), kk[h // 2]) + bias)
        ms = [jnp.max(x, axis=1, keepdims=True) for x in logits]
        ps = [jnp.exp(x - m) for x, m in zip(logits, ms)]
        ls = [jnp.sum(p, axis=1, keepdims=True) for p in ps]
        pvs = [_dot(p.astype(BF16), vv[h // 2]) for h, p in enumerate(ps)]
        for pair in range(N_HEADS // 2):
            m_new, l_new, pv = (jnp.where(first, x[2 * pair], x[2 * pair + 1]) for x in (ms, ls, pvs))
            m_old = m_ref[pair, cur, :]
            m_tot = jnp.maximum(m_old, m_new)
            a_old = jnp.exp(m_old - m_tot)
            a_new = jnp.exp(m_new - m_tot)
            m_ref[pair, cur, :] = m_tot
            l_ref[pair, cur, :] = a_old * l_ref[pair, cur, :] + a_new * l_new
            acc_ref[pair, cur, :] = a_old * acc_ref[pair, cur, :] + a_new * pv
        return carry

    lax.fori_loop(0, dil * n_band, unit, 0, unroll=2)


def _get_rows(ref, rows):
    return jnp.concatenate([ref[0, rows, :], ref[1, rows, :]], axis=1)


def _set_rows(ref, rows, val):
    ref[0, rows, :] = val[:, :LANE]
    ref[1, rows, :] = val[:, LANE:]


def _dil_kernel(q_ref, k_ref, v_ref, toep_ref, o_ref, qf_ref, kf_ref, vf_ref, m_ref, l_ref, acc_ref, *, seq):
    g = pl.program_id(1)
    everything = pl.ds(0, seq)
    _set_rows(qf_ref, everything, q_ref[0].astype(F32))
    _set_rows(kf_ref, everything, k_ref[0].astype(F32))
    _set_rows(vf_ref, everything, v_ref[0].astype(F32))

    @pl.when(g == 0)
    def _():
        m_ref[...] = jnp.full(m_ref.shape, NEG_INF, F32)
        l_ref[...] = jnp.zeros_like(l_ref)
        acc_ref[...] = jnp.zeros_like(acc_ref)

    for gi, (_, dil) in enumerate(DIL_PATTERNS):
        pl.when(g == gi)(functools.partial(_dil_group, dil, seq, toep_ref, gi, qf_ref, kf_ref, vf_ref,
                                           m_ref, l_ref, acc_ref))

    @pl.when(g == DIL_GROUPS - 1)
    def _():
        o_ref[0] = (_get_rows(acc_ref, everything) / _get_rows(l_ref, everything)).astype(o_ref.dtype)


def _dil(q, k, v, toep):
    b, seq, _ = q.shape
    group = pl.BlockSpec((1, seq, 256), lambda bi, g: (bi, 0, g))
    state = pltpu.VMEM((2, seq, LANE), F32)
    return pl.pallas_call(
        functools.partial(_dil_kernel, seq=seq),
        grid=(b, DIL_GROUPS),
        in_specs=[group, group, group, _full(toep.shape)],
        out_specs=pl.BlockSpec((1, seq, 256), lambda bi, g: (bi, 0, 0)),
        out_shape=jax.ShapeDtypeStruct((b, seq, 256), BF16),
        scratch_shapes=[state] * 6,
        compiler_params=_params("parallel", "arbitrary"),
        name="dil",
    )(q, k, v, toep)


def _merge_kernel(x_ref, xn_ref, oa_ref, ob_ref, oc_ref, od_ref, wg_ref, bg_ref, wb_ref, wo_ref, xo_ref):
    branches = (oa_ref, ob_ref, oc_ref, od_ref)
    xn = xn_ref[...]
    total = jnp.zeros(x_ref.shape, F32)
    for n in range(N_BRANCH):
        cols = slice(n * D_MODEL, (n + 1) * D_MODEL)
        gate = _sigmoid(_dot(xn, wg_ref[:, cols]) + bg_ref[:, cols])
        total = total + gate * _dot(branches[n][...], wb_ref[n])
    xo_ref[...] = x_ref[...] + _dot(total.astype(BF16), wo_ref[...])


def _merge(x, xn, outs, wg, bg, wb, wo):
    n = x.shape[0]
    rows = lambda width: pl.BlockSpec((ROW_TILE, width), lambda i: (i, 0))
    return pl.pallas_call(
        _merge_kernel,
        grid=(n // ROW_TILE,),
        in_specs=[rows(D_MODEL), rows(D_MODEL)] + [rows(256)] * N_BRANCH
                 + [_full(wg.shape), _full(bg.shape), _full(wb.shape), _full(wo.shape)],
        out_specs=rows(D_MODEL),
        out_shape=jax.ShapeDtypeStruct((n, D_MODEL), F32),
        compiler_params=_params("parallel"),
        name="merge",
    )(x, xn, *outs, wg, bg, wb, wo)


def _bucket_of_distance():
    d = np.arange(REL_MAX_DIST + 1)
    exact = REL_BUCKETS // 2
    nf = np.maximum(d, 1).astype(np.float32)
    log_b = exact + (np.log(nf / exact) / math.log(REL_MAX_DIST / exact) * (REL_BUCKETS - exact)).astype(np.int32)
    return np.where(d < exact, d, np.minimum(log_b, REL_BUCKETS - 1))


def _toeplitz_tiles(table, n_off, dil, max_steps=None):
    span = 2 * LANE
    m = np.arange(span)
    off = np.arange(n_off)[:, None]
    steps = off * LANE + LANE - 1 - m[None, :]
    live = (steps >= 0) if max_steps is None else (steps >= 0) & (steps <= max_steps)
    v = table[:, _bucket_of_distance()[np.clip(steps * dil, 0, REL_MAX_DIST)]]
    v = jnp.where(live[None], v, NEG_INF)
    flat = jnp.tile(v, (1, 1, LANE))[..., :LANE * (span - 1)]
    tiles = flat.reshape(table.shape[0], n_off, LANE, span - 1)[..., LANE - 1:]
    return jnp.concatenate([jnp.full((table.shape[0], 1, LANE, LANE), NEG_INF, F32), tiles], axis=1)


def _same_head(width, head):
    idx = np.arange(width) // head
    return jnp.asarray(idx[:, None] == idx[None, :], BF16)


def _rope_tables(seq):
    half = MLA_ROPE // 2
    freqs = ROPE_THETA ** (-np.arange(half, dtype=np.float64) / half)
    ang = np.arange(seq, dtype=np.float64)[:, None] * freqs[None, :]
    pad = np.zeros((seq, MLA_SLOT - MLA_QK))
    cos_h = np.concatenate([np.ones((seq, MLA_NOPE)), np.cos(ang), np.cos(ang), pad], axis=1)
    sin_h = np.concatenate([np.zeros((seq, MLA_NOPE)), np.sin(ang), np.sin(ang), pad], axis=1)
    return (jnp.asarray(np.tile(cos_h, (1, N_HEADS)), F32), jnp.asarray(np.tile(sin_h, (1, N_HEADS)), F32))


def _rot_half_cols():
    half = MLA_ROPE // 2
    src = np.arange(MLA_SLOT)
    sign = np.zeros(MLA_SLOT, np.float32)
    src[MLA_NOPE:MLA_NOPE + half] = np.arange(MLA_NOPE + half, MLA_QK)
    sign[MLA_NOPE:MLA_NOPE + half] = -1.0
    src[MLA_NOPE + half:MLA_QK] = np.arange(MLA_NOPE, MLA_NOPE + half)
    sign[MLA_NOPE + half:MLA_QK] = 1.0
    return src, sign


def _mixer(x, xn, seq, w_in, b_gate, qk_a, qk_b, qk_c, qk_d, mla_nq, w_uq, mla_nkv, w_ukv,
           w_branch, w_out, toeps, consts):
    n = x.shape[0]
    b = n // seq
    e64, e96, tri, cos, sin = consts
    toep_a, toep_b, toep_d = toeps
    row = lambda a: a.reshape(1, -1)
    tile4 = lambda g: jnp.tile(g, N_HEADS).reshape(1, -1)
    r3 = lambda a: a.reshape(b, seq, a.shape[-1])

    off_iq = OFF_A + 6 * HEAD_DIM
    off_ik = off_iq + N_HEADS * IDX_DIM
    iq_rep = jnp.concatenate([jnp.tile(w_in[:, off_iq + h * IDX_DIM:off_iq + (h + 1) * IDX_DIM], (1, 4))
                              for h in range(N_HEADS)], axis=1)
    ik_rep = jnp.tile(w_in[:, off_ik:off_ik + IDX_DIM], (1, 4))
    iw_pad = jnp.pad(w_in[:, off_ik + IDX_DIM:OFF_A + W_A], ((0, 0), (0, LANE - N_HEADS)))
    half_pad = ((0, 0), (0, LANE - HEAD_DIM))
    q_slots = jnp.concatenate([jnp.pad(w_in[:, OFF_A + h * HEAD_DIM:OFF_A + (h + 1) * HEAD_DIM], half_pad)
                               for h in range(N_HEADS)], axis=1)
    off_k = OFF_A + N_HEADS * HEAD_DIM
    k_slot = jnp.pad(w_in[:, off_k:off_k + HEAD_DIM], half_pad)
    v_twice = jnp.tile(w_in[:, off_k + HEAD_DIM:off_iq], (1, 2))
    wa = jnp.concatenate([q_slots, k_slot, v_twice, iq_rep, ik_rep, iw_pad], axis=1).astype(BF16)
    gq_slots = jnp.tile(jnp.pad(qk_a[0], (0, LANE - HEAD_DIM)), N_HEADS).reshape(1, -1)
    gk_slot = jnp.pad(qk_a[1], (0, LANE - HEAD_DIM)).reshape(1, -1)
    qa, ka, va, vb, iq, ik, iw = _proj_a(xn, wa, e96, gq_slots, gk_slot)
    out_a = _dsa(r3(qa), r3(ka), r3(va), r3(vb), r3(iq), r3(ik), r3(iw), toep_a, tri).reshape(n, 256)

    wb_in = w_in[:, OFF_B:OFF_B + W_B].astype(BF16)
    qb, kb, vb_first, vb_second, kmean = _proj_b(xn, wb_in, e64, tile4(qk_b[0]), tile4(qk_b[1]))
    out_b = _moba(r3(qb), r3(kb), r3(vb_first), r3(vb_second),
                  kmean.reshape(b, seq // MOBA_BLOCK, BRANCH_WIDTH), toep_b).reshape(n, 256)

    src, sign = _rot_half_cols()
    src4 = np.concatenate([h * MLA_SLOT + src for h in range(N_HEADS)])
    sign4 = jnp.asarray(np.tile(sign, N_HEADS))
    slot_pad = MLA_SLOT - MLA_QK
    w_kr = w_in[:, OFF_C + MLA_Q_LORA + MLA_KV_LORA:OFF_C + W_C]
    kr_slots = jnp.pad(w_kr, ((0, 0), (MLA_NOPE, slot_pad)))
    kr_slots = jnp.tile(kr_slots, (1, N_HEADS))
    wc = jnp.concatenate([w_in[:, OFF_C:OFF_C + MLA_Q_LORA + MLA_KV_LORA], kr_slots,
                          kr_slots[:, src4] * sign4], axis=1).astype(BF16)
    slots = lambda w: jnp.pad(w.reshape(w.shape[0], N_HEADS, -1),
                              ((0, 0), (0, 0), (0, MLA_SLOT - w.shape[1] // N_HEADS))).reshape(w.shape[0], -1)
    wuq = slots(w_uq)
    ukv = w_ukv.reshape(MLA_KV_LORA, N_HEADS, MLA_NOPE + MLA_V)
    wuk = slots(ukv[:, :, :MLA_NOPE].reshape(MLA_KV_LORA, -1))
    wuv = ukv[:, :, MLA_NOPE:].reshape(MLA_KV_LORA, -1)
    gq = jnp.tile(jnp.pad(qk_c[0], (0, slot_pad)), N_HEADS)
    gk = jnp.tile(jnp.pad(qk_c[1], (0, slot_pad)), N_HEADS)
    qc, kc, vc_first, vc_second = _proj_c(
        xn, wc, e96, row(mla_nq), row(mla_nkv), wuq.astype(BF16), (wuq[:, src4] * sign4).astype(BF16),
        wuk.astype(BF16), wuv.astype(BF16), row(gq), row(gq[src4]), row(gk), row(gk[src4]), cos, sin, seq)
    out_c = _mla(r3(qc), r3(kc), r3(vc_first), r3(vc_second)).reshape(n, 256)

    wd = w_in[:, OFF_D:OFF_D + W_D].astype(BF16)
    qd, kd, vd = _proj_d(xn, wd, e64, tile4(qk_d[0]), tile4(qk_d[1]))
    out_d = _dil(r3(qd), r3(kd), r3(vd), toep_d).reshape(n, 256)

    wg = w_in[:, OFF_G:OFF_G + W_G].astype(BF16)
    return _merge(x, xn, (out_a, out_b, out_c, out_d), wg, b_gate.reshape(1, -1),
                  w_branch.astype(BF16), w_out.astype(BF16))


def kernel(x, norm_gain, w_in, b_gate, qk_gain_a, qk_gain_b, qk_gain_c, qk_gain_d, mla_norm_q,
           w_mla_uq, mla_norm_kv, w_mla_ukv, w_branch, w_out, rel_bias, w_ffn_in, w_ffn_out):
    b, seq, d = x.shape
    depth = norm_gain.shape[0]
    toep_a = _toeplitz_tiles(rel_bias[0:4], seq // LANE, 1)
    toep_b = _toeplitz_tiles(rel_bias[4:8], seq // LANE, 1)
    toep_d = jnp.stack([_toeplitz_tiles(rel_bias[8 + 4 * g:12 + 4 * g], 2, dil, max_steps=window // dil)
                        for g, (window, dil) in enumerate(DIL_PATTERNS)])
    tri = jnp.asarray(np.arange(LANE)[:, None] < np.arange(LANE)[None, :], BF16)
    consts = (_same_head(256, HEAD_DIM), _same_head(256, MLA_SLOT), tri) + _rope_tables(seq)

    x = x.reshape(b * seq, d)
    for l in range(depth):
        g = norm_gain[l]
        x, xn = _ffn(x, g[0:1], g[1:2], w_ffn_in[l, 0].astype(BF16), w_ffn_out[l, 0].astype(BF16))
        x = _mixer(x, xn, seq, w_in[l], b_gate[l], qk_gain_a[l], qk_gain_b[l], qk_gain_c[l],
                   qk_gain_d[l], mla_norm_q[l], w_mla_uq[l], mla_norm_kv[l], w_mla_ukv[l],
                   w_branch[l], w_out[l], (toep_a, toep_b, toep_d), consts)
        x, _ = _ffn(x, g[2:3], g[2:3], w_ffn_in[l, 1].astype(BF16), w_ffn_out[l, 1].astype(BF16))
    return x.reshape(b, seq, d)
```

```python
import functools
import math

import numpy as np
import jax
import jax.numpy as jnp
from jax import lax
from jax.experimental import pallas as pl
from jax.experimental.pallas import tpu as pltpu

D_MODEL = 1024
HEAD_DIM = 64
N_HEADS = 4
BRANCH_WIDTH = N_HEADS * HEAD_DIM
IDX_DIM = 32
DSA_TOPK = 256
MOBA_BLOCK = 256
MOBA_TOPK = 3
MLA_Q_LORA = 384
MLA_KV_LORA = 256
MLA_NOPE = 64
MLA_ROPE = 32
MLA_QK = MLA_NOPE + MLA_ROPE
MLA_V = 64
MLA_SLOT = 128
ROPE_THETA = 10000.0
DIL_PATTERNS = ((128, 1), (512, 4), (2048, 16))
DIL_GROUPS = 3
N_BRANCH = 4
D_FF = 2816
REL_BUCKETS = 32
REL_MAX_DIST = 2048
RMS_EPS = 1e-6

OFF_A = 0
W_A = 4 * HEAD_DIM + 2 * HEAD_DIM + 4 * IDX_DIM + IDX_DIM + 4
OFF_B = OFF_A + W_A
W_B = 3 * BRANCH_WIDTH
OFF_C = OFF_B + W_B
W_C = MLA_Q_LORA + MLA_KV_LORA + MLA_ROPE
OFF_D = OFF_C + W_C
W_D = 3 * DIL_GROUPS * BRANCH_WIDTH
OFF_G = OFF_D + W_D
W_G = N_BRANCH * D_MODEL

LANE = 128
ROW_TILE = 512
FFN_ROWS = 512
FF_CHUNK = 1408
WIDTH_CLASSES = 8
ATT_ROWS = 256
MLA_ROWS = 256
ATT_WIDTH_CLASSES = 4
DSA_WIDTH_CLASSES = 4
DSA_ROWS = 256
MASKED = -1e30
VMEM_LIMIT = 56 * 1024 * 1024

F32 = jnp.float32
BF16 = jnp.bfloat16
NEG_INF = float("-inf")
INT_MIN = -2 ** 31

_CONTRACT_LAST = (((1,), (1,)), ((), ()))


def _dot(a, b, precision=None):
    return jnp.dot(a, b, preferred_element_type=F32, precision=precision)


def _dot_t(a, b, precision=None):
    return lax.dot_general(a, b, _CONTRACT_LAST, preferred_element_type=F32, precision=precision)


def _rms(x, gain):
    return x * lax.rsqrt(jnp.mean(x * x, axis=-1, keepdims=True) + RMS_EPS) * gain


def _sigmoid(x):
    return 1.0 / (1.0 + jnp.exp(-x))


def _head_sumsq(y, e_ref):
    y2 = y * y
    hi = y2.astype(BF16)
    lo = (y2 - hi.astype(F32)).astype(BF16)
    e = e_ref[...]
    step = e.shape[0]
    blocks = [_dot(hi[:, c:c + step], e) + _dot(lo[:, c:c + step], e) for c in range(0, y.shape[1], step)]
    return blocks[0] if len(blocks) == 1 else jnp.concatenate(blocks, axis=1)


def _head_norm(y, e_ref, gain, width):
    return y * lax.rsqrt(_head_sumsq(y, e_ref) * (1.0 / width) + RMS_EPS) * gain


def _params(*sem):
    return pltpu.CompilerParams(dimension_semantics=sem, vmem_limit_bytes=VMEM_LIMIT)


def _full(shape):
    return pl.BlockSpec(shape, lambda *_: (0,) * len(shape))


def _resident(shape):
    return pl.BlockSpec(shape, lambda *_: (0,) * len(shape), pipeline_mode=pl.Buffered(1))


def _ffn_kernel(x_ref, g_ref, gn_ref, wi_ref, wo_ref, xo_ref, xn_ref):
    half = FFN_ROWS // 2
    for s in range(2):
        rows = slice(s * half, (s + 1) * half)
        x = x_ref[rows, :]
        xb = _rms(x, g_ref[...]).astype(BF16)
        acc = None
        for j in range(D_FF // FF_CHUNK):
            gate = _dot(xb, wi_ref[:, j * FF_CHUNK:(j + 1) * FF_CHUNK])
            up = _dot(xb, wi_ref[:, D_FF + j * FF_CHUNK:D_FF + (j + 1) * FF_CHUNK])
            h = (gate * _sigmoid(gate) * up).astype(BF16)
            part = _dot(h, wo_ref[j * FF_CHUNK:(j + 1) * FF_CHUNK, :])
            acc = part if acc is None else acc + part
        xo = x + 0.5 * acc
        xo_ref[rows, :] = xo
        xn_ref[rows, :] = _rms(xo, gn_ref[...]).astype(BF16)


def _ffn(x, gain, gain_next, w_in, w_out):
    n = x.shape[0]
    rows = pl.BlockSpec((FFN_ROWS, D_MODEL), lambda i: (i, 0))
    return pl.pallas_call(
        _ffn_kernel,
        grid=(n // FFN_ROWS,),
        in_specs=[rows, _full((1, D_MODEL)), _full((1, D_MODEL)), _resident(w_in.shape), _resident(w_out.shape)],
        out_specs=[rows, rows],
        out_shape=[jax.ShapeDtypeStruct((n, D_MODEL), F32), jax.ShapeDtypeStruct((n, D_MODEL), BF16)],
        compiler_params=_params("parallel"),
        name="ffn",
    )(x, gain, gain_next, w_in, w_out)


def _split_terms(rep, hi_lanes):
    hi = rep.astype(BF16).astype(F32)
    return jnp.where(hi_lanes, hi, rep - hi).astype(BF16)


def _with_ones(v):
    first = (lax.broadcasted_iota(jnp.int32, (1, v.shape[1]), 1) & (LANE - 1)) < HEAD_DIM
    return jnp.where(first, v, 1.0).astype(BF16), jnp.where(first, 1.0, v).astype(BF16)


def _proj_a_kernel(xn_ref, w_ref, e_ref, gq_ref, gk_ref, q_ref, k_ref, va_ref, vb_ref, iq_ref, ik_ref, iw_ref):
    p = _dot(xn_ref[...], w_ref[...])
    q = _head_norm(p[:, :512], e_ref, gq_ref[...], HEAD_DIM) * HEAD_DIM ** -0.5
    q_ref[...] = q.astype(BF16)
    k = p[:, 512:640]
    k_ms = jnp.sum(k * k, axis=-1, keepdims=True) * (1.0 / HEAD_DIM)
    k_ref[...] = (k * lax.rsqrt(k_ms + RMS_EPS) * gk_ref[...]).astype(BF16)
    va_ref[...], vb_ref[...] = _with_ones(p[:, 640:768])
    lane = lax.broadcasted_iota(jnp.int32, (1, 4 * LANE), 1)
    copy = lax.shift_right_logical(lane, int(math.log2(IDX_DIM))) & 3
    iq_ref[...] = _split_terms(p[:, 768:1280], copy < 2)
    ik_ref[...] = _split_terms(p[:, 1280:1408], (copy[:, :LANE] & 1) == 0)
    iw_ref[...] = p[:, 1408:1536]


def _row_spec(width):
    return pl.BlockSpec((ROW_TILE, width), lambda i: (i, 0))


def _const_spec(a):
    return _resident(a.shape) if a.size >= LANE * LANE * 8 else _full(a.shape)


def _proj_a(n, w, e_slots, gq, gk):
    consts = (w, e_slots, gq, gk)
    widths = ((512, BF16), (128, BF16), (128, BF16), (128, BF16), (512, BF16), (128, BF16), (128, F32))
    return (_proj_a_kernel, consts, [_const_spec(c) for c in consts], [_row_spec(wd) for wd, _ in widths],
            [jax.ShapeDtypeStruct((n, wd), dt) for wd, dt in widths])


def _proj_b_kernel(xn_ref, w_ref, e_ref, gq_ref, gk_ref, q_ref, k_ref, va_ref, vb_ref, km_ref):
    p = _dot(xn_ref[...], w_ref[...])
    q_ref[...] = _head_norm(p[:, :256], e_ref, gq_ref[...], HEAD_DIM)
    k = _head_norm(p[:, 256:512], e_ref, gk_ref[...], HEAD_DIM)
    k_ref[...] = k.astype(BF16)
    va_ref[...], vb_ref[...] = _with_ones(p[:, 512:768])
    km = jnp.mean(k.reshape(ROW_TILE // MOBA_BLOCK, MOBA_BLOCK, BRANCH_WIDTH), axis=1)
    for j in range(ROW_TILE // MOBA_BLOCK):
        km_ref[j] = km[j:j + 1]


def _proj_b(n, w, e64, gq, gk):
    consts = (w, e64, gq, gk)
    widths = ((256, F32), (256, BF16), (256, BF16), (256, BF16))
    per_tile = ROW_TILE // MOBA_BLOCK
    return (_proj_b_kernel, consts, [_const_spec(c) for c in consts],
            [_row_spec(wd) for wd, _ in widths] + [pl.BlockSpec((per_tile, 1, BRANCH_WIDTH), lambda i: (i, 0, 0))],
            [jax.ShapeDtypeStruct((n, wd), dt) for wd, dt in widths]
            + [jax.ShapeDtypeStruct((n // MOBA_BLOCK, 1, BRANCH_WIDTH), F32)])


def _proj_c_kernel(xn_ref, w_ref, e_ref, nq_ref, nkv_ref, wuq_ref, wuqs_ref, wuk_ref, wuv_ref,
                   gq_ref, gqs_ref, gk_ref, gks_ref, cos_ref, sin_ref, q_ref, k_ref, va_ref, vb_ref):
    p = _dot(xn_ref[...], w_ref[...])
    cos = cos_ref[...]
    sin = sin_ref[...]
    xq = _rms(p[:, :384], nq_ref[...]).astype(BF16)
    qa = _dot(xq, wuq_ref[...])
    qs = _dot(xq, wuqs_ref[...])
    rq = lax.rsqrt(_head_sumsq(qa, e_ref) * (1.0 / MLA_QK) + RMS_EPS)
    q_ref[...] = (rq * (qa * gq_ref[...] * cos + qs * gqs_ref[...] * sin)).astype(BF16)
    xkv = _rms(p[:, 384:640], nkv_ref[...]).astype(BF16)
    ka = _dot(xkv, wuk_ref[...]) + p[:, 640:1152]
    ks = p[:, 1152:1664]
    rk = lax.rsqrt(_head_sumsq(ka, e_ref) * (1.0 / MLA_QK) + RMS_EPS)
    k_ref[...] = (rk * (ka * gk_ref[...] * cos + ks * gks_ref[...] * sin)).astype(BF16)
    va_ref[...], vb_ref[...] = _with_ones(_dot(xkv, wuv_ref[...]))


def _proj_c(n, w, e96, nq, nkv, wuq, wuqs, wuk, wuv, gq, gqs, gk, gks, cos, sin, seq):
    pos = pl.BlockSpec((ROW_TILE, N_HEADS * MLA_SLOT), lambda i: (i % (seq // ROW_TILE), 0))
    consts = (w, e96, nq, nkv, wuq, wuqs, wuk, wuv, gq, gqs, gk, gks)
    widths = ((N_HEADS * MLA_SLOT, BF16), (N_HEADS * MLA_SLOT, BF16), (256, BF16), (256, BF16))
    return (_proj_c_kernel, consts + (cos, sin), [_const_spec(c) for c in consts] + [pos, pos],
            [_row_spec(wd) for wd, _ in widths], [jax.ShapeDtypeStruct((n, wd), dt) for wd, dt in widths])


def _proj_d_kernel(xn_ref, w_ref, e_ref, gq_ref, gk_ref, q_ref, k_ref, v_ref):
    xn = xn_ref[...]
    for c in range(DIL_GROUPS):
        lo, hi = c * 256, (c + 1) * 256
        pq = _dot(xn, w_ref[:, lo:hi])
        q_ref[:, lo:hi] = (_head_norm(pq, e_ref, gq_ref[...], HEAD_DIM) * HEAD_DIM ** -0.5).astype(BF16)
        pk = _dot(xn, w_ref[:, 768 + lo:768 + hi])
        k_ref[:, lo:hi] = _head_norm(pk, e_ref, gk_ref[...], HEAD_DIM).astype(BF16)
        v_ref[:, lo:hi] = _dot(xn, w_ref[:, 1536 + lo:1536 + hi]).astype(BF16)


def _proj_d(n, w, e64, gq, gk):
    consts = (w, e64, gq, gk)
    return (_proj_d_kernel, consts, [_const_spec(c) for c in consts], [_row_spec(768)] * 3,
            [jax.ShapeDtypeStruct((n, 768), BF16)] * 3)


def _proj_fused_kernel(*refs, pieces):
    xn_ref, refs = refs[0], refs[1:]
    n_in = sum(k for _, k, _ in pieces)
    ins, outs = refs[:n_in], refs[n_in:]
    for body, k_in, k_out in pieces:
        body(xn_ref, *ins[:k_in], *outs[:k_out])
        ins, outs = ins[k_in:], outs[k_out:]


def _project(xn, pieces):
    n = xn.shape[0]
    layout = tuple((body, len(consts), len(out_specs)) for body, consts, _, out_specs, _ in pieces)
    flat = pl.pallas_call(
        functools.partial(_proj_fused_kernel, pieces=layout),
        grid=(n // ROW_TILE,),
        in_specs=[_row_spec(D_MODEL)] + [s for p in pieces for s in p[2]],
        out_specs=[s for p in pieces for s in p[3]],
        out_shape=[s for p in pieces for s in p[4]],
        compiler_params=_params("parallel"),
        name="proj",
    )(xn, *[c for p in pieces for c in p[1]])
    outs = []
    for _, _, k_out in layout:
        outs.append(flat[:k_out])
        flat = flat[k_out:]
    return outs


def _for_causal_width(qi, n_qtiles, tile_rows, body, classes=WIDTH_CLASSES):
    n_cls = min(classes, n_qtiles)
    per = n_qtiles // n_cls
    for c in range(n_cls):
        pl.when((qi >= c * per) & (qi < (c + 1) * per))(functools.partial(body, (c + 1) * per * tile_rows))


def _bias_rows(toep_ref, h, qblk, nchunk):
    return jnp.concatenate([toep_ref[h, jnp.maximum(qblk - c + 1, 0)] for c in range(nchunk)], axis=1)


def _bias_tile(toep_ref, h, qi, sub_blocks, nchunk):
    return jnp.concatenate([_bias_rows(toep_ref, h, qi * sub_blocks + j, nchunk) for j in range(sub_blocks)],
                           axis=0)


def _softmax_pv(logits, v):
    m = jnp.max(logits, axis=1, keepdims=True)
    p = jnp.exp(logits - m)
    l = jnp.sum(p, axis=1, keepdims=True)
    return _dot(p.astype(BF16), v) / l


def _exp_pv(logits, v_ones):
    return _dot(_exp_weights(logits), v_ones)


def _exp_weights(logits):
    m = jnp.max(logits, axis=1, keepdims=True)
    return jnp.exp(logits - m).astype(BF16)


def _pair_output(first_head, second_head):
    first = lax.broadcasted_iota(jnp.int32, first_head.shape, 1) < HEAD_DIM
    numer = jnp.where(first, first_head, second_head)
    denom = jnp.where(first, pltpu.roll(first_head, HEAD_DIM, axis=1), pltpu.roll(second_head, HEAD_DIM, axis=1))
    return numer / denom


def _dsa_body(width, qi, q_ref, k_ref, va_ref, vb_ref, iq_ref, ik_ref, iw_ref, toep_ref, tri_ref, o_ref,
              sel_ref, n_sel):
    rows = DSA_ROWS
    nchunk = width // LANE
    ik = ik_ref[0, :width, :]
    iw = iw_ref[0][:, :N_HEADS] * (N_HEADS ** -0.5 * IDX_DIM ** -0.5)

    r = _dot_t(jnp.concatenate([iq_ref[0, :, h * LANE:(h + 1) * LANE] for h in range(N_HEADS)], axis=0), ik)
    score = None
    for h in range(N_HEADS):
        term = jnp.maximum(r[h * rows:(h + 1) * rows], 0.0) * iw[:, h:h + 1]
        score = term if score is None else score + term

    t = qi * rows + lax.broadcasted_iota(jnp.int32, (rows, width), 0)
    s = lax.broadcasted_iota(jnp.int32, (rows, width), 1)

    bits = pltpu.bitcast(score, jnp.int32)
    key = jnp.where(bits < 0, jnp.int32(INT_MIN) - bits, bits)
    key = jnp.where(s <= t, key, jnp.int32(INT_MIN))

    half_min = -2 ** 15
    upper = lax.shift_right_arithmetic(key, 16).astype(jnp.int16)
    lower = ((key & jnp.int32(0xFFFF)) + half_min).astype(jnp.int16)

    def lane_counts(hit):
        ones = jnp.where(hit, jnp.int16(1), jnp.int16(0))
        acc = ones[:, :LANE]
        for c in range(1, nchunk):
            acc = acc + ones[:, c * LANE:(c + 1) * LANE]
        return acc

    def total(acc):
        return jnp.sum(acc.astype(F32), axis=1, keepdims=True)

    def count(hit):
        return total(lane_counts(hit))

    def search(half, need):
        top, bottom = slice(0, rows // 2), slice(rows // 2, rows)
        need_rows = jnp.broadcast_to(need, (rows, 1)).astype(F32)

        def bit(step):
            return jnp.where(step < 16, lax.shift_left(jnp.int32(1), jnp.maximum(15 - step, 0)), 0)

        def hits(part, cand):
            return lane_counts(half[part] >= (cand + half_min).astype(jnp.int16))

        def body(i, carry):
            ans_top, acc_top, ans_bottom = carry
            cand_top = ans_top | bit(i)
            ans_top = jnp.where(total(acc_top) >= need_rows[top], cand_top, ans_top)
            cand_bottom = ans_bottom | bit(i)
            ans_bottom = jnp.where(total(hits(bottom, cand_bottom)) >= need_rows[bottom], cand_bottom, ans_bottom)
            return ans_top, hits(top, ans_top | bit(i + 1)), ans_bottom

        zero = jnp.zeros((rows // 2, 1), jnp.int32)
        ans_top, _, ans_bottom = lax.fori_loop(0, 16, body, (zero, hits(top, zero | bit(0)), zero))
        return jnp.concatenate([ans_top, ans_bottom], axis=0)

    thr_upper = search(upper, n_sel) + half_min
    thr_upper16 = thr_upper.astype(jnp.int16)
    n_over = count(upper > thr_upper16)
    candidates = jnp.where(upper == thr_upper16, lower, jnp.int16(half_min))
    thr = thr_upper * 65536 + search(candidates, n_sel - n_over)

    sel = jnp.where(key >= jnp.maximum(thr, jnp.int32(INT_MIN + 1)), 1.0, 0.0)
    sel_ref[:, :width] = sel

    overflow = jnp.sum(sel, axis=1, keepdims=True) > n_sel

    @pl.when(jnp.max(jnp.where(overflow, 1.0, 0.0)) > 0.0)
    def _():
        above = key > thr
        room = n_sel - jnp.sum(jnp.where(above, 1.0, 0.0), axis=1, keepdims=True)
        before = jnp.zeros((rows, 1), F32)
        for c in range(nchunk):
            cols = slice(c * LANE, (c + 1) * LANE)
            tc = jnp.where(key[:, cols] == thr, 1.0, 0.0)
            rank = before + _dot(tc.astype(BF16), tri_ref[...])
            keep = above[:, cols] | ((tc > 0.0) & (rank < room))
            sel_ref[:, cols] = jnp.where(keep, 1.0, 0.0)
            before = before + jnp.sum(tc, axis=1, keepdims=True)

    unselected = (sel_ref[:, :width] - 1.0) * -MASKED
    k = k_ref[0, :width, :]
    qk = _dot_t(jnp.concatenate([q_ref[0, :, h * LANE:(h + 1) * LANE] for h in range(N_HEADS)], axis=0), k)
    logits = [qk[h * rows:(h + 1) * rows] + _bias_tile(toep_ref, h, qi, rows // LANE, nchunk) + unselected
              for h in range(N_HEADS)]
    weights = [_exp_weights(x) for x in logits]
    for pair in range(N_HEADS // 2):
        outs = [_dot(weights[2 * pair], va_ref[0, :width, :]), _dot(weights[2 * pair + 1], vb_ref[0, :width, :])]
        o_ref[0, :, pair * LANE:(pair + 1) * LANE] = _pair_output(*outs).astype(o_ref.dtype)


def _dsa_kernel(*refs, seq, n_sel):
    qi = pl.program_id(1)
    _for_causal_width(qi, seq // DSA_ROWS, DSA_ROWS,
                      functools.partial(_dsa_body, qi=qi, n_sel=n_sel, **_named(refs)), classes=DSA_WIDTH_CLASSES)


def _named(refs):
    names = ("q_ref", "k_ref", "va_ref", "vb_ref", "iq_ref", "ik_ref", "iw_ref", "toep_ref", "tri_ref", "o_ref",
             "sel_ref")
    return dict(zip(names, refs, strict=True))


def _dsa(q, k, va, vb, iq, ik, iw, toep, tri):
    b, seq, _ = q.shape
    n_sel = min(DSA_TOPK, seq // 4)
    qblock = lambda width: pl.BlockSpec((1, DSA_ROWS, width), lambda bi, qi: (bi, qi, 0))
    whole = lambda width: pl.BlockSpec((1, seq, width), lambda bi, qi: (bi, 0, 0))
    return pl.pallas_call(
        functools.partial(_dsa_kernel, seq=seq, n_sel=n_sel),
        grid=(b, seq // DSA_ROWS),
        in_specs=[qblock(512), whole(128), whole(128), whole(128), qblock(512), whole(128), qblock(128),
                  _full(toep.shape), _full(tri.shape)],
        out_specs=qblock(256),
        out_shape=jax.ShapeDtypeStruct((b, seq, 256), BF16),
        scratch_shapes=[pltpu.VMEM((DSA_ROWS, seq), F32)],
        compiler_params=_params("parallel", "parallel"),
        name="dsa",
    )(q, k, va, vb, iq, ik, iw, toep, tri)


def _moba_body(width, qi, q_ref, k_ref, va_ref, vb_ref, km_ref, toep_ref, spread_ref, o_ref, nblk):
    nchunk = width // LANE
    n_sel = min(MOBA_TOPK, nblk - 1)
    rows = ATT_ROWS
    own = lax.shift_right_logical(qi * rows, int(math.log2(MOBA_BLOCK)))
    q = q_ref[0]

    km = km_ref[0]
    head_of_col = lax.shift_right_logical(lax.broadcasted_iota(jnp.int32, (nblk, BRANCH_WIDTH), 1),
                                          int(math.log2(HEAD_DIM)))
    per_head = [jnp.where(head_of_col == h, km, 0.0) for h in range(N_HEADS)]
    pad = jnp.zeros((LANE - N_HEADS * nblk, BRANCH_WIDTH), F32)
    gate = _dot_t(q, jnp.concatenate(per_head + [pad], axis=0), precision=lax.Precision.HIGHEST)

    lane = lax.broadcasted_iota(jnp.int32, (rows, LANE), 1)
    blk = lane & (nblk - 1)
    past = (blk < own) & (lane < N_HEADS * nblk)
    gate = jnp.where(past, gate, NEG_INF)
    rank = jnp.zeros((rows, LANE), F32)
    for d in range(1, nblk):
        before = pltpu.roll(gate, d, axis=1)
        rank = rank + jnp.where(blk >= d, jnp.where(before >= gate, 1.0, 0.0), 0.0)
        after = pltpu.roll(gate, LANE - d, axis=1)
        rank = rank + jnp.where(blk < nblk - d, jnp.where(after > gate, 1.0, 0.0), 0.0)
    penalty = jnp.where(past, jnp.where(rank < n_sel, 0.0, MASKED), jnp.where(blk == own, 0.0, MASKED)).astype(BF16)

    first = lane < HEAD_DIM
    logits = []
    for h in range(N_HEADS):
        ps = slice(h // 2 * LANE, (h // 2 + 1) * LANE)
        mine = first if h % 2 == 0 else ~first
        qh = jnp.where(mine, q[:, ps] * HEAD_DIM ** -0.5, 0.0).astype(BF16)
        unpicked = _dot(penalty, spread_ref[h, :, :width])
        logits.append(_dot_t(qh, k_ref[0, :width, ps]) + _bias_tile(toep_ref, h, qi, rows // LANE, nchunk)
                      + unpicked)
    weights = [_exp_weights(x) for x in logits]
    for pair in range(N_HEADS // 2):
        ps = slice(pair * LANE, (pair + 1) * LANE)
        outs = [_dot(weights[2 * pair], va_ref[0, :width, ps]), _dot(weights[2 * pair + 1], vb_ref[0, :width, ps])]
        o_ref[0, :, ps] = _pair_output(*outs).astype(o_ref.dtype)


def _moba_kernel(q_ref, k_ref, va_ref, vb_ref, km_ref, toep_ref, spread_ref, o_ref, *, seq):
    qi = pl.program_id(1)
    body = functools.partial(_moba_body, qi=qi, q_ref=q_ref, k_ref=k_ref, va_ref=va_ref, vb_ref=vb_ref,
                             km_ref=km_ref, toep_ref=toep_ref, spread_ref=spread_ref, o_ref=o_ref,
                             nblk=seq // MOBA_BLOCK)
    _for_causal_width(qi, seq // ATT_ROWS, ATT_ROWS, body, classes=ATT_WIDTH_CLASSES)


def _moba_spread(seq):
    nblk = seq // MOBA_BLOCK
    row = np.arange(LANE)[None, :, None]
    h = np.arange(N_HEADS)[:, None, None]
    blk_of_key = (np.arange(seq) // MOBA_BLOCK)[None, None, :]
    return jnp.asarray(row == h * nblk + blk_of_key, BF16)


def _moba(q, k, va, vb, k_mean, toep):
    b, seq, _ = q.shape
    nblk = seq // MOBA_BLOCK
    if nblk & (nblk - 1) or N_HEADS * nblk > LANE:
        raise ValueError("MoBA lane layout needs a power-of-two block count with heads*blocks <= 128")
    spread = _moba_spread(seq)
    qblock = pl.BlockSpec((1, ATT_ROWS, 256), lambda bi, qi: (bi, qi, 0))
    whole = pl.BlockSpec((1, seq, 256), lambda bi, qi: (bi, 0, 0))
    means = pl.BlockSpec((1, nblk, 256), lambda bi, qi: (bi, 0, 0))
    return pl.pallas_call(
        functools.partial(_moba_kernel, seq=seq),
        grid=(b, seq // ATT_ROWS),
        in_specs=[qblock, whole, whole, whole, means, _full(toep.shape), _full(spread.shape)],
        out_specs=qblock,
        out_shape=jax.ShapeDtypeStruct((b, seq, 256), BF16),
        compiler_params=_params("parallel", "parallel"),
        name="moba",
    )(q, k, va, vb, k_mean, toep, spread)


def _mla_body(width, qi, q_ref, k_ref, va_ref, vb_ref, causal_ref, o_ref):
    nchunk = width // LANE
    sub = MLA_ROWS // LANE
    causal = jnp.concatenate(
        [jnp.concatenate([causal_ref[0, jnp.clip(qi * sub + j - c + 1, 0, 2)] for c in range(nchunk)], axis=1)
         for j in range(sub)], axis=0)
    logits = []
    for h in range(N_HEADS):
        qs = slice(h * MLA_SLOT, (h + 1) * MLA_SLOT)
        logits.append(_dot_t(q_ref[0, :, qs], k_ref[0, :width, qs]) * MLA_QK ** -0.5 + causal)
    weights = [_exp_weights(x) for x in logits]
    for pair in range(N_HEADS // 2):
        ps = slice(pair * LANE, (pair + 1) * LANE)
        outs = [_dot(weights[2 * pair], va_ref[0, :width, ps]), _dot(weights[2 * pair + 1], vb_ref[0, :width, ps])]
        o_ref[0, :, ps] = _pair_output(*outs).astype(o_ref.dtype)


def _mla_kernel(q_ref, k_ref, va_ref, vb_ref, causal_ref, o_ref, *, seq):
    qi = pl.program_id(1)
    body = functools.partial(_mla_body, qi=qi, q_ref=q_ref, k_ref=k_ref, va_ref=va_ref, vb_ref=vb_ref,
                             causal_ref=causal_ref, o_ref=o_ref)
    _for_causal_width(qi, seq // MLA_ROWS, MLA_ROWS, body, classes=ATT_WIDTH_CLASSES)


def _mla(q, k, va, vb):
    b, seq, _ = q.shape
    tri = np.where(np.arange(LANE)[:, None] >= np.arange(LANE)[None, :], 0.0, NEG_INF)
    causal = jnp.asarray(np.stack([np.full((LANE, LANE), NEG_INF), tri, np.zeros((LANE, LANE))])[None], F32)
    return pl.pallas_call(
        functools.partial(_mla_kernel, seq=seq),
        grid=(b, seq // MLA_ROWS),
        in_specs=[pl.BlockSpec((1, MLA_ROWS, N_HEADS * MLA_SLOT), lambda bi, qi: (bi, qi, 0)),
                  pl.BlockSpec((1, seq, N_HEADS * MLA_SLOT), lambda bi, qi: (bi, 0, 0)),
                  pl.BlockSpec((1, seq, 256), lambda bi, qi: (bi, 0, 0)),
                  pl.BlockSpec((1, seq, 256), lambda bi, qi: (bi, 0, 0)),
                  _full(causal.shape)],
        out_specs=pl.BlockSpec((1, MLA_ROWS, 256), lambda bi, qi: (bi, qi, 0)),
        out_shape=jax.ShapeDtypeStruct((b, seq, 256), BF16),
        compiler_params=_params("parallel", "parallel"),
        name="mla",
    )(q, k, va, vb, causal)


def _dil_group(dil, seq, toep_ref, g, qf_ref, kf_ref, vf_ref, m_ref, l_ref, acc_ref):
    n_band = seq // (dil * LANE)

    def unit(u, carry):
        r = lax.div(u, n_band)
        n = lax.rem(u, n_band)
        if dil == 1:
            cur = pl.ds(pl.multiple_of(n * LANE, LANE), LANE)
            prev = pl.ds(pl.multiple_of(jnp.maximum(n - 1, 0) * LANE, LANE), LANE)
        else:
            cur = pl.ds(n * (LANE * dil) + r, LANE, stride=dil)
            prev = pl.ds(jnp.maximum(n - 1, 0) * (LANE * dil) + r, LANE, stride=dil)
        prev_tile = jnp.where(n > 0, 2, 0)
        first = lax.broadcasted_iota(jnp.int32, (LANE, LANE), 1) < HEAD_DIM
        kk = [jnp.concatenate([kf_ref[pair, prev, :], kf_ref[pair, cur, :]], axis=0).astype(BF16)
              for pair in range(N_HEADS // 2)]
        vv = [jnp.concatenate([vf_ref[pair, prev, :], vf_ref[pair, cur, :]], axis=0).astype(BF16)
              for pair in range(N_HEADS // 2)]
        logits = []
        for h in range(N_HEADS):
            mine = first if h % 2 == 0 else ~first
            bias = jnp.concatenate([toep_ref[g, h, prev_tile], toep_ref[g, h, 1]], axis=1)
            logits.append(_dot_t(jnp.where(mine, qf_ref[h // 2, cur, :], 0.0).astype(BF16), kk[h // 2]) + bias)
        ms = [jnp.max(x, axis=1, keepdims=True) for x in logits]
        ps = [jnp.exp(x - m) for x, m in zip(logits, ms)]
        ls = [jnp.sum(p, axis=1, keepdims=True) for p in ps]
        pvs = [_dot(p.astype(BF16), vv[h // 2]) for h, p in enumerate(ps)]
        for pair in range(N_HEADS // 2):
            m_new, l_new, pv = (jnp.where(first, x[2 * pair], x[2 * pair + 1]) for x in (ms, ls, pvs))
            m_old = m_ref[pair, cur, :]
            m_tot = jnp.maximum(m_old, m_new)
            a_old = jnp.exp(m_old - m_tot)
            a_new = jnp.exp(m_new - m_tot)
            m_ref[pair, cur, :] = m_tot
            l_ref[pair, cur, :] = a_old * l_ref[pair, cur, :] + a_new * l_new
            acc_ref[pair, cur, :] = a_old * acc_ref[pair, cur, :] + a_new * pv
        return carry

    lax.fori_loop(0, dil * n_band, unit, 0, unroll=2)


def _get_rows(ref, rows):
    return jnp.concatenate([ref[0, rows, :], ref[1, rows, :]], axis=1)


def _set_rows(ref, rows, val):
    ref[0, rows, :] = val[:, :LANE]
    ref[1, rows, :] = val[:, LANE:]


def _dil_kernel(q_ref, k_ref, v_ref, toep_ref, o_ref, qf_ref, kf_ref, vf_ref, m_ref, l_ref, acc_ref, *, seq):
    g = pl.program_id(1)
    everything = pl.ds(0, seq)
    _set_rows(qf_ref, everything, q_ref[0].astype(F32))
    _set_rows(kf_ref, everything, k_ref[0].astype(F32))
    _set_rows(vf_ref, everything, v_ref[0].astype(F32))

    @pl.when(g == 0)
    def _():
        m_ref[...] = jnp.full(m_ref.shape, NEG_INF, F32)
        l_ref[...] = jnp.zeros_like(l_ref)
        acc_ref[...] = jnp.zeros_like(acc_ref)

    for gi, (_, dil) in enumerate(DIL_PATTERNS):
        pl.when(g == gi)(functools.partial(_dil_group, dil, seq, toep_ref, gi, qf_ref, kf_ref, vf_ref,
                                           m_ref, l_ref, acc_ref))

    @pl.when(g == DIL_GROUPS - 1)
    def _():
        o_ref[0] = (_get_rows(acc_ref, everything) / _get_rows(l_ref, everything)).astype(o_ref.dtype)


def _dil(q, k, v, toep):
    b, seq, _ = q.shape
    group = pl.BlockSpec((1, seq, 256), lambda bi, g: (bi, 0, g))
    state = pltpu.VMEM((2, seq, LANE), F32)
    return pl.pallas_call(
        functools.partial(_dil_kernel, seq=seq),
        grid=(b, DIL_GROUPS),
        in_specs=[group, group, group, _full(toep.shape)],
        out_specs=pl.BlockSpec((1, seq, 256), lambda bi, g: (bi, 0, 0)),
        out_shape=jax.ShapeDtypeStruct((b, seq, 256), BF16),
        scratch_shapes=[state] * 6,
        compiler_params=_params("parallel", "arbitrary"),
        name="dil",
    )(q, k, v, toep)


def _merge_kernel(x_ref, xn_ref, oa_ref, ob_ref, oc_ref, od_ref, wg_ref, bg_ref, wb_ref, wo_ref, xo_ref):
    branches = (oa_ref, ob_ref, oc_ref, od_ref)
    xn = xn_ref[...]
    total = jnp.zeros(x_ref.shape, F32)
    for n in range(N_BRANCH):
        cols = slice(n * D_MODEL, (n + 1) * D_MODEL)
        gate = _sigmoid(_dot(xn, wg_ref[:, cols]) + bg_ref[:, cols])
        total = total + gate * _dot(branches[n][...], wb_ref[n])
    xo_ref[...] = x_ref[...] + _dot(total.astype(BF16), wo_ref[...])


def _merge(x, xn, outs, wg, bg, wb, wo):
    n = x.shape[0]
    rows = lambda width: pl.BlockSpec((ROW_TILE, width), lambda i: (i, 0))
    return pl.pallas_call(
        _merge_kernel,
        grid=(n // ROW_TILE,),
        in_specs=[rows(D_MODEL), rows(D_MODEL)] + [rows(256)] * N_BRANCH
                 + [_full(wg.shape), _full(bg.shape), _full(wb.shape), _full(wo.shape)],
        out_specs=rows(D_MODEL),
        out_shape=jax.ShapeDtypeStruct((n, D_MODEL), F32),
        compiler_params=_params("parallel"),
        name="merge",
    )(x, xn, *outs, wg, bg, wb, wo)


def _bucket_of_distance():
    d = np.arange(REL_MAX_DIST + 1)
    exact = REL_BUCKETS // 2
    nf = np.maximum(d, 1).astype(np.float32)
    log_b = exact + (np.log(nf / exact) / math.log(REL_MAX_DIST / exact) * (REL_BUCKETS - exact)).astype(np.int32)
    return np.where(d < exact, d, np.minimum(log_b, REL_BUCKETS - 1))


def _toeplitz_tiles(table, n_off, dil, max_steps=None):
    span = 2 * LANE
    m = np.arange(span)
    off = np.arange(n_off)[:, None]
    steps = off * LANE + LANE - 1 - m[None, :]
    live = (steps >= 0) if max_steps is None else (steps >= 0) & (steps <= max_steps)
    v = table[:, _bucket_of_distance()[np.clip(steps * dil, 0, REL_MAX_DIST)]]
    v = jnp.where(live[None], v, NEG_INF)
    flat = jnp.tile(v, (1, 1, LANE))[..., :LANE * (span - 1)]
    tiles = flat.reshape(table.shape[0], n_off, LANE, span - 1)[..., LANE - 1:]
    return jnp.concatenate([jnp.full((table.shape[0], 1, LANE, LANE), NEG_INF, F32), tiles], axis=1)


def _same_head(width, head):
    idx = np.arange(width) // head
    return jnp.asarray(idx[:, None] == idx[None, :], BF16)


def _rope_tables(seq):
    half = MLA_ROPE // 2
    freqs = ROPE_THETA ** (-np.arange(half, dtype=np.float64) / half)
    ang = np.arange(seq, dtype=np.float64)[:, None] * freqs[None, :]
    pad = np.zeros((seq, MLA_SLOT - MLA_QK))
    cos_h = np.concatenate([np.ones((seq, MLA_NOPE)), np.cos(ang), np.cos(ang), pad], axis=1)
    sin_h = np.concatenate([np.zeros((seq, MLA_NOPE)), np.sin(ang), np.sin(ang), pad], axis=1)
    return (jnp.asarray(np.tile(cos_h, (1, N_HEADS)), F32), jnp.asarray(np.tile(sin_h, (1, N_HEADS)), F32))


def _rot_half_cols():
    half = MLA_ROPE // 2
    src = np.arange(MLA_SLOT)
    sign = np.zeros(MLA_SLOT, np.float32)
    src[MLA_NOPE:MLA_NOPE + half] = np.arange(MLA_NOPE + half, MLA_QK)
    sign[MLA_NOPE:MLA_NOPE + half] = -1.0
    src[MLA_NOPE + half:MLA_QK] = np.arange(MLA_NOPE, MLA_NOPE + half)
    sign[MLA_NOPE + half:MLA_QK] = 1.0
    return src, sign


def _mixer(x, xn, seq, w_in, b_gate, qk_a, qk_b, qk_c, qk_d, mla_nq, w_uq, mla_nkv, w_ukv,
           w_branch, w_out, toeps, consts):
    n = x.shape[0]
    b = n // seq
    e64, e96, tri, cos, sin = consts
    toep_a, toep_b, toep_d = toeps
    row = lambda a: a.reshape(1, -1)
    tile4 = lambda g: jnp.tile(g, N_HEADS).reshape(1, -1)
    r3 = lambda a: a.reshape(b, seq, a.shape[-1])

    off_iq = OFF_A + 6 * HEAD_DIM
    off_ik = off_iq + N_HEADS * IDX_DIM
    iq_rep = jnp.concatenate([jnp.tile(w_in[:, off_iq + h * IDX_DIM:off_iq + (h + 1) * IDX_DIM], (1, 4))
                              for h in range(N_HEADS)], axis=1)
    ik_rep = jnp.tile(w_in[:, off_ik:off_ik + IDX_DIM], (1, 4))
    iw_pad = jnp.pad(w_in[:, off_ik + IDX_DIM:OFF_A + W_A], ((0, 0), (0, LANE - N_HEADS)))
    half_pad = ((0, 0), (0, LANE - HEAD_DIM))
    q_slots = jnp.concatenate([jnp.pad(w_in[:, OFF_A + h * HEAD_DIM:OFF_A + (h + 1) * HEAD_DIM], half_pad)
                               for h in range(N_HEADS)], axis=1)
    off_k = OFF_A + N_HEADS * HEAD_DIM
    k_slot = jnp.pad(w_in[:, off_k:off_k + HEAD_DIM], half_pad)
    v_twice = jnp.tile(w_in[:, off_k + HEAD_DIM:off_iq], (1, 2))
    wa = jnp.concatenate([q_slots, k_slot, v_twice, iq_rep, ik_rep, iw_pad], axis=1).astype(BF16)
    gq_slots = jnp.tile(jnp.pad(qk_a[0], (0, LANE - HEAD_DIM)), N_HEADS).reshape(1, -1)
    gk_slot = jnp.pad(qk_a[1], (0, LANE - HEAD_DIM)).reshape(1, -1)
    piece_a = _proj_a(n, wa, e96, gq_slots, gk_slot)

    wb_in = w_in[:, OFF_B:OFF_B + W_B].astype(BF16)
    piece_b = _proj_b(n, wb_in, e64, tile4(qk_b[0]), tile4(qk_b[1]))

    src, sign = _rot_half_cols()
    src4 = np.concatenate([h * MLA_SLOT + src for h in range(N_HEADS)])
    sign4 = jnp.asarray(np.tile(sign, N_HEADS))
    slot_pad = MLA_SLOT - MLA_QK
    w_kr = w_in[:, OFF_C + MLA_Q_LORA + MLA_KV_LORA:OFF_C + W_C]
    kr_slots = jnp.pad(w_kr, ((0, 0), (MLA_NOPE, slot_pad)))
    kr_slots = jnp.tile(kr_slots, (1, N_HEADS))
    wc = jnp.concatenate([w_in[:, OFF_C:OFF_C + MLA_Q_LORA + MLA_KV_LORA], kr_slots,
                          kr_slots[:, src4] * sign4], axis=1).astype(BF16)
    slots = lambda w: jnp.pad(w.reshape(w.shape[0], N_HEADS, -1),
                              ((0, 0), (0, 0), (0, MLA_SLOT - w.shape[1] // N_HEADS))).reshape(w.shape[0], -1)
    wuq = slots(w_uq)
    ukv = w_ukv.reshape(MLA_KV_LORA, N_HEADS, MLA_NOPE + MLA_V)
    wuk = slots(ukv[:, :, :MLA_NOPE].reshape(MLA_KV_LORA, -1))
    wuv = ukv[:, :, MLA_NOPE:].reshape(MLA_KV_LORA, -1)
    gq = jnp.tile(jnp.pad(qk_c[0], (0, slot_pad)), N_HEADS)
    gk = jnp.tile(jnp.pad(qk_c[1], (0, slot_pad)), N_HEADS)
    piece_c = _proj_c(
        n, wc, e96, row(mla_nq), row(mla_nkv), wuq.astype(BF16), (wuq[:, src4] * sign4).astype(BF16),
        wuk.astype(BF16), wuv.astype(BF16), row(gq), row(gq[src4]), row(gk), row(gk[src4]), cos, sin, seq)

    wd = w_in[:, OFF_D:OFF_D + W_D].astype(BF16)
    piece_d = _proj_d(n, wd, e64, tile4(qk_d[0]), tile4(qk_d[1]))

    proj_a, proj_b, proj_c, proj_d = _project(xn, (piece_a, piece_b, piece_c, piece_d))
    out_a = _dsa(*map(r3, proj_a), toep_a, tri).reshape(n, 256)
    qb, kb, vb_first, vb_second, kmean = proj_b
    out_b = _moba(r3(qb), r3(kb), r3(vb_first), r3(vb_second),
                  kmean.reshape(b, seq // MOBA_BLOCK, BRANCH_WIDTH), toep_b).reshape(n, 256)
    out_c = _mla(*map(r3, proj_c)).reshape(n, 256)
    out_d = _dil(*map(r3, proj_d), toep_d).reshape(n, 256)

    wg = w_in[:, OFF_G:OFF_G + W_G].astype(BF16)
    return _merge(x, xn, (out_a, out_b, out_c, out_d), wg, b_gate.reshape(1, -1),
                  w_branch.astype(BF16), w_out.astype(BF16))


def kernel(x, norm_gain, w_in, b_gate, qk_gain_a, qk_gain_b, qk_gain_c, qk_gain_d, mla_norm_q,
           w_mla_uq, mla_norm_kv, w_mla_ukv, w_branch, w_out, rel_bias, w_ffn_in, w_ffn_out):
    b, seq, d = x.shape
    depth = norm_gain.shape[0]
    toep_a = _toeplitz_tiles(rel_bias[0:4], seq // LANE, 1)
    toep_b = _toeplitz_tiles(rel_bias[4:8], seq // LANE, 1)
    toep_d = jnp.stack([_toeplitz_tiles(rel_bias[8 + 4 * g:12 + 4 * g], 2, dil, max_steps=window // dil)
                        for g, (window, dil) in enumerate(DIL_PATTERNS)])
    tri = jnp.asarray(np.arange(LANE)[:, None] < np.arange(LANE)[None, :], BF16)
    consts = (_same_head(256, HEAD_DIM), _same_head(256, MLA_SLOT), tri) + _rope_tables(seq)

    x = x.reshape(b * seq, d)
    for l in range(depth):
        g = norm_gain[l]
        x, xn = _ffn(x, g[0:1], g[1:2], w_ffn_in[l, 0].astype(BF16), w_ffn_out[l, 0].astype(BF16))
        x = _mixer(x, xn, seq, w_in[l], b_gate[l], qk_gain_a[l], qk_gain_b[l], qk_gain_c[l],
                   qk_gain_d[l], mla_norm_q[l], w_mla_uq[l], mla_norm_kv[l], w_mla_ukv[l],
                   w_branch[l], w_out[l], (toep_a, toep_b, toep_d), consts)
        x, _ = _ffn(x, g[2:3], g[2:3], w_ffn_in[l, 1].astype(BF16), w_ffn_out[l, 1].astype(BF16))
    return x.reshape(b, seq, d)
```

```python
import functools
import math

import numpy as np
import jax
import jax.numpy as jnp
from jax import lax
from jax.experimental import pallas as pl
from jax.experimental.pallas import tpu as pltpu

D_MODEL = 1024
HEAD_DIM = 64
N_HEADS = 4
BRANCH_WIDTH = N_HEADS * HEAD_DIM
IDX_DIM = 32
DSA_TOPK = 256
MOBA_BLOCK = 256
MOBA_TOPK = 3
MLA_Q_LORA = 384
MLA_KV_LORA = 256
MLA_NOPE = 64
MLA_ROPE = 32
MLA_QK = MLA_NOPE + MLA_ROPE
MLA_V = 64
MLA_SLOT = 128
ROPE_THETA = 10000.0
DIL_PATTERNS = ((128, 1), (512, 4), (2048, 16))
DIL_GROUPS = 3
N_BRANCH = 4
D_FF = 2816
REL_BUCKETS = 32
REL_MAX_DIST = 2048
RMS_EPS = 1e-6

OFF_A = 0
W_A = 4 * HEAD_DIM + 2 * HEAD_DIM + 4 * IDX_DIM + IDX_DIM + 4
OFF_B = OFF_A + W_A
W_B = 3 * BRANCH_WIDTH
OFF_C = OFF_B + W_B
W_C = MLA_Q_LORA + MLA_KV_LORA + MLA_ROPE
OFF_D = OFF_C + W_C
W_D = 3 * DIL_GROUPS * BRANCH_WIDTH
OFF_G = OFF_D + W_D
W_G = N_BRANCH * D_MODEL

LANE = 128
ROW_TILE = 512
FFN_ROWS = 512
FF_CHUNK = 1408
WIDTH_CLASSES = 8
ATT_ROWS = 256
MLA_ROWS = 256
ATT_WIDTH_CLASSES = 4
DSA_WIDTH_CLASSES = 4
DSA_ROWS = 256
MASKED = -1e30
VMEM_LIMIT = 56 * 1024 * 1024

F32 = jnp.float32
BF16 = jnp.bfloat16
NEG_INF = float("-inf")
INT_MIN = -2 ** 31

_CONTRACT_LAST = (((1,), (1,)), ((), ()))


def _dot(a, b, precision=None):
    return jnp.dot(a, b, preferred_element_type=F32, precision=precision)


def _dot_t(a, b, precision=None):
    return lax.dot_general(a, b, _CONTRACT_LAST, preferred_element_type=F32, precision=precision)


def _rms(x, gain):
    return x * lax.rsqrt(jnp.mean(x * x, axis=-1, keepdims=True) + RMS_EPS) * gain


def _sigmoid(x):
    return 1.0 / (1.0 + jnp.exp(-x))


def _head_sumsq(y, e_ref):
    y2 = y * y
    hi = y2.astype(BF16)
    lo = (y2 - hi.astype(F32)).astype(BF16)
    e = e_ref[...]
    step = e.shape[0]
    blocks = [_dot(hi[:, c:c + step], e) + _dot(lo[:, c:c + step], e) for c in range(0, y.shape[1], step)]
    return blocks[0] if len(blocks) == 1 else jnp.concatenate(blocks, axis=1)


def _head_norm(y, e_ref, gain, width):
    return y * lax.rsqrt(_head_sumsq(y, e_ref) * (1.0 / width) + RMS_EPS) * gain


def _params(*sem):
    return pltpu.CompilerParams(dimension_semantics=sem, vmem_limit_bytes=VMEM_LIMIT)


def _full(shape):
    return pl.BlockSpec(shape, lambda *_: (0,) * len(shape))


def _resident(shape):
    return pl.BlockSpec(shape, lambda *_: (0,) * len(shape), pipeline_mode=pl.Buffered(1))


def _ffn_kernel(x_ref, g_ref, gn_ref, wi_ref, wo_ref, xo_ref, xn_ref):
    half = FFN_ROWS // 2
    for s in range(2):
        rows = slice(s * half, (s + 1) * half)
        x = x_ref[rows, :]
        xb = _rms(x, g_ref[...]).astype(BF16)
        acc = None
        for j in range(D_FF // FF_CHUNK):
            gate = _dot(xb, wi_ref[:, j * FF_CHUNK:(j + 1) * FF_CHUNK])
            up = _dot(xb, wi_ref[:, D_FF + j * FF_CHUNK:D_FF + (j + 1) * FF_CHUNK])
            h = (gate * _sigmoid(gate) * up).astype(BF16)
            part = _dot(h, wo_ref[j * FF_CHUNK:(j + 1) * FF_CHUNK, :])
            acc = part if acc is None else acc + part
        xo = x + 0.5 * acc
        xo_ref[rows, :] = xo
        xn_ref[rows, :] = _rms(xo, gn_ref[...]).astype(BF16)


def _ffn(x, gain, gain_next, w_in, w_out):
    n = x.shape[0]
    rows = pl.BlockSpec((FFN_ROWS, D_MODEL), lambda i: (i, 0))
    return pl.pallas_call(
        _ffn_kernel,
        grid=(n // FFN_ROWS,),
        in_specs=[rows, _full((1, D_MODEL)), _full((1, D_MODEL)), _resident(w_in.shape), _resident(w_out.shape)],
        out_specs=[rows, rows],
        out_shape=[jax.ShapeDtypeStruct((n, D_MODEL), F32), jax.ShapeDtypeStruct((n, D_MODEL), BF16)],
        compiler_params=_params("parallel"),
        name="ffn",
    )(x, gain, gain_next, w_in, w_out)


def _split_terms(rep, hi_lanes):
    hi = rep.astype(BF16).astype(F32)
    return jnp.where(hi_lanes, hi, rep - hi).astype(BF16)


def _with_ones(v):
    first = (lax.broadcasted_iota(jnp.int32, (1, v.shape[1]), 1) & (LANE - 1)) < HEAD_DIM
    return jnp.where(first, v, 1.0).astype(BF16), jnp.where(first, 1.0, v).astype(BF16)


def _proj_a_kernel(xn_ref, w_ref, e_ref, gq_ref, gk_ref, q_ref, k_ref, va_ref, vb_ref, iq_ref, ik_ref, iw_ref):
    p = _dot(xn_ref[...], w_ref[...])
    q = _head_norm(p[:, :512], e_ref, gq_ref[...], HEAD_DIM) * HEAD_DIM ** -0.5
    q_ref[...] = q.astype(BF16)
    k = p[:, 512:640]
    k_ms = jnp.sum(k * k, axis=-1, keepdims=True) * (1.0 / HEAD_DIM)
    k_ref[...] = (k * lax.rsqrt(k_ms + RMS_EPS) * gk_ref[...]).astype(BF16)
    va_ref[...], vb_ref[...] = _with_ones(p[:, 640:768])
    lane = lax.broadcasted_iota(jnp.int32, (1, 4 * LANE), 1)
    copy = lax.shift_right_logical(lane, int(math.log2(IDX_DIM))) & 3
    iq_ref[...] = _split_terms(p[:, 768:1280], copy < 2)
    ik_ref[...] = _split_terms(p[:, 1280:1408], (copy[:, :LANE] & 1) == 0)
    iw_ref[...] = p[:, 1408:1536]


def _row_spec(width):
    return pl.BlockSpec((ROW_TILE, width), lambda i: (i, 0))


def _const_spec(a):
    return _resident(a.shape) if a.size >= LANE * LANE * 8 else _full(a.shape)


def _proj_a(n, w, e_slots, gq, gk):
    consts = (w, e_slots, gq, gk)
    widths = ((512, BF16), (128, BF16), (128, BF16), (128, BF16), (512, BF16), (128, BF16), (128, F32))
    return (_proj_a_kernel, consts, [_const_spec(c) for c in consts], [_row_spec(wd) for wd, _ in widths],
            [jax.ShapeDtypeStruct((n, wd), dt) for wd, dt in widths])


def _proj_b_kernel(xn_ref, w_ref, e_ref, gq_ref, gk_ref, q_ref, k_ref, va_ref, vb_ref, km_ref):
    p = _dot(xn_ref[...], w_ref[...])
    q_ref[...] = _head_norm(p[:, :256], e_ref, gq_ref[...], HEAD_DIM)
    k = _head_norm(p[:, 256:512], e_ref, gk_ref[...], HEAD_DIM)
    k_ref[...] = k.astype(BF16)
    va_ref[...], vb_ref[...] = _with_ones(p[:, 512:768])
    km = jnp.mean(k.reshape(ROW_TILE // MOBA_BLOCK, MOBA_BLOCK, BRANCH_WIDTH), axis=1)
    for j in range(ROW_TILE // MOBA_BLOCK):
        km_ref[j] = km[j:j + 1]


def _proj_b(n, w, e64, gq, gk):
    consts = (w, e64, gq, gk)
    widths = ((256, F32), (256, BF16), (256, BF16), (256, BF16))
    per_tile = ROW_TILE // MOBA_BLOCK
    return (_proj_b_kernel, consts, [_const_spec(c) for c in consts],
            [_row_spec(wd) for wd, _ in widths] + [pl.BlockSpec((per_tile, 1, BRANCH_WIDTH), lambda i: (i, 0, 0))],
            [jax.ShapeDtypeStruct((n, wd), dt) for wd, dt in widths]
            + [jax.ShapeDtypeStruct((n // MOBA_BLOCK, 1, BRANCH_WIDTH), F32)])


def _proj_c_kernel(xn_ref, w_ref, e_ref, nq_ref, nkv_ref, wuq_ref, wuqs_ref, wuk_ref, wuv_ref,
                   gq_ref, gqs_ref, gk_ref, gks_ref, cos_ref, sin_ref, q_ref, k_ref, va_ref, vb_ref):
    p = _dot(xn_ref[...], w_ref[...])
    cos = cos_ref[...]
    sin = sin_ref[...]
    xq = _rms(p[:, :384], nq_ref[...]).astype(BF16)
    qa = _dot(xq, wuq_ref[...])
    qs = _dot(xq, wuqs_ref[...])
    rq = lax.rsqrt(_head_sumsq(qa, e_ref) * (1.0 / MLA_QK) + RMS_EPS)
    q_ref[...] = (rq * (qa * gq_ref[...] * cos + qs * gqs_ref[...] * sin)).astype(BF16)
    xkv = _rms(p[:, 384:640], nkv_ref[...]).astype(BF16)
    ka = _dot(xkv, wuk_ref[...]) + p[:, 640:1152]
    ks = p[:, 1152:1664]
    rk = lax.rsqrt(_head_sumsq(ka, e_ref) * (1.0 / MLA_QK) + RMS_EPS)
    k_ref[...] = (rk * (ka * gk_ref[...] * cos + ks * gks_ref[...] * sin)).astype(BF16)
    va_ref[...], vb_ref[...] = _with_ones(_dot(xkv, wuv_ref[...]))


def _proj_c(n, w, e96, nq, nkv, wuq, wuqs, wuk, wuv, gq, gqs, gk, gks, cos, sin, seq):
    pos = pl.BlockSpec((ROW_TILE, N_HEADS * MLA_SLOT), lambda i: (i % (seq // ROW_TILE), 0))
    consts = (w, e96, nq, nkv, wuq, wuqs, wuk, wuv, gq, gqs, gk, gks)
    widths = ((N_HEADS * MLA_SLOT, BF16), (N_HEADS * MLA_SLOT, BF16), (256, BF16), (256, BF16))
    return (_proj_c_kernel, consts + (cos, sin), [_const_spec(c) for c in consts] + [pos, pos],
            [_row_spec(wd) for wd, _ in widths], [jax.ShapeDtypeStruct((n, wd), dt) for wd, dt in widths])


def _proj_d_kernel(xn_ref, w_ref, e_ref, gq_ref, gk_ref, q_ref, k_ref, v_ref):
    xn = xn_ref[...]
    for c in range(DIL_GROUPS):
        lo, hi = c * 256, (c + 1) * 256
        pq = _dot(xn, w_ref[:, lo:hi])
        q_ref[:, lo:hi] = (_head_norm(pq, e_ref, gq_ref[...], HEAD_DIM) * HEAD_DIM ** -0.5).astype(BF16)
        pk = _dot(xn, w_ref[:, 768 + lo:768 + hi])
        k_ref[:, lo:hi] = _head_norm(pk, e_ref, gk_ref[...], HEAD_DIM).astype(BF16)
        v_ref[:, lo:hi] = _dot(xn, w_ref[:, 1536 + lo:1536 + hi]).astype(BF16)


def _proj_d(n, w, e64, gq, gk):
    consts = (w, e64, gq, gk)
    return (_proj_d_kernel, consts, [_const_spec(c) for c in consts], [_row_spec(768)] * 3,
            [jax.ShapeDtypeStruct((n, 768), BF16)] * 3)


def _proj_fused_kernel(*refs, pieces):
    xn_ref, refs = refs[0], refs[1:]
    n_in = sum(k for _, k, _ in pieces)
    ins, outs = refs[:n_in], refs[n_in:]
    for body, k_in, k_out in pieces:
        body(xn_ref, *ins[:k_in], *outs[:k_out])
        ins, outs = ins[k_in:], outs[k_out:]


def _project(xn, pieces):
    n = xn.shape[0]
    layout = tuple((body, len(consts), len(out_specs)) for body, consts, _, out_specs, _ in pieces)
    flat = pl.pallas_call(
        functools.partial(_proj_fused_kernel, pieces=layout),
        grid=(n // ROW_TILE,),
        in_specs=[_row_spec(D_MODEL)] + [s for p in pieces for s in p[2]],
        out_specs=[s for p in pieces for s in p[3]],
        out_shape=[s for p in pieces for s in p[4]],
        compiler_params=_params("parallel"),
        name="proj",
    )(xn, *[c for p in pieces for c in p[1]])
    outs = []
    for _, _, k_out in layout:
        outs.append(flat[:k_out])
        flat = flat[k_out:]
    return outs


def _for_causal_width(qi, n_qtiles, tile_rows, body, classes=WIDTH_CLASSES):
    n_cls = min(classes, n_qtiles)
    per = n_qtiles // n_cls
    for c in range(n_cls):
        pl.when((qi >= c * per) & (qi < (c + 1) * per))(functools.partial(body, (c + 1) * per * tile_rows))


def _bias_rows(toep_ref, h, qblk, nchunk):
    return jnp.concatenate([toep_ref[h, jnp.maximum(qblk - c + 1, 0)] for c in range(nchunk)], axis=1)


def _bias_tile(toep_ref, h, qi, sub_blocks, nchunk):
    return jnp.concatenate([_bias_rows(toep_ref, h, qi * sub_blocks + j, nchunk) for j in range(sub_blocks)],
                           axis=0)


def _softmax_pv(logits, v):
    m = jnp.max(logits, axis=1, keepdims=True)
    p = jnp.exp(logits - m)
    l = jnp.sum(p, axis=1, keepdims=True)
    return _dot(p.astype(BF16), v) / l


def _exp_pv(logits, v_ones):
    return _dot(_exp_weights(logits), v_ones)


def _exp_weights(logits):
    m = jnp.max(logits, axis=1, keepdims=True)
    return jnp.exp(logits - m).astype(BF16)


def _pair_output(first_head, second_head):
    first = lax.broadcasted_iota(jnp.int32, first_head.shape, 1) < HEAD_DIM
    numer = jnp.where(first, first_head, second_head)
    denom = jnp.where(first, pltpu.roll(first_head, HEAD_DIM, axis=1), pltpu.roll(second_head, HEAD_DIM, axis=1))
    return numer / denom


def _dsa_body(width, qi, q_ref, k_ref, va_ref, vb_ref, iq_ref, ik_ref, iw_ref, toep_ref, tri_ref, o_ref,
              sel_ref, n_sel):
    rows = DSA_ROWS
    nchunk = width // LANE
    ik = ik_ref[0, :width, :]
    iw = iw_ref[0][:, :N_HEADS] * (N_HEADS ** -0.5 * IDX_DIM ** -0.5)

    r = _dot_t(jnp.concatenate([iq_ref[0, :, h * LANE:(h + 1) * LANE] for h in range(N_HEADS)], axis=0), ik)
    score = None
    for h in range(N_HEADS):
        term = jnp.maximum(r[h * rows:(h + 1) * rows], 0.0) * iw[:, h:h + 1]
        score = term if score is None else score + term

    t = qi * rows + lax.broadcasted_iota(jnp.int32, (rows, width), 0)
    s = lax.broadcasted_iota(jnp.int32, (rows, width), 1)

    bits = pltpu.bitcast(score, jnp.int32)
    key = jnp.where(bits < 0, jnp.int32(INT_MIN) - bits, bits)
    key = jnp.where(s <= t, key, jnp.int32(INT_MIN))

    half_min = -2 ** 15
    upper = lax.shift_right_arithmetic(key, 16).astype(jnp.int16)
    lower = ((key & jnp.int32(0xFFFF)) + half_min).astype(jnp.int16)

    def lane_counts(hit):
        ones = jnp.where(hit, jnp.int16(1), jnp.int16(0))
        acc = ones[:, :LANE]
        for c in range(1, nchunk):
            acc = acc + ones[:, c * LANE:(c + 1) * LANE]
        return acc

    def total(acc):
        return jnp.sum(acc.astype(F32), axis=1, keepdims=True)

    def count(hit):
        return total(lane_counts(hit))

    def search(half, need):
        top, bottom = slice(0, rows // 2), slice(rows // 2, rows)
        need_rows = jnp.broadcast_to(need, (rows, 1)).astype(F32)

        def bit(step):
            return jnp.where(step < 16, lax.shift_left(jnp.int32(1), jnp.maximum(15 - step, 0)), 0)

        def hits(part, cand):
            return lane_counts(half[part] >= (cand + half_min).astype(jnp.int16))

        def body(i, carry):
            ans_top, acc_top, ans_bottom = carry
            cand_top = ans_top | bit(i)
            ans_top = jnp.where(total(acc_top) >= need_rows[top], cand_top, ans_top)
            cand_bottom = ans_bottom | bit(i)
            ans_bottom = jnp.where(total(hits(bottom, cand_bottom)) >= need_rows[bottom], cand_bottom, ans_bottom)
            return ans_top, hits(top, ans_top | bit(i + 1)), ans_bottom

        zero = jnp.zeros((rows // 2, 1), jnp.int32)
        ans_top, _, ans_bottom = lax.fori_loop(0, 16, body, (zero, hits(top, zero | bit(0)), zero))
        return jnp.concatenate([ans_top, ans_bottom], axis=0)

    def searched_threshold():
        thr_upper = search(upper, n_sel) + half_min
        thr_upper16 = thr_upper.astype(jnp.int16)
        n_over = count(upper > thr_upper16)
        candidates = jnp.where(upper == thr_upper16, lower, jnp.int16(half_min))
        return thr_upper * 65536 + search(candidates, n_sel - n_over)

    thr = lax.cond((qi + 1) * rows <= n_sel, lambda: jnp.full((rows, 1), INT_MIN, jnp.int32), searched_threshold)

    sel = jnp.where(key >= jnp.maximum(thr, jnp.int32(INT_MIN + 1)), 1.0, 0.0)
    sel_ref[:, :width] = sel

    overflow = jnp.sum(sel, axis=1, keepdims=True) > n_sel

    @pl.when(jnp.max(jnp.where(overflow, 1.0, 0.0)) > 0.0)
    def _():
        above = key > thr
        room = n_sel - jnp.sum(jnp.where(above, 1.0, 0.0), axis=1, keepdims=True)
        before = jnp.zeros((rows, 1), F32)
        for c in range(nchunk):
            cols = slice(c * LANE, (c + 1) * LANE)
            tc = jnp.where(key[:, cols] == thr, 1.0, 0.0)
            rank = before + _dot(tc.astype(BF16), tri_ref[...])
            keep = above[:, cols] | ((tc > 0.0) & (rank < room))
            sel_ref[:, cols] = jnp.where(keep, 1.0, 0.0)
            before = before + jnp.sum(tc, axis=1, keepdims=True)

    unselected = (sel_ref[:, :width] - 1.0) * -MASKED
    k = k_ref[0, :width, :]
    qk = _dot_t(jnp.concatenate([q_ref[0, :, h * LANE:(h + 1) * LANE] for h in range(N_HEADS)], axis=0), k)
    logits = [qk[h * rows:(h + 1) * rows] + _bias_tile(toep_ref, h, qi, rows // LANE, nchunk) + unselected
              for h in range(N_HEADS)]
    weights = [_exp_weights(x) for x in logits]
    for pair in range(N_HEADS // 2):
        outs = [_dot(weights[2 * pair], va_ref[0, :width, :]), _dot(weights[2 * pair + 1], vb_ref[0, :width, :])]
        o_ref[0, :, pair * LANE:(pair + 1) * LANE] = _pair_output(*outs).astype(o_ref.dtype)


def _dsa_kernel(*refs, seq, n_sel):
    qi = pl.program_id(1)
    _for_causal_width(qi, seq // DSA_ROWS, DSA_ROWS,
                      functools.partial(_dsa_body, qi=qi, n_sel=n_sel, **_named(refs)), classes=DSA_WIDTH_CLASSES)


def _named(refs):
    names = ("q_ref", "k_ref", "va_ref", "vb_ref", "iq_ref", "ik_ref", "iw_ref", "toep_ref", "tri_ref", "o_ref",
             "sel_ref")
    return dict(zip(names, refs, strict=True))


def _dsa(q, k, va, vb, iq, ik, iw, toep, tri):
    b, seq, _ = q.shape
    n_sel = min(DSA_TOPK, seq // 4)
    qblock = lambda width: pl.BlockSpec((1, DSA_ROWS, width), lambda bi, qi: (bi, qi, 0))
    whole = lambda width: pl.BlockSpec((1, seq, width), lambda bi, qi: (bi, 0, 0))
    return pl.pallas_call(
        functools.partial(_dsa_kernel, seq=seq, n_sel=n_sel),
        grid=(b, seq // DSA_ROWS),
        in_specs=[qblock(512), whole(128), whole(128), whole(128), qblock(512), whole(128), qblock(128),
                  _full(toep.shape), _full(tri.shape)],
        out_specs=qblock(256),
        out_shape=jax.ShapeDtypeStruct((b, seq, 256), BF16),
        scratch_shapes=[pltpu.VMEM((DSA_ROWS, seq), F32)],
        compiler_params=_params("parallel", "parallel"),
        name="dsa",
    )(q, k, va, vb, iq, ik, iw, toep, tri)


def _moba_body(width, qi, q_ref, k_ref, va_ref, vb_ref, km_ref, toep_ref, spread_ref, o_ref, nblk):
    nchunk = width // LANE
    n_sel = min(MOBA_TOPK, nblk - 1)
    rows = ATT_ROWS
    own = lax.shift_right_logical(qi * rows, int(math.log2(MOBA_BLOCK)))
    q = q_ref[0]

    km = km_ref[0]
    head_of_col = lax.shift_right_logical(lax.broadcasted_iota(jnp.int32, (nblk, BRANCH_WIDTH), 1),
                                          int(math.log2(HEAD_DIM)))
    per_head = [jnp.where(head_of_col == h, km, 0.0) for h in range(N_HEADS)]
    pad = jnp.zeros((LANE - N_HEADS * nblk, BRANCH_WIDTH), F32)
    gate = _dot_t(q, jnp.concatenate(per_head + [pad], axis=0), precision=lax.Precision.HIGHEST)

    lane = lax.broadcasted_iota(jnp.int32, (rows, LANE), 1)
    blk = lane & (nblk - 1)
    past = (blk < own) & (lane < N_HEADS * nblk)
    gate = jnp.where(past, gate, NEG_INF)
    rank = jnp.zeros((rows, LANE), F32)
    for d in range(1, nblk):
        before = pltpu.roll(gate, d, axis=1)
        rank = rank + jnp.where(blk >= d, jnp.where(before >= gate, 1.0, 0.0), 0.0)
        after = pltpu.roll(gate, LANE - d, axis=1)
        rank = rank + jnp.where(blk < nblk - d, jnp.where(after > gate, 1.0, 0.0), 0.0)
    penalty = jnp.where(past, jnp.where(rank < n_sel, 0.0, MASKED), jnp.where(blk == own, 0.0, MASKED)).astype(BF16)

    first = lane < HEAD_DIM
    logits = []
    for h in range(N_HEADS):
        ps = slice(h // 2 * LANE, (h // 2 + 1) * LANE)
        mine = first if h % 2 == 0 else ~first
        qh = jnp.where(mine, q[:, ps] * HEAD_DIM ** -0.5, 0.0).astype(BF16)
        unpicked = _dot(penalty, spread_ref[h, :, :width])
        logits.append(_dot_t(qh, k_ref[0, :width, ps]) + _bias_tile(toep_ref, h, qi, rows // LANE, nchunk)
                      + unpicked)
    weights = [_exp_weights(x) for x in logits]
    for pair in range(N_HEADS // 2):
        ps = slice(pair * LANE, (pair + 1) * LANE)
        outs = [_dot(weights[2 * pair], va_ref[0, :width, ps]), _dot(weights[2 * pair + 1], vb_ref[0, :width, ps])]
        o_ref[0, :, ps] = _pair_output(*outs).astype(o_ref.dtype)


def _moba_kernel(q_ref, k_ref, va_ref, vb_ref, km_ref, toep_ref, spread_ref, o_ref, *, seq):
    qi = pl.program_id(1)
    body = functools.partial(_moba_body, qi=qi, q_ref=q_ref, k_ref=k_ref, va_ref=va_ref, vb_ref=vb_ref,
                             km_ref=km_ref, toep_ref=toep_ref, spread_ref=spread_ref, o_ref=o_ref,
                             nblk=seq // MOBA_BLOCK)
    _for_causal_width(qi, seq // ATT_ROWS, ATT_ROWS, body, classes=ATT_WIDTH_CLASSES)


def _moba_spread(seq):
    nblk = seq // MOBA_BLOCK
    row = np.arange(LANE)[None, :, None]
    h = np.arange(N_HEADS)[:, None, None]
    blk_of_key = (np.arange(seq) // MOBA_BLOCK)[None, None, :]
    return jnp.asarray(row == h * nblk + blk_of_key, BF16)


def _moba(q, k, va, vb, k_mean, toep):
    b, seq, _ = q.shape
    nblk = seq // MOBA_BLOCK
    if nblk & (nblk - 1) or N_HEADS * nblk > LANE:
        raise ValueError("MoBA lane layout needs a power-of-two block count with heads*blocks <= 128")
    spread = _moba_spread(seq)
    qblock = pl.BlockSpec((1, ATT_ROWS, 256), lambda bi, qi: (bi, qi, 0))
    whole = pl.BlockSpec((1, seq, 256), lambda bi, qi: (bi, 0, 0))
    means = pl.BlockSpec((1, nblk, 256), lambda bi, qi: (bi, 0, 0))
    return pl.pallas_call(
        functools.partial(_moba_kernel, seq=seq),
        grid=(b, seq // ATT_ROWS),
        in_specs=[qblock, whole, whole, whole, means, _full(toep.shape), _full(spread.shape)],
        out_specs=qblock,
        out_shape=jax.ShapeDtypeStruct((b, seq, 256), BF16),
        compiler_params=_params("parallel", "parallel"),
        name="moba",
    )(q, k, va, vb, k_mean, toep, spread)


def _mla_body(width, qi, q_ref, k_ref, va_ref, vb_ref, causal_ref, o_ref):
    nchunk = width // LANE
    sub = MLA_ROWS // LANE
    causal = jnp.concatenate(
        [jnp.concatenate([causal_ref[0, jnp.clip(qi * sub + j - c + 1, 0, 2)] for c in range(nchunk)], axis=1)
         for j in range(sub)], axis=0)
    logits = []
    for h in range(N_HEADS):
        qs = slice(h * MLA_SLOT, (h + 1) * MLA_SLOT)
        logits.append(_dot_t(q_ref[0, :, qs], k_ref[0, :width, qs]) * MLA_QK ** -0.5 + causal)
    weights = [_exp_weights(x) for x in logits]
    for pair in range(N_HEADS // 2):
        ps = slice(pair * LANE, (pair + 1) * LANE)
        outs = [_dot(weights[2 * pair], va_ref[0, :width, ps]), _dot(weights[2 * pair + 1], vb_ref[0, :width, ps])]
        o_ref[0, :, ps] = _pair_output(*outs).astype(o_ref.dtype)


def _mla_kernel(q_ref, k_ref, va_ref, vb_ref, causal_ref, o_ref, *, seq):
    qi = pl.program_id(1)
    body = functools.partial(_mla_body, qi=qi, q_ref=q_ref, k_ref=k_ref, va_ref=va_ref, vb_ref=vb_ref,
                             causal_ref=causal_ref, o_ref=o_ref)
    _for_causal_width(qi, seq // MLA_ROWS, MLA_ROWS, body, classes=ATT_WIDTH_CLASSES)


def _mla(q, k, va, vb):
    b, seq, _ = q.shape
    tri = np.where(np.arange(LANE)[:, None] >= np.arange(LANE)[None, :], 0.0, NEG_INF)
    causal = jnp.asarray(np.stack([np.full((LANE, LANE), NEG_INF), tri, np.zeros((LANE, LANE))])[None], F32)
    return pl.pallas_call(
        functools.partial(_mla_kernel, seq=seq),
        grid=(b, seq // MLA_ROWS),
        in_specs=[pl.BlockSpec((1, MLA_ROWS, N_HEADS * MLA_SLOT), lambda bi, qi: (bi, qi, 0)),
                  pl.BlockSpec((1, seq, N_HEADS * MLA_SLOT), lambda bi, qi: (bi, 0, 0)),
                  pl.BlockSpec((1, seq, 256), lambda bi, qi: (bi, 0, 0)),
                  pl.BlockSpec((1, seq, 256), lambda bi, qi: (bi, 0, 0)),
                  _full(causal.shape)],
        out_specs=pl.BlockSpec((1, MLA_ROWS, 256), lambda bi, qi: (bi, qi, 0)),
        out_shape=jax.ShapeDtypeStruct((b, seq, 256), BF16),
        compiler_params=_params("parallel", "parallel"),
        name="mla",
    )(q, k, va, vb, causal)


def _dil_group(dil, seq, toep_ref, g, qf_ref, kf_ref, vf_ref, m_ref, l_ref, acc_ref):
    n_band = seq // (dil * LANE)

    def unit(u, carry):
        r = lax.div(u, n_band)
        n = lax.rem(u, n_band)
        if dil == 1:
            cur = pl.ds(pl.multiple_of(n * LANE, LANE), LANE)
            prev = pl.ds(pl.multiple_of(jnp.maximum(n - 1, 0) * LANE, LANE), LANE)
        else:
            cur = pl.ds(n * (LANE * dil) + r, LANE, stride=dil)
            prev = pl.ds(jnp.maximum(n - 1, 0) * (LANE * dil) + r, LANE, stride=dil)
        prev_tile = jnp.where(n > 0, 2, 0)
        first = lax.broadcasted_iota(jnp.int32, (LANE, LANE), 1) < HEAD_DIM
        kk = [jnp.concatenate([kf_ref[pair, prev, :], kf_ref[pair, cur, :]], axis=0).astype(BF16)
              for pair in range(N_HEADS // 2)]
        vv = [jnp.concatenate([vf_ref[pair, prev, :], vf_ref[pair, cur, :]], axis=0).astype(BF16)
              for pair in range(N_HEADS // 2)]
        logits = []
        for h in range(N_HEADS):
            mine = first if h % 2 == 0 else ~first
            bias = jnp.concatenate([toep_ref[g, h, prev_tile], toep_ref[g, h, 1]], axis=1)
            logits.append(_dot_t(jnp.where(mine, qf_ref[h // 2, cur, :], 0.0).astype(BF16), kk[h // 2]) + bias)
        ms = [jnp.max(x, axis=1, keepdims=True) for x in logits]
        ps = [jnp.exp(x - m) for x, m in zip(logits, ms)]
        ls = [jnp.sum(p, axis=1, keepdims=True) for p in ps]
        pvs = [_dot(p.astype(BF16), vv[h // 2]) for h, p in enumerate(ps)]
        for pair in range(N_HEADS // 2):
            m_new, l_new, pv = (jnp.where(first, x[2 * pair], x[2 * pair + 1]) for x in (ms, ls, pvs))
            m_old = m_ref[pair, cur, :]
            m_tot = jnp.maximum(m_old, m_new)
            a_old = jnp.exp(m_old - m_tot)
            a_new = jnp.exp(m_new - m_tot)
            m_ref[pair, cur, :] = m_tot
            l_ref[pair, cur, :] = a_old * l_ref[pair, cur, :] + a_new * l_new
            acc_ref[pair, cur, :] = a_old * acc_ref[pair, cur, :] + a_new * pv
        return carry

    lax.fori_loop(0, dil * n_band, unit, 0, unroll=2)


def _get_rows(ref, rows):
    return jnp.concatenate([ref[0, rows, :], ref[1, rows, :]], axis=1)


def _set_rows(ref, rows, val):
    ref[0, rows, :] = val[:, :LANE]
    ref[1, rows, :] = val[:, LANE:]


def _dil_kernel(q_ref, k_ref, v_ref, toep_ref, o_ref, qf_ref, kf_ref, vf_ref, m_ref, l_ref, acc_ref, *, seq):
    g = pl.program_id(1)
    everything = pl.ds(0, seq)
    _set_rows(qf_ref, everything, q_ref[0].astype(F32))
    _set_rows(kf_ref, everything, k_ref[0].astype(F32))
    _set_rows(vf_ref, everything, v_ref[0].astype(F32))

    @pl.when(g == 0)
    def _():
        m_ref[...] = jnp.full(m_ref.shape, NEG_INF, F32)
        l_ref[...] = jnp.zeros_like(l_ref)
        acc_ref[...] = jnp.zeros_like(acc_ref)

    for gi, (_, dil) in enumerate(DIL_PATTERNS):
        pl.when(g == gi)(functools.partial(_dil_group, dil, seq, toep_ref, gi, qf_ref, kf_ref, vf_ref,
                                           m_ref, l_ref, acc_ref))

    @pl.when(g == DIL_GROUPS - 1)
    def _():
        o_ref[0] = (_get_rows(acc_ref, everything) / _get_rows(l_ref, everything)).astype(o_ref.dtype)


def _dil(q, k, v, toep):
    b, seq, _ = q.shape
    group = pl.BlockSpec((1, seq, 256), lambda bi, g: (bi, 0, g))
    state = pltpu.VMEM((2, seq, LANE), F32)
    return pl.pallas_call(
        functools.partial(_dil_kernel, seq=seq),
        grid=(b, DIL_GROUPS),
        in_specs=[group, group, group, _full(toep.shape)],
        out_specs=pl.BlockSpec((1, seq, 256), lambda bi, g: (bi, 0, 0)),
        out_shape=jax.ShapeDtypeStruct((b, seq, 256), BF16),
        scratch_shapes=[state] * 6,
        compiler_params=_params("parallel", "arbitrary"),
        name="dil",
    )(q, k, v, toep)


def _merge_kernel(x_ref, xn_ref, oa_ref, ob_ref, oc_ref, od_ref, wg_ref, bg_ref, wb_ref, wo_ref, xo_ref):
    branches = (oa_ref, ob_ref, oc_ref, od_ref)
    xn = xn_ref[...]
    total = jnp.zeros(x_ref.shape, F32)
    for n in range(N_BRANCH):
        cols = slice(n * D_MODEL, (n + 1) * D_MODEL)
        gate = _sigmoid(_dot(xn, wg_ref[:, cols]) + bg_ref[:, cols])
        total = total + gate * _dot(branches[n][...], wb_ref[n])
    xo_ref[...] = x_ref[...] + _dot(total.astype(BF16), wo_ref[...])


def _merge(x, xn, outs, wg, bg, wb, wo):
    n = x.shape[0]
    rows = lambda width: pl.BlockSpec((ROW_TILE, width), lambda i: (i, 0))
    return pl.pallas_call(
        _merge_kernel,
        grid=(n // ROW_TILE,),
        in_specs=[rows(D_MODEL), rows(D_MODEL)] + [rows(256)] * N_BRANCH
                 + [_full(wg.shape), _full(bg.shape), _full(wb.shape), _full(wo.shape)],
        out_specs=rows(D_MODEL),
        out_shape=jax.ShapeDtypeStruct((n, D_MODEL), F32),
        compiler_params=_params("parallel"),
        name="merge",
    )(x, xn, *outs, wg, bg, wb, wo)


def _bucket_of_distance():
    d = np.arange(REL_MAX_DIST + 1)
    exact = REL_BUCKETS // 2
    nf = np.maximum(d, 1).astype(np.float32)
    log_b = exact + (np.log(nf / exact) / math.log(REL_MAX_DIST / exact) * (REL_BUCKETS - exact)).astype(np.int32)
    return np.where(d < exact, d, np.minimum(log_b, REL_BUCKETS - 1))


def _toeplitz_tiles(table, n_off, dil, max_steps=None):
    span = 2 * LANE
    m = np.arange(span)
    off = np.arange(n_off)[:, None]
    steps = off * LANE + LANE - 1 - m[None, :]
    live = (steps >= 0) if max_steps is None else (steps >= 0) & (steps <= max_steps)
    v = table[:, _bucket_of_distance()[np.clip(steps * dil, 0, REL_MAX_DIST)]]
    v = jnp.where(live[None], v, NEG_INF)
    flat = jnp.tile(v, (1, 1, LANE))[..., :LANE * (span - 1)]
    tiles = flat.reshape(table.shape[0], n_off, LANE, span - 1)[..., LANE - 1:]
    return jnp.concatenate([jnp.full((table.shape[0], 1, LANE, LANE), NEG_INF, F32), tiles], axis=1)


def _same_head(width, head):
    idx = np.arange(width) // head
    return jnp.asarray(idx[:, None] == idx[None, :], BF16)


def _rope_tables(seq):
    half = MLA_ROPE // 2
    freqs = ROPE_THETA ** (-np.arange(half, dtype=np.float64) / half)
    ang = np.arange(seq, dtype=np.float64)[:, None] * freqs[None, :]
    pad = np.zeros((seq, MLA_SLOT - MLA_QK))
    cos_h = np.concatenate([np.ones((seq, MLA_NOPE)), np.cos(ang), np.cos(ang), pad], axis=1)
    sin_h = np.concatenate([np.zeros((seq, MLA_NOPE)), np.sin(ang), np.sin(ang), pad], axis=1)
    return (jnp.asarray(np.tile(cos_h, (1, N_HEADS)), F32), jnp.asarray(np.tile(sin_h, (1, N_HEADS)), F32))


def _rot_half_cols():
    half = MLA_ROPE // 2
    src = np.arange(MLA_SLOT)
    sign = np.zeros(MLA_SLOT, np.float32)
    src[MLA_NOPE:MLA_NOPE + half] = np.arange(MLA_NOPE + half, MLA_QK)
    sign[MLA_NOPE:MLA_NOPE + half] = -1.0
    src[MLA_NOPE + half:MLA_QK] = np.arange(MLA_NOPE, MLA_NOPE + half)
    sign[MLA_NOPE + half:MLA_QK] = 1.0
    return src, sign


def _mixer(x, xn, seq, w_in, b_gate, qk_a, qk_b, qk_c, qk_d, mla_nq, w_uq, mla_nkv, w_ukv,
           w_branch, w_out, toeps, consts):
    n = x.shape[0]
    b = n // seq
    e64, e96, tri, cos, sin = consts
    toep_a, toep_b, toep_d = toeps
    row = lambda a: a.reshape(1, -1)
    tile4 = lambda g: jnp.tile(g, N_HEADS).reshape(1, -1)
    r3 = lambda a: a.reshape(b, seq, a.shape[-1])

    off_iq = OFF_A + 6 * HEAD_DIM
    off_ik = off_iq + N_HEADS * IDX_DIM
    iq_rep = jnp.concatenate([jnp.tile(w_in[:, off_iq + h * IDX_DIM:off_iq + (h + 1) * IDX_DIM], (1, 4))
                              for h in range(N_HEADS)], axis=1)
    ik_rep = jnp.tile(w_in[:, off_ik:off_ik + IDX_DIM], (1, 4))
    iw_pad = jnp.pad(w_in[:, off_ik + IDX_DIM:OFF_A + W_A], ((0, 0), (0, LANE - N_HEADS)))
    half_pad = ((0, 0), (0, LANE - HEAD_DIM))
    q_slots = jnp.concatenate([jnp.pad(w_in[:, OFF_A + h * HEAD_DIM:OFF_A + (h + 1) * HEAD_DIM], half_pad)
                               for h in range(N_HEADS)], axis=1)
    off_k = OFF_A + N_HEADS * HEAD_DIM
    k_slot = jnp.pad(w_in[:, off_k:off_k + HEAD_DIM], half_pad)
    v_twice = jnp.tile(w_in[:, off_k + HEAD_DIM:off_iq], (1, 2))
    wa = jnp.concatenate([q_slots, k_slot, v_twice, iq_rep, ik_rep, iw_pad], axis=1).astype(BF16)
    gq_slots = jnp.tile(jnp.pad(qk_a[0], (0, LANE - HEAD_DIM)), N_HEADS).reshape(1, -1)
    gk_slot = jnp.pad(qk_a[1], (0, LANE - HEAD_DIM)).reshape(1, -1)
    piece_a = _proj_a(n, wa, e96, gq_slots, gk_slot)

    wb_in = w_in[:, OFF_B:OFF_B + W_B].astype(BF16)
    piece_b = _proj_b(n, wb_in, e64, tile4(qk_b[0]), tile4(qk_b[1]))

    src, sign = _rot_half_cols()
    src4 = np.concatenate([h * MLA_SLOT + src for h in range(N_HEADS)])
    sign4 = jnp.asarray(np.tile(sign, N_HEADS))
    slot_pad = MLA_SLOT - MLA_QK
    w_kr = w_in[:, OFF_C + MLA_Q_LORA + MLA_KV_LORA:OFF_C + W_C]
    kr_slots = jnp.pad(w_kr, ((0, 0), (MLA_NOPE, slot_pad)))
    kr_slots = jnp.tile(kr_slots, (1, N_HEADS))
    wc = jnp.concatenate([w_in[:, OFF_C:OFF_C + MLA_Q_LORA + MLA_KV_LORA], kr_slots,
                          kr_slots[:, src4] * sign4], axis=1).astype(BF16)
    slots = lambda w: jnp.pad(w.reshape(w.shape[0], N_HEADS, -1),
                              ((0, 0), (0, 0), (0, MLA_SLOT - w.shape[1] // N_HEADS))).reshape(w.shape[0], -1)
    wuq = slots(w_uq)
    ukv = w_ukv.reshape(MLA_KV_LORA, N_HEADS, MLA_NOPE + MLA_V)
    wuk = slots(ukv[:, :, :MLA_NOPE].reshape(MLA_KV_LORA, -1))
    wuv = ukv[:, :, MLA_NOPE:].reshape(MLA_KV_LORA, -1)
    gq = jnp.tile(jnp.pad(qk_c[0], (0, slot_pad)), N_HEADS)
    gk = jnp.tile(jnp.pad(qk_c[1], (0, slot_pad)), N_HEADS)
    piece_c = _proj_c(
        n, wc, e96, row(mla_nq), row(mla_nkv), wuq.astype(BF16), (wuq[:, src4] * sign4).astype(BF16),
        wuk.astype(BF16), wuv.astype(BF16), row(gq), row(gq[src4]), row(gk), row(gk[src4]), cos, sin, seq)

    wd = w_in[:, OFF_D:OFF_D + W_D].astype(BF16)
    piece_d = _proj_d(n, wd, e64, tile4(qk_d[0]), tile4(qk_d[1]))

    proj_a, proj_b, proj_c, proj_d = _project(xn, (piece_a, piece_b, piece_c, piece_d))
    out_a = _dsa(*map(r3, proj_a), toep_a, tri).reshape(n, 256)
    qb, kb, vb_first, vb_second, kmean = proj_b
    out_b = _moba(r3(qb), r3(kb), r3(vb_first), r3(vb_second),
                  kmean.reshape(b, seq // MOBA_BLOCK, BRANCH_WIDTH), toep_b).reshape(n, 256)
    out_c = _mla(*map(r3, proj_c)).reshape(n, 256)
    out_d = _dil(*map(r3, proj_d), toep_d).reshape(n, 256)

    wg = w_in[:, OFF_G:OFF_G + W_G].astype(BF16)
    return _merge(x, xn, (out_a, out_b, out_c, out_d), wg, b_gate.reshape(1, -1),
                  w_branch.astype(BF16), w_out.astype(BF16))


def kernel(x, norm_gain, w_in, b_gate, qk_gain_a, qk_gain_b, qk_gain_c, qk_gain_d, mla_norm_q,
           w_mla_uq, mla_norm_kv, w_mla_ukv, w_branch, w_out, rel_bias, w_ffn_in, w_ffn_out):
    b, seq, d = x.shape
    depth = norm_gain.shape[0]
    toep_a = _toeplitz_tiles(rel_bias[0:4], seq // LANE, 1)
    toep_b = _toeplitz_tiles(rel_bias[4:8], seq // LANE, 1)
    toep_d = jnp.stack([_toeplitz_tiles(rel_bias[8 + 4 * g:12 + 4 * g], 2, dil, max_steps=window // dil)
                        for g, (window, dil) in enumerate(DIL_PATTERNS)])
    tri = jnp.asarray(np.arange(LANE)[:, None] < np.arange(LANE)[None, :], BF16)
    consts = (_same_head(256, HEAD_DIM), _same_head(256, MLA_SLOT), tri) + _rope_tables(seq)

    x = x.reshape(b * seq, d)
    for l in range(depth):
        g = norm_gain[l]
        x, xn = _ffn(x, g[0:1], g[1:2], w_ffn_in[l, 0].astype(BF16), w_ffn_out[l, 0].astype(BF16))
        x = _mixer(x, xn, seq, w_in[l], b_gate[l], qk_gain_a[l], qk_gain_b[l], qk_gain_c[l],
                   qk_gain_d[l], mla_norm_q[l], w_mla_uq[l], mla_norm_kv[l], w_mla_ukv[l],
                   w_branch[l], w_out[l], (toep_a, toep_b, toep_d), consts)
        x, _ = _ffn(x, g[2:3], g[2:3], w_ffn_in[l, 1].astype(BF16), w_ffn_out[l, 1].astype(BF16))
    return x.reshape(b, seq, d)
```

```python
import functools
import math

import numpy as np
import jax
import jax.numpy as jnp
from jax import lax
from jax.experimental import pallas as pl
from jax.experimental.pallas import tpu as pltpu

D_MODEL = 1024
HEAD_DIM = 64
N_HEADS = 4
BRANCH_WIDTH = N_HEADS * HEAD_DIM
IDX_DIM = 32
DSA_TOPK = 256
MOBA_BLOCK = 256
MOBA_TOPK = 3
MLA_Q_LORA = 384
MLA_KV_LORA = 256
MLA_NOPE = 64
MLA_ROPE = 32
MLA_QK = MLA_NOPE + MLA_ROPE
MLA_V = 64
MLA_SLOT = 128
ROPE_THETA = 10000.0
DIL_PATTERNS = ((128, 1), (512, 4), (2048, 16))
DIL_GROUPS = 3
N_BRANCH = 4
D_FF = 2816
REL_BUCKETS = 32
REL_MAX_DIST = 2048
RMS_EPS = 1e-6

OFF_A = 0
W_A = 4 * HEAD_DIM + 2 * HEAD_DIM + 4 * IDX_DIM + IDX_DIM + 4
OFF_B = OFF_A + W_A
W_B = 3 * BRANCH_WIDTH
OFF_C = OFF_B + W_B
W_C = MLA_Q_LORA + MLA_KV_LORA + MLA_ROPE
OFF_D = OFF_C + W_C
W_D = 3 * DIL_GROUPS * BRANCH_WIDTH
OFF_G = OFF_D + W_D
W_G = N_BRANCH * D_MODEL

LANE = 128
ROW_TILE = 512
FF_CHUNK = 1408
ATT_ROWS = 256
WIDTH_CLASSES = 4
MASKED = -1e30
VMEM_LIMIT = 56 * 1024 * 1024

F32 = jnp.float32
BF16 = jnp.bfloat16
NEG_INF = float("-inf")
INT_MIN = -2 ** 31

_CONTRACT_LAST = (((1,), (1,)), ((), ()))


def _dot(a, b, precision=None):
    return jnp.dot(a, b, preferred_element_type=F32, precision=precision)


def _dot_t(a, b, precision=None):
    return lax.dot_general(a, b, _CONTRACT_LAST, preferred_element_type=F32, precision=precision)


def _rms(x, gain):
    return x * lax.rsqrt(jnp.mean(x * x, axis=-1, keepdims=True) + RMS_EPS) * gain


def _sigmoid(x):
    return 1.0 / (1.0 + jnp.exp(-x))


def _head_sumsq(y, e_ref):
    y2 = y * y
    hi = y2.astype(BF16)
    lo = (y2 - hi.astype(F32)).astype(BF16)
    e = e_ref[...]
    step = e.shape[0]
    blocks = [_dot(hi[:, c:c + step], e) + _dot(lo[:, c:c + step], e) for c in range(0, y.shape[1], step)]
    return blocks[0] if len(blocks) == 1 else jnp.concatenate(blocks, axis=1)


def _head_norm(y, e_ref, gain, width):
    return y * lax.rsqrt(_head_sumsq(y, e_ref) * (1.0 / width) + RMS_EPS) * gain


def _params(*sem):
    return pltpu.CompilerParams(dimension_semantics=sem, vmem_limit_bytes=VMEM_LIMIT)


def _full(shape):
    return pl.BlockSpec(shape, lambda *_: (0,) * len(shape))


def _resident(shape):
    return pl.BlockSpec(shape, lambda *_: (0,) * len(shape), pipeline_mode=pl.Buffered(1))


def _ffn_kernel(x_ref, g_ref, gn_ref, wi_ref, wo_ref, xo_ref, xn_ref):
    half = ROW_TILE // 2
    for s in range(2):
        rows = slice(s * half, (s + 1) * half)
        x = x_ref[rows, :]
        xb = _rms(x, g_ref[...]).astype(BF16)
        acc = None
        for j in range(D_FF // FF_CHUNK):
            gate = _dot(xb, wi_ref[:, j * FF_CHUNK:(j + 1) * FF_CHUNK])
            up = _dot(xb, wi_ref[:, D_FF + j * FF_CHUNK:D_FF + (j + 1) * FF_CHUNK])
            h = (gate * _sigmoid(gate) * up).astype(BF16)
            part = _dot(h, wo_ref[j * FF_CHUNK:(j + 1) * FF_CHUNK, :])
            acc = part if acc is None else acc + part
        xo = x + 0.5 * acc
        xo_ref[rows, :] = xo
        xn_ref[rows, :] = _rms(xo, gn_ref[...]).astype(BF16)


def _ffn(x, gain, gain_next, w_in, w_out):
    n = x.shape[0]
    rows = pl.BlockSpec((ROW_TILE, D_MODEL), lambda i: (i, 0))
    return pl.pallas_call(
        _ffn_kernel,
        grid=(n // ROW_TILE,),
        in_specs=[rows, _full((1, D_MODEL)), _full((1, D_MODEL)), _resident(w_in.shape), _resident(w_out.shape)],
        out_specs=[rows, rows],
        out_shape=[jax.ShapeDtypeStruct((n, D_MODEL), F32), jax.ShapeDtypeStruct((n, D_MODEL), BF16)],
        compiler_params=_params("parallel"),
        name="ffn",
    )(x, gain, gain_next, w_in, w_out)


def _split_terms(rep, hi_lanes):
    hi = rep.astype(BF16).astype(F32)
    return jnp.where(hi_lanes, hi, rep - hi).astype(BF16)


def _with_ones(v):
    first = (lax.broadcasted_iota(jnp.int32, (1, v.shape[1]), 1) & (LANE - 1)) < HEAD_DIM
    return jnp.where(first, v, 1.0).astype(BF16), jnp.where(first, 1.0, v).astype(BF16)


def _proj_a_kernel(xn_ref, w_ref, e_ref, gq_ref, gk_ref, q_ref, k_ref, va_ref, vb_ref, iq_ref, ik_ref, iw_ref):
    p = _dot(xn_ref[...], w_ref[...])
    q = _head_norm(p[:, :512], e_ref, gq_ref[...], HEAD_DIM) * HEAD_DIM ** -0.5
    q_ref[...] = q.astype(BF16)
    k = p[:, 512:640]
    k_ms = jnp.sum(k * k, axis=-1, keepdims=True) * (1.0 / HEAD_DIM)
    k_ref[...] = (k * lax.rsqrt(k_ms + RMS_EPS) * gk_ref[...]).astype(BF16)
    va_ref[...], vb_ref[...] = _with_ones(p[:, 640:768])
    lane = lax.broadcasted_iota(jnp.int32, (1, 4 * LANE), 1)
    copy = lax.shift_right_logical(lane, int(math.log2(IDX_DIM))) & 3
    iq_ref[...] = _split_terms(p[:, 768:1280], copy < 2)
    ik_ref[...] = _split_terms(p[:, 1280:1408], (copy[:, :LANE] & 1) == 0)
    iw_ref[...] = p[:, 1408:1536]


def _row_spec(width):
    return pl.BlockSpec((ROW_TILE, width), lambda i: (i, 0))


def _const_spec(a):
    return _resident(a.shape)


def _proj_a(n, w, e_slots, gq, gk):
    consts = (w, e_slots, gq, gk)
    widths = ((512, BF16), (128, BF16), (128, BF16), (128, BF16), (512, BF16), (128, BF16), (128, F32))
    return (_proj_a_kernel, consts, [_const_spec(c) for c in consts], [_row_spec(wd) for wd, _ in widths],
            [jax.ShapeDtypeStruct((n, wd), dt) for wd, dt in widths])


def _proj_b_kernel(xn_ref, w_ref, e_ref, gq_ref, gk_ref, q_ref, k_ref, va_ref, vb_ref, km_ref):
    p = _dot(xn_ref[...], w_ref[...])
    q_ref[...] = _head_norm(p[:, :256], e_ref, gq_ref[...], HEAD_DIM)
    k = _head_norm(p[:, 256:512], e_ref, gk_ref[...], HEAD_DIM)
    k_ref[...] = k.astype(BF16)
    va_ref[...], vb_ref[...] = _with_ones(p[:, 512:768])
    km = jnp.mean(k.reshape(ROW_TILE // MOBA_BLOCK, MOBA_BLOCK, BRANCH_WIDTH), axis=1)
    for j in range(ROW_TILE // MOBA_BLOCK):
        km_ref[j] = km[j:j + 1]


def _proj_b(n, w, e64, gq, gk):
    consts = (w, e64, gq, gk)
    widths = ((256, F32), (256, BF16), (256, BF16), (256, BF16))
    per_tile = ROW_TILE // MOBA_BLOCK
    return (_proj_b_kernel, consts, [_const_spec(c) for c in consts],
            [_row_spec(wd) for wd, _ in widths] + [pl.BlockSpec((per_tile, 1, BRANCH_WIDTH), lambda i: (i, 0, 0))],
            [jax.ShapeDtypeStruct((n, wd), dt) for wd, dt in widths]
            + [jax.ShapeDtypeStruct((n // MOBA_BLOCK, 1, BRANCH_WIDTH), F32)])


def _proj_c_kernel(xn_ref, w_ref, e_ref, nq_ref, nkv_ref, wuq_ref, wuqs_ref, wuk_ref, wuv_ref,
                   gq_ref, gqs_ref, gk_ref, gks_ref, cos_ref, sin_ref, q_ref, k_ref, va_ref, vb_ref):
    p = _dot(xn_ref[...], w_ref[...])
    cos = cos_ref[...]
    sin = sin_ref[...]
    xq = _rms(p[:, :384], nq_ref[...]).astype(BF16)
    qa = _dot(xq, wuq_ref[...])
    qs = _dot(xq, wuqs_ref[...])
    rq = lax.rsqrt(_head_sumsq(qa, e_ref) * (1.0 / MLA_QK) + RMS_EPS)
    q_ref[...] = (rq * (qa * gq_ref[...] * cos + qs * gqs_ref[...] * sin)).astype(BF16)
    xkv = _rms(p[:, 384:640], nkv_ref[...]).astype(BF16)
    ka = _dot(xkv, wuk_ref[...]) + p[:, 640:1152]
    ks = p[:, 1152:1664]
    rk = lax.rsqrt(_head_sumsq(ka, e_ref) * (1.0 / MLA_QK) + RMS_EPS)
    k_ref[...] = (rk * (ka * gk_ref[...] * cos + ks * gks_ref[...] * sin)).astype(BF16)
    va_ref[...], vb_ref[...] = _with_ones(_dot(xkv, wuv_ref[...]))


def _proj_c(n, w, e_slots, nq, nkv, wuq, wuqs, wuk, wuv, gq, gqs, gk, gks, cos, sin, seq):
    pos = pl.BlockSpec((ROW_TILE, N_HEADS * MLA_SLOT), lambda i: (i % (seq // ROW_TILE), 0))
    consts = (w, e_slots, nq, nkv, wuq, wuqs, wuk, wuv, gq, gqs, gk, gks)
    widths = ((N_HEADS * MLA_SLOT, BF16), (N_HEADS * MLA_SLOT, BF16), (256, BF16), (256, BF16))
    return (_proj_c_kernel, consts + (cos, sin), [_const_spec(c) for c in consts] + [pos, pos],
            [_row_spec(wd) for wd, _ in widths], [jax.ShapeDtypeStruct((n, wd), dt) for wd, dt in widths])


def _proj_d_kernel(xn_ref, w_ref, e_ref, gq_ref, gk_ref, q_ref, k_ref, v_ref):
    xn = xn_ref[...]

    def put(ref, c, val):
        ref[2 * c] = val[:, :LANE]
        ref[2 * c + 1] = val[:, LANE:]

    for c in range(DIL_GROUPS):
        lo, hi = c * 256, (c + 1) * 256
        put(q_ref, c, _head_norm(_dot(xn, w_ref[:, lo:hi]), e_ref, gq_ref[...], HEAD_DIM) * HEAD_DIM ** -0.5)
        put(k_ref, c, _head_norm(_dot(xn, w_ref[:, 768 + lo:768 + hi]), e_ref, gk_ref[...], HEAD_DIM))
        put(v_ref, c, _dot(xn, w_ref[:, 1536 + lo:1536 + hi]))


def _proj_d(n, w, e64, gq, gk):
    consts = (w, e64, gq, gk)
    tiles = 2 * DIL_GROUPS
    return (_proj_d_kernel, consts, [_const_spec(c) for c in consts],
            [pl.BlockSpec((tiles, ROW_TILE, LANE), lambda i: (0, i, 0))] * 3,
            [jax.ShapeDtypeStruct((tiles, n, LANE), F32)] * 3)


def _proj_fused_kernel(*refs, pieces):
    xn_ref, refs = refs[0], refs[1:]
    n_in = sum(k for _, k, _ in pieces)
    ins, outs = refs[:n_in], refs[n_in:]
    for body, k_in, k_out in pieces:
        body(xn_ref, *ins[:k_in], *outs[:k_out])
        ins, outs = ins[k_in:], outs[k_out:]


def _project(xn, pieces):
    n = xn.shape[0]
    layout = tuple((body, len(consts), len(out_specs)) for body, consts, _, out_specs, _ in pieces)
    flat = pl.pallas_call(
        functools.partial(_proj_fused_kernel, pieces=layout),
        grid=(n // ROW_TILE,),
        in_specs=[_row_spec(D_MODEL)] + [s for p in pieces for s in p[2]],
        out_specs=[s for p in pieces for s in p[3]],
        out_shape=[s for p in pieces for s in p[4]],
        compiler_params=_params("parallel"),
        name="proj",
    )(xn, *[c for p in pieces for c in p[1]])
    outs = []
    for _, _, k_out in layout:
        outs.append(flat[:k_out])
        flat = flat[k_out:]
    return outs


def _for_causal_width(qi, seq, body):
    n_qtiles = seq // ATT_ROWS
    n_cls = min(WIDTH_CLASSES, n_qtiles)
    per = n_qtiles // n_cls
    for c in range(n_cls):
        pl.when((qi >= c * per) & (qi < (c + 1) * per))(functools.partial(body, (c + 1) * per * ATT_ROWS))


def _bias_rows(toep_ref, h, qblk, nchunk):
    return jnp.concatenate([toep_ref[h, jnp.maximum(qblk - c + 1, 0)] for c in range(nchunk)], axis=1)


def _bias_tile(toep_ref, h, qi, sub_blocks, nchunk):
    return jnp.concatenate([_bias_rows(toep_ref, h, qi * sub_blocks + j, nchunk) for j in range(sub_blocks)],
                           axis=0)


def _exp_weights(logits):
    m = jnp.max(logits, axis=1, keepdims=True)
    return jnp.exp(logits - m).astype(BF16)


def _pair_output(first_head, second_head):
    first = lax.broadcasted_iota(jnp.int32, first_head.shape, 1) < HEAD_DIM
    numer = jnp.where(first, first_head, second_head)
    denom = jnp.where(first, pltpu.roll(first_head, HEAD_DIM, axis=1), pltpu.roll(second_head, HEAD_DIM, axis=1))
    return numer / denom


def _dsa_body(width, qi, q_ref, k_ref, va_ref, vb_ref, iq_ref, ik_ref, iw_ref, toep_ref, tri_ref, o_ref,
              sel_ref, n_sel):
    rows = ATT_ROWS
    nchunk = width // LANE
    ik = ik_ref[0, :width, :]
    iw = iw_ref[0][:, :N_HEADS] * (N_HEADS ** -0.5 * IDX_DIM ** -0.5)

    r = _dot_t(jnp.concatenate([iq_ref[0, :, h * LANE:(h + 1) * LANE] for h in range(N_HEADS)], axis=0), ik)
    score = None
    for h in range(N_HEADS):
        term = jnp.maximum(r[h * rows:(h + 1) * rows], 0.0) * iw[:, h:h + 1]
        score = term if score is None else score + term

    t = qi * rows + lax.broadcasted_iota(jnp.int32, (rows, width), 0)
    s = lax.broadcasted_iota(jnp.int32, (rows, width), 1)

    bits = pltpu.bitcast(score, jnp.int32)
    key = jnp.where(bits < 0, jnp.int32(INT_MIN) - bits, bits)
    key = jnp.where(s <= t, key, jnp.int32(INT_MIN))

    half_min = -2 ** 15
    upper = lax.shift_right_arithmetic(key, 16).astype(jnp.int16)
    lower = ((key & jnp.int32(0xFFFF)) + half_min).astype(jnp.int16)

    def lane_counts(hit):
        ones = jnp.where(hit, jnp.int16(1), jnp.int16(0))
        acc = ones[:, :LANE]
        for c in range(1, nchunk):
            acc = acc + ones[:, c * LANE:(c + 1) * LANE]
        return acc

    def total(acc):
        return jnp.sum(acc.astype(F32), axis=1, keepdims=True)

    def count(hit):
        return total(lane_counts(hit))

    def search(half, need):
        top, bottom = slice(0, rows // 2), slice(rows // 2, rows)
        need_rows = jnp.broadcast_to(need, (rows, 1)).astype(F32)

        def bit(step):
            return jnp.where(step < 16, lax.shift_left(jnp.int32(1), jnp.maximum(15 - step, 0)), 0)

        def hits(part, cand):
            return lane_counts(half[part] >= (cand + half_min).astype(jnp.int16))

        def body(i, carry):
            ans_top, acc_top, ans_bottom = carry
            cand_top = ans_top | bit(i)
            ans_top = jnp.where(total(acc_top) >= need_rows[top], cand_top, ans_top)
            cand_bottom = ans_bottom | bit(i)
            ans_bottom = jnp.where(total(hits(bottom, cand_bottom)) >= need_rows[bottom], cand_bottom, ans_bottom)
            return ans_top, hits(top, ans_top | bit(i + 1)), ans_bottom

        zero = jnp.zeros((rows // 2, 1), jnp.int32)
        ans_top, _, ans_bottom = lax.fori_loop(0, 16, body, (zero, hits(top, zero | bit(0)), zero))
        return jnp.concatenate([ans_top, ans_bottom], axis=0)

    def searched_threshold():
        thr_upper = search(upper, n_sel) + half_min
        thr_upper16 = thr_upper.astype(jnp.int16)
        n_over = count(upper > thr_upper16)
        candidates = jnp.where(upper == thr_upper16, lower, jnp.int16(half_min))
        return thr_upper * 65536 + search(candidates, n_sel - n_over)

    thr = lax.cond((qi + 1) * rows <= n_sel, lambda: jnp.full((rows, 1), INT_MIN, jnp.int32), searched_threshold)

    sel = jnp.where(key >= jnp.maximum(thr, jnp.int32(INT_MIN + 1)), 1.0, 0.0)
    sel_ref[:, :width] = sel

    overflow = jnp.sum(sel, axis=1, keepdims=True) > n_sel

    @pl.when(jnp.max(jnp.where(overflow, 1.0, 0.0)) > 0.0)
    def _():
        above = key > thr
        room = n_sel - jnp.sum(jnp.where(above, 1.0, 0.0), axis=1, keepdims=True)
        before = jnp.zeros((rows, 1), F32)
        for c in range(nchunk):
            cols = slice(c * LANE, (c + 1) * LANE)
            tc = jnp.where(key[:, cols] == thr, 1.0, 0.0)
            rank = before + _dot(tc.astype(BF16), tri_ref[...])
            keep = above[:, cols] | ((tc > 0.0) & (rank < room))
            sel_ref[:, cols] = jnp.where(keep, 1.0, 0.0)
            before = before + jnp.sum(tc, axis=1, keepdims=True)

    unselected = (sel_ref[:, :width] - 1.0) * -MASKED
    k = k_ref[0, :width, :]
    qk = _dot_t(jnp.concatenate([q_ref[0, :, h * LANE:(h + 1) * LANE] for h in range(N_HEADS)], axis=0), k)
    logits = [qk[h * rows:(h + 1) * rows] + _bias_tile(toep_ref, h, qi, rows // LANE, nchunk) + unselected
              for h in range(N_HEADS)]
    weights = [_exp_weights(x) for x in logits]
    for pair in range(N_HEADS // 2):
        outs = [_dot(weights[2 * pair], va_ref[0, :width, :]), _dot(weights[2 * pair + 1], vb_ref[0, :width, :])]
        o_ref[0, :, pair * LANE:(pair + 1) * LANE] = _pair_output(*outs).astype(o_ref.dtype)


def _dsa_kernel(*refs, seq, n_sel):
    qi = pl.program_id(1)
    _for_causal_width(qi, seq, functools.partial(_dsa_body, qi=qi, n_sel=n_sel, **_named(refs)))


def _named(refs):
    names = ("q_ref", "k_ref", "va_ref", "vb_ref", "iq_ref", "ik_ref", "iw_ref", "toep_ref", "tri_ref", "o_ref",
             "sel_ref")
    return dict(zip(names, refs, strict=True))


def _dsa(q, k, va, vb, iq, ik, iw, toep, tri):
    b, seq, _ = q.shape
    n_sel = min(DSA_TOPK, seq // 4)
    qblock = lambda width: pl.BlockSpec((1, ATT_ROWS, width), lambda bi, qi: (bi, qi, 0))
    whole = lambda width: pl.BlockSpec((1, seq, width), lambda bi, qi: (bi, 0, 0))
    return pl.pallas_call(
        functools.partial(_dsa_kernel, seq=seq, n_sel=n_sel),
        grid=(b, seq // ATT_ROWS),
        in_specs=[qblock(512), whole(128), whole(128), whole(128), qblock(512), whole(128), qblock(128),
                  _full(toep.shape), _full(tri.shape)],
        out_specs=qblock(256),
        out_shape=jax.ShapeDtypeStruct((b, seq, 256), BF16),
        scratch_shapes=[pltpu.VMEM((ATT_ROWS, seq), F32)],
        compiler_params=_params("parallel", "parallel"),
        name="dsa",
    )(q, k, va, vb, iq, ik, iw, toep, tri)


def _moba_body(width, qi, q_ref, k_ref, va_ref, vb_ref, km_ref, toep_ref, spread_ref, o_ref, nblk):
    nchunk = width // LANE
    n_sel = min(MOBA_TOPK, nblk - 1)
    rows = ATT_ROWS
    own = lax.shift_right_logical(qi * rows, int(math.log2(MOBA_BLOCK)))
    q = q_ref[0]

    km = km_ref[0]
    head_of_col = lax.shift_right_logical(lax.broadcasted_iota(jnp.int32, (nblk, BRANCH_WIDTH), 1),
                                          int(math.log2(HEAD_DIM)))
    per_head = [jnp.where(head_of_col == h, km, 0.0) for h in range(N_HEADS)]
    pad = jnp.zeros((LANE - N_HEADS * nblk, BRANCH_WIDTH), F32)
    gate = _dot_t(q, jnp.concatenate(per_head + [pad], axis=0), precision=lax.Precision.HIGHEST)

    lane = lax.broadcasted_iota(jnp.int32, (rows, LANE), 1)
    blk = lane & (nblk - 1)
    past = (blk < own) & (lane < N_HEADS * nblk)
    gate = jnp.where(past, gate, NEG_INF)
    rank = jnp.zeros((rows, LANE), F32)
    for d in range(1, nblk):
        before = pltpu.roll(gate, d, axis=1)
        rank = rank + jnp.where(blk >= d, jnp.where(before >= gate, 1.0, 0.0), 0.0)
        after = pltpu.roll(gate, LANE - d, axis=1)
        rank = rank + jnp.where(blk < nblk - d, jnp.where(after > gate, 1.0, 0.0), 0.0)
    penalty = jnp.where(past, jnp.where(rank < n_sel, 0.0, MASKED), jnp.where(blk == own, 0.0, MASKED)).astype(BF16)

    first = lane < HEAD_DIM
    logits = []
    for h in range(N_HEADS):
        ps = slice(h // 2 * LANE, (h // 2 + 1) * LANE)
        mine = first if h % 2 == 0 else ~first
        qh = jnp.where(mine, q[:, ps] * HEAD_DIM ** -0.5, 0.0).astype(BF16)
        unpicked = _dot(penalty, spread_ref[h, :, :width])
        logits.append(_dot_t(qh, k_ref[0, :width, ps]) + _bias_tile(toep_ref, h, qi, rows // LANE, nchunk)
                      + unpicked)
    weights = [_exp_weights(x) for x in logits]
    for pair in range(N_HEADS // 2):
        ps = slice(pair * LANE, (pair + 1) * LANE)
        outs = [_dot(weights[2 * pair], va_ref[0, :width, ps]), _dot(weights[2 * pair + 1], vb_ref[0, :width, ps])]
        o_ref[0, :, ps] = _pair_output(*outs).astype(o_ref.dtype)


def _moba_kernel(q_ref, k_ref, va_ref, vb_ref, km_ref, toep_ref, spread_ref, o_ref, *, seq):
    qi = pl.program_id(1)
    body = functools.partial(_moba_body, qi=qi, q_ref=q_ref, k_ref=k_ref, va_ref=va_ref, vb_ref=vb_ref,
                             km_ref=km_ref, toep_ref=toep_ref, spread_ref=spread_ref, o_ref=o_ref,
                             nblk=seq // MOBA_BLOCK)
    _for_causal_width(qi, seq, body)


def _moba_spread(seq):
    nblk = seq // MOBA_BLOCK
    row = np.arange(LANE)[None, :, None]
    h = np.arange(N_HEADS)[:, None, None]
    blk_of_key = (np.arange(seq) // MOBA_BLOCK)[None, None, :]
    return jnp.asarray(row == h * nblk + blk_of_key, BF16)


def _moba(q, k, va, vb, k_mean, toep):
    b, seq, _ = q.shape
    nblk = seq // MOBA_BLOCK
    if nblk & (nblk - 1) or N_HEADS * nblk > LANE:
        raise ValueError("MoBA lane layout needs a power-of-two block count with heads*blocks <= 128")
    spread = _moba_spread(seq)
    qblock = pl.BlockSpec((1, ATT_ROWS, 256), lambda bi, qi: (bi, qi, 0))
    whole = pl.BlockSpec((1, seq, 256), lambda bi, qi: (bi, 0, 0))
    means = pl.BlockSpec((1, nblk, 256), lambda bi, qi: (bi, 0, 0))
    return pl.pallas_call(
        functools.partial(_moba_kernel, seq=seq),
        grid=(b, seq // ATT_ROWS),
        in_specs=[qblock, whole, whole, whole, means, _full(toep.shape), _full(spread.shape)],
        out_specs=qblock,
        out_shape=jax.ShapeDtypeStruct((b, seq, 256), BF16),
        compiler_params=_params("parallel", "parallel"),
        name="moba",
    )(q, k, va, vb, k_mean, toep, spread)


def _mla_body(width, qi, q_ref, k_ref, va_ref, vb_ref, causal_ref, o_ref):
    nchunk = width // LANE
    sub = ATT_ROWS // LANE
    causal = jnp.concatenate(
        [jnp.concatenate([causal_ref[0, jnp.clip(qi * sub + j - c + 1, 0, 2)] for c in range(nchunk)], axis=1)
         for j in range(sub)], axis=0)
    logits = []
    for h in range(N_HEADS):
        qs = slice(h * MLA_SLOT, (h + 1) * MLA_SLOT)
        logits.append(_dot_t(q_ref[0, :, qs], k_ref[0, :width, qs]) * MLA_QK ** -0.5 + causal)
    weights = [_exp_weights(x) for x in logits]
    for pair in range(N_HEADS // 2):
        ps = slice(pair * LANE, (pair + 1) * LANE)
        outs = [_dot(weights[2 * pair], va_ref[0, :width, ps]), _dot(weights[2 * pair + 1], vb_ref[0, :width, ps])]
        o_ref[0, :, ps] = _pair_output(*outs).astype(o_ref.dtype)


def _mla_kernel(q_ref, k_ref, va_ref, vb_ref, causal_ref, o_ref, *, seq):
    qi = pl.program_id(1)
    body = functools.partial(_mla_body, qi=qi, q_ref=q_ref, k_ref=k_ref, va_ref=va_ref, vb_ref=vb_ref,
                             causal_ref=causal_ref, o_ref=o_ref)
    _for_causal_width(qi, seq, body)


def _mla(q, k, va, vb):
    b, seq, _ = q.shape
    tri = np.where(np.arange(LANE)[:, None] >= np.arange(LANE)[None, :], 0.0, NEG_INF)
    causal = jnp.asarray(np.stack([np.full((LANE, LANE), NEG_INF), tri, np.zeros((LANE, LANE))])[None], F32)
    return pl.pallas_call(
        functools.partial(_mla_kernel, seq=seq),
        grid=(b, seq // ATT_ROWS),
        in_specs=[pl.BlockSpec((1, ATT_ROWS, N_HEADS * MLA_SLOT), lambda bi, qi: (bi, qi, 0)),
                  pl.BlockSpec((1, seq, N_HEADS * MLA_SLOT), lambda bi, qi: (bi, 0, 0)),
                  pl.BlockSpec((1, seq, 256), lambda bi, qi: (bi, 0, 0)),
                  pl.BlockSpec((1, seq, 256), lambda bi, qi: (bi, 0, 0)),
                  _full(causal.shape)],
        out_specs=pl.BlockSpec((1, ATT_ROWS, 256), lambda bi, qi: (bi, qi, 0)),
        out_shape=jax.ShapeDtypeStruct((b, seq, 256), BF16),
        compiler_params=_params("parallel", "parallel"),
        name="mla",
    )(q, k, va, vb, causal)


def _dil_group(dil, seq, toep_ref, g, qf_ref, kf_ref, vf_ref, m_ref, l_ref, acc_ref):
    n_band = seq // (dil * LANE)

    def unit(u, carry):
        r = lax.div(u, n_band)
        n = lax.rem(u, n_band)
        if dil == 1:
            cur = pl.ds(pl.multiple_of(n * LANE, LANE), LANE)
            prev = pl.ds(pl.multiple_of(jnp.maximum(n - 1, 0) * LANE, LANE), LANE)
        else:
            cur = pl.ds(n * (LANE * dil) + r, LANE, stride=dil)
            prev = pl.ds(jnp.maximum(n - 1, 0) * (LANE * dil) + r, LANE, stride=dil)
        prev_tile = jnp.where(n > 0, 2, 0)
        first = lax.broadcasted_iota(jnp.int32, (LANE, LANE), 1) < HEAD_DIM
        kk = [jnp.concatenate([kf_ref[pair, prev, :], kf_ref[pair, cur, :]], axis=0).astype(BF16)
              for pair in range(N_HEADS // 2)]
        vv = [jnp.concatenate([vf_ref[pair, prev, :], vf_ref[pair, cur, :]], axis=0).astype(BF16)
              for pair in range(N_HEADS // 2)]
        logits = []
        for h in range(N_HEADS):
            mine = first if h % 2 == 0 else ~first
            bias = jnp.concatenate([toep_ref[g, h, prev_tile], toep_ref[g, h, 1]], axis=1)
            logits.append(_dot_t(jnp.where(mine, qf_ref[h // 2, cur, :], 0.0).astype(BF16), kk[h // 2]) + bias)
        ms = [jnp.max(x, axis=1, keepdims=True) for x in logits]
        ps = [jnp.exp(x - m) for x, m in zip(logits, ms)]
        ls = [jnp.sum(p, axis=1, keepdims=True) for p in ps]
        pvs = [_dot(p.astype(BF16), vv[h // 2]) for h, p in enumerate(ps)]
        for pair in range(N_HEADS // 2):
            m_new, l_new, pv = (jnp.where(first, x[2 * pair], x[2 * pair + 1]) for x in (ms, ls, pvs))
            m_old = m_ref[pair, cur, :]
            m_tot = jnp.maximum(m_old, m_new)
            a_old = jnp.exp(m_old - m_tot)
            a_new = jnp.exp(m_new - m_tot)
            m_ref[pair, cur, :] = m_tot
            l_ref[pair, cur, :] = a_old * l_ref[pair, cur, :] + a_new * l_new
            acc_ref[pair, cur, :] = a_old * acc_ref[pair, cur, :] + a_new * pv
        return carry

    lax.fori_loop(0, dil * n_band, unit, 0, unroll=2)


def _get_rows(ref, rows):
    return jnp.concatenate([ref[0, rows, :], ref[1, rows, :]], axis=1)


def _set_rows(ref, rows, val):
    ref[0, rows, :] = val[:, :LANE]
    ref[1, rows, :] = val[:, LANE:]


def _dil_kernel(q_ref, k_ref, v_ref, toep_ref, o_ref, m_ref, l_ref, acc_ref, *, seq):
    g = pl.program_id(1)
    everything = pl.ds(0, seq)

    @pl.when(g == 0)
    def _():
        m_ref[...] = jnp.full(m_ref.shape, NEG_INF, F32)
        l_ref[...] = jnp.zeros_like(l_ref)
        acc_ref[...] = jnp.zeros_like(acc_ref)

    for gi, (_, dil) in enumerate(DIL_PATTERNS):
        pl.when(g == gi)(functools.partial(_dil_group, dil, seq, toep_ref, gi, q_ref, k_ref, v_ref,
                                           m_ref, l_ref, acc_ref))

    @pl.when(g == DIL_GROUPS - 1)
    def _():
        o_ref[0] = (_get_rows(acc_ref, everything) / _get_rows(l_ref, everything)).astype(o_ref.dtype)


def _dil(q, k, v, toep, seq):
    b = q.shape[1] // seq
    group = pl.BlockSpec((2, seq, LANE), lambda bi, g: (g, bi, 0))
    state = pltpu.VMEM((2, seq, LANE), F32)
    return pl.pallas_call(
        functools.partial(_dil_kernel, seq=seq),
        grid=(b, DIL_GROUPS),
        in_specs=[group, group, group, _full(toep.shape)],
        out_specs=pl.BlockSpec((1, seq, 256), lambda bi, g: (bi, 0, 0)),
        out_shape=jax.ShapeDtypeStruct((b, seq, 256), BF16),
        scratch_shapes=[state] * 3,
        compiler_params=_params("parallel", "arbitrary"),
        name="dil",
    )(q, k, v, toep)


def _merge_kernel(x_ref, xn_ref, oa_ref, ob_ref, oc_ref, od_ref, wg_ref, bg_ref, wb_ref, wo_ref, xo_ref):
    branches = (oa_ref, ob_ref, oc_ref, od_ref)
    half = ROW_TILE // 2
    for s in range(2):
        rows = slice(s * half, (s + 1) * half)
        xn = xn_ref[rows, :]
        total = None
        for n in range(N_BRANCH):
            cols = slice(n * D_MODEL, (n + 1) * D_MODEL)
            gate = _sigmoid(_dot(xn, wg_ref[:, cols]) + bg_ref[:, cols])
            term = gate * _dot(branches[n][rows, :], wb_ref[n])
            total = term if total is None else total + term
        xo_ref[rows, :] = x_ref[rows, :] + _dot(total.astype(BF16), wo_ref[...])


def _merge(x, xn, outs, wg, bg, wb, wo):
    n = x.shape[0]
    rows = lambda width: pl.BlockSpec((ROW_TILE, width), lambda i: (i, 0))
    return pl.pallas_call(
        _merge_kernel,
        grid=(n // ROW_TILE,),
        in_specs=[rows(D_MODEL), rows(D_MODEL)] + [rows(256)] * N_BRANCH
                 + [_full(wg.shape), _full(bg.shape), _full(wb.shape), _full(wo.shape)],
        out_specs=rows(D_MODEL),
        out_shape=jax.ShapeDtypeStruct((n, D_MODEL), F32),
        compiler_params=_params("parallel"),
        name="merge",
    )(x, xn, *outs, wg, bg, wb, wo)


def _bucket_of_distance():
    d = np.arange(REL_MAX_DIST + 1)
    exact = REL_BUCKETS // 2
    nf = np.maximum(d, 1).astype(np.float32)
    log_b = exact + (np.log(nf / exact) / math.log(REL_MAX_DIST / exact) * (REL_BUCKETS - exact)).astype(np.int32)
    return np.where(d < exact, d, np.minimum(log_b, REL_BUCKETS - 1))


def _toeplitz_tiles(table, n_off, dil, max_steps=None):
    span = 2 * LANE
    m = np.arange(span)
    off = np.arange(n_off)[:, None]
    steps = off * LANE + LANE - 1 - m[None, :]
    live = (steps >= 0) if max_steps is None else (steps >= 0) & (steps <= max_steps)
    v = table[:, _bucket_of_distance()[np.clip(steps * dil, 0, REL_MAX_DIST)]]
    v = jnp.where(live[None], v, NEG_INF)
    flat = jnp.tile(v, (1, 1, LANE))[..., :LANE * (span - 1)]
    tiles = flat.reshape(table.shape[0], n_off, LANE, span - 1)[..., LANE - 1:]
    return jnp.concatenate([jnp.full((table.shape[0], 1, LANE, LANE), NEG_INF, F32), tiles], axis=1)


def _same_head(width, head):
    idx = np.arange(width) // head
    return jnp.asarray(idx[:, None] == idx[None, :], BF16)


def _rope_tables(seq):
    half = MLA_ROPE // 2
    freqs = ROPE_THETA ** (-np.arange(half, dtype=np.float64) / half)
    ang = np.arange(seq, dtype=np.float64)[:, None] * freqs[None, :]
    pad = np.zeros((seq, MLA_SLOT - MLA_QK))
    cos_h = np.concatenate([np.ones((seq, MLA_NOPE)), np.cos(ang), np.cos(ang), pad], axis=1)
    sin_h = np.concatenate([np.zeros((seq, MLA_NOPE)), np.sin(ang), np.sin(ang), pad], axis=1)
    return (jnp.asarray(np.tile(cos_h, (1, N_HEADS)), F32), jnp.asarray(np.tile(sin_h, (1, N_HEADS)), F32))


def _rot_half_cols():
    half = MLA_ROPE // 2
    src = np.arange(MLA_SLOT)
    sign = np.zeros(MLA_SLOT, np.float32)
    src[MLA_NOPE:MLA_NOPE + half] = np.arange(MLA_NOPE + half, MLA_QK)
    sign[MLA_NOPE:MLA_NOPE + half] = -1.0
    src[MLA_NOPE + half:MLA_QK] = np.arange(MLA_NOPE, MLA_NOPE + half)
    sign[MLA_NOPE + half:MLA_QK] = 1.0
    return src, sign


def _mixer(x, xn, seq, w_in, b_gate, qk_a, qk_b, qk_c, qk_d, mla_nq, w_uq, mla_nkv, w_ukv,
           w_branch, w_out, toeps, consts):
    n = x.shape[0]
    b = n // seq
    e64, e_slots, tri, cos, sin = consts
    toep_a, toep_b, toep_d = toeps
    row = lambda a: a.reshape(1, -1)
    tile4 = lambda g: jnp.tile(g, N_HEADS).reshape(1, -1)
    r3 = lambda a: a.reshape(b, seq, a.shape[-1])

    off_iq = OFF_A + 6 * HEAD_DIM
    off_ik = off_iq + N_HEADS * IDX_DIM
    iq_rep = jnp.concatenate([jnp.tile(w_in[:, off_iq + h * IDX_DIM:off_iq + (h + 1) * IDX_DIM], (1, 4))
                              for h in range(N_HEADS)], axis=1)
    ik_rep = jnp.tile(w_in[:, off_ik:off_ik + IDX_DIM], (1, 4))
    iw_pad = jnp.pad(w_in[:, off_ik + IDX_DIM:OFF_A + W_A], ((0, 0), (0, LANE - N_HEADS)))
    half_pad = ((0, 0), (0, LANE - HEAD_DIM))
    q_slots = jnp.concatenate([jnp.pad(w_in[:, OFF_A + h * HEAD_DIM:OFF_A + (h + 1) * HEAD_DIM], half_pad)
                               for h in range(N_HEADS)], axis=1)
    off_k = OFF_A + N_HEADS * HEAD_DIM
    k_slot = jnp.pad(w_in[:, off_k:off_k + HEAD_DIM], half_pad)
    v_twice = jnp.tile(w_in[:, off_k + HEAD_DIM:off_iq], (1, 2))
    wa = jnp.concatenate([q_slots, k_slot, v_twice, iq_rep, ik_rep, iw_pad], axis=1).astype(BF16)
    gq_slots = jnp.tile(jnp.pad(qk_a[0], (0, LANE - HEAD_DIM)), N_HEADS).reshape(1, -1)
    gk_slot = jnp.pad(qk_a[1], (0, LANE - HEAD_DIM)).reshape(1, -1)
    piece_a = _proj_a(n, wa, e_slots, gq_slots, gk_slot)

    wb_in = w_in[:, OFF_B:OFF_B + W_B].astype(BF16)
    piece_b = _proj_b(n, wb_in, e64, tile4(qk_b[0]), tile4(qk_b[1]))

    src, sign = _rot_half_cols()
    src4 = np.concatenate([h * MLA_SLOT + src for h in range(N_HEADS)])
    sign4 = jnp.asarray(np.tile(sign, N_HEADS))
    slot_pad = MLA_SLOT - MLA_QK
    w_kr = w_in[:, OFF_C + MLA_Q_LORA + MLA_KV_LORA:OFF_C + W_C]
    kr_slots = jnp.pad(w_kr, ((0, 0), (MLA_NOPE, slot_pad)))
    kr_slots = jnp.tile(kr_slots, (1, N_HEADS))
    wc = jnp.concatenate([w_in[:, OFF_C:OFF_C + MLA_Q_LORA + MLA_KV_LORA], kr_slots,
                          kr_slots[:, src4] * sign4], axis=1).astype(BF16)
    slots = lambda w: jnp.pad(w.reshape(w.shape[0], N_HEADS, -1),
                              ((0, 0), (0, 0), (0, MLA_SLOT - w.shape[1] // N_HEADS))).reshape(w.shape[0], -1)
    wuq = slots(w_uq)
    ukv = w_ukv.reshape(MLA_KV_LORA, N_HEADS, MLA_NOPE + MLA_V)
    wuk = slots(ukv[:, :, :MLA_NOPE].reshape(MLA_KV_LORA, -1))
    wuv = ukv[:, :, MLA_NOPE:].reshape(MLA_KV_LORA, -1)
    gq = jnp.tile(jnp.pad(qk_c[0], (0, slot_pad)), N_HEADS)
    gk = jnp.tile(jnp.pad(qk_c[1], (0, slot_pad)), N_HEADS)
    piece_c = _proj_c(
        n, wc, e_slots, row(mla_nq), row(mla_nkv), wuq.astype(BF16), (wuq[:, src4] * sign4).astype(BF16),
        wuk.astype(BF16), wuv.astype(BF16), row(gq), row(gq[src4]), row(gk), row(gk[src4]), cos, sin, seq)

    wd = w_in[:, OFF_D:OFF_D + W_D].astype(BF16)
    piece_d = _proj_d(n, wd, e64, tile4(qk_d[0]), tile4(qk_d[1]))

    proj_a, proj_b, proj_c, proj_d = _project(xn, (piece_a, piece_b, piece_c, piece_d))
    out_a = _dsa(*map(r3, proj_a), toep_a, tri).reshape(n, 256)
    qb, kb, vb_first, vb_second, kmean = proj_b
    out_b = _moba(r3(qb), r3(kb), r3(vb_first), r3(vb_second),
                  kmean.reshape(b, seq // MOBA_BLOCK, BRANCH_WIDTH), toep_b).reshape(n, 256)
    out_c = _mla(*map(r3, proj_c)).reshape(n, 256)
    out_d = _dil(*proj_d, toep_d, seq).reshape(n, 256)

    wg = w_in[:, OFF_G:OFF_G + W_G].astype(BF16)
    return _merge(x, xn, (out_a, out_b, out_c, out_d), wg, b_gate.reshape(1, -1),
                  w_branch.astype(BF16), w_out.astype(BF16))


def kernel(x, norm_gain, w_in, b_gate, qk_gain_a, qk_gain_b, qk_gain_c, qk_gain_d, mla_norm_q,
           w_mla_uq, mla_norm_kv, w_mla_ukv, w_branch, w_out, rel_bias, w_ffn_in, w_ffn_out):
    b, seq, d = x.shape
    depth = norm_gain.shape[0]
    toep_a = _toeplitz_tiles(rel_bias[0:4], seq // LANE, 1)
    toep_b = _toeplitz_tiles(rel_bias[4:8], seq // LANE, 1)
    toep_d = jnp.stack([_toeplitz_tiles(rel_bias[8 + 4 * g:12 + 4 * g], 2, dil, max_steps=window // dil)
                        for g, (window, dil) in enumerate(DIL_PATTERNS)])
    tri = jnp.asarray(np.arange(LANE)[:, None] < np.arange(LANE)[None, :], BF16)
    consts = (_same_head(256, HEAD_DIM), _same_head(256, MLA_SLOT), tri) + _rope_tables(seq)

    x = x.reshape(b * seq, d)
    for l in range(depth):
        g = norm_gain[l]
        x, xn = _ffn(x, g[0:1], g[1:2], w_ffn_in[l, 0].astype(BF16), w_ffn_out[l, 0].astype(BF16))
        x = _mixer(x, xn, seq, w_in[l], b_gate[l], qk_gain_a[l], qk_gain_b[l], qk_gain_c[l],
                   qk_gain_d[l], mla_norm_q[l], w_mla_uq[l], mla_norm_kv[l], w_mla_ukv[l],
                   w_branch[l], w_out[l], (toep_a, toep_b, toep_d), consts)
        x, _ = _ffn(x, g[2:3], g[2:3], w_ffn_in[l, 1].astype(BF16), w_ffn_out[l, 1].astype(BF16))
    return x.reshape(b, seq, d)
```

```python
import functools
import math

import numpy as np
import jax
import jax.numpy as jnp
from jax import lax
from jax.experimental import pallas as pl
from jax.experimental.pallas import tpu as pltpu

D_MODEL = 1024
HEAD_DIM = 64
N_HEADS = 4
BRANCH_WIDTH = N_HEADS * HEAD_DIM
IDX_DIM = 32
DSA_TOPK = 256
MOBA_BLOCK = 256
MOBA_TOPK = 3
MLA_Q_LORA = 384
MLA_KV_LORA = 256
MLA_NOPE = 64
MLA_ROPE = 32
MLA_QK = MLA_NOPE + MLA_ROPE
MLA_V = 64
MLA_SLOT = 128
ROPE_THETA = 10000.0
DIL_PATTERNS = ((128, 1), (512, 4), (2048, 16))
DIL_GROUPS = 3
N_BRANCH = 4
D_FF = 2816
REL_BUCKETS = 32
REL_MAX_DIST = 2048
RMS_EPS = 1e-6

OFF_A = 0
W_A = 4 * HEAD_DIM + 2 * HEAD_DIM + 4 * IDX_DIM + IDX_DIM + 4
OFF_B = OFF_A + W_A
W_B = 3 * BRANCH_WIDTH
OFF_C = OFF_B + W_B
W_C = MLA_Q_LORA + MLA_KV_LORA + MLA_ROPE
OFF_D = OFF_C + W_C
W_D = 3 * DIL_GROUPS * BRANCH_WIDTH
OFF_G = OFF_D + W_D
W_G = N_BRANCH * D_MODEL

LANE = 128
ROW_TILE = 512
FF_CHUNK = 1408
ATT_ROWS = 256
WIDTH_CLASSES = 4
MASKED = -1e30
VMEM_LIMIT = 56 * 1024 * 1024

F32 = jnp.float32
BF16 = jnp.bfloat16
NEG_INF = float("-inf")
INT_MIN = -2 ** 31

_CONTRACT_LAST = (((1,), (1,)), ((), ()))


def _dot(a, b, precision=None):
    return jnp.dot(a, b, preferred_element_type=F32, precision=precision)


def _dot_t(a, b, precision=None):
    return lax.dot_general(a, b, _CONTRACT_LAST, preferred_element_type=F32, precision=precision)


def _rms(x, gain):
    return x * lax.rsqrt(jnp.mean(x * x, axis=-1, keepdims=True) + RMS_EPS) * gain


def _sigmoid(x):
    return 1.0 / (1.0 + jnp.exp(-x))


def _head_sumsq(y, e_ref):
    y2 = y * y
    hi = y2.astype(BF16)
    lo = (y2 - hi.astype(F32)).astype(BF16)
    e = e_ref[...]
    step = e.shape[0]
    blocks = [_dot(hi[:, c:c + step], e) + _dot(lo[:, c:c + step], e) for c in range(0, y.shape[1], step)]
    return blocks[0] if len(blocks) == 1 else jnp.concatenate(blocks, axis=1)


def _head_norm(y, e_ref, gain, width):
    return y * lax.rsqrt(_head_sumsq(y, e_ref) * (1.0 / width) + RMS_EPS) * gain


def _params(*sem):
    return pltpu.CompilerParams(dimension_semantics=sem, vmem_limit_bytes=VMEM_LIMIT)


def _full(shape):
    return pl.BlockSpec(shape, lambda *_: (0,) * len(shape))


def _resident(shape):
    return pl.BlockSpec(shape, lambda *_: (0,) * len(shape), pipeline_mode=pl.Buffered(1))


def _ffn_kernel(x_ref, g_ref, gn_ref, wi_ref, wo_ref, xo_ref, xn_ref):
    half = ROW_TILE // 2
    for s in range(2):
        rows = slice(s * half, (s + 1) * half)
        x = x_ref[rows, :]
        xb = _rms(x, g_ref[...]).astype(BF16)
        acc = None
        for j in range(D_FF // FF_CHUNK):
            gate = _dot(xb, wi_ref[:, j * FF_CHUNK:(j + 1) * FF_CHUNK])
            up = _dot(xb, wi_ref[:, D_FF + j * FF_CHUNK:D_FF + (j + 1) * FF_CHUNK])
            h = (gate * _sigmoid(gate) * up).astype(BF16)
            part = _dot(h, wo_ref[j * FF_CHUNK:(j + 1) * FF_CHUNK, :])
            acc = part if acc is None else acc + part
        xo = x + 0.5 * acc
        xo_ref[rows, :] = xo
        xn_ref[rows, :] = _rms(xo, gn_ref[...]).astype(BF16)


def _ffn(x, gain, gain_next, w_in, w_out):
    n = x.shape[0]
    rows = pl.BlockSpec((ROW_TILE, D_MODEL), lambda i: (i, 0))
    return pl.pallas_call(
        _ffn_kernel,
        grid=(n // ROW_TILE,),
        in_specs=[rows, _full((1, D_MODEL)), _full((1, D_MODEL)), _resident(w_in.shape), _resident(w_out.shape)],
        out_specs=[rows, rows],
        out_shape=[jax.ShapeDtypeStruct((n, D_MODEL), F32), jax.ShapeDtypeStruct((n, D_MODEL), BF16)],
        compiler_params=_params("parallel"),
        name="ffn",
    )(x, gain, gain_next, w_in, w_out)


def _split_terms(rep, hi_lanes):
    hi = rep.astype(BF16).astype(F32)
    return jnp.where(hi_lanes, hi, rep - hi).astype(BF16)


def _with_ones(v):
    first = (lax.broadcasted_iota(jnp.int32, (1, v.shape[1]), 1) & (LANE - 1)) < HEAD_DIM
    return jnp.where(first, v, 1.0).astype(BF16), jnp.where(first, 1.0, v).astype(BF16)


def _proj_a_kernel(xn_ref, w_ref, e_ref, gq_ref, gk_ref, q_ref, k_ref, va_ref, vb_ref, iq_ref, ik_ref, iw_ref):
    p = _dot(xn_ref[...], w_ref[...])
    q = _head_norm(p[:, :512], e_ref, gq_ref[...], HEAD_DIM) * HEAD_DIM ** -0.5
    q_ref[...] = q.astype(BF16)
    k = p[:, 512:640]
    k_ms = jnp.sum(k * k, axis=-1, keepdims=True) * (1.0 / HEAD_DIM)
    k_ref[...] = (k * lax.rsqrt(k_ms + RMS_EPS) * gk_ref[...]).astype(BF16)
    va_ref[...], vb_ref[...] = _with_ones(p[:, 640:768])
    lane = lax.broadcasted_iota(jnp.int32, (1, 4 * LANE), 1)
    copy = lax.shift_right_logical(lane, int(math.log2(IDX_DIM))) & 3
    iq_ref[...] = _split_terms(p[:, 768:1280], copy < 2)
    ik_ref[...] = _split_terms(p[:, 1280:1408], (copy[:, :LANE] & 1) == 0)
    iw_ref[...] = p[:, 1408:1536]


def _row_spec(width):
    return pl.BlockSpec((ROW_TILE, width), lambda i: (i, 0))


def _const_spec(a):
    return _resident(a.shape)


def _proj_a(n, w, e_slots, gq, gk):
    consts = (w, e_slots, gq, gk)
    widths = ((512, BF16), (128, BF16), (128, BF16), (128, BF16), (512, BF16), (128, BF16), (128, F32))
    return (_proj_a_kernel, consts, [_const_spec(c) for c in consts], [_row_spec(wd) for wd, _ in widths],
            [jax.ShapeDtypeStruct((n, wd), dt) for wd, dt in widths])


def _proj_b_kernel(xn_ref, w_ref, e_ref, gq_ref, gk_ref, q_ref, k_ref, va_ref, vb_ref, km_ref):
    p = _dot(xn_ref[...], w_ref[...])
    q_ref[...] = _head_norm(p[:, :256], e_ref, gq_ref[...], HEAD_DIM)
    k = _head_norm(p[:, 256:512], e_ref, gk_ref[...], HEAD_DIM)
    k_ref[...] = k.astype(BF16)
    va_ref[...], vb_ref[...] = _with_ones(p[:, 512:768])
    km = jnp.mean(k.reshape(ROW_TILE // MOBA_BLOCK, MOBA_BLOCK, BRANCH_WIDTH), axis=1)
    for j in range(ROW_TILE // MOBA_BLOCK):
        km_ref[j] = km[j:j + 1]


def _proj_b(n, w, e64, gq, gk):
    consts = (w, e64, gq, gk)
    widths = ((256, F32), (256, BF16), (256, BF16), (256, BF16))
    per_tile = ROW_TILE // MOBA_BLOCK
    return (_proj_b_kernel, consts, [_const_spec(c) for c in consts],
            [_row_spec(wd) for wd, _ in widths] + [pl.BlockSpec((per_tile, 1, BRANCH_WIDTH), lambda i: (i, 0, 0))],
            [jax.ShapeDtypeStruct((n, wd), dt) for wd, dt in widths]
            + [jax.ShapeDtypeStruct((n // MOBA_BLOCK, 1, BRANCH_WIDTH), F32)])


def _proj_c_kernel(xn_ref, w_ref, e_ref, nq_ref, nkv_ref, wuq_ref, wuqs_ref, wuk_ref, wuv_ref,
                   gq_ref, gqs_ref, gk_ref, gks_ref, cos_ref, sin_ref, q_ref, k_ref, va_ref, vb_ref):
    p = _dot(xn_ref[...], w_ref[...])
    cos = cos_ref[...]
    sin = sin_ref[...]
    xq = _rms(p[:, :384], nq_ref[...]).astype(BF16)
    qa = _dot(xq, wuq_ref[...])
    qs = _dot(xq, wuqs_ref[...])
    rq = lax.rsqrt(_head_sumsq(qa, e_ref) * (1.0 / MLA_QK) + RMS_EPS)
    q_ref[...] = (rq * (qa * gq_ref[...] * cos + qs * gqs_ref[...] * sin)).astype(BF16)
    xkv = _rms(p[:, 384:640], nkv_ref[...]).astype(BF16)
    ka = _dot(xkv, wuk_ref[...]) + p[:, 640:1152]
    ks = p[:, 1152:1664]
    rk = lax.rsqrt(_head_sumsq(ka, e_ref) * (1.0 / MLA_QK) + RMS_EPS)
    k_ref[...] = (rk * (ka * gk_ref[...] * cos + ks * gks_ref[...] * sin)).astype(BF16)
    va_ref[...], vb_ref[...] = _with_ones(_dot(xkv, wuv_ref[...]))


def _proj_c(n, w, e_slots, nq, nkv, wuq, wuqs, wuk, wuv, gq, gqs, gk, gks, cos, sin, seq):
    pos = pl.BlockSpec((ROW_TILE, N_HEADS * MLA_SLOT), lambda i: (i % (seq // ROW_TILE), 0))
    consts = (w, e_slots, nq, nkv, wuq, wuqs, wuk, wuv, gq, gqs, gk, gks)
    widths = ((N_HEADS * MLA_SLOT, BF16), (N_HEADS * MLA_SLOT, BF16), (256, BF16), (256, BF16))
    return (_proj_c_kernel, consts + (cos, sin), [_const_spec(c) for c in consts] + [pos, pos],
            [_row_spec(wd) for wd, _ in widths], [jax.ShapeDtypeStruct((n, wd), dt) for wd, dt in widths])


def _proj_d_kernel(xn_ref, w_ref, e_ref, gq_ref, gk_ref, q_ref, k_ref, v_ref):
    xn = xn_ref[...]

    def put(ref, c, val):
        ref[2 * c] = val[:, :LANE]
        ref[2 * c + 1] = val[:, LANE:]

    for c in range(DIL_GROUPS):
        lo, hi = c * 256, (c + 1) * 256
        put(q_ref, c, _head_norm(_dot(xn, w_ref[:, lo:hi]), e_ref, gq_ref[...], HEAD_DIM) * HEAD_DIM ** -0.5)
        put(k_ref, c, _head_norm(_dot(xn, w_ref[:, 768 + lo:768 + hi]), e_ref, gk_ref[...], HEAD_DIM))
        put(v_ref, c, _dot(xn, w_ref[:, 1536 + lo:1536 + hi]))


def _proj_d(n, w, e64, gq, gk):
    consts = (w, e64, gq, gk)
    tiles = 2 * DIL_GROUPS
    return (_proj_d_kernel, consts, [_const_spec(c) for c in consts],
            [pl.BlockSpec((tiles, ROW_TILE, LANE), lambda i: (0, i, 0))] * 3,
            [jax.ShapeDtypeStruct((tiles, n, LANE), F32)] * 3)


def _proj_fused_kernel(*refs, pieces):
    xn_ref, refs = refs[0], refs[1:]
    n_in = sum(k for _, k, _ in pieces)
    ins, outs = refs[:n_in], refs[n_in:]
    for body, k_in, k_out in pieces:
        body(xn_ref, *ins[:k_in], *outs[:k_out])
        ins, outs = ins[k_in:], outs[k_out:]


def _project(xn, pieces):
    n = xn.shape[0]
    layout = tuple((body, len(consts), len(out_specs)) for body, consts, _, out_specs, _ in pieces)
    flat = pl.pallas_call(
        functools.partial(_proj_fused_kernel, pieces=layout),
        grid=(n // ROW_TILE,),
        in_specs=[_row_spec(D_MODEL)] + [s for p in pieces for s in p[2]],
        out_specs=[s for p in pieces for s in p[3]],
        out_shape=[s for p in pieces for s in p[4]],
        compiler_params=_params("parallel"),
        name="proj",
    )(xn, *[c for p in pieces for c in p[1]])
    outs = []
    for _, _, k_out in layout:
        outs.append(flat[:k_out])
        flat = flat[k_out:]
    return outs


def _for_causal_width(qi, seq, body):
    n_qtiles = seq // ATT_ROWS
    n_cls = min(WIDTH_CLASSES, n_qtiles)
    per = n_qtiles // n_cls
    for c in range(n_cls):
        pl.when((qi >= c * per) & (qi < (c + 1) * per))(functools.partial(body, (c + 1) * per * ATT_ROWS))


def _bias_rows(toep_ref, h, qblk, nchunk):
    return jnp.concatenate([toep_ref[h, jnp.maximum(qblk - c + 1, 0)] for c in range(nchunk)], axis=1)


def _bias_tile(toep_ref, h, qi, sub_blocks, nchunk):
    return jnp.concatenate([_bias_rows(toep_ref, h, qi * sub_blocks + j, nchunk) for j in range(sub_blocks)],
                           axis=0)


def _exp_weights(logits):
    m = jnp.max(logits, axis=1, keepdims=True)
    return jnp.exp(logits - m).astype(BF16)


def _pair_output(first_head, second_head):
    first = lax.broadcasted_iota(jnp.int32, first_head.shape, 1) < HEAD_DIM
    numer = jnp.where(first, first_head, second_head)
    denom = jnp.where(first, pltpu.roll(first_head, HEAD_DIM, axis=1), pltpu.roll(second_head, HEAD_DIM, axis=1))
    return numer / denom


def _dsa_body(width, qi, q_ref, k_ref, va_ref, vb_ref, iq_ref, ik_ref, iw_ref, toep_ref, tri_ref, o_ref,
              sel_ref, n_sel):
    rows = ATT_ROWS
    nchunk = width // LANE
    ik = ik_ref[0, :width, :]
    iw = iw_ref[0][:, :N_HEADS] * (N_HEADS ** -0.5 * IDX_DIM ** -0.5)

    r = _dot_t(jnp.concatenate([iq_ref[0, :, h * LANE:(h + 1) * LANE] for h in range(N_HEADS)], axis=0), ik)
    score = None
    for h in range(N_HEADS):
        term = jnp.maximum(r[h * rows:(h + 1) * rows], 0.0) * iw[:, h:h + 1]
        score = term if score is None else score + term

    t = qi * rows + lax.broadcasted_iota(jnp.int32, (rows, width), 0)
    s = lax.broadcasted_iota(jnp.int32, (rows, width), 1)

    bits = pltpu.bitcast(score, jnp.int32)
    key = jnp.where(bits < 0, jnp.int32(INT_MIN) - bits, bits)
    key = jnp.where(s <= t, key, jnp.int32(INT_MIN))

    half_min = -2 ** 15
    upper = lax.shift_right_arithmetic(key, 16).astype(jnp.int16)
    lower = ((key & jnp.int32(0xFFFF)) + half_min).astype(jnp.int16)

    def lane_counts(hit):
        ones = jnp.where(hit, jnp.int16(1), jnp.int16(0))
        acc = ones[:, :LANE]
        for c in range(1, nchunk):
            acc = acc + ones[:, c * LANE:(c + 1) * LANE]
        return acc

    def total(acc):
        return jnp.sum(acc.astype(F32), axis=1, keepdims=True)

    def count(hit):
        return total(lane_counts(hit))

    def search(half, need):
        top, bottom = slice(0, rows // 2), slice(rows // 2, rows)
        need_rows = jnp.broadcast_to(need, (rows, 1)).astype(F32)

        def bit(step):
            return jnp.where(step < 16, lax.shift_left(jnp.int32(1), jnp.maximum(15 - step, 0)), 0)

        def hits(part, cand):
            return lane_counts(half[part] >= (cand + half_min).astype(jnp.int16))

        def body(i, carry):
            ans_top, acc_top, ans_bottom = carry
            cand_top = ans_top | bit(i)
            ans_top = jnp.where(total(acc_top) >= need_rows[top], cand_top, ans_top)
            cand_bottom = ans_bottom | bit(i)
            ans_bottom = jnp.where(total(hits(bottom, cand_bottom)) >= need_rows[bottom], cand_bottom, ans_bottom)
            return ans_top, hits(top, ans_top | bit(i + 1)), ans_bottom

        zero = jnp.zeros((rows // 2, 1), jnp.int32)
        ans_top, _, ans_bottom = lax.fori_loop(0, 16, body, (zero, hits(top, zero | bit(0)), zero))
        return jnp.concatenate([ans_top, ans_bottom], axis=0)

    def searched_threshold():
        thr_upper = search(upper, n_sel) + half_min
        thr_upper16 = thr_upper.astype(jnp.int16)
        n_over = count(upper > thr_upper16)
        candidates = jnp.where(upper == thr_upper16, lower, jnp.int16(half_min))
        return thr_upper * 65536 + search(candidates, n_sel - n_over)

    thr = lax.cond((qi + 1) * rows <= n_sel, lambda: jnp.full((rows, 1), INT_MIN, jnp.int32), searched_threshold)

    sel = jnp.where(key >= jnp.maximum(thr, jnp.int32(INT_MIN + 1)), 1.0, 0.0)
    sel_ref[:, :width] = sel

    overflow = jnp.sum(sel, axis=1, keepdims=True) > n_sel

    @pl.when(jnp.max(jnp.where(overflow, 1.0, 0.0)) > 0.0)
    def _():
        above = key > thr
        room = n_sel - jnp.sum(jnp.where(above, 1.0, 0.0), axis=1, keepdims=True)
        before = jnp.zeros((rows, 1), F32)
        for c in range(nchunk):
            cols = slice(c * LANE, (c + 1) * LANE)
            tc = jnp.where(key[:, cols] == thr, 1.0, 0.0)
            rank = before + _dot(tc.astype(BF16), tri_ref[...])
            keep = above[:, cols] | ((tc > 0.0) & (rank < room))
            sel_ref[:, cols] = jnp.where(keep, 1.0, 0.0)
            before = before + jnp.sum(tc, axis=1, keepdims=True)

    unselected = (sel_ref[:, :width] - 1.0) * -MASKED
    k = k_ref[0, :width, :]
    qk = _dot_t(jnp.concatenate([q_ref[0, :, h * LANE:(h + 1) * LANE] for h in range(N_HEADS)], axis=0), k)
    logits = [qk[h * rows:(h + 1) * rows] + _bias_tile(toep_ref, h, qi, rows // LANE, nchunk) + unselected
              for h in range(N_HEADS)]
    weights = [_exp_weights(x) for x in logits]
    for pair in range(N_HEADS // 2):
        outs = [_dot(weights[2 * pair], va_ref[0, :width, :]), _dot(weights[2 * pair + 1], vb_ref[0, :width, :])]
        o_ref[0, :, pair * LANE:(pair + 1) * LANE] = _pair_output(*outs).astype(o_ref.dtype)


def _dsa_kernel(*refs, seq, n_sel):
    qi = pl.program_id(1)
    _for_causal_width(qi, seq, functools.partial(_dsa_body, qi=qi, n_sel=n_sel, **_named(refs)))


def _named(refs):
    names = ("q_ref", "k_ref", "va_ref", "vb_ref", "iq_ref", "ik_ref", "iw_ref", "toep_ref", "tri_ref", "o_ref",
             "sel_ref")
    return dict(zip(names, refs, strict=True))


def _dsa(q, k, va, vb, iq, ik, iw, toep, tri):
    b, seq, _ = q.shape
    n_sel = min(DSA_TOPK, seq // 4)
    qblock = lambda width: pl.BlockSpec((1, ATT_ROWS, width), lambda bi, qi: (bi, qi, 0))
    whole = lambda width: pl.BlockSpec((1, seq, width), lambda bi, qi: (bi, 0, 0))
    return pl.pallas_call(
        functools.partial(_dsa_kernel, seq=seq, n_sel=n_sel),
        grid=(b, seq // ATT_ROWS),
        in_specs=[qblock(512), whole(128), whole(128), whole(128), qblock(512), whole(128), qblock(128),
                  _full(toep.shape), _full(tri.shape)],
        out_specs=qblock(256),
        out_shape=jax.ShapeDtypeStruct((b, seq, 256), BF16),
        scratch_shapes=[pltpu.VMEM((ATT_ROWS, seq), F32)],
        compiler_params=_params("parallel", "parallel"),
        name="dsa",
    )(q, k, va, vb, iq, ik, iw, toep, tri)


def _moba_body(width, qi, q_ref, k_ref, va_ref, vb_ref, km_ref, toep_ref, spread_ref, o_ref, nblk):
    nchunk = width // LANE
    n_sel = min(MOBA_TOPK, nblk - 1)
    rows = ATT_ROWS
    own = lax.shift_right_logical(qi * rows, int(math.log2(MOBA_BLOCK)))
    q = q_ref[0]

    lane = lax.broadcasted_iota(jnp.int32, (rows, LANE), 1)
    blk = lane & (nblk - 1)
    past = (blk < own) & (lane < N_HEADS * nblk)

    def gated_rank():
        km = km_ref[0]
        head_of_col = lax.shift_right_logical(lax.broadcasted_iota(jnp.int32, (nblk, BRANCH_WIDTH), 1),
                                              int(math.log2(HEAD_DIM)))
        per_head = [jnp.where(head_of_col == h, km, 0.0) for h in range(N_HEADS)]
        pad = jnp.zeros((LANE - N_HEADS * nblk, BRANCH_WIDTH), F32)
        gate = _dot_t(q, jnp.concatenate(per_head + [pad], axis=0), precision=lax.Precision.HIGHEST)
        gate = jnp.where(past, gate, NEG_INF)
        rank = jnp.zeros((rows, LANE), F32)
        for d in range(1, nblk):
            before = pltpu.roll(gate, d, axis=1)
            rank = rank + jnp.where(blk >= d, jnp.where(before >= gate, 1.0, 0.0), 0.0)
            after = pltpu.roll(gate, LANE - d, axis=1)
            rank = rank + jnp.where(blk < nblk - d, jnp.where(after > gate, 1.0, 0.0), 0.0)
        return rank

    rank = lax.cond(own <= n_sel, lambda: jnp.zeros((rows, LANE), F32), gated_rank)
    penalty = jnp.where(past, jnp.where(rank < n_sel, 0.0, MASKED), jnp.where(blk == own, 0.0, MASKED)).astype(BF16)

    first = lane < HEAD_DIM
    logits = []
    for h in range(N_HEADS):
        ps = slice(h // 2 * LANE, (h // 2 + 1) * LANE)
        mine = first if h % 2 == 0 else ~first
        qh = jnp.where(mine, q[:, ps] * HEAD_DIM ** -0.5, 0.0).astype(BF16)
        unpicked = _dot(penalty, spread_ref[h, :, :width])
        logits.append(_dot_t(qh, k_ref[0, :width, ps]) + _bias_tile(toep_ref, h, qi, rows // LANE, nchunk)
                      + unpicked)
    weights = [_exp_weights(x) for x in logits]
    for pair in range(N_HEADS // 2):
        ps = slice(pair * LANE, (pair + 1) * LANE)
        outs = [_dot(weights[2 * pair], va_ref[0, :width, ps]), _dot(weights[2 * pair + 1], vb_ref[0, :width, ps])]
        o_ref[0, :, ps] = _pair_output(*outs).astype(o_ref.dtype)


def _moba_kernel(q_ref, k_ref, va_ref, vb_ref, km_ref, toep_ref, spread_ref, o_ref, *, seq):
    qi = pl.program_id(1)
    body = functools.partial(_moba_body, qi=qi, q_ref=q_ref, k_ref=k_ref, va_ref=va_ref, vb_ref=vb_ref,
                             km_ref=km_ref, toep_ref=toep_ref, spread_ref=spread_ref, o_ref=o_ref,
                             nblk=seq // MOBA_BLOCK)
    _for_causal_width(qi, seq, body)


def _moba_spread(seq):
    nblk = seq // MOBA_BLOCK
    row = np.arange(LANE)[None, :, None]
    h = np.arange(N_HEADS)[:, None, None]
    blk_of_key = (np.arange(seq) // MOBA_BLOCK)[None, None, :]
    return jnp.asarray(row == h * nblk + blk_of_key, BF16)


def _moba(q, k, va, vb, k_mean, toep):
    b, seq, _ = q.shape
    nblk = seq // MOBA_BLOCK
    if nblk & (nblk - 1) or N_HEADS * nblk > LANE:
        raise ValueError("MoBA lane layout needs a power-of-two block count with heads*blocks <= 128")
    spread = _moba_spread(seq)
    qblock = pl.BlockSpec((1, ATT_ROWS, 256), lambda bi, qi: (bi, qi, 0))
    whole = pl.BlockSpec((1, seq, 256), lambda bi, qi: (bi, 0, 0))
    means = pl.BlockSpec((1, nblk, 256), lambda bi, qi: (bi, 0, 0))
    return pl.pallas_call(
        functools.partial(_moba_kernel, seq=seq),
        grid=(b, seq // ATT_ROWS),
        in_specs=[qblock, whole, whole, whole, means, _full(toep.shape), _full(spread.shape)],
        out_specs=qblock,
        out_shape=jax.ShapeDtypeStruct((b, seq, 256), BF16),
        compiler_params=_params("parallel", "parallel"),
        name="moba",
    )(q, k, va, vb, k_mean, toep, spread)


def _mla_body(width, qi, q_ref, k_ref, va_ref, vb_ref, causal_ref, o_ref):
    nchunk = width // LANE
    sub = ATT_ROWS // LANE
    causal = jnp.concatenate(
        [jnp.concatenate([causal_ref[0, jnp.clip(qi * sub + j - c + 1, 0, 2)] for c in range(nchunk)], axis=1)
         for j in range(sub)], axis=0)
    logits = []
    for h in range(N_HEADS):
        qs = slice(h * MLA_SLOT, (h + 1) * MLA_SLOT)
        logits.append(_dot_t(q_ref[0, :, qs], k_ref[0, :width, qs]) * MLA_QK ** -0.5 + causal)
    weights = [_exp_weights(x) for x in logits]
    for pair in range(N_HEADS // 2):
        ps = slice(pair * LANE, (pair + 1) * LANE)
        outs = [_dot(weights[2 * pair], va_ref[0, :width, ps]), _dot(weights[2 * pair + 1], vb_ref[0, :width, ps])]
        o_ref[0, :, ps] = _pair_output(*outs).astype(o_ref.dtype)


def _mla_kernel(q_ref, k_ref, va_ref, vb_ref, causal_ref, o_ref, *, seq):
    qi = pl.program_id(1)
    body = functools.partial(_mla_body, qi=qi, q_ref=q_ref, k_ref=k_ref, va_ref=va_ref, vb_ref=vb_ref,
                             causal_ref=causal_ref, o_ref=o_ref)
    _for_causal_width(qi, seq, body)


def _mla(q, k, va, vb):
    b, seq, _ = q.shape
    tri = np.where(np.arange(LANE)[:, None] >= np.arange(LANE)[None, :], 0.0, NEG_INF)
    causal = jnp.asarray(np.stack([np.full((LANE, LANE), NEG_INF), tri, np.zeros((LANE, LANE))])[None], F32)
    return pl.pallas_call(
        functools.partial(_mla_kernel, seq=seq),
        grid=(b, seq // ATT_ROWS),
        in_specs=[pl.BlockSpec((1, ATT_ROWS, N_HEADS * MLA_SLOT), lambda bi, qi: (bi, qi, 0)),
                  pl.BlockSpec((1, seq, N_HEADS * MLA_SLOT), lambda bi, qi: (bi, 0, 0)),
                  pl.BlockSpec((1, seq, 256), lambda bi, qi: (bi, 0, 0)),
                  pl.BlockSpec((1, seq, 256), lambda bi, qi: (bi, 0, 0)),
                  _full(causal.shape)],
        out_specs=pl.BlockSpec((1, ATT_ROWS, 256), lambda bi, qi: (bi, qi, 0)),
        out_shape=jax.ShapeDtypeStruct((b, seq, 256), BF16),
        compiler_params=_params("parallel", "parallel"),
        name="mla",
    )(q, k, va, vb, causal)


def _dil_group(dil, seq, toep_ref, g, qf_ref, kf_ref, vf_ref, m_ref, l_ref, acc_ref):
    n_band = seq // (dil * LANE)

    def unit(u, carry):
        r = lax.div(u, n_band)
        n = lax.rem(u, n_band)
        if dil == 1:
            cur = pl.ds(pl.multiple_of(n * LANE, LANE), LANE)
            prev = pl.ds(pl.multiple_of(jnp.maximum(n - 1, 0) * LANE, LANE), LANE)
        else:
            cur = pl.ds(n * (LANE * dil) + r, LANE, stride=dil)
            prev = pl.ds(jnp.maximum(n - 1, 0) * (LANE * dil) + r, LANE, stride=dil)
        prev_tile = jnp.where(n > 0, 2, 0)
        first = lax.broadcasted_iota(jnp.int32, (LANE, LANE), 1) < HEAD_DIM
        kk = [jnp.concatenate([kf_ref[pair, prev, :], kf_ref[pair, cur, :]], axis=0).astype(BF16)
              for pair in range(N_HEADS // 2)]
        vv = [jnp.concatenate([vf_ref[pair, prev, :], vf_ref[pair, cur, :]], axis=0).astype(BF16)
              for pair in range(N_HEADS // 2)]
        logits = []
        for h in range(N_HEADS):
            mine = first if h % 2 == 0 else ~first
            bias = jnp.concatenate([toep_ref[g, h, prev_tile], toep_ref[g, h, 1]], axis=1)
            logits.append(_dot_t(jnp.where(mine, qf_ref[h // 2, cur, :], 0.0).astype(BF16), kk[h // 2]) + bias)
        ms = [jnp.max(x, axis=1, keepdims=True) for x in logits]
        ps = [jnp.exp(x - m) for x, m in zip(logits, ms)]
        ls = [jnp.sum(p, axis=1, keepdims=True) for p in ps]
        pvs = [_dot(p.astype(BF16), vv[h // 2]) for h, p in enumerate(ps)]
        for pair in range(N_HEADS // 2):
            m_new, l_new, pv = (jnp.where(first, x[2 * pair], x[2 * pair + 1]) for x in (ms, ls, pvs))
            m_old = m_ref[pair, cur, :]
            m_tot = jnp.maximum(m_old, m_new)
            a_old = jnp.exp(m_old - m_tot)
            a_new = jnp.exp(m_new - m_tot)
            m_ref[pair, cur, :] = m_tot
            l_ref[pair, cur, :] = a_old * l_ref[pair, cur, :] + a_new * l_new
            acc_ref[pair, cur, :] = a_old * acc_ref[pair, cur, :] + a_new * pv
        return carry

    lax.fori_loop(0, dil * n_band, unit, 0, unroll=2)


def _get_rows(ref, rows):
    return jnp.concatenate([ref[0, rows, :], ref[1, rows, :]], axis=1)


def _set_rows(ref, rows, val):
    ref[0, rows, :] = val[:, :LANE]
    ref[1, rows, :] = val[:, LANE:]


def _dil_kernel(q_ref, k_ref, v_ref, toep_ref, o_ref, m_ref, l_ref, acc_ref, *, seq):
    g = pl.program_id(1)
    everything = pl.ds(0, seq)

    @pl.when(g == 0)
    def _():
        m_ref[...] = jnp.full(m_ref.shape, NEG_INF, F32)
        l_ref[...] = jnp.zeros_like(l_ref)
        acc_ref[...] = jnp.zeros_like(acc_ref)

    for gi, (_, dil) in enumerate(DIL_PATTERNS):
        pl.when(g == gi)(functools.partial(_dil_group, dil, seq, toep_ref, gi, q_ref, k_ref, v_ref,
                                           m_ref, l_ref, acc_ref))

    @pl.when(g == DIL_GROUPS - 1)
    def _():
        o_ref[0] = (_get_rows(acc_ref, everything) / _get_rows(l_ref, everything)).astype(o_ref.dtype)


def _dil(q, k, v, toep, seq):
    b = q.shape[1] // seq
    group = pl.BlockSpec((2, seq, LANE), lambda bi, g: (g, bi, 0))
    state = pltpu.VMEM((2, seq, LANE), F32)
    return pl.pallas_call(
        functools.partial(_dil_kernel, seq=seq),
        grid=(b, DIL_GROUPS),
        in_specs=[group, group, group, _full(toep.shape)],
        out_specs=pl.BlockSpec((1, seq, 256), lambda bi, g: (bi, 0, 0)),
        out_shape=jax.ShapeDtypeStruct((b, seq, 256), BF16),
        scratch_shapes=[state] * 3,
        compiler_params=_params("parallel", "arbitrary"),
        name="dil",
    )(q, k, v, toep)


def _merge_kernel(x_ref, xn_ref, oa_ref, ob_ref, oc_ref, od_ref, wg_ref, bg_ref, wb_ref, wo_ref, xo_ref):
    branches = (oa_ref, ob_ref, oc_ref, od_ref)
    half = ROW_TILE // 2
    for s in range(2):
        rows = slice(s * half, (s + 1) * half)
        xn = xn_ref[rows, :]
        total = None
        for n in range(N_BRANCH):
            cols = slice(n * D_MODEL, (n + 1) * D_MODEL)
            gate = _sigmoid(_dot(xn, wg_ref[:, cols]) + bg_ref[:, cols])
            term = gate * _dot(branches[n][rows, :], wb_ref[n])
            total = term if total is None else total + term
        xo_ref[rows, :] = x_ref[rows, :] + _dot(total.astype(BF16), wo_ref[...])


def _merge(x, xn, outs, wg, bg, wb, wo):
    n = x.shape[0]
    rows = lambda width: pl.BlockSpec((ROW_TILE, width), lambda i: (i, 0))
    return pl.pallas_call(
        _merge_kernel,
        grid=(n // ROW_TILE,),
        in_specs=[rows(D_MODEL), rows(D_MODEL)] + [rows(256)] * N_BRANCH
                 + [_full(wg.shape), _full(bg.shape), _full(wb.shape), _full(wo.shape)],
        out_specs=rows(D_MODEL),
        out_shape=jax.ShapeDtypeStruct((n, D_MODEL), F32),
        compiler_params=_params("parallel"),
        name="merge",
    )(x, xn, *outs, wg, bg, wb, wo)


def _bucket_of_distance():
    d = np.arange(REL_MAX_DIST + 1)
    exact = REL_BUCKETS // 2
    nf = np.maximum(d, 1).astype(np.float32)
    log_b = exact + (np.log(nf / exact) / math.log(REL_MAX_DIST / exact) * (REL_BUCKETS - exact)).astype(np.int32)
    return np.where(d < exact, d, np.minimum(log_b, REL_BUCKETS - 1))


def _toeplitz_tiles(table, n_off, dil, max_steps=None):
    span = 2 * LANE
    m = np.arange(span)
    off = np.arange(n_off)[:, None]
    steps = off * LANE + LANE - 1 - m[None, :]
    live = (steps >= 0) if max_steps is None else (steps >= 0) & (steps <= max_steps)
    v = table[:, _bucket_of_distance()[np.clip(steps * dil, 0, REL_MAX_DIST)]]
    v = jnp.where(live[None], v, NEG_INF)
    flat = jnp.tile(v, (1, 1, LANE))[..., :LANE * (span - 1)]
    tiles = flat.reshape(table.shape[0], n_off, LANE, span - 1)[..., LANE - 1:]
    return jnp.concatenate([jnp.full((table.shape[0], 1, LANE, LANE), NEG_INF, F32), tiles], axis=1)


def _same_head(width, head):
    idx = np.arange(width) // head
    return jnp.asarray(idx[:, None] == idx[None, :], BF16)


def _rope_tables(seq):
    half = MLA_ROPE // 2
    freqs = ROPE_THETA ** (-np.arange(half, dtype=np.float64) / half)
    ang = np.arange(seq, dtype=np.float64)[:, None] * freqs[None, :]
    pad = np.zeros((seq, MLA_SLOT - MLA_QK))
    cos_h = np.concatenate([np.ones((seq, MLA_NOPE)), np.cos(ang), np.cos(ang), pad], axis=1)
    sin_h = np.concatenate([np.zeros((seq, MLA_NOPE)), np.sin(ang), np.sin(ang), pad], axis=1)
    return (jnp.asarray(np.tile(cos_h, (1, N_HEADS)), F32), jnp.asarray(np.tile(sin_h, (1, N_HEADS)), F32))


def _rot_half_cols():
    half = MLA_ROPE // 2
    src = np.arange(MLA_SLOT)
    sign = np.zeros(MLA_SLOT, np.float32)
    src[MLA_NOPE:MLA_NOPE + half] = np.arange(MLA_NOPE + half, MLA_QK)
    sign[MLA_NOPE:MLA_NOPE + half] = -1.0
    src[MLA_NOPE + half:MLA_QK] = np.arange(MLA_NOPE, MLA_NOPE + half)
    sign[MLA_NOPE + half:MLA_QK] = 1.0
    return src, sign


def _mixer(x, xn, seq, w_in, b_gate, qk_a, qk_b, qk_c, qk_d, mla_nq, w_uq, mla_nkv, w_ukv,
           w_branch, w_out, toeps, consts):
    n = x.shape[0]
    b = n // seq
    e64, e_slots, tri, cos, sin = consts
    toep_a, toep_b, toep_d = toeps
    row = lambda a: a.reshape(1, -1)
    tile4 = lambda g: jnp.tile(g, N_HEADS).reshape(1, -1)
    r3 = lambda a: a.reshape(b, seq, a.shape[-1])

    off_iq = OFF_A + 6 * HEAD_DIM
    off_ik = off_iq + N_HEADS * IDX_DIM
    iq_rep = jnp.concatenate([jnp.tile(w_in[:, off_iq + h * IDX_DIM:off_iq + (h + 1) * IDX_DIM], (1, 4))
                              for h in range(N_HEADS)], axis=1)
    ik_rep = jnp.tile(w_in[:, off_ik:off_ik + IDX_DIM], (1, 4))
    iw_pad = jnp.pad(w_in[:, off_ik + IDX_DIM:OFF_A + W_A], ((0, 0), (0, LANE - N_HEADS)))
    half_pad = ((0, 0), (0, LANE - HEAD_DIM))
    q_slots = jnp.concatenate([jnp.pad(w_in[:, OFF_A + h * HEAD_DIM:OFF_A + (h + 1) * HEAD_DIM], half_pad)
                               for h in range(N_HEADS)], axis=1)
    off_k = OFF_A + N_HEADS * HEAD_DIM
    k_slot = jnp.pad(w_in[:, off_k:off_k + HEAD_DIM], half_pad)
    v_twice = jnp.tile(w_in[:, off_k + HEAD_DIM:off_iq], (1, 2))
    wa = jnp.concatenate([q_slots, k_slot, v_twice, iq_rep, ik_rep, iw_pad], axis=1).astype(BF16)
    gq_slots = jnp.tile(jnp.pad(qk_a[0], (0, LANE - HEAD_DIM)), N_HEADS).reshape(1, -1)
    gk_slot = jnp.pad(qk_a[1], (0, LANE - HEAD_DIM)).reshape(1, -1)
    piece_a = _proj_a(n, wa, e_slots, gq_slots, gk_slot)

    wb_in = w_in[:, OFF_B:OFF_B + W_B].astype(BF16)
    piece_b = _proj_b(n, wb_in, e64, tile4(qk_b[0]), tile4(qk_b[1]))

    src, sign = _rot_half_cols()
    src4 = np.concatenate([h * MLA_SLOT + src for h in range(N_HEADS)])
    sign4 = jnp.asarray(np.tile(sign, N_HEADS))
    slot_pad = MLA_SLOT - MLA_QK
    w_kr = w_in[:, OFF_C + MLA_Q_LORA + MLA_KV_LORA:OFF_C + W_C]
    kr_slots = jnp.pad(w_kr, ((0, 0), (MLA_NOPE, slot_pad)))
    kr_slots = jnp.tile(kr_slots, (1, N_HEADS))
    wc = jnp.concatenate([w_in[:, OFF_C:OFF_C + MLA_Q_LORA + MLA_KV_LORA], kr_slots,
                          kr_slots[:, src4] * sign4], axis=1).astype(BF16)
    slots = lambda w: jnp.pad(w.reshape(w.shape[0], N_HEADS, -1),
                              ((0, 0), (0, 0), (0, MLA_SLOT - w.shape[1] // N_HEADS))).reshape(w.shape[0], -1)
    wuq = slots(w_uq)
    ukv = w_ukv.reshape(MLA_KV_LORA, N_HEADS, MLA_NOPE + MLA_V)
    wuk = slots(ukv[:, :, :MLA_NOPE].reshape(MLA_KV_LORA, -1))
    wuv = ukv[:, :, MLA_NOPE:].reshape(MLA_KV_LORA, -1)
    gq = jnp.tile(jnp.pad(qk_c[0], (0, slot_pad)), N_HEADS)
    gk = jnp.tile(jnp.pad(qk_c[1], (0, slot_pad)), N_HEADS)
    piece_c = _proj_c(
        n, wc, e_slots, row(mla_nq), row(mla_nkv), wuq.astype(BF16), (wuq[:, src4] * sign4).astype(BF16),
        wuk.astype(BF16), wuv.astype(BF16), row(gq), row(gq[src4]), row(gk), row(gk[src4]), cos, sin, seq)

    wd = w_in[:, OFF_D:OFF_D + W_D].astype(BF16)
    piece_d = _proj_d(n, wd, e64, tile4(qk_d[0]), tile4(qk_d[1]))

    proj_a, proj_b, proj_c, proj_d = _project(xn, (piece_a, piece_b, piece_c, piece_d))
    out_a = _dsa(*map(r3, proj_a), toep_a, tri).reshape(n, 256)
    qb, kb, vb_first, vb_second, kmean = proj_b
    out_b = _moba(r3(qb), r3(kb), r3(vb_first), r3(vb_second),
                  kmean.reshape(b, seq // MOBA_BLOCK, BRANCH_WIDTH), toep_b).reshape(n, 256)
    out_c = _mla(*map(r3, proj_c)).reshape(n, 256)
    out_d = _dil(*proj_d, toep_d, seq).reshape(n, 256)

    wg = w_in[:, OFF_G:OFF_G + W_G].astype(BF16)
    return _merge(x, xn, (out_a, out_b, out_c, out_d), wg, b_gate.reshape(1, -1),
                  w_branch.astype(BF16), w_out.astype(BF16))


def kernel(x, norm_gain, w_in, b_gate, qk_gain_a, qk_gain_b, qk_gain_c, qk_gain_d, mla_norm_q,
           w_mla_uq, mla_norm_kv, w_mla_ukv, w_branch, w_out, rel_bias, w_ffn_in, w_ffn_out):
    b, seq, d = x.shape
    depth = norm_gain.shape[0]
    toep_a = _toeplitz_tiles(rel_bias[0:4], seq // LANE, 1)
    toep_b = _toeplitz_tiles(rel_bias[4:8], seq // LANE, 1)
    toep_d = jnp.stack([_toeplitz_tiles(rel_bias[8 + 4 * g:12 + 4 * g], 2, dil, max_steps=window // dil)
                        for g, (window, dil) in enumerate(DIL_PATTERNS)])
    tri = jnp.asarray(np.arange(LANE)[:, None] < np.arange(LANE)[None, :], BF16)
    consts = (_same_head(256, HEAD_DIM), _same_head(256, MLA_SLOT), tri) + _rope_tables(seq)

    x = x.reshape(b * seq, d)
    for l in range(depth):
        g = norm_gain[l]
        x, xn = _ffn(x, g[0:1], g[1:2], w_ffn_in[l, 0].astype(BF16), w_ffn_out[l, 0].astype(BF16))
        x = _mixer(x, xn, seq, w_in[l], b_gate[l], qk_gain_a[l], qk_gain_b[l], qk_gain_c[l],
                   qk_gain_d[l], mla_norm_q[l], w_mla_uq[l], mla_norm_kv[l], w_mla_ukv[l],
                   w_branch[l], w_out[l], (toep_a, toep_b, toep_d), consts)
        x, _ = _ffn(x, g[2:3], g[2:3], w_ffn_in[l, 1].astype(BF16), w_ffn_out[l, 1].astype(BF16))
    return x.reshape(b, seq, d)
```

```python
import functools
import math

import numpy as np
import jax
import jax.numpy as jnp
from jax import lax
from jax.experimental import pallas as pl
from jax.experimental.pallas import tpu as pltpu

D_MODEL = 1024
HEAD_DIM = 64
N_HEADS = 4
BRANCH_WIDTH = N_HEADS * HEAD_DIM
IDX_DIM = 32
DSA_TOPK = 256
MOBA_BLOCK = 256
MOBA_TOPK = 3
MLA_Q_LORA = 384
MLA_KV_LORA = 256
MLA_NOPE = 64
MLA_ROPE = 32
MLA_QK = MLA_NOPE + MLA_ROPE
MLA_V = 64
MLA_SLOT = 128
ROPE_THETA = 10000.0
DIL_PATTERNS = ((128, 1), (512, 4), (2048, 16))
DIL_GROUPS = 3
N_BRANCH = 4
D_FF = 2816
REL_BUCKETS = 32
REL_MAX_DIST = 2048
RMS_EPS = 1e-6

OFF_A = 0
W_A = 4 * HEAD_DIM + 2 * HEAD_DIM + 4 * IDX_DIM + IDX_DIM + 4
OFF_B = OFF_A + W_A
W_B = 3 * BRANCH_WIDTH
OFF_C = OFF_B + W_B
W_C = MLA_Q_LORA + MLA_KV_LORA + MLA_ROPE
OFF_D = OFF_C + W_C
W_D = 3 * DIL_GROUPS * BRANCH_WIDTH
OFF_G = OFF_D + W_D
W_G = N_BRANCH * D_MODEL

LANE = 128
ROW_TILE = 512
FF_CHUNK = 1408
ATT_ROWS = 256
WIDTH_CLASSES = 4
MASKED = -1e30
VMEM_LIMIT = 56 * 1024 * 1024

F32 = jnp.float32
BF16 = jnp.bfloat16
NEG_INF = float("-inf")
INT_MIN = -2 ** 31

_CONTRACT_LAST = (((1,), (1,)), ((), ()))


def _dot(a, b, precision=None):
    return jnp.dot(a, b, preferred_element_type=F32, precision=precision)


def _dot_t(a, b, precision=None):
    return lax.dot_general(a, b, _CONTRACT_LAST, preferred_element_type=F32, precision=precision)


def _rms(x, gain):
    return x * lax.rsqrt(jnp.mean(x * x, axis=-1, keepdims=True) + RMS_EPS) * gain


def _sigmoid(x):
    return 1.0 / (1.0 + jnp.exp(-x))


def _head_sumsq(y, e_ref):
    y2 = y * y
    hi = y2.astype(BF16)
    lo = (y2 - hi.astype(F32)).astype(BF16)
    e = e_ref[...]
    step = e.shape[0]
    blocks = [_dot(hi[:, c:c + step], e) + _dot(lo[:, c:c + step], e) for c in range(0, y.shape[1], step)]
    return blocks[0] if len(blocks) == 1 else jnp.concatenate(blocks, axis=1)


def _head_norm(y, e_ref, gain, width):
    return y * lax.rsqrt(_head_sumsq(y, e_ref) * (1.0 / width) + RMS_EPS) * gain


def _params(*sem):
    return pltpu.CompilerParams(dimension_semantics=sem, vmem_limit_bytes=VMEM_LIMIT)


def _full(shape):
    return pl.BlockSpec(shape, lambda *_: (0,) * len(shape))


def _resident(shape):
    return pl.BlockSpec(shape, lambda *_: (0,) * len(shape), pipeline_mode=pl.Buffered(1))


def _ffn_kernel(x_ref, g_ref, gn_ref, wi_ref, wo_ref, xo_ref, xn_ref):
    half = ROW_TILE // 2
    for s in range(2):
        rows = slice(s * half, (s + 1) * half)
        x = x_ref[rows, :]
        xb = _rms(x, g_ref[...]).astype(BF16)
        acc = None
        for j in range(D_FF // FF_CHUNK):
            gate = _dot(xb, wi_ref[:, j * FF_CHUNK:(j + 1) * FF_CHUNK])
            up = _dot(xb, wi_ref[:, D_FF + j * FF_CHUNK:D_FF + (j + 1) * FF_CHUNK])
            h = (gate * _sigmoid(gate) * up).astype(BF16)
            part = _dot(h, wo_ref[j * FF_CHUNK:(j + 1) * FF_CHUNK, :])
            acc = part if acc is None else acc + part
        xo = x + 0.5 * acc
        xo_ref[rows, :] = xo
        xn_ref[rows, :] = _rms(xo, gn_ref[...]).astype(BF16)


def _ffn(x, gain, gain_next, w_in, w_out):
    n = x.shape[0]
    rows = pl.BlockSpec((ROW_TILE, D_MODEL), lambda i: (i, 0))
    return pl.pallas_call(
        _ffn_kernel,
        grid=(n // ROW_TILE,),
        in_specs=[rows, _full((1, D_MODEL)), _full((1, D_MODEL)), _resident(w_in.shape), _resident(w_out.shape)],
        out_specs=[rows, rows],
        out_shape=[jax.ShapeDtypeStruct((n, D_MODEL), F32), jax.ShapeDtypeStruct((n, D_MODEL), BF16)],
        compiler_params=_params("parallel"),
        name="ffn",
    )(x, gain, gain_next, w_in, w_out)


def _split_terms(rep, hi_lanes):
    hi = rep.astype(BF16).astype(F32)
    return jnp.where(hi_lanes, hi, rep - hi).astype(BF16)


def _with_ones(v):
    first = (lax.broadcasted_iota(jnp.int32, (1, v.shape[1]), 1) & (LANE - 1)) < HEAD_DIM
    return jnp.where(first, v, 1.0).astype(BF16), jnp.where(first, 1.0, v).astype(BF16)


def _proj_a_kernel(xn_ref, w_ref, e_ref, gq_ref, gk_ref, q_ref, k_ref, va_ref, vb_ref, iq_ref, ik_ref, iw_ref):
    p = _dot(xn_ref[...], w_ref[...])
    q = _head_norm(p[:, :512], e_ref, gq_ref[...], HEAD_DIM) * HEAD_DIM ** -0.5
    q_ref[...] = q.astype(BF16)
    k = p[:, 512:640]
    k_ms = jnp.sum(k * k, axis=-1, keepdims=True) * (1.0 / HEAD_DIM)
    k_ref[...] = (k * lax.rsqrt(k_ms + RMS_EPS) * gk_ref[...]).astype(BF16)
    va_ref[...], vb_ref[...] = _with_ones(p[:, 640:768])
    lane = lax.broadcasted_iota(jnp.int32, (1, 4 * LANE), 1)
    copy = lax.shift_right_logical(lane, int(math.log2(IDX_DIM))) & 3
    iq_ref[...] = _split_terms(p[:, 768:1280], copy < 2)
    ik_ref[...] = _split_terms(p[:, 1280:1408], (copy[:, :LANE] & 1) == 0)
    iw_ref[...] = p[:, 1408:1536]


def _row_spec(width):
    return pl.BlockSpec((ROW_TILE, width), lambda i: (i, 0))


def _const_spec(a):
    return _resident(a.shape)


def _proj_a(n, w, e_slots, gq, gk):
    consts = (w, e_slots, gq, gk)
    widths = ((512, BF16), (128, BF16), (128, BF16), (128, BF16), (512, BF16), (128, BF16), (128, F32))
    return (_proj_a_kernel, consts, [_const_spec(c) for c in consts], [_row_spec(wd) for wd, _ in widths],
            [jax.ShapeDtypeStruct((n, wd), dt) for wd, dt in widths])


def _proj_b_kernel(xn_ref, w_ref, e_ref, gq_ref, gk_ref, q_ref, k_ref, va_ref, vb_ref, km_ref):
    p = _dot(xn_ref[...], w_ref[...])
    q_ref[...] = _head_norm(p[:, :256], e_ref, gq_ref[...], HEAD_DIM)
    k = _head_norm(p[:, 256:512], e_ref, gk_ref[...], HEAD_DIM)
    k_ref[...] = k.astype(BF16)
    va_ref[...], vb_ref[...] = _with_ones(p[:, 512:768])
    km = jnp.mean(k.reshape(ROW_TILE // MOBA_BLOCK, MOBA_BLOCK, BRANCH_WIDTH), axis=1)
    for j in range(ROW_TILE // MOBA_BLOCK):
        km_ref[j] = km[j:j + 1]


def _proj_b(n, w, e64, gq, gk):
    consts = (w, e64, gq, gk)
    widths = ((256, F32), (256, BF16), (256, BF16), (256, BF16))
    per_tile = ROW_TILE // MOBA_BLOCK
    return (_proj_b_kernel, consts, [_const_spec(c) for c in consts],
            [_row_spec(wd) for wd, _ in widths] + [pl.BlockSpec((per_tile, 1, BRANCH_WIDTH), lambda i: (i, 0, 0))],
            [jax.ShapeDtypeStruct((n, wd), dt) for wd, dt in widths]
            + [jax.ShapeDtypeStruct((n // MOBA_BLOCK, 1, BRANCH_WIDTH), F32)])


def _proj_c_kernel(xn_ref, w_ref, e_ref, nq_ref, nkv_ref, wuq_ref, wuqs_ref, wuk_ref, wuv_ref,
                   gq_ref, gqs_ref, gk_ref, gks_ref, cos_ref, sin_ref, q_ref, k_ref, va_ref, vb_ref):
    p = _dot(xn_ref[...], w_ref[...])
    cos = cos_ref[...]
    sin = sin_ref[...]
    xq = _rms(p[:, :384], nq_ref[...]).astype(BF16)
    qa = _dot(xq, wuq_ref[...])
    qs = _dot(xq, wuqs_ref[...])
    rq = lax.rsqrt(_head_sumsq(qa, e_ref) * (1.0 / MLA_QK) + RMS_EPS)
    q_ref[...] = (rq * (qa * gq_ref[...] * cos + qs * gqs_ref[...] * sin)).astype(BF16)
    xkv = _rms(p[:, 384:640], nkv_ref[...]).astype(BF16)
    ka = _dot(xkv, wuk_ref[...]) + p[:, 640:1152]
    ks = p[:, 1152:1664]
    rk = lax.rsqrt(_head_sumsq(ka, e_ref) * (1.0 / MLA_QK) + RMS_EPS)
    k_ref[...] = (rk * (ka * gk_ref[...] * cos + ks * gks_ref[...] * sin)).astype(BF16)
    va_ref[...], vb_ref[...] = _with_ones(_dot(xkv, wuv_ref[...]))


def _proj_c(n, w, e_slots, nq, nkv, wuq, wuqs, wuk, wuv, gq, gqs, gk, gks, cos, sin, seq):
    pos = pl.BlockSpec((ROW_TILE, N_HEADS * MLA_SLOT), lambda i: (i % (seq // ROW_TILE), 0))
    consts = (w, e_slots, nq, nkv, wuq, wuqs, wuk, wuv, gq, gqs, gk, gks)
    widths = ((N_HEADS * MLA_SLOT, BF16), (N_HEADS * MLA_SLOT, BF16), (256, BF16), (256, BF16))
    return (_proj_c_kernel, consts + (cos, sin), [_const_spec(c) for c in consts] + [pos, pos],
            [_row_spec(wd) for wd, _ in widths], [jax.ShapeDtypeStruct((n, wd), dt) for wd, dt in widths])


def _proj_d_kernel(xn_ref, w_ref, e_ref, gq_ref, gk_ref, q_ref, k_ref, v_ref):
    xn = xn_ref[...]

    def put(ref, c, val):
        ref[2 * c] = val[:, :LANE]
        ref[2 * c + 1] = val[:, LANE:]

    for c in range(DIL_GROUPS):
        lo, hi = c * 256, (c + 1) * 256
        put(q_ref, c, _head_norm(_dot(xn, w_ref[:, lo:hi]), e_ref, gq_ref[...], HEAD_DIM) * HEAD_DIM ** -0.5)
        put(k_ref, c, _head_norm(_dot(xn, w_ref[:, 768 + lo:768 + hi]), e_ref, gk_ref[...], HEAD_DIM))
        put(v_ref, c, _dot(xn, w_ref[:, 1536 + lo:1536 + hi]))


def _proj_d(n, w, e64, gq, gk):
    consts = (w, e64, gq, gk)
    tiles = 2 * DIL_GROUPS
    return (_proj_d_kernel, consts, [_const_spec(c) for c in consts],
            [pl.BlockSpec((tiles, ROW_TILE, LANE), lambda i: (0, i, 0))] * 3,
            [jax.ShapeDtypeStruct((tiles, n, LANE), F32)] * 3)


def _proj_fused_kernel(*refs, pieces):
    xn_ref, refs = refs[0], refs[1:]
    n_in = sum(k for _, k, _ in pieces)
    ins, outs = refs[:n_in], refs[n_in:]
    for body, k_in, k_out in pieces:
        body(xn_ref, *ins[:k_in], *outs[:k_out])
        ins, outs = ins[k_in:], outs[k_out:]


def _project(xn, pieces):
    n = xn.shape[0]
    layout = tuple((body, len(consts), len(out_specs)) for body, consts, _, out_specs, _ in pieces)
    flat = pl.pallas_call(
        functools.partial(_proj_fused_kernel, pieces=layout),
        grid=(n // ROW_TILE,),
        in_specs=[_row_spec(D_MODEL)] + [s for p in pieces for s in p[2]],
        out_specs=[s for p in pieces for s in p[3]],
        out_shape=[s for p in pieces for s in p[4]],
        compiler_params=_params("parallel"),
        name="proj",
    )(xn, *[c for p in pieces for c in p[1]])
    outs = []
    for _, _, k_out in layout:
        outs.append(flat[:k_out])
        flat = flat[k_out:]
    return outs


def _for_causal_width(qi, seq, body):
    n_qtiles = seq // ATT_ROWS
    n_cls = min(WIDTH_CLASSES, n_qtiles)
    per = n_qtiles // n_cls
    for c in range(n_cls):
        pl.when((qi >= c * per) & (qi < (c + 1) * per))(functools.partial(body, (c + 1) * per * ATT_ROWS))


def _bias_rows(toep_ref, h, qblk, nchunk):
    return jnp.concatenate([toep_ref[h, jnp.maximum(qblk - c + 1, 0)] for c in range(nchunk)], axis=1)


def _bias_tile(toep_ref, h, qi, sub_blocks, nchunk):
    return jnp.concatenate([_bias_rows(toep_ref, h, qi * sub_blocks + j, nchunk) for j in range(sub_blocks)],
                           axis=0)


def _exp_weights(logits):
    m = jnp.max(logits, axis=1, keepdims=True)
    return jnp.exp(logits - m).astype(BF16)


def _pair_output(first_head, second_head):
    first = lax.broadcasted_iota(jnp.int32, first_head.shape, 1) < HEAD_DIM
    numer = jnp.where(first, first_head, second_head)
    denom = jnp.where(first, pltpu.roll(first_head, HEAD_DIM, axis=1), pltpu.roll(second_head, HEAD_DIM, axis=1))
    return numer / denom


def _dsa_body(width, qi, q_ref, k_ref, va_ref, vb_ref, iq_ref, ik_ref, iw_ref, toep_ref, tri_ref, o_ref,
              sel_ref, n_sel):
    rows = ATT_ROWS
    nchunk = width // LANE
    ik = ik_ref[0, :width, :]
    iw = iw_ref[0][:, :N_HEADS] * (N_HEADS ** -0.5 * IDX_DIM ** -0.5)

    r = _dot_t(jnp.concatenate([iq_ref[0, :, h * LANE:(h + 1) * LANE] for h in range(N_HEADS)], axis=0), ik)
    score = None
    for h in range(N_HEADS):
        term = jnp.maximum(r[h * rows:(h + 1) * rows], 0.0) * iw[:, h:h + 1]
        score = term if score is None else score + term

    t = qi * rows + lax.broadcasted_iota(jnp.int32, (rows, width), 0)
    s = lax.broadcasted_iota(jnp.int32, (rows, width), 1)

    bits = pltpu.bitcast(score, jnp.int32)
    key = jnp.where(bits < 0, jnp.int32(INT_MIN) - bits, bits)
    key = jnp.where(s <= t, key, jnp.int32(INT_MIN))

    half_min = -2 ** 15
    upper = lax.shift_right_arithmetic(key, 16).astype(jnp.int16)
    lower = ((key & jnp.int32(0xFFFF)) + half_min).astype(jnp.int16)

    def lane_counts(hit):
        ones = jnp.where(hit, jnp.int16(1), jnp.int16(0))
        acc = ones[:, :LANE]
        for c in range(1, nchunk):
            acc = acc + ones[:, c * LANE:(c + 1) * LANE]
        return acc

    def total(acc):
        return jnp.sum(acc.astype(F32), axis=1, keepdims=True)

    def count(hit):
        return total(lane_counts(hit))

    def search(half, need):
        top, bottom = slice(0, rows // 2), slice(rows // 2, rows)
        need_rows = jnp.broadcast_to(need, (rows, 1)).astype(F32)

        def bit(step):
            return jnp.where(step < 16, lax.shift_left(jnp.int32(1), jnp.maximum(15 - step, 0)), 0)

        def hits(part, cand):
            return lane_counts(half[part] >= (cand + half_min).astype(jnp.int16))

        def body(i, carry):
            ans_top, acc_top, ans_bottom = carry
            cand_top = ans_top | bit(i)
            ans_top = jnp.where(total(acc_top) >= need_rows[top], cand_top, ans_top)
            cand_bottom = ans_bottom | bit(i)
            ans_bottom = jnp.where(total(hits(bottom, cand_bottom)) >= need_rows[bottom], cand_bottom, ans_bottom)
            return ans_top, hits(top, ans_top | bit(i + 1)), ans_bottom

        zero = jnp.zeros((rows // 2, 1), jnp.int32)
        ans_top, _, ans_bottom = lax.fori_loop(0, 16, body, (zero, hits(top, zero | bit(0)), zero))
        return jnp.concatenate([ans_top, ans_bottom], axis=0)

    def searched_threshold():
        thr_upper = search(upper, n_sel) + half_min
        thr_upper16 = thr_upper.astype(jnp.int16)
        n_over = count(upper > thr_upper16)
        candidates = jnp.where(upper == thr_upper16, lower, jnp.int16(half_min))
        return thr_upper * 65536 + search(candidates, n_sel - n_over)

    thr = lax.cond((qi + 1) * rows <= n_sel, lambda: jnp.full((rows, 1), INT_MIN, jnp.int32), searched_threshold)

    sel = jnp.where(key >= jnp.maximum(thr, jnp.int32(INT_MIN + 1)), 1.0, 0.0)
    sel_ref[:, :width] = sel

    overflow = jnp.sum(sel, axis=1, keepdims=True) > n_sel

    @pl.when(jnp.max(jnp.where(overflow, 1.0, 0.0)) > 0.0)
    def _():
        above = key > thr
        room = n_sel - jnp.sum(jnp.where(above, 1.0, 0.0), axis=1, keepdims=True)
        before = jnp.zeros((rows, 1), F32)
        for c in range(nchunk):
            cols = slice(c * LANE, (c + 1) * LANE)
            tc = jnp.where(key[:, cols] == thr, 1.0, 0.0)
            rank = before + _dot(tc.astype(BF16), tri_ref[...])
            keep = above[:, cols] | ((tc > 0.0) & (rank < room))
            sel_ref[:, cols] = jnp.where(keep, 1.0, 0.0)
            before = before + jnp.sum(tc, axis=1, keepdims=True)

    unselected = (sel_ref[:, :width] - 1.0) * -MASKED
    k = k_ref[0, :width, :]
    qk = _dot_t(jnp.concatenate([q_ref[0, :, h * LANE:(h + 1) * LANE] for h in range(N_HEADS)], axis=0), k)
    logits = [qk[h * rows:(h + 1) * rows] + _bias_tile(toep_ref, h, qi, rows // LANE, nchunk) + unselected
              for h in range(N_HEADS)]
    weights = [_exp_weights(x) for x in logits]
    for pair in range(N_HEADS // 2):
        outs = [_dot(weights[2 * pair], va_ref[0, :width, :]), _dot(weights[2 * pair + 1], vb_ref[0, :width, :])]
        o_ref[0, :, pair * LANE:(pair + 1) * LANE] = _pair_output(*outs).astype(o_ref.dtype)


def _dsa_kernel(*refs, seq, n_sel):
    qi = pl.program_id(1)
    _for_causal_width(qi, seq, functools.partial(_dsa_body, qi=qi, n_sel=n_sel, **_named(refs)))


def _named(refs):
    names = ("q_ref", "k_ref", "va_ref", "vb_ref", "iq_ref", "ik_ref", "iw_ref", "toep_ref", "tri_ref", "o_ref",
             "sel_ref")
    return dict(zip(names, refs, strict=True))


def _dsa(q, k, va, vb, iq, ik, iw, toep, tri):
    b, seq, _ = q.shape
    n_sel = min(DSA_TOPK, seq // 4)
    qblock = lambda width: pl.BlockSpec((1, ATT_ROWS, width), lambda bi, qi: (bi, qi, 0))
    whole = lambda width: pl.BlockSpec((1, seq, width), lambda bi, qi: (bi, 0, 0))
    return pl.pallas_call(
        functools.partial(_dsa_kernel, seq=seq, n_sel=n_sel),
        grid=(b, seq // ATT_ROWS),
        in_specs=[qblock(512), whole(128), whole(128), whole(128), qblock(512), whole(128), qblock(128),
                  _full(toep.shape), _full(tri.shape)],
        out_specs=qblock(256),
        out_shape=jax.ShapeDtypeStruct((b, seq, 256), BF16),
        scratch_shapes=[pltpu.VMEM((ATT_ROWS, seq), F32)],
        compiler_params=_params("parallel", "parallel"),
        name="dsa",
    )(q, k, va, vb, iq, ik, iw, toep, tri)


def _moba_body(width, qi, q_ref, k_ref, va_ref, vb_ref, km_ref, toep_ref, spread_ref, o_ref, nblk):
    nchunk = width // LANE
    n_sel = min(MOBA_TOPK, nblk - 1)
    rows = ATT_ROWS
    own = lax.shift_right_logical(qi * rows, int(math.log2(MOBA_BLOCK)))
    q = q_ref[0]

    lane = lax.broadcasted_iota(jnp.int32, (rows, LANE), 1)
    blk = lane & (nblk - 1)
    past = (blk < own) & (lane < N_HEADS * nblk)

    def gated_rank():
        km = km_ref[0]
        head_of_col = lax.shift_right_logical(lax.broadcasted_iota(jnp.int32, (nblk, BRANCH_WIDTH), 1),
                                              int(math.log2(HEAD_DIM)))
        per_head = [jnp.where(head_of_col == h, km, 0.0) for h in range(N_HEADS)]
        pad = jnp.zeros((LANE - N_HEADS * nblk, BRANCH_WIDTH), F32)
        gate = _dot_t(q, jnp.concatenate(per_head + [pad], axis=0), precision=lax.Precision.HIGHEST)
        gate = jnp.where(past, gate, NEG_INF)
        rank = jnp.zeros((rows, LANE), F32)
        for d in range(1, nblk):
            before = pltpu.roll(gate, d, axis=1)
            rank = rank + jnp.where(blk >= d, jnp.where(before >= gate, 1.0, 0.0), 0.0)
            after = pltpu.roll(gate, LANE - d, axis=1)
            rank = rank + jnp.where(blk < nblk - d, jnp.where(after > gate, 1.0, 0.0), 0.0)
        return rank

    rank = lax.cond(own <= n_sel, lambda: jnp.zeros((rows, LANE), F32), gated_rank)
    penalty = jnp.where(past, jnp.where(rank < n_sel, 0.0, MASKED), jnp.where(blk == own, 0.0, MASKED)).astype(BF16)

    first = lane < HEAD_DIM
    logits = []
    for h in range(N_HEADS):
        ps = slice(h // 2 * LANE, (h // 2 + 1) * LANE)
        mine = first if h % 2 == 0 else ~first
        qh = jnp.where(mine, q[:, ps] * HEAD_DIM ** -0.5, 0.0).astype(BF16)
        unpicked = _dot(penalty, spread_ref[h, :, :width])
        logits.append(_dot_t(qh, k_ref[0, :width, ps]) + _bias_tile(toep_ref, h, qi, rows // LANE, nchunk)
                      + unpicked)
    weights = [_exp_weights(x) for x in logits]
    for pair in range(N_HEADS // 2):
        ps = slice(pair * LANE, (pair + 1) * LANE)
        outs = [_dot(weights[2 * pair], va_ref[0, :width, ps]), _dot(weights[2 * pair + 1], vb_ref[0, :width, ps])]
        o_ref[0, :, ps] = _pair_output(*outs).astype(o_ref.dtype)


def _moba_kernel(q_ref, k_ref, va_ref, vb_ref, km_ref, toep_ref, spread_ref, o_ref, *, seq):
    qi = pl.program_id(1)
    body = functools.partial(_moba_body, qi=qi, q_ref=q_ref, k_ref=k_ref, va_ref=va_ref, vb_ref=vb_ref,
                             km_ref=km_ref, toep_ref=toep_ref, spread_ref=spread_ref, o_ref=o_ref,
                             nblk=seq // MOBA_BLOCK)
    _for_causal_width(qi, seq, body)


def _moba_spread(seq):
    nblk = seq // MOBA_BLOCK
    row = np.arange(LANE)[None, :, None]
    h = np.arange(N_HEADS)[:, None, None]
    blk_of_key = (np.arange(seq) // MOBA_BLOCK)[None, None, :]
    return jnp.asarray(row == h * nblk + blk_of_key, BF16)


def _moba(q, k, va, vb, k_mean, toep):
    b, seq, _ = q.shape
    nblk = seq // MOBA_BLOCK
    if nblk & (nblk - 1) or N_HEADS * nblk > LANE:
        raise ValueError("MoBA lane layout needs a power-of-two block count with heads*blocks <= 128")
    spread = _moba_spread(seq)
    qblock = pl.BlockSpec((1, ATT_ROWS, 256), lambda bi, qi: (bi, qi, 0))
    whole = pl.BlockSpec((1, seq, 256), lambda bi, qi: (bi, 0, 0))
    means = pl.BlockSpec((1, nblk, 256), lambda bi, qi: (bi, 0, 0))
    return pl.pallas_call(
        functools.partial(_moba_kernel, seq=seq),
        grid=(b, seq // ATT_ROWS),
        in_specs=[qblock, whole, whole, whole, means, _full(toep.shape), _full(spread.shape)],
        out_specs=qblock,
        out_shape=jax.ShapeDtypeStruct((b, seq, 256), BF16),
        compiler_params=_params("parallel", "parallel"),
        name="moba",
    )(q, k, va, vb, k_mean, toep, spread)


def _mla_body(width, qi, q_ref, k_ref, va_ref, vb_ref, causal_ref, o_ref):
    nchunk = width // LANE
    sub = ATT_ROWS // LANE
    causal = jnp.concatenate(
        [jnp.concatenate([causal_ref[0, jnp.clip(qi * sub + j - c + 1, 0, 2)] for c in range(nchunk)], axis=1)
         for j in range(sub)], axis=0)
    logits = []
    for h in range(N_HEADS):
        qs = slice(h * MLA_SLOT, (h + 1) * MLA_SLOT)
        logits.append(_dot_t(q_ref[0, :, qs], k_ref[0, :width, qs]) * MLA_QK ** -0.5 + causal)
    weights = [_exp_weights(x) for x in logits]
    for pair in range(N_HEADS // 2):
        ps = slice(pair * LANE, (pair + 1) * LANE)
        outs = [_dot(weights[2 * pair], va_ref[0, :width, ps]), _dot(weights[2 * pair + 1], vb_ref[0, :width, ps])]
        o_ref[0, :, ps] = _pair_output(*outs).astype(o_ref.dtype)


def _mla_kernel(q_ref, k_ref, va_ref, vb_ref, causal_ref, o_ref, *, seq):
    qi = pl.program_id(1)
    body = functools.partial(_mla_body, qi=qi, q_ref=q_ref, k_ref=k_ref, va_ref=va_ref, vb_ref=vb_ref,
                             causal_ref=causal_ref, o_ref=o_ref)
    _for_causal_width(qi, seq, body)


def _mla(q, k, va, vb):
    b, seq, _ = q.shape
    tri = np.where(np.arange(LANE)[:, None] >= np.arange(LANE)[None, :], 0.0, NEG_INF)
    causal = jnp.asarray(np.stack([np.full((LANE, LANE), NEG_INF), tri, np.zeros((LANE, LANE))])[None], F32)
    return pl.pallas_call(
        functools.partial(_mla_kernel, seq=seq),
        grid=(b, seq // ATT_ROWS),
        in_specs=[pl.BlockSpec((1, ATT_ROWS, N_HEADS * MLA_SLOT), lambda bi, qi: (bi, qi, 0)),
                  pl.BlockSpec((1, seq, N_HEADS * MLA_SLOT), lambda bi, qi: (bi, 0, 0)),
                  pl.BlockSpec((1, seq, 256), lambda bi, qi: (bi, 0, 0)),
                  pl.BlockSpec((1, seq, 256), lambda bi, qi: (bi, 0, 0)),
                  _full(causal.shape)],
        out_specs=pl.BlockSpec((1, ATT_ROWS, 256), lambda bi, qi: (bi, qi, 0)),
        out_shape=jax.ShapeDtypeStruct((b, seq, 256), BF16),
        compiler_params=_params("parallel", "parallel"),
        name="mla",
    )(q, k, va, vb, causal)


def _dil_group(dil, seq, toep_ref, g, q_ref, k_ref, v_ref, m_ref, l_ref, acc_ref):
    n_band = seq // (dil * LANE)

    def unit(u, carry):
        r = lax.div(u, n_band)
        n = lax.rem(u, n_band)
        if dil == 1:
            cur = pl.ds(pl.multiple_of(n * LANE, LANE), LANE)
            prev = pl.ds(pl.multiple_of(jnp.maximum(n - 1, 0) * LANE, LANE), LANE)
        else:
            cur = pl.ds(n * (LANE * dil) + r, LANE, stride=dil)
            prev = pl.ds(jnp.maximum(n - 1, 0) * (LANE * dil) + r, LANE, stride=dil)
        prev_tile = jnp.where(n > 0, 2, 0)
        first = lax.broadcasted_iota(jnp.int32, (LANE, LANE), 1) < HEAD_DIM
        kk = [jnp.concatenate([k_ref[pair, prev, :], k_ref[pair, cur, :]], axis=0).astype(BF16)
              for pair in range(N_HEADS // 2)]
        vv = [jnp.concatenate([v_ref[pair, prev, :], v_ref[pair, cur, :]], axis=0).astype(BF16)
              for pair in range(N_HEADS // 2)]
        logits = []
        for h in range(N_HEADS):
            mine = first if h % 2 == 0 else ~first
            bias = jnp.concatenate([toep_ref[g, h, prev_tile], toep_ref[g, h, 1]], axis=1)
            logits.append(_dot_t(jnp.where(mine, q_ref[h // 2, cur, :], 0.0).astype(BF16), kk[h // 2]) + bias)
        ms = [jnp.max(x, axis=1, keepdims=True) for x in logits]
        ps = [jnp.exp(x - m) for x, m in zip(logits, ms)]
        ls = [jnp.sum(p, axis=1, keepdims=True) for p in ps]
        pvs = [_dot(p.astype(BF16), vv[h // 2]) for h, p in enumerate(ps)]
        for pair in range(N_HEADS // 2):
            m_new, l_new, pv = (jnp.where(first, x[2 * pair], x[2 * pair + 1]) for x in (ms, ls, pvs))
            m_old = m_ref[pair, cur, :]
            m_tot = jnp.maximum(m_old, m_new)
            a_old = jnp.exp(m_old - m_tot)
            a_new = jnp.exp(m_new - m_tot)
            m_ref[pair, cur, :] = m_tot
            l_ref[pair, cur, :] = a_old * l_ref[pair, cur, :] + a_new * l_new
            acc_ref[pair, cur, :] = a_old * acc_ref[pair, cur, :] + a_new * pv
        return carry

    lax.fori_loop(0, dil * n_band, unit, 0, unroll=2)


def _get_rows(ref, rows):
    return jnp.concatenate([ref[0, rows, :], ref[1, rows, :]], axis=1)


def _dil_kernel(q_ref, k_ref, v_ref, toep_ref, o_ref, m_ref, l_ref, acc_ref, *, seq):
    g = pl.program_id(1)
    everything = pl.ds(0, seq)

    @pl.when(g == 0)
    def _():
        m_ref[...] = jnp.full(m_ref.shape, NEG_INF, F32)
        l_ref[...] = jnp.zeros_like(l_ref)
        acc_ref[...] = jnp.zeros_like(acc_ref)

    for gi, (_, dil) in enumerate(DIL_PATTERNS):
        pl.when(g == gi)(functools.partial(_dil_group, dil, seq, toep_ref, gi, q_ref, k_ref, v_ref,
                                           m_ref, l_ref, acc_ref))

    @pl.when(g == DIL_GROUPS - 1)
    def _():
        o_ref[0] = (_get_rows(acc_ref, everything) / _get_rows(l_ref, everything)).astype(o_ref.dtype)


def _dil(q, k, v, toep, seq):
    b = q.shape[1] // seq
    group = pl.BlockSpec((2, seq, LANE), lambda bi, g: (g, bi, 0))
    state = pltpu.VMEM((2, seq, LANE), F32)
    return pl.pallas_call(
        functools.partial(_dil_kernel, seq=seq),
        grid=(b, DIL_GROUPS),
        in_specs=[group, group, group, _full(toep.shape)],
        out_specs=pl.BlockSpec((1, seq, 256), lambda bi, g: (bi, 0, 0)),
        out_shape=jax.ShapeDtypeStruct((b, seq, 256), BF16),
        scratch_shapes=[state] * 3,
        compiler_params=_params("parallel", "arbitrary"),
        name="dil",
    )(q, k, v, toep)


def _merge_kernel(x_ref, xn_ref, oa_ref, ob_ref, oc_ref, od_ref, wg_ref, bg_ref, wb_ref, wo_ref, xo_ref):
    branches = (oa_ref, ob_ref, oc_ref, od_ref)
    half = ROW_TILE // 2
    for s in range(2):
        rows = slice(s * half, (s + 1) * half)
        xn = xn_ref[rows, :]
        total = None
        for n in range(N_BRANCH):
            cols = slice(n * D_MODEL, (n + 1) * D_MODEL)
            gate = _sigmoid(_dot(xn, wg_ref[:, cols]) + bg_ref[:, cols])
            term = gate * _dot(branches[n][rows, :], wb_ref[n])
            total = term if total is None else total + term
        xo_ref[rows, :] = x_ref[rows, :] + _dot(total.astype(BF16), wo_ref[...])


def _merge(x, xn, outs, wg, bg, wb, wo):
    n = x.shape[0]
    rows = lambda width: pl.BlockSpec((ROW_TILE, width), lambda i: (i, 0))
    return pl.pallas_call(
        _merge_kernel,
        grid=(n // ROW_TILE,),
        in_specs=[rows(D_MODEL), rows(D_MODEL)] + [rows(256)] * N_BRANCH
                 + [_full(wg.shape), _full(bg.shape), _full(wb.shape), _full(wo.shape)],
        out_specs=rows(D_MODEL),
        out_shape=jax.ShapeDtypeStruct((n, D_MODEL), F32),
        compiler_params=_params("parallel"),
        name="merge",
    )(x, xn, *outs, wg, bg, wb, wo)


def _bucket_of_distance():
    d = np.arange(REL_MAX_DIST + 1)
    exact = REL_BUCKETS // 2
    nf = np.maximum(d, 1).astype(np.float32)
    log_b = exact + (np.log(nf / exact) / math.log(REL_MAX_DIST / exact) * (REL_BUCKETS - exact)).astype(np.int32)
    return np.where(d < exact, d, np.minimum(log_b, REL_BUCKETS - 1))


def _toeplitz_tiles(table, n_off, dil, max_steps=None):
    span = 2 * LANE
    m = np.arange(span)
    off = np.arange(n_off)[:, None]
    steps = off * LANE + LANE - 1 - m[None, :]
    live = (steps >= 0) if max_steps is None else (steps >= 0) & (steps <= max_steps)
    v = table[:, _bucket_of_distance()[np.clip(steps * dil, 0, REL_MAX_DIST)]]
    v = jnp.where(live[None], v, NEG_INF)
    flat = jnp.tile(v, (1, 1, LANE))[..., :LANE * (span - 1)]
    tiles = flat.reshape(table.shape[0], n_off, LANE, span - 1)[..., LANE - 1:]
    return jnp.concatenate([jnp.full((table.shape[0], 1, LANE, LANE), NEG_INF, F32), tiles], axis=1)


def _same_head(width, head):
    idx = np.arange(width) // head
    return jnp.asarray(idx[:, None] == idx[None, :], BF16)


def _rope_tables(seq):
    half = MLA_ROPE // 2
    freqs = ROPE_THETA ** (-np.arange(half, dtype=np.float64) / half)
    ang = np.arange(seq, dtype=np.float64)[:, None] * freqs[None, :]
    pad = np.zeros((seq, MLA_SLOT - MLA_QK))
    cos_h = np.concatenate([np.ones((seq, MLA_NOPE)), np.cos(ang), np.cos(ang), pad], axis=1)
    sin_h = np.concatenate([np.zeros((seq, MLA_NOPE)), np.sin(ang), np.sin(ang), pad], axis=1)
    return (jnp.asarray(np.tile(cos_h, (1, N_HEADS)), F32), jnp.asarray(np.tile(sin_h, (1, N_HEADS)), F32))


def _rot_half_cols():
    half = MLA_ROPE // 2
    src = np.arange(MLA_SLOT)
    sign = np.zeros(MLA_SLOT, np.float32)
    src[MLA_NOPE:MLA_NOPE + half] = np.arange(MLA_NOPE + half, MLA_QK)
    sign[MLA_NOPE:MLA_NOPE + half] = -1.0
    src[MLA_NOPE + half:MLA_QK] = np.arange(MLA_NOPE, MLA_NOPE + half)
    sign[MLA_NOPE + half:MLA_QK] = 1.0
    return src, sign


def _mixer(x, xn, seq, w_in, b_gate, qk_a, qk_b, qk_c, qk_d, mla_nq, w_uq, mla_nkv, w_ukv,
           w_branch, w_out, toeps, consts):
    n = x.shape[0]
    b = n // seq
    e64, e_slots, tri, cos, sin = consts
    toep_a, toep_b, toep_d = toeps
    row = lambda a: a.reshape(1, -1)
    tile4 = lambda g: jnp.tile(g, N_HEADS).reshape(1, -1)
    r3 = lambda a: a.reshape(b, seq, a.shape[-1])

    off_iq = OFF_A + 6 * HEAD_DIM
    off_ik = off_iq + N_HEADS * IDX_DIM
    iq_rep = jnp.concatenate([jnp.tile(w_in[:, off_iq + h * IDX_DIM:off_iq + (h + 1) * IDX_DIM], (1, 4))
                              for h in range(N_HEADS)], axis=1)
    ik_rep = jnp.tile(w_in[:, off_ik:off_ik + IDX_DIM], (1, 4))
    iw_pad = jnp.pad(w_in[:, off_ik + IDX_DIM:OFF_A + W_A], ((0, 0), (0, LANE - N_HEADS)))
    half_pad = ((0, 0), (0, LANE - HEAD_DIM))
    q_slots = jnp.concatenate([jnp.pad(w_in[:, OFF_A + h * HEAD_DIM:OFF_A + (h + 1) * HEAD_DIM], half_pad)
                               for h in range(N_HEADS)], axis=1)
    off_k = OFF_A + N_HEADS * HEAD_DIM
    k_slot = jnp.pad(w_in[:, off_k:off_k + HEAD_DIM], half_pad)
    v_twice = jnp.tile(w_in[:, off_k + HEAD_DIM:off_iq], (1, 2))
    wa = jnp.concatenate([q_slots, k_slot, v_twice, iq_rep, ik_rep, iw_pad], axis=1).astype(BF16)
    gq_slots = jnp.tile(jnp.pad(qk_a[0], (0, LANE - HEAD_DIM)), N_HEADS).reshape(1, -1)
    gk_slot = jnp.pad(qk_a[1], (0, LANE - HEAD_DIM)).reshape(1, -1)
    piece_a = _proj_a(n, wa, e_slots, gq_slots, gk_slot)

    wb_in = w_in[:, OFF_B:OFF_B + W_B].astype(BF16)
    piece_b = _proj_b(n, wb_in, e64, tile4(qk_b[0]), tile4(qk_b[1]))

    src, sign = _rot_half_cols()
    src4 = np.concatenate([h * MLA_SLOT + src for h in range(N_HEADS)])
    sign4 = jnp.asarray(np.tile(sign, N_HEADS))
    slot_pad = MLA_SLOT - MLA_QK
    w_kr = w_in[:, OFF_C + MLA_Q_LORA + MLA_KV_LORA:OFF_C + W_C]
    kr_slots = jnp.pad(w_kr, ((0, 0), (MLA_NOPE, slot_pad)))
    kr_slots = jnp.tile(kr_slots, (1, N_HEADS))
    wc = jnp.concatenate([w_in[:, OFF_C:OFF_C + MLA_Q_LORA + MLA_KV_LORA], kr_slots,
                          kr_slots[:, src4] * sign4], axis=1).astype(BF16)
    slots = lambda w: jnp.pad(w.reshape(w.shape[0], N_HEADS, -1),
                              ((0, 0), (0, 0), (0, MLA_SLOT - w.shape[1] // N_HEADS))).reshape(w.shape[0], -1)
    wuq = slots(w_uq)
    ukv = w_ukv.reshape(MLA_KV_LORA, N_HEADS, MLA_NOPE + MLA_V)
    wuk = slots(ukv[:, :, :MLA_NOPE].reshape(MLA_KV_LORA, -1))
    wuv = ukv[:, :, MLA_NOPE:].reshape(MLA_KV_LORA, -1)
    gq = jnp.tile(jnp.pad(qk_c[0], (0, slot_pad)), N_HEADS)
    gk = jnp.tile(jnp.pad(qk_c[1], (0, slot_pad)), N_HEADS)
    piece_c = _proj_c(
        n, wc, e_slots, row(mla_nq), row(mla_nkv), wuq.astype(BF16), (wuq[:, src4] * sign4).astype(BF16),
        wuk.astype(BF16), wuv.astype(BF16), row(gq), row(gq[src4]), row(gk), row(gk[src4]), cos, sin, seq)

    wd = w_in[:, OFF_D:OFF_D + W_D].astype(BF16)
    piece_d = _proj_d(n, wd, e64, tile4(qk_d[0]), tile4(qk_d[1]))

    proj_a, proj_b, proj_c, proj_d = _project(xn, (piece_a, piece_b, piece_c, piece_d))
    out_a = _dsa(*map(r3, proj_a), toep_a, tri).reshape(n, 256)
    qb, kb, vb_first, vb_second, kmean = proj_b
    out_b = _moba(r3(qb), r3(kb), r3(vb_first), r3(vb_second),
                  kmean.reshape(b, seq // MOBA_BLOCK, BRANCH_WIDTH), toep_b).reshape(n, 256)
    out_c = _mla(*map(r3, proj_c)).reshape(n, 256)
    out_d = _dil(*proj_d, toep_d, seq).reshape(n, 256)

    wg = w_in[:, OFF_G:OFF_G + W_G].astype(BF16)
    return _merge(x, xn, (out_a, out_b, out_c, out_d), wg, b_gate.reshape(1, -1),
                  w_branch.astype(BF16), w_out.astype(BF16))


def kernel(x, norm_gain, w_in, b_gate, qk_gain_a, qk_gain_b, qk_gain_c, qk_gain_d, mla_norm_q,
           w_mla_uq, mla_norm_kv, w_mla_ukv, w_branch, w_out, rel_bias, w_ffn_in, w_ffn_out):
    b, seq, d = x.shape
    depth = norm_gain.shape[0]
    toep_a = _toeplitz_tiles(rel_bias[0:4], seq // LANE, 1)
    toep_b = _toeplitz_tiles(rel_bias[4:8], seq // LANE, 1)
    toep_d = jnp.stack([_toeplitz_tiles(rel_bias[8 + 4 * g:12 + 4 * g], 2, dil, max_steps=window // dil)
                        for g, (window, dil) in enumerate(DIL_PATTERNS)])
    tri = jnp.asarray(np.arange(LANE)[:, None] < np.arange(LANE)[None, :], BF16)
    consts = (_same_head(256, HEAD_DIM), _same_head(256, MLA_SLOT), tri) + _rope_tables(seq)

    x = x.reshape(b * seq, d)
    for l in range(depth):
        g = norm_gain[l]
        x, xn = _ffn(x, g[0:1], g[1:2], w_ffn_in[l, 0].astype(BF16), w_ffn_out[l, 0].astype(BF16))
        x = _mixer(x, xn, seq, w_in[l], b_gate[l], qk_gain_a[l], qk_gain_b[l], qk_gain_c[l],
                   qk_gain_d[l], mla_norm_q[l], w_mla_uq[l], mla_norm_kv[l], w_mla_ukv[l],
                   w_branch[l], w_out[l], (toep_a, toep_b, toep_d), consts)
        x, _ = _ffn(x, g[2:3], g[2:3], w_ffn_in[l, 1].astype(BF16), w_ffn_out[l, 1].astype(BF16))
    return x.reshape(b, seq, d)
```

```python
import functools
import math

import numpy as np
import jax
import jax.numpy as jnp
from jax import lax
from jax.experimental import pallas as pl
from jax.experimental.pallas import tpu as pltpu

D_MODEL = 1024
HEAD_DIM = 64
N_HEADS = 4
BRANCH_WIDTH = N_HEADS * HEAD_DIM
IDX_DIM = 32
DSA_TOPK = 256
MOBA_BLOCK = 256
MOBA_TOPK = 3
MLA_Q_LORA = 384
MLA_KV_LORA = 256
MLA_NOPE = 64
MLA_ROPE = 32
MLA_QK = MLA_NOPE + MLA_ROPE
MLA_V = 64
MLA_SLOT = 128
ROPE_THETA = 10000.0
DIL_PATTERNS = ((128, 1), (512, 4), (2048, 16))
DIL_GROUPS = 3
N_BRANCH = 4
D_FF = 2816
REL_BUCKETS = 32
REL_MAX_DIST = 2048
RMS_EPS = 1e-6

OFF_A = 0
W_A = 4 * HEAD_DIM + 2 * HEAD_DIM + 4 * IDX_DIM + IDX_DIM + 4
OFF_B = OFF_A + W_A
W_B = 3 * BRANCH_WIDTH
OFF_C = OFF_B + W_B
W_C = MLA_Q_LORA + MLA_KV_LORA + MLA_ROPE
OFF_D = OFF_C + W_C
W_D = 3 * DIL_GROUPS * BRANCH_WIDTH
OFF_G = OFF_D + W_D
W_G = N_BRANCH * D_MODEL

LANE = 128
ROW_TILE = 512
FF_CHUNK = 1408
ATT_ROWS = 256
WIDTH_CLASSES = 4
MASKED = -1e30
VMEM_LIMIT = 56 * 1024 * 1024

F32 = jnp.float32
BF16 = jnp.bfloat16
NEG_INF = float("-inf")
INT_MIN = -2 ** 31

_CONTRACT_LAST = (((1,), (1,)), ((), ()))


def _dot(a, b, precision=None):
    return jnp.dot(a, b, preferred_element_type=F32, precision=precision)


def _dot_t(a, b, precision=None):
    return lax.dot_general(a, b, _CONTRACT_LAST, preferred_element_type=F32, precision=precision)


def _rms(x, gain):
    return x * lax.rsqrt(jnp.mean(x * x, axis=-1, keepdims=True) + RMS_EPS) * gain


def _sigmoid(x):
    return 1.0 / (1.0 + jnp.exp(-x))


def _head_sumsq(y, e_ref):
    y2 = y * y
    hi = y2.astype(BF16)
    lo = (y2 - hi.astype(F32)).astype(BF16)
    e = e_ref[...]
    step = e.shape[0]
    blocks = [_dot(hi[:, c:c + step], e) + _dot(lo[:, c:c + step], e) for c in range(0, y.shape[1], step)]
    return blocks[0] if len(blocks) == 1 else jnp.concatenate(blocks, axis=1)


def _head_norm(y, e_ref, gain, width):
    return y * lax.rsqrt(_head_sumsq(y, e_ref) * (1.0 / width) + RMS_EPS) * gain


def _params(*sem):
    return pltpu.CompilerParams(dimension_semantics=sem, vmem_limit_bytes=VMEM_LIMIT)


def _full(shape):
    return pl.BlockSpec(shape, lambda *_: (0,) * len(shape))


def _resident(shape):
    return pl.BlockSpec(shape, lambda *_: (0,) * len(shape), pipeline_mode=pl.Buffered(1))


def _ffn_kernel(x_ref, g_ref, gn_ref, wi_ref, wo_ref, xo_ref, xn_ref):
    half = ROW_TILE // 2
    for s in range(2):
        rows = slice(s * half, (s + 1) * half)
        x = x_ref[rows, :]
        xb = _rms(x, g_ref[...]).astype(BF16)
        acc = None
        for j in range(D_FF // FF_CHUNK):
            gate = _dot(xb, wi_ref[:, j * FF_CHUNK:(j + 1) * FF_CHUNK])
            up = _dot(xb, wi_ref[:, D_FF + j * FF_CHUNK:D_FF + (j + 1) * FF_CHUNK])
            h = (gate * _sigmoid(gate) * up).astype(BF16)
            part = _dot(h, wo_ref[j * FF_CHUNK:(j + 1) * FF_CHUNK, :])
            acc = part if acc is None else acc + part
        xo = x + 0.5 * acc
        xo_ref[rows, :] = xo
        xn_ref[rows, :] = _rms(xo, gn_ref[...]).astype(BF16)


def _ffn(x, gain, gain_next, w_in, w_out):
    n = x.shape[0]
    rows = pl.BlockSpec((ROW_TILE, D_MODEL), lambda i: (i, 0))
    return pl.pallas_call(
        _ffn_kernel,
        grid=(n // ROW_TILE,),
        in_specs=[rows, _full((1, D_MODEL)), _full((1, D_MODEL)), _resident(w_in.shape), _resident(w_out.shape)],
        out_specs=[rows, rows],
        out_shape=[jax.ShapeDtypeStruct((n, D_MODEL), F32), jax.ShapeDtypeStruct((n, D_MODEL), BF16)],
        compiler_params=_params("parallel"),
        name="ffn",
    )(x, gain, gain_next, w_in, w_out)


def _split_terms(rep, hi_lanes):
    hi = rep.astype(BF16).astype(F32)
    return jnp.where(hi_lanes, hi, rep - hi).astype(BF16)


def _with_ones(v):
    first = (lax.broadcasted_iota(jnp.int32, (1, v.shape[1]), 1) & (LANE - 1)) < HEAD_DIM
    return jnp.where(first, v, 1.0).astype(BF16), jnp.where(first, 1.0, v).astype(BF16)


def _proj_a_kernel(xn_ref, w_ref, e_ref, gq_ref, gk_ref, q_ref, k_ref, va_ref, vb_ref, iq_ref, ik_ref, iw_ref):
    p = _dot(xn_ref[...], w_ref[...])
    q = _head_norm(p[:, :512], e_ref, gq_ref[...], HEAD_DIM) * HEAD_DIM ** -0.5
    q_ref[...] = q.astype(BF16)
    k = p[:, 512:640]
    k_ms = jnp.sum(k * k, axis=-1, keepdims=True) * (1.0 / HEAD_DIM)
    k_ref[...] = (k * lax.rsqrt(k_ms + RMS_EPS) * gk_ref[...]).astype(BF16)
    va_ref[...], vb_ref[...] = _with_ones(p[:, 640:768])
    lane = lax.broadcasted_iota(jnp.int32, (1, 4 * LANE), 1)
    copy = lax.shift_right_logical(lane, int(math.log2(IDX_DIM))) & 3
    iq_ref[...] = _split_terms(p[:, 768:1280], copy < 2)
    ik_ref[...] = _split_terms(p[:, 1280:1408], (copy[:, :LANE] & 1) == 0)
    iw_ref[...] = p[:, 1408:1536]


def _row_spec(width):
    return pl.BlockSpec((ROW_TILE, width), lambda i: (i, 0))


def _const_spec(a):
    return _resident(a.shape)


def _proj_a(n, w, e_slots, gq, gk):
    consts = (w, e_slots, gq, gk)
    widths = ((512, BF16), (128, BF16), (128, BF16), (128, BF16), (512, BF16), (128, BF16), (128, F32))
    return (_proj_a_kernel, consts, [_const_spec(c) for c in consts], [_row_spec(wd) for wd, _ in widths],
            [jax.ShapeDtypeStruct((n, wd), dt) for wd, dt in widths])


def _proj_b_kernel(xn_ref, w_ref, e_ref, gq_ref, gk_ref, q_ref, k_ref, va_ref, vb_ref, km_ref):
    p = _dot(xn_ref[...], w_ref[...])
    q_ref[...] = _head_norm(p[:, :256], e_ref, gq_ref[...], HEAD_DIM)
    k = _head_norm(p[:, 256:512], e_ref, gk_ref[...], HEAD_DIM)
    k_ref[...] = k.astype(BF16)
    va_ref[...], vb_ref[...] = _with_ones(p[:, 512:768])
    km = jnp.mean(k.reshape(ROW_TILE // MOBA_BLOCK, MOBA_BLOCK, BRANCH_WIDTH), axis=1)
    for j in range(ROW_TILE // MOBA_BLOCK):
        km_ref[j] = km[j:j + 1]


def _proj_b(n, w, e64, gq, gk):
    consts = (w, e64, gq, gk)
    widths = ((256, F32), (256, BF16), (256, BF16), (256, BF16))
    per_tile = ROW_TILE // MOBA_BLOCK
    return (_proj_b_kernel, consts, [_const_spec(c) for c in consts],
            [_row_spec(wd) for wd, _ in widths] + [pl.BlockSpec((per_tile, 1, BRANCH_WIDTH), lambda i: (i, 0, 0))],
            [jax.ShapeDtypeStruct((n, wd), dt) for wd, dt in widths]
            + [jax.ShapeDtypeStruct((n // MOBA_BLOCK, 1, BRANCH_WIDTH), F32)])


def _proj_c_kernel(xn_ref, w_ref, e_ref, nq_ref, nkv_ref, wuq_ref, wuqs_ref, wuk_ref, wuv_ref,
                   gq_ref, gqs_ref, gk_ref, gks_ref, cos_ref, sin_ref, q_ref, k_ref, va_ref, vb_ref):
    p = _dot(xn_ref[...], w_ref[...])
    cos = cos_ref[...]
    sin = sin_ref[...]
    xq = _rms(p[:, :384], nq_ref[...]).astype(BF16)
    qa = _dot(xq, wuq_ref[...])
    qs = _dot(xq, wuqs_ref[...])
    rq = lax.rsqrt(_head_sumsq(qa, e_ref) * (1.0 / MLA_QK) + RMS_EPS)
    q_ref[...] = (rq * (qa * gq_ref[...] * cos + qs * gqs_ref[...] * sin)).astype(BF16)
    xkv = _rms(p[:, 384:640], nkv_ref[...]).astype(BF16)
    ka = _dot(xkv, wuk_ref[...]) + p[:, 640:1152]
    ks = p[:, 1152:1664]
    rk = lax.rsqrt(_head_sumsq(ka, e_ref) * (1.0 / MLA_QK) + RMS_EPS)
    k_ref[...] = (rk * (ka * gk_ref[...] * cos + ks * gks_ref[...] * sin)).astype(BF16)
    va_ref[...], vb_ref[...] = _with_ones(_dot(xkv, wuv_ref[...]))


def _proj_c(n, w, e_slots, nq, nkv, wuq, wuqs, wuk, wuv, gq, gqs, gk, gks, cos, sin, seq):
    pos = pl.BlockSpec((ROW_TILE, N_HEADS * MLA_SLOT), lambda i: (i % (seq // ROW_TILE), 0))
    consts = (w, e_slots, nq, nkv, wuq, wuqs, wuk, wuv, gq, gqs, gk, gks)
    widths = ((N_HEADS * MLA_SLOT, BF16), (N_HEADS * MLA_SLOT, BF16), (256, BF16), (256, BF16))
    return (_proj_c_kernel, consts + (cos, sin), [_const_spec(c) for c in consts] + [pos, pos],
            [_row_spec(wd) for wd, _ in widths], [jax.ShapeDtypeStruct((n, wd), dt) for wd, dt in widths])


def _proj_d_kernel(xn_ref, w_ref, e_ref, gq_ref, gk_ref, q_ref, k_ref, v_ref):
    xn = xn_ref[...]

    def put(ref, c, val):
        ref[2 * c] = val[:, :LANE]
        ref[2 * c + 1] = val[:, LANE:]

    for c in range(DIL_GROUPS):
        lo, hi = c * 256, (c + 1) * 256
        put(q_ref, c, _head_norm(_dot(xn, w_ref[:, lo:hi]), e_ref, gq_ref[...], HEAD_DIM) * HEAD_DIM ** -0.5)
        put(k_ref, c, _head_norm(_dot(xn, w_ref[:, 768 + lo:768 + hi]), e_ref, gk_ref[...], HEAD_DIM))
        put(v_ref, c, _dot(xn, w_ref[:, 1536 + lo:1536 + hi]))


def _proj_d(n, w, e64, gq, gk):
    consts = (w, e64, gq, gk)
    tiles = 2 * DIL_GROUPS
    return (_proj_d_kernel, consts, [_const_spec(c) for c in consts],
            [pl.BlockSpec((tiles, ROW_TILE, LANE), lambda i: (0, i, 0))] * 3,
            [jax.ShapeDtypeStruct((tiles, n, LANE), F32)] * 3)


def _proj_fused_kernel(*refs, pieces):
    xn_ref, refs = refs[0], refs[1:]
    n_in = sum(k for _, k, _ in pieces)
    ins, outs = refs[:n_in], refs[n_in:]
    for body, k_in, k_out in pieces:
        body(xn_ref, *ins[:k_in], *outs[:k_out])
        ins, outs = ins[k_in:], outs[k_out:]


def _project(xn, pieces):
    n = xn.shape[0]
    layout = tuple((body, len(consts), len(out_specs)) for body, consts, _, out_specs, _ in pieces)
    flat = pl.pallas_call(
        functools.partial(_proj_fused_kernel, pieces=layout),
        grid=(n // ROW_TILE,),
        in_specs=[_row_spec(D_MODEL)] + [s for p in pieces for s in p[2]],
        out_specs=[s for p in pieces for s in p[3]],
        out_shape=[s for p in pieces for s in p[4]],
        compiler_params=_params("parallel"),
        name="proj",
    )(xn, *[c for p in pieces for c in p[1]])
    outs = []
    for _, _, k_out in layout:
        outs.append(flat[:k_out])
        flat = flat[k_out:]
    return outs


def _for_causal_width(qi, seq, body, classes=WIDTH_CLASSES):
    n_qtiles = seq // ATT_ROWS
    n_cls = min(classes, n_qtiles)
    per = n_qtiles // n_cls
    for c in range(n_cls):
        pl.when((qi >= c * per) & (qi < (c + 1) * per))(functools.partial(body, (c + 1) * per * ATT_ROWS))


def _bias_rows(toep_ref, h, qblk, nchunk):
    return jnp.concatenate([toep_ref[h, jnp.maximum(qblk - c + 1, 0)] for c in range(nchunk)], axis=1)


def _bias_tile(toep_ref, h, qi, sub_blocks, nchunk):
    return jnp.concatenate([_bias_rows(toep_ref, h, qi * sub_blocks + j, nchunk) for j in range(sub_blocks)],
                           axis=0)


def _exp_weights(logits):
    m = jnp.max(logits, axis=1, keepdims=True)
    return jnp.exp(logits - m).astype(BF16)


def _pair_output(first_head, second_head):
    first = lax.broadcasted_iota(jnp.int32, first_head.shape, 1) < HEAD_DIM
    numer = jnp.where(first, first_head, second_head)
    denom = jnp.where(first, pltpu.roll(first_head, HEAD_DIM, axis=1), pltpu.roll(second_head, HEAD_DIM, axis=1))
    return numer / denom


def _dsa_body(width, qi, q_ref, k_ref, va_ref, vb_ref, iq_ref, ik_ref, iw_ref, toep_ref, tri_ref, o_ref,
              sel_ref, n_sel):
    rows = ATT_ROWS
    nchunk = width // LANE
    ik = ik_ref[0, :width, :]
    iw = iw_ref[0][:, :N_HEADS] * (N_HEADS ** -0.5 * IDX_DIM ** -0.5)

    r = _dot_t(jnp.concatenate([iq_ref[0, :, h * LANE:(h + 1) * LANE] for h in range(N_HEADS)], axis=0), ik)
    score = None
    for h in range(N_HEADS):
        term = jnp.maximum(r[h * rows:(h + 1) * rows], 0.0) * iw[:, h:h + 1]
        score = term if score is None else score + term

    t = qi * rows + lax.broadcasted_iota(jnp.int32, (rows, width), 0)
    s = lax.broadcasted_iota(jnp.int32, (rows, width), 1)

    bits = pltpu.bitcast(score, jnp.int32)
    key = jnp.where(bits < 0, jnp.int32(INT_MIN) - bits, bits)
    key = jnp.where(s <= t, key, jnp.int32(INT_MIN))

    half_min = -2 ** 15
    upper = lax.shift_right_arithmetic(key, 16).astype(jnp.int16)
    lower = ((key & jnp.int32(0xFFFF)) + half_min).astype(jnp.int16)

    def lane_counts(hit):
        ones = jnp.where(hit, jnp.int16(1), jnp.int16(0))
        acc = ones[:, :LANE]
        for c in range(1, nchunk):
            acc = acc + ones[:, c * LANE:(c + 1) * LANE]
        return acc

    def total(acc):
        return jnp.sum(acc.astype(F32), axis=1, keepdims=True)

    def count(hit):
        return total(lane_counts(hit))

    def search(half, need):
        top, bottom = slice(0, rows // 2), slice(rows // 2, rows)
        need_rows = jnp.broadcast_to(need, (rows, 1)).astype(F32)

        def bit(step):
            return jnp.where(step < 16, lax.shift_left(jnp.int32(1), jnp.maximum(15 - step, 0)), 0)

        def hits(part, cand):
            return lane_counts(half[part] >= (cand + half_min).astype(jnp.int16))

        def body(i, carry):
            ans_top, acc_top, ans_bottom = carry
            cand_top = ans_top | bit(i)
            ans_top = jnp.where(total(acc_top) >= need_rows[top], cand_top, ans_top)
            cand_bottom = ans_bottom | bit(i)
            ans_bottom = jnp.where(total(hits(bottom, cand_bottom)) >= need_rows[bottom], cand_bottom, ans_bottom)
            return ans_top, hits(top, ans_top | bit(i + 1)), ans_bottom

        zero = jnp.zeros((rows // 2, 1), jnp.int32)
        ans_top, _, ans_bottom = lax.fori_loop(0, 16, body, (zero, hits(top, zero | bit(0)), zero))
        return jnp.concatenate([ans_top, ans_bottom], axis=0)

    def searched_threshold():
        thr_upper = search(upper, n_sel) + half_min
        thr_upper16 = thr_upper.astype(jnp.int16)
        n_over = count(upper > thr_upper16)
        candidates = jnp.where(upper == thr_upper16, lower, jnp.int16(half_min))
        return thr_upper * 65536 + search(candidates, n_sel - n_over)

    thr = lax.cond((qi + 1) * rows <= n_sel, lambda: jnp.full((rows, 1), INT_MIN, jnp.int32), searched_threshold)

    sel = jnp.where(key >= jnp.maximum(thr, jnp.int32(INT_MIN + 1)), 1.0, 0.0)
    sel_ref[:, :width] = sel

    overflow = jnp.sum(sel, axis=1, keepdims=True) > n_sel

    @pl.when(jnp.max(jnp.where(overflow, 1.0, 0.0)) > 0.0)
    def _():
        above = key > thr
        room = n_sel - jnp.sum(jnp.where(above, 1.0, 0.0), axis=1, keepdims=True)
        before = jnp.zeros((rows, 1), F32)
        for c in range(nchunk):
            cols = slice(c * LANE, (c + 1) * LANE)
            tc = jnp.where(key[:, cols] == thr, 1.0, 0.0)
            rank = before + _dot(tc.astype(BF16), tri_ref[...])
            keep = above[:, cols] | ((tc > 0.0) & (rank < room))
            sel_ref[:, cols] = jnp.where(keep, 1.0, 0.0)
            before = before + jnp.sum(tc, axis=1, keepdims=True)

    unselected = (sel_ref[:, :width] - 1.0) * -MASKED
    k = k_ref[0, :width, :]
    qk = _dot_t(jnp.concatenate([q_ref[0, :, h * LANE:(h + 1) * LANE] for h in range(N_HEADS)], axis=0), k)
    logits = [qk[h * rows:(h + 1) * rows] + _bias_tile(toep_ref, h, qi, rows // LANE, nchunk) + unselected
              for h in range(N_HEADS)]
    weights = [_exp_weights(x) for x in logits]
    for pair in range(N_HEADS // 2):
        outs = [_dot(weights[2 * pair], va_ref[0, :width, :]), _dot(weights[2 * pair + 1], vb_ref[0, :width, :])]
        o_ref[0, :, pair * LANE:(pair + 1) * LANE] = _pair_output(*outs).astype(o_ref.dtype)


def _dsa_kernel(*refs, seq, n_sel):
    qi = pl.program_id(1)
    _for_causal_width(qi, seq, functools.partial(_dsa_body, qi=qi, n_sel=n_sel, **_named(refs)))


def _named(refs):
    names = ("q_ref", "k_ref", "va_ref", "vb_ref", "iq_ref", "ik_ref", "iw_ref", "toep_ref", "tri_ref", "o_ref",
             "sel_ref")
    return dict(zip(names, refs, strict=True))


def _dsa(q, k, va, vb, iq, ik, iw, toep, tri):
    b, seq, _ = q.shape
    n_sel = min(DSA_TOPK, seq // 4)
    qblock = lambda width: pl.BlockSpec((1, ATT_ROWS, width), lambda bi, qi: (bi, qi, 0))
    whole = lambda width: pl.BlockSpec((1, seq, width), lambda bi, qi: (bi, 0, 0))
    return pl.pallas_call(
        functools.partial(_dsa_kernel, seq=seq, n_sel=n_sel),
        grid=(b, seq // ATT_ROWS),
        in_specs=[qblock(512), whole(128), whole(128), whole(128), qblock(512), whole(128), qblock(128),
                  _full(toep.shape), _full(tri.shape)],
        out_specs=qblock(256),
        out_shape=jax.ShapeDtypeStruct((b, seq, 256), BF16),
        scratch_shapes=[pltpu.VMEM((ATT_ROWS, seq), F32)],
        compiler_params=_params("parallel", "parallel"),
        name="dsa",
    )(q, k, va, vb, iq, ik, iw, toep, tri)


def _moba_body(width, qi, q_ref, k_ref, va_ref, vb_ref, km_ref, toep_ref, spread_ref, o_ref, nblk):
    nchunk = width // LANE
    n_sel = min(MOBA_TOPK, nblk - 1)
    rows = ATT_ROWS
    own = lax.shift_right_logical(qi * rows, int(math.log2(MOBA_BLOCK)))
    q = q_ref[0]

    lane = lax.broadcasted_iota(jnp.int32, (rows, LANE), 1)
    blk = lane & (nblk - 1)
    past = (blk < own) & (lane < N_HEADS * nblk)

    def gated_rank():
        km = km_ref[0]
        head_of_col = lax.shift_right_logical(lax.broadcasted_iota(jnp.int32, (nblk, BRANCH_WIDTH), 1),
                                              int(math.log2(HEAD_DIM)))
        per_head = [jnp.where(head_of_col == h, km, 0.0) for h in range(N_HEADS)]
        pad = jnp.zeros((LANE - N_HEADS * nblk, BRANCH_WIDTH), F32)
        gate = _dot_t(q, jnp.concatenate(per_head + [pad], axis=0), precision=lax.Precision.HIGHEST)
        gate = jnp.where(past, gate, NEG_INF)
        rank = jnp.zeros((rows, LANE), F32)
        for d in range(1, nblk):
            before = pltpu.roll(gate, d, axis=1)
            rank = rank + jnp.where(blk >= d, jnp.where(before >= gate, 1.0, 0.0), 0.0)
            after = pltpu.roll(gate, LANE - d, axis=1)
            rank = rank + jnp.where(blk < nblk - d, jnp.where(after > gate, 1.0, 0.0), 0.0)
        return rank

    rank = lax.cond(own <= n_sel, lambda: jnp.zeros((rows, LANE), F32), gated_rank)
    penalty = jnp.where(past, jnp.where(rank < n_sel, 0.0, MASKED), jnp.where(blk == own, 0.0, MASKED)).astype(BF16)

    first = lane < HEAD_DIM
    logits = []
    for h in range(N_HEADS):
        ps = slice(h // 2 * LANE, (h // 2 + 1) * LANE)
        mine = first if h % 2 == 0 else ~first
        qh = jnp.where(mine, q[:, ps] * HEAD_DIM ** -0.5, 0.0).astype(BF16)
        unpicked = _dot(penalty, spread_ref[h, :, :width])
        logits.append(_dot_t(qh, k_ref[0, :width, ps]) + _bias_tile(toep_ref, h, qi, rows // LANE, nchunk)
                      + unpicked)
    weights = [_exp_weights(x) for x in logits]
    for pair in range(N_HEADS // 2):
        ps = slice(pair * LANE, (pair + 1) * LANE)
        outs = [_dot(weights[2 * pair], va_ref[0, :width, ps]), _dot(weights[2 * pair + 1], vb_ref[0, :width, ps])]
        o_ref[0, :, ps] = _pair_output(*outs).astype(o_ref.dtype)


def _moba_kernel(q_ref, k_ref, va_ref, vb_ref, km_ref, toep_ref, spread_ref, o_ref, *, seq):
    qi = pl.program_id(1)
    body = functools.partial(_moba_body, qi=qi, q_ref=q_ref, k_ref=k_ref, va_ref=va_ref, vb_ref=vb_ref,
                             km_ref=km_ref, toep_ref=toep_ref, spread_ref=spread_ref, o_ref=o_ref,
                             nblk=seq // MOBA_BLOCK)
    _for_causal_width(qi, seq, body, classes=2 * WIDTH_CLASSES)


def _moba_spread(seq):
    nblk = seq // MOBA_BLOCK
    row = np.arange(LANE)[None, :, None]
    h = np.arange(N_HEADS)[:, None, None]
    blk_of_key = (np.arange(seq) // MOBA_BLOCK)[None, None, :]
    return jnp.asarray(row == h * nblk + blk_of_key, BF16)


def _moba(q, k, va, vb, k_mean, toep):
    b, seq, _ = q.shape
    nblk = seq // MOBA_BLOCK
    if nblk & (nblk - 1) or N_HEADS * nblk > LANE:
        raise ValueError("MoBA lane layout needs a power-of-two block count with heads*blocks <= 128")
    spread = _moba_spread(seq)
    qblock = pl.BlockSpec((1, ATT_ROWS, 256), lambda bi, qi: (bi, qi, 0))
    whole = pl.BlockSpec((1, seq, 256), lambda bi, qi: (bi, 0, 0))
    means = pl.BlockSpec((1, nblk, 256), lambda bi, qi: (bi, 0, 0))
    return pl.pallas_call(
        functools.partial(_moba_kernel, seq=seq),
        grid=(b, seq // ATT_ROWS),
        in_specs=[qblock, whole, whole, whole, means, _full(toep.shape), _full(spread.shape)],
        out_specs=qblock,
        out_shape=jax.ShapeDtypeStruct((b, seq, 256), BF16),
        compiler_params=_params("parallel", "parallel"),
        name="moba",
    )(q, k, va, vb, k_mean, toep, spread)


def _mla_body(width, qi, q_ref, k_ref, va_ref, vb_ref, causal_ref, o_ref):
    nchunk = width // LANE
    sub = ATT_ROWS // LANE
    causal = jnp.concatenate(
        [jnp.concatenate([causal_ref[0, jnp.clip(qi * sub + j - c + 1, 0, 2)] for c in range(nchunk)], axis=1)
         for j in range(sub)], axis=0)
    logits = []
    for h in range(N_HEADS):
        qs = slice(h * MLA_SLOT, (h + 1) * MLA_SLOT)
        logits.append(_dot_t(q_ref[0, :, qs], k_ref[0, :width, qs]) * MLA_QK ** -0.5 + causal)
    weights = [_exp_weights(x) for x in logits]
    for pair in range(N_HEADS // 2):
        ps = slice(pair * LANE, (pair + 1) * LANE)
        outs = [_dot(weights[2 * pair], va_ref[0, :width, ps]), _dot(weights[2 * pair + 1], vb_ref[0, :width, ps])]
        o_ref[0, :, ps] = _pair_output(*outs).astype(o_ref.dtype)


def _mla_kernel(q_ref, k_ref, va_ref, vb_ref, causal_ref, o_ref, *, seq):
    qi = pl.program_id(1)
    body = functools.partial(_mla_body, qi=qi, q_ref=q_ref, k_ref=k_ref, va_ref=va_ref, vb_ref=vb_ref,
                             causal_ref=causal_ref, o_ref=o_ref)
    _for_causal_width(qi, seq, body, classes=2 * WIDTH_CLASSES)


def _mla(q, k, va, vb):
    b, seq, _ = q.shape
    tri = np.where(np.arange(LANE)[:, None] >= np.arange(LANE)[None, :], 0.0, NEG_INF)
    causal = jnp.asarray(np.stack([np.full((LANE, LANE), NEG_INF), tri, np.zeros((LANE, LANE))])[None], F32)
    return pl.pallas_call(
        functools.partial(_mla_kernel, seq=seq),
        grid=(b, seq // ATT_ROWS),
        in_specs=[pl.BlockSpec((1, ATT_ROWS, N_HEADS * MLA_SLOT), lambda bi, qi: (bi, qi, 0)),
                  pl.BlockSpec((1, seq, N_HEADS * MLA_SLOT), lambda bi, qi: (bi, 0, 0)),
                  pl.BlockSpec((1, seq, 256), lambda bi, qi: (bi, 0, 0)),
                  pl.BlockSpec((1, seq, 256), lambda bi, qi: (bi, 0, 0)),
                  _full(causal.shape)],
        out_specs=pl.BlockSpec((1, ATT_ROWS, 256), lambda bi, qi: (bi, qi, 0)),
        out_shape=jax.ShapeDtypeStruct((b, seq, 256), BF16),
        compiler_params=_params("parallel", "parallel"),
        name="mla",
    )(q, k, va, vb, causal)


def _dil_group(dil, seq, toep_ref, g, q_ref, k_ref, v_ref, m_ref, l_ref, acc_ref):
    n_band = seq // (dil * LANE)

    def unit(u, carry):
        r = lax.div(u, n_band)
        n = lax.rem(u, n_band)
        if dil == 1:
            cur = pl.ds(pl.multiple_of(n * LANE, LANE), LANE)
            prev = pl.ds(pl.multiple_of(jnp.maximum(n - 1, 0) * LANE, LANE), LANE)
        else:
            cur = pl.ds(n * (LANE * dil) + r, LANE, stride=dil)
            prev = pl.ds(jnp.maximum(n - 1, 0) * (LANE * dil) + r, LANE, stride=dil)
        prev_tile = jnp.where(n > 0, 2, 0)
        first = lax.broadcasted_iota(jnp.int32, (LANE, LANE), 1) < HEAD_DIM
        kk = [jnp.concatenate([k_ref[pair, prev, :], k_ref[pair, cur, :]], axis=0).astype(BF16)
              for pair in range(N_HEADS // 2)]
        vv = [jnp.concatenate([v_ref[pair, prev, :], v_ref[pair, cur, :]], axis=0).astype(BF16)
              for pair in range(N_HEADS // 2)]
        logits = []
        for h in range(N_HEADS):
            mine = first if h % 2 == 0 else ~first
            bias = jnp.concatenate([toep_ref[g, h, prev_tile], toep_ref[g, h, 1]], axis=1)
            logits.append(_dot_t(jnp.where(mine, q_ref[h // 2, cur, :], 0.0).astype(BF16), kk[h // 2]) + bias)
        ms = [jnp.max(x, axis=1, keepdims=True) for x in logits]
        ps = [jnp.exp(x - m) for x, m in zip(logits, ms)]
        ls = [jnp.sum(p, axis=1, keepdims=True) for p in ps]
        pvs = [_dot(p.astype(BF16), vv[h // 2]) for h, p in enumerate(ps)]
        for pair in range(N_HEADS // 2):
            m_new, l_new, pv = (jnp.where(first, x[2 * pair], x[2 * pair + 1]) for x in (ms, ls, pvs))
            m_old = m_ref[pair, cur, :]
            m_tot = jnp.maximum(m_old, m_new)
            a_old = jnp.exp(m_old - m_tot)
            a_new = jnp.exp(m_new - m_tot)
            m_ref[pair, cur, :] = m_tot
            l_ref[pair, cur, :] = a_old * l_ref[pair, cur, :] + a_new * l_new
            acc_ref[pair, cur, :] = a_old * acc_ref[pair, cur, :] + a_new * pv
        return carry

    lax.fori_loop(0, dil * n_band, unit, 0, unroll=2)


def _get_rows(ref, rows):
    return jnp.concatenate([ref[0, rows, :], ref[1, rows, :]], axis=1)


def _dil_kernel(q_ref, k_ref, v_ref, toep_ref, o_ref, m_ref, l_ref, acc_ref, *, seq):
    g = pl.program_id(1)
    everything = pl.ds(0, seq)

    @pl.when(g == 0)
    def _():
        m_ref[...] = jnp.full(m_ref.shape, NEG_INF, F32)
        l_ref[...] = jnp.zeros_like(l_ref)
        acc_ref[...] = jnp.zeros_like(acc_ref)

    for gi, (_, dil) in enumerate(DIL_PATTERNS):
        pl.when(g == gi)(functools.partial(_dil_group, dil, seq, toep_ref, gi, q_ref, k_ref, v_ref,
                                           m_ref, l_ref, acc_ref))

    @pl.when(g == DIL_GROUPS - 1)
    def _():
        o_ref[0] = (_get_rows(acc_ref, everything) / _get_rows(l_ref, everything)).astype(o_ref.dtype)


def _dil(q, k, v, toep, seq):
    b = q.shape[1] // seq
    group = pl.BlockSpec((2, seq, LANE), lambda bi, g: (g, bi, 0))
    state = pltpu.VMEM((2, seq, LANE), F32)
    return pl.pallas_call(
        functools.partial(_dil_kernel, seq=seq),
        grid=(b, DIL_GROUPS),
        in_specs=[group, group, group, _full(toep.shape)],
        out_specs=pl.BlockSpec((1, seq, 256), lambda bi, g: (bi, 0, 0)),
        out_shape=jax.ShapeDtypeStruct((b, seq, 256), BF16),
        scratch_shapes=[state] * 3,
        compiler_params=_params("parallel", "arbitrary"),
        name="dil",
    )(q, k, v, toep)


def _merge_kernel(x_ref, xn_ref, oa_ref, ob_ref, oc_ref, od_ref, wg_ref, bg_ref, wb_ref, wo_ref, xo_ref):
    branches = (oa_ref, ob_ref, oc_ref, od_ref)
    half = ROW_TILE // 2
    for s in range(2):
        rows = slice(s * half, (s + 1) * half)
        xn = xn_ref[rows, :]
        total = None
        for n in range(N_BRANCH):
            cols = slice(n * D_MODEL, (n + 1) * D_MODEL)
            gate = _sigmoid(_dot(xn, wg_ref[:, cols]) + bg_ref[:, cols])
            term = gate * _dot(branches[n][rows, :], wb_ref[n])
            total = term if total is None else total + term
        xo_ref[rows, :] = x_ref[rows, :] + _dot(total.astype(BF16), wo_ref[...])


def _merge(x, xn, outs, wg, bg, wb, wo):
    n = x.shape[0]
    rows = lambda width: pl.BlockSpec((ROW_TILE, width), lambda i: (i, 0))
    return pl.pallas_call(
        _merge_kernel,
        grid=(n // ROW_TILE,),
        in_specs=[rows(D_MODEL), rows(D_MODEL)] + [rows(256)] * N_BRANCH
                 + [_full(wg.shape), _full(bg.shape), _full(wb.shape), _full(wo.shape)],
        out_specs=rows(D_MODEL),
        out_shape=jax.ShapeDtypeStruct((n, D_MODEL), F32),
        compiler_params=_params("parallel"),
        name="merge",
    )(x, xn, *outs, wg, bg, wb, wo)


def _bucket_of_distance():
    d = np.arange(REL_MAX_DIST + 1)
    exact = REL_BUCKETS // 2
    nf = np.maximum(d, 1).astype(np.float32)
    log_b = exact + (np.log(nf / exact) / math.log(REL_MAX_DIST / exact) * (REL_BUCKETS - exact)).astype(np.int32)
    return np.where(d < exact, d, np.minimum(log_b, REL_BUCKETS - 1))


def _toeplitz_tiles(table, n_off, dil, max_steps=None):
    span = 2 * LANE
    m = np.arange(span)
    off = np.arange(n_off)[:, None]
    steps = off * LANE + LANE - 1 - m[None, :]
    live = (steps >= 0) if max_steps is None else (steps >= 0) & (steps <= max_steps)
    v = table[:, _bucket_of_distance()[np.clip(steps * dil, 0, REL_MAX_DIST)]]
    v = jnp.where(live[None], v, NEG_INF)
    flat = jnp.tile(v, (1, 1, LANE))[..., :LANE * (span - 1)]
    tiles = flat.reshape(table.shape[0], n_off, LANE, span - 1)[..., LANE - 1:]
    return jnp.concatenate([jnp.full((table.shape[0], 1, LANE, LANE), NEG_INF, F32), tiles], axis=1)


def _same_head(width, head):
    idx = np.arange(width) // head
    return jnp.asarray(idx[:, None] == idx[None, :], BF16)


def _rope_tables(seq):
    half = MLA_ROPE // 2
    freqs = ROPE_THETA ** (-np.arange(half, dtype=np.float64) / half)
    ang = np.arange(seq, dtype=np.float64)[:, None] * freqs[None, :]
    pad = np.zeros((seq, MLA_SLOT - MLA_QK))
    cos_h = np.concatenate([np.ones((seq, MLA_NOPE)), np.cos(ang), np.cos(ang), pad], axis=1)
    sin_h = np.concatenate([np.zeros((seq, MLA_NOPE)), np.sin(ang), np.sin(ang), pad], axis=1)
    return (jnp.asarray(np.tile(cos_h, (1, N_HEADS)), F32), jnp.asarray(np.tile(sin_h, (1, N_HEADS)), F32))


def _rot_half_cols():
    half = MLA_ROPE // 2
    src = np.arange(MLA_SLOT)
    sign = np.zeros(MLA_SLOT, np.float32)
    src[MLA_NOPE:MLA_NOPE + half] = np.arange(MLA_NOPE + half, MLA_QK)
    sign[MLA_NOPE:MLA_NOPE + half] = -1.0
    src[MLA_NOPE + half:MLA_QK] = np.arange(MLA_NOPE, MLA_NOPE + half)
    sign[MLA_NOPE + half:MLA_QK] = 1.0
    return src, sign


def _mixer(x, xn, seq, w_in, b_gate, qk_a, qk_b, qk_c, qk_d, mla_nq, w_uq, mla_nkv, w_ukv,
           w_branch, w_out, toeps, consts):
    n = x.shape[0]
    b = n // seq
    e64, e_slots, tri, cos, sin = consts
    toep_a, toep_b, toep_d = toeps
    row = lambda a: a.reshape(1, -1)
    tile4 = lambda g: jnp.tile(g, N_HEADS).reshape(1, -1)
    r3 = lambda a: a.reshape(b, seq, a.shape[-1])

    off_iq = OFF_A + 6 * HEAD_DIM
    off_ik = off_iq + N_HEADS * IDX_DIM
    iq_rep = jnp.concatenate([jnp.tile(w_in[:, off_iq + h * IDX_DIM:off_iq + (h + 1) * IDX_DIM], (1, 4))
                              for h in range(N_HEADS)], axis=1)
    ik_rep = jnp.tile(w_in[:, off_ik:off_ik + IDX_DIM], (1, 4))
    iw_pad = jnp.pad(w_in[:, off_ik + IDX_DIM:OFF_A + W_A], ((0, 0), (0, LANE - N_HEADS)))
    half_pad = ((0, 0), (0, LANE - HEAD_DIM))
    q_slots = jnp.concatenate([jnp.pad(w_in[:, OFF_A + h * HEAD_DIM:OFF_A + (h + 1) * HEAD_DIM], half_pad)
                               for h in range(N_HEADS)], axis=1)
    off_k = OFF_A + N_HEADS * HEAD_DIM
    k_slot = jnp.pad(w_in[:, off_k:off_k + HEAD_DIM], half_pad)
    v_twice = jnp.tile(w_in[:, off_k + HEAD_DIM:off_iq], (1, 2))
    wa = jnp.concatenate([q_slots, k_slot, v_twice, iq_rep, ik_rep, iw_pad], axis=1).astype(BF16)
    gq_slots = jnp.tile(jnp.pad(qk_a[0], (0, LANE - HEAD_DIM)), N_HEADS).reshape(1, -1)
    gk_slot = jnp.pad(qk_a[1], (0, LANE - HEAD_DIM)).reshape(1, -1)
    piece_a = _proj_a(n, wa, e_slots, gq_slots, gk_slot)

    wb_in = w_in[:, OFF_B:OFF_B + W_B].astype(BF16)
    piece_b = _proj_b(n, wb_in, e64, tile4(qk_b[0]), tile4(qk_b[1]))

    src, sign = _rot_half_cols()
    src4 = np.concatenate([h * MLA_SLOT + src for h in range(N_HEADS)])
    sign4 = jnp.asarray(np.tile(sign, N_HEADS))
    slot_pad = MLA_SLOT - MLA_QK
    w_kr = w_in[:, OFF_C + MLA_Q_LORA + MLA_KV_LORA:OFF_C + W_C]
    kr_slots = jnp.pad(w_kr, ((0, 0), (MLA_NOPE, slot_pad)))
    kr_slots = jnp.tile(kr_slots, (1, N_HEADS))
    wc = jnp.concatenate([w_in[:, OFF_C:OFF_C + MLA_Q_LORA + MLA_KV_LORA], kr_slots,
                          kr_slots[:, src4] * sign4], axis=1).astype(BF16)
    slots = lambda w: jnp.pad(w.reshape(w.shape[0], N_HEADS, -1),
                              ((0, 0), (0, 0), (0, MLA_SLOT - w.shape[1] // N_HEADS))).reshape(w.shape[0], -1)
    wuq = slots(w_uq)
    ukv = w_ukv.reshape(MLA_KV_LORA, N_HEADS, MLA_NOPE + MLA_V)
    wuk = slots(ukv[:, :, :MLA_NOPE].reshape(MLA_KV_LORA, -1))
    wuv = ukv[:, :, MLA_NOPE:].reshape(MLA_KV_LORA, -1)
    gq = jnp.tile(jnp.pad(qk_c[0], (0, slot_pad)), N_HEADS)
    gk = jnp.tile(jnp.pad(qk_c[1], (0, slot_pad)), N_HEADS)
    piece_c = _proj_c(
        n, wc, e_slots, row(mla_nq), row(mla_nkv), wuq.astype(BF16), (wuq[:, src4] * sign4).astype(BF16),
        wuk.astype(BF16), wuv.astype(BF16), row(gq), row(gq[src4]), row(gk), row(gk[src4]), cos, sin, seq)

    wd = w_in[:, OFF_D:OFF_D + W_D].astype(BF16)
    piece_d = _proj_d(n, wd, e64, tile4(qk_d[0]), tile4(qk_d[1]))

    proj_a, proj_b, proj_c, proj_d = _project(xn, (piece_a, piece_b, piece_c, piece_d))
    out_a = _dsa(*map(r3, proj_a), toep_a, tri).reshape(n, 256)
    qb, kb, vb_first, vb_second, kmean = proj_b
    out_b = _moba(r3(qb), r3(kb), r3(vb_first), r3(vb_second),
                  kmean.reshape(b, seq // MOBA_BLOCK, BRANCH_WIDTH), toep_b).reshape(n, 256)
    out_c = _mla(*map(r3, proj_c)).reshape(n, 256)
    out_d = _dil(*proj_d, toep_d, seq).reshape(n, 256)

    wg = w_in[:, OFF_G:OFF_G + W_G].astype(BF16)
    return _merge(x, xn, (out_a, out_b, out_c, out_d), wg, b_gate.reshape(1, -1),
                  w_branch.astype(BF16), w_out.astype(BF16))


def kernel(x, norm_gain, w_in, b_gate, qk_gain_a, qk_gain_b, qk_gain_c, qk_gain_d, mla_norm_q,
           w_mla_uq, mla_norm_kv, w_mla_ukv, w_branch, w_out, rel_bias, w_ffn_in, w_ffn_out):
    b, seq, d = x.shape
    depth = norm_gain.shape[0]
    toep_a = _toeplitz_tiles(rel_bias[0:4], seq // LANE, 1)
    toep_b = _toeplitz_tiles(rel_bias[4:8], seq // LANE, 1)
    toep_d = jnp.stack([_toeplitz_tiles(rel_bias[8 + 4 * g:12 + 4 * g], 2, dil, max_steps=window // dil)
                        for g, (window, dil) in enumerate(DIL_PATTERNS)])
    tri = jnp.asarray(np.arange(LANE)[:, None] < np.arange(LANE)[None, :], BF16)
    consts = (_same_head(256, HEAD_DIM), _same_head(256, MLA_SLOT), tri) + _rope_tables(seq)

    x = x.reshape(b * seq, d)
    for l in range(depth):
        g = norm_gain[l]
        x, xn = _ffn(x, g[0:1], g[1:2], w_ffn_in[l, 0].astype(BF16), w_ffn_out[l, 0].astype(BF16))
        x = _mixer(x, xn, seq, w_in[l], b_gate[l], qk_gain_a[l], qk_gain_b[l], qk_gain_c[l],
                   qk_gain_d[l], mla_norm_q[l], w_mla_uq[l], mla_norm_kv[l], w_mla_ukv[l],
                   w_branch[l], w_out[l], (toep_a, toep_b, toep_d), consts)
        x, _ = _ffn(x, g[2:3], g[2:3], w_ffn_in[l, 1].astype(BF16), w_ffn_out[l, 1].astype(BF16))
    return x.reshape(b, seq, d)
```

```python
import functools
import math

import numpy as np
import jax
import jax.numpy as jnp
from jax import lax
from jax.experimental import pallas as pl
from jax.experimental.pallas import tpu as pltpu

D_MODEL = 1024
HEAD_DIM = 64
N_HEADS = 4
BRANCH_WIDTH = N_HEADS * HEAD_DIM
IDX_DIM = 32
DSA_TOPK = 256
MOBA_BLOCK = 256
MOBA_TOPK = 3
MLA_Q_LORA = 384
MLA_KV_LORA = 256
MLA_NOPE = 64
MLA_ROPE = 32
MLA_QK = MLA_NOPE + MLA_ROPE
MLA_V = 64
MLA_SLOT = 128
ROPE_THETA = 10000.0
DIL_PATTERNS = ((128, 1), (512, 4), (2048, 16))
DIL_GROUPS = 3
N_BRANCH = 4
D_FF = 2816
REL_BUCKETS = 32
REL_MAX_DIST = 2048
RMS_EPS = 1e-6

OFF_A = 0
W_A = 4 * HEAD_DIM + 2 * HEAD_DIM + 4 * IDX_DIM + IDX_DIM + 4
OFF_B = OFF_A + W_A
W_B = 3 * BRANCH_WIDTH
OFF_C = OFF_B + W_B
W_C = MLA_Q_LORA + MLA_KV_LORA + MLA_ROPE
OFF_D = OFF_C + W_C
W_D = 3 * DIL_GROUPS * BRANCH_WIDTH
OFF_G = OFF_D + W_D
W_G = N_BRANCH * D_MODEL

LANE = 128
ROW_TILE = 512
FF_CHUNK = 1408
ATT_ROWS = 256
WIDTH_CLASSES = 4
MASKED = -1e30
VMEM_LIMIT = 56 * 1024 * 1024

F32 = jnp.float32
BF16 = jnp.bfloat16
NEG_INF = float("-inf")
INT_MIN = -2 ** 31

_CONTRACT_LAST = (((1,), (1,)), ((), ()))


def _dot(a, b, precision=None):
    return jnp.dot(a, b, preferred_element_type=F32, precision=precision)


def _dot_t(a, b, precision=None):
    return lax.dot_general(a, b, _CONTRACT_LAST, preferred_element_type=F32, precision=precision)


def _rms(x, gain):
    return x * lax.rsqrt(jnp.mean(x * x, axis=-1, keepdims=True) + RMS_EPS) * gain


def _sigmoid(x):
    return 1.0 / (1.0 + jnp.exp(-x))


def _head_sumsq(y, e_ref):
    y2 = y * y
    hi = y2.astype(BF16)
    lo = (y2 - hi.astype(F32)).astype(BF16)
    e = e_ref[...]
    step = e.shape[0]
    blocks = [_dot(hi[:, c:c + step], e) + _dot(lo[:, c:c + step], e) for c in range(0, y.shape[1], step)]
    return blocks[0] if len(blocks) == 1 else jnp.concatenate(blocks, axis=1)


def _head_norm(y, e_ref, gain, width):
    return y * lax.rsqrt(_head_sumsq(y, e_ref) * (1.0 / width) + RMS_EPS) * gain


def _params(*sem):
    return pltpu.CompilerParams(dimension_semantics=sem, vmem_limit_bytes=VMEM_LIMIT)


def _full(shape):
    return pl.BlockSpec(shape, lambda *_: (0,) * len(shape))


def _resident(shape):
    return pl.BlockSpec(shape, lambda *_: (0,) * len(shape), pipeline_mode=pl.Buffered(1))


def _ffn_kernel(x_ref, g_ref, gn_ref, wi_ref, wo_ref, xo_ref, xn_ref):
    half = ROW_TILE // 2
    for s in range(2):
        rows = slice(s * half, (s + 1) * half)
        x = x_ref[rows, :]
        xb = _rms(x, g_ref[...]).astype(BF16)
        acc = None
        for j in range(D_FF // FF_CHUNK):
            gate = _dot(xb, wi_ref[:, j * FF_CHUNK:(j + 1) * FF_CHUNK])
            up = _dot(xb, wi_ref[:, D_FF + j * FF_CHUNK:D_FF + (j + 1) * FF_CHUNK])
            h = (gate * _sigmoid(gate) * up).astype(BF16)
            part = _dot(h, wo_ref[j * FF_CHUNK:(j + 1) * FF_CHUNK, :])
            acc = part if acc is None else acc + part
        xo = x + 0.5 * acc
        xo_ref[rows, :] = xo
        xn_ref[rows, :] = _rms(xo, gn_ref[...]).astype(BF16)


def _ffn(x, gain, gain_next, w_in, w_out):
    n = x.shape[0]
    rows = pl.BlockSpec((ROW_TILE, D_MODEL), lambda i: (i, 0))
    return pl.pallas_call(
        _ffn_kernel,
        grid=(n // ROW_TILE,),
        in_specs=[rows, _full((1, D_MODEL)), _full((1, D_MODEL)), _resident(w_in.shape), _resident(w_out.shape)],
        out_specs=[rows, rows],
        out_shape=[jax.ShapeDtypeStruct((n, D_MODEL), F32), jax.ShapeDtypeStruct((n, D_MODEL), BF16)],
        compiler_params=_params("parallel"),
        name="ffn",
    )(x, gain, gain_next, w_in, w_out)


def _split_terms(rep, hi_lanes):
    hi = rep.astype(BF16).astype(F32)
    return jnp.where(hi_lanes, hi, rep - hi).astype(BF16)


def _with_ones(v):
    first = (lax.broadcasted_iota(jnp.int32, (1, v.shape[1]), 1) & (LANE - 1)) < HEAD_DIM
    return jnp.where(first, v, 1.0).astype(BF16), jnp.where(first, 1.0, v).astype(BF16)


def _proj_a_kernel(xn_ref, w_ref, e_ref, gq_ref, gk_ref, q_ref, k_ref, va_ref, vb_ref, iq_ref, ik_ref, iw_ref):
    p = _dot(xn_ref[...], w_ref[...])
    q = _head_norm(p[:, :512], e_ref, gq_ref[...], HEAD_DIM) * HEAD_DIM ** -0.5
    q_ref[...] = q.astype(BF16)
    k = p[:, 512:640]
    k_ms = jnp.sum(k * k, axis=-1, keepdims=True) * (1.0 / HEAD_DIM)
    k_ref[...] = (k * lax.rsqrt(k_ms + RMS_EPS) * gk_ref[...]).astype(BF16)
    va_ref[...], vb_ref[...] = _with_ones(p[:, 640:768])
    lane = lax.broadcasted_iota(jnp.int32, (1, 4 * LANE), 1)
    copy = lax.shift_right_logical(lane, int(math.log2(IDX_DIM))) & 3
    iq_ref[...] = _split_terms(p[:, 768:1280], copy < 2)
    ik_ref[...] = _split_terms(p[:, 1280:1408], (copy[:, :LANE] & 1) == 0)
    iw_ref[...] = p[:, 1408:1536]


def _row_spec(width):
    return pl.BlockSpec((ROW_TILE, width), lambda i: (i, 0))


def _const_spec(a):
    return _resident(a.shape)


def _proj_a(n, w, e_slots, gq, gk):
    consts = (w, e_slots, gq, gk)
    widths = ((512, BF16), (128, BF16), (128, BF16), (128, BF16), (512, BF16), (128, BF16), (128, F32))
    return (_proj_a_kernel, consts, [_const_spec(c) for c in consts], [_row_spec(wd) for wd, _ in widths],
            [jax.ShapeDtypeStruct((n, wd), dt) for wd, dt in widths])


def _proj_b_kernel(xn_ref, w_ref, e_ref, gq_ref, gk_ref, q_ref, k_ref, va_ref, vb_ref, km_ref):
    p = _dot(xn_ref[...], w_ref[...])
    q_ref[...] = _head_norm(p[:, :256], e_ref, gq_ref[...], HEAD_DIM)
    k = _head_norm(p[:, 256:512], e_ref, gk_ref[...], HEAD_DIM)
    k_ref[...] = k.astype(BF16)
    va_ref[...], vb_ref[...] = _with_ones(p[:, 512:768])
    km = jnp.mean(k.reshape(ROW_TILE // MOBA_BLOCK, MOBA_BLOCK, BRANCH_WIDTH), axis=1)
    for j in range(ROW_TILE // MOBA_BLOCK):
        km_ref[j] = km[j:j + 1]


def _proj_b(n, w, e64, gq, gk):
    consts = (w, e64, gq, gk)
    widths = ((256, F32), (256, BF16), (256, BF16), (256, BF16))
    per_tile = ROW_TILE // MOBA_BLOCK
    return (_proj_b_kernel, consts, [_const_spec(c) for c in consts],
            [_row_spec(wd) for wd, _ in widths] + [pl.BlockSpec((per_tile, 1, BRANCH_WIDTH), lambda i: (i, 0, 0))],
            [jax.ShapeDtypeStruct((n, wd), dt) for wd, dt in widths]
            + [jax.ShapeDtypeStruct((n // MOBA_BLOCK, 1, BRANCH_WIDTH), F32)])


def _proj_c_kernel(xn_ref, w_ref, e_ref, nq_ref, nkv_ref, wuq_ref, wuqs_ref, wuk_ref, wuv_ref,
                   gq_ref, gqs_ref, gk_ref, gks_ref, cos_ref, sin_ref, q_ref, k_ref, va_ref, vb_ref):
    p = _dot(xn_ref[...], w_ref[...])
    cos = cos_ref[...]
    sin = sin_ref[...]
    xq = _rms(p[:, :384], nq_ref[...]).astype(BF16)
    qa = _dot(xq, wuq_ref[...])
    qs = _dot(xq, wuqs_ref[...])
    rq = lax.rsqrt(_head_sumsq(qa, e_ref) * (1.0 / MLA_QK) + RMS_EPS)
    q_ref[...] = (rq * (qa * gq_ref[...] * cos + qs * gqs_ref[...] * sin)).astype(BF16)
    xkv = _rms(p[:, 384:640], nkv_ref[...]).astype(BF16)
    ka = _dot(xkv, wuk_ref[...]) + p[:, 640:1152]
    ks = p[:, 1152:1664]
    rk = lax.rsqrt(_head_sumsq(ka, e_ref) * (1.0 / MLA_QK) + RMS_EPS)
    k_ref[...] = (rk * (ka * gk_ref[...] * cos + ks * gks_ref[...] * sin)).astype(BF16)
    va_ref[...], vb_ref[...] = _with_ones(_dot(xkv, wuv_ref[...]))


def _proj_c(n, w, e_slots, nq, nkv, wuq, wuqs, wuk, wuv, gq, gqs, gk, gks, cos, sin, seq):
    pos = pl.BlockSpec((ROW_TILE, N_HEADS * MLA_SLOT), lambda i: (i % (seq // ROW_TILE), 0))
    consts = (w, e_slots, nq, nkv, wuq, wuqs, wuk, wuv, gq, gqs, gk, gks)
    widths = ((N_HEADS * MLA_SLOT, BF16), (N_HEADS * MLA_SLOT, BF16), (256, BF16), (256, BF16))
    return (_proj_c_kernel, consts + (cos, sin), [_const_spec(c) for c in consts] + [pos, pos],
            [_row_spec(wd) for wd, _ in widths], [jax.ShapeDtypeStruct((n, wd), dt) for wd, dt in widths])


def _proj_d_kernel(xn_ref, w_ref, e_ref, gq_ref, gk_ref, q_ref, k_ref, v_ref):
    xn = xn_ref[...]

    def put(ref, c, val):
        ref[2 * c] = val[:, :LANE]
        ref[2 * c + 1] = val[:, LANE:]

    for c in range(DIL_GROUPS):
        lo, hi = c * 256, (c + 1) * 256
        put(q_ref, c, _head_norm(_dot(xn, w_ref[:, lo:hi]), e_ref, gq_ref[...], HEAD_DIM) * HEAD_DIM ** -0.5)
        put(k_ref, c, _head_norm(_dot(xn, w_ref[:, 768 + lo:768 + hi]), e_ref, gk_ref[...], HEAD_DIM))
        put(v_ref, c, _dot(xn, w_ref[:, 1536 + lo:1536 + hi]))


def _proj_d(n, w, e64, gq, gk):
    consts = (w, e64, gq, gk)
    tiles = 2 * DIL_GROUPS
    return (_proj_d_kernel, consts, [_const_spec(c) for c in consts],
            [pl.BlockSpec((tiles, ROW_TILE, LANE), lambda i: (0, i, 0))] * 3,
            [jax.ShapeDtypeStruct((tiles, n, LANE), F32)] * 3)


def _proj_fused_kernel(*refs, pieces):
    xn_ref, refs = refs[0], refs[1:]
    n_in = sum(k for _, k, _ in pieces)
    ins, outs = refs[:n_in], refs[n_in:]
    for body, k_in, k_out in pieces:
        body(xn_ref, *ins[:k_in], *outs[:k_out])
        ins, outs = ins[k_in:], outs[k_out:]


def _project(xn, pieces):
    n = xn.shape[0]
    layout = tuple((body, len(consts), len(out_specs)) for body, consts, _, out_specs, _ in pieces)
    flat = pl.pallas_call(
        functools.partial(_proj_fused_kernel, pieces=layout),
        grid=(n // ROW_TILE,),
        in_specs=[_row_spec(D_MODEL)] + [s for p in pieces for s in p[2]],
        out_specs=[s for p in pieces for s in p[3]],
        out_shape=[s for p in pieces for s in p[4]],
        compiler_params=_params("parallel"),
        name="proj",
    )(xn, *[c for p in pieces for c in p[1]])
    outs = []
    for _, _, k_out in layout:
        outs.append(flat[:k_out])
        flat = flat[k_out:]
    return outs


def _for_causal_width(qi, seq, body, classes=WIDTH_CLASSES):
    n_qtiles = seq // ATT_ROWS
    n_cls = min(classes, n_qtiles)
    per = n_qtiles // n_cls
    for c in range(n_cls):
        pl.when((qi >= c * per) & (qi < (c + 1) * per))(functools.partial(body, (c + 1) * per * ATT_ROWS))


def _bias_rows(toep_ref, h, qblk, nchunk):
    return jnp.concatenate([toep_ref[h, jnp.maximum(qblk - c + 1, 0)] for c in range(nchunk)], axis=1)


def _bias_tile(toep_ref, h, qi, sub_blocks, nchunk):
    return jnp.concatenate([_bias_rows(toep_ref, h, qi * sub_blocks + j, nchunk) for j in range(sub_blocks)],
                           axis=0)


def _exp_weights(logits):
    m = jnp.max(logits, axis=1, keepdims=True)
    return jnp.exp(logits - m).astype(BF16)


def _pair_output(first_head, second_head):
    first = lax.broadcasted_iota(jnp.int32, first_head.shape, 1) < HEAD_DIM
    numer = jnp.where(first, first_head, second_head)
    denom = jnp.where(first, pltpu.roll(first_head, HEAD_DIM, axis=1), pltpu.roll(second_head, HEAD_DIM, axis=1))
    return numer / denom


def _dsa_body(width, qi, q_ref, k_ref, va_ref, vb_ref, iq_ref, ik_ref, iw_ref, toep_ref, tri_ref, o_ref,
              sel_ref, n_sel):
    rows = ATT_ROWS
    nchunk = width // LANE
    ik = ik_ref[0, :width, :]
    iw = iw_ref[0][:, :N_HEADS] * (N_HEADS ** -0.5 * IDX_DIM ** -0.5)

    r = _dot_t(jnp.concatenate([iq_ref[0, :, h * LANE:(h + 1) * LANE] for h in range(N_HEADS)], axis=0), ik)
    score = None
    for h in range(N_HEADS):
        term = jnp.maximum(r[h * rows:(h + 1) * rows], 0.0) * iw[:, h:h + 1]
        score = term if score is None else score + term

    t = qi * rows + lax.broadcasted_iota(jnp.int32, (rows, width), 0)
    s = lax.broadcasted_iota(jnp.int32, (rows, width), 1)

    bits = pltpu.bitcast(score, jnp.int32)
    key = jnp.where(bits < 0, jnp.int32(INT_MIN) - bits, bits)
    key = jnp.where(s <= t, key, jnp.int32(INT_MIN))

    half_min = -2 ** 15
    upper = lax.shift_right_arithmetic(key, 16).astype(jnp.int16)
    lower = ((key & jnp.int32(0xFFFF)) + half_min).astype(jnp.int16)

    def lane_counts(hit):
        ones = jnp.where(hit, jnp.int16(1), jnp.int16(0))
        acc = ones[:, :LANE]
        for c in range(1, nchunk):
            acc = acc + ones[:, c * LANE:(c + 1) * LANE]
        return acc

    def total(acc):
        return jnp.sum(acc.astype(F32), axis=1, keepdims=True)

    def count(hit):
        return total(lane_counts(hit))

    def search(half, need):
        top, bottom = slice(0, rows // 2), slice(rows // 2, rows)
        need_rows = jnp.broadcast_to(need, (rows, 1)).astype(F32)

        def bit(step):
            return jnp.where(step < 16, lax.shift_left(jnp.int32(1), jnp.maximum(15 - step, 0)), 0)

        def hits(part, cand):
            return lane_counts(half[part] >= (cand + half_min).astype(jnp.int16))

        def body(i, carry):
            ans_top, acc_top, ans_bottom = carry
            cand_top = ans_top | bit(i)
            ans_top = jnp.where(total(acc_top) >= need_rows[top], cand_top, ans_top)
            cand_bottom = ans_bottom | bit(i)
            ans_bottom = jnp.where(total(hits(bottom, cand_bottom)) >= need_rows[bottom], cand_bottom, ans_bottom)
            return ans_top, hits(top, ans_top | bit(i + 1)), ans_bottom

        zero = jnp.zeros((rows // 2, 1), jnp.int32)
        ans_top, _, ans_bottom = lax.fori_loop(0, 16, body, (zero, hits(top, zero | bit(0)), zero))
        return jnp.concatenate([ans_top, ans_bottom], axis=0)

    def searched_threshold():
        thr_upper = search(upper, n_sel) + half_min
        thr_upper16 = thr_upper.astype(jnp.int16)
        n_over = count(upper > thr_upper16)
        candidates = jnp.where(upper == thr_upper16, lower, jnp.int16(half_min))
        return thr_upper * 65536 + search(candidates, n_sel - n_over)

    thr = lax.cond((qi + 1) * rows <= n_sel, lambda: jnp.full((rows, 1), INT_MIN, jnp.int32), searched_threshold)

    sel = jnp.where(key >= jnp.maximum(thr, jnp.int32(INT_MIN + 1)), 1.0, 0.0)
    sel_ref[:, :width] = sel

    overflow = jnp.sum(sel, axis=1, keepdims=True) > n_sel

    @pl.when(jnp.max(jnp.where(overflow, 1.0, 0.0)) > 0.0)
    def _():
        above = key > thr
        room = n_sel - jnp.sum(jnp.where(above, 1.0, 0.0), axis=1, keepdims=True)
        before = jnp.zeros((rows, 1), F32)
        for c in range(nchunk):
            cols = slice(c * LANE, (c + 1) * LANE)
            tc = jnp.where(key[:, cols] == thr, 1.0, 0.0)
            rank = before + _dot(tc.astype(BF16), tri_ref[...])
            keep = above[:, cols] | ((tc > 0.0) & (rank < room))
            sel_ref[:, cols] = jnp.where(keep, 1.0, 0.0)
            before = before + jnp.sum(tc, axis=1, keepdims=True)

    unselected = (sel_ref[:, :width] - 1.0) * -MASKED
    k = k_ref[0, :width, :]
    qk = _dot_t(jnp.concatenate([q_ref[0, :, h * LANE:(h + 1) * LANE] for h in range(N_HEADS)], axis=0), k)
    logits = [qk[h * rows:(h + 1) * rows] + _bias_tile(toep_ref, h, qi, rows // LANE, nchunk) + unselected
              for h in range(N_HEADS)]
    weights = [_exp_weights(x) for x in logits]
    for pair in range(N_HEADS // 2):
        outs = [_dot(weights[2 * pair], va_ref[0, :width, :]), _dot(weights[2 * pair + 1], vb_ref[0, :width, :])]
        o_ref[0, :, pair * LANE:(pair + 1) * LANE] = _pair_output(*outs).astype(o_ref.dtype)


def _dsa_kernel(*refs, seq, n_sel):
    qi = pl.program_id(1)
    _for_causal_width(qi, seq, functools.partial(_dsa_body, qi=qi, n_sel=n_sel, **_named(refs)))


def _named(refs):
    names = ("q_ref", "k_ref", "va_ref", "vb_ref", "iq_ref", "ik_ref", "iw_ref", "toep_ref", "tri_ref", "o_ref",
             "sel_ref")
    return dict(zip(names, refs, strict=True))


def _dsa(q, k, va, vb, iq, ik, iw, toep, tri):
    b, seq, _ = q.shape
    n_sel = min(DSA_TOPK, seq // 4)
    qblock = lambda width: pl.BlockSpec((1, ATT_ROWS, width), lambda bi, qi: (bi, qi, 0))
    whole = lambda width: pl.BlockSpec((1, seq, width), lambda bi, qi: (bi, 0, 0))
    return pl.pallas_call(
        functools.partial(_dsa_kernel, seq=seq, n_sel=n_sel),
        grid=(b, seq // ATT_ROWS),
        in_specs=[qblock(512), whole(128), whole(128), whole(128), qblock(512), whole(128), qblock(128),
                  _full(toep.shape), _full(tri.shape)],
        out_specs=qblock(256),
        out_shape=jax.ShapeDtypeStruct((b, seq, 256), BF16),
        scratch_shapes=[pltpu.VMEM((ATT_ROWS, seq), F32)],
        compiler_params=_params("parallel", "parallel"),
        name="dsa",
    )(q, k, va, vb, iq, ik, iw, toep, tri)


def _moba_body(width, qi, q_ref, k_ref, va_ref, vb_ref, km_ref, toep_ref, spread_ref, o_ref, nblk):
    nchunk = width // LANE
    n_sel = min(MOBA_TOPK, nblk - 1)
    rows = ATT_ROWS
    own = lax.shift_right_logical(qi * rows, int(math.log2(MOBA_BLOCK)))
    q = q_ref[0]

    lane = lax.broadcasted_iota(jnp.int32, (rows, LANE), 1)
    blk = lane & (nblk - 1)
    past = (blk < own) & (lane < N_HEADS * nblk)

    def gated_rank():
        km = km_ref[0]
        head_of_col = lax.shift_right_logical(lax.broadcasted_iota(jnp.int32, (nblk, BRANCH_WIDTH), 1),
                                              int(math.log2(HEAD_DIM)))
        per_head = [jnp.where(head_of_col == h, km, 0.0) for h in range(N_HEADS)]
        pad = jnp.zeros((LANE - N_HEADS * nblk, BRANCH_WIDTH), F32)
        gate = _dot_t(q, jnp.concatenate(per_head + [pad], axis=0), precision=lax.Precision.HIGHEST)
        gate = jnp.where(past, gate, NEG_INF)
        rank = jnp.zeros((rows, LANE), F32)
        for d in range(1, nblk):
            before = pltpu.roll(gate, d, axis=1)
            rank = rank + jnp.where(blk >= d, jnp.where(before >= gate, 1.0, 0.0), 0.0)
            after = pltpu.roll(gate, LANE - d, axis=1)
            rank = rank + jnp.where(blk < nblk - d, jnp.where(after > gate, 1.0, 0.0), 0.0)
        return rank

    rank = lax.cond(own <= n_sel, lambda: jnp.zeros((rows, LANE), F32), gated_rank)
    penalty = jnp.where(past, jnp.where(rank < n_sel, 0.0, MASKED), jnp.where(blk == own, 0.0, MASKED)).astype(BF16)

    first = lane < HEAD_DIM
    logits = []
    for h in range(N_HEADS):
        ps = slice(h // 2 * LANE, (h // 2 + 1) * LANE)
        mine = first if h % 2 == 0 else ~first
        qh = jnp.where(mine, q[:, ps] * HEAD_DIM ** -0.5, 0.0).astype(BF16)
        unpicked = _dot(penalty, spread_ref[h, :, :width])
        logits.append(_dot_t(qh, k_ref[0, :width, ps]) + _bias_tile(toep_ref, h, qi, rows // LANE, nchunk)
                      + unpicked)
    weights = [_exp_weights(x) for x in logits]
    for pair in range(N_HEADS // 2):
        ps = slice(pair * LANE, (pair + 1) * LANE)
        outs = [_dot(weights[2 * pair], va_ref[0, :width, ps]), _dot(weights[2 * pair + 1], vb_ref[0, :width, ps])]
        o_ref[0, :, ps] = _pair_output(*outs).astype(o_ref.dtype)


def _moba_kernel(q_ref, k_ref, va_ref, vb_ref, km_ref, toep_ref, spread_ref, o_ref, *, seq):
    qi = pl.program_id(1)
    body = functools.partial(_moba_body, qi=qi, q_ref=q_ref, k_ref=k_ref, va_ref=va_ref, vb_ref=vb_ref,
                             km_ref=km_ref, toep_ref=toep_ref, spread_ref=spread_ref, o_ref=o_ref,
                             nblk=seq // MOBA_BLOCK)
    _for_causal_width(qi, seq, body)


def _moba_spread(seq):
    nblk = seq // MOBA_BLOCK
    row = np.arange(LANE)[None, :, None]
    h = np.arange(N_HEADS)[:, None, None]
    blk_of_key = (np.arange(seq) // MOBA_BLOCK)[None, None, :]
    return jnp.asarray(row == h * nblk + blk_of_key, BF16)


def _moba(q, k, va, vb, k_mean, toep):
    b, seq, _ = q.shape
    nblk = seq // MOBA_BLOCK
    if nblk & (nblk - 1) or N_HEADS * nblk > LANE:
        raise ValueError("MoBA lane layout needs a power-of-two block count with heads*blocks <= 128")
    spread = _moba_spread(seq)
    qblock = pl.BlockSpec((1, ATT_ROWS, 256), lambda bi, qi: (bi, qi, 0))
    whole = pl.BlockSpec((1, seq, 256), lambda bi, qi: (bi, 0, 0))
    means = pl.BlockSpec((1, nblk, 256), lambda bi, qi: (bi, 0, 0))
    return pl.pallas_call(
        functools.partial(_moba_kernel, seq=seq),
        grid=(b, seq // ATT_ROWS),
        in_specs=[qblock, whole, whole, whole, means, _full(toep.shape), _full(spread.shape)],
        out_specs=qblock,
        out_shape=jax.ShapeDtypeStruct((b, seq, 256), BF16),
        compiler_params=_params("parallel", "parallel"),
        name="moba",
    )(q, k, va, vb, k_mean, toep, spread)


def _mla_body(width, qi, q_ref, k_ref, va_ref, vb_ref, causal_ref, o_ref):
    nchunk = width // LANE
    sub = ATT_ROWS // LANE
    causal = jnp.concatenate(
        [jnp.concatenate([causal_ref[0, jnp.clip(qi * sub + j - c + 1, 0, 2)] for c in range(nchunk)], axis=1)
         for j in range(sub)], axis=0)
    logits = []
    for h in range(N_HEADS):
        qs = slice(h * MLA_SLOT, (h + 1) * MLA_SLOT)
        logits.append(_dot_t(q_ref[0, :, qs], k_ref[0, :width, qs]) * MLA_QK ** -0.5 + causal)
    weights = [_exp_weights(x) for x in logits]
    for pair in range(N_HEADS // 2):
        ps = slice(pair * LANE, (pair + 1) * LANE)
        outs = [_dot(weights[2 * pair], va_ref[0, :width, ps]), _dot(weights[2 * pair + 1], vb_ref[0, :width, ps])]
        o_ref[0, :, ps] = _pair_output(*outs).astype(o_ref.dtype)


def _mla_kernel(q_ref, k_ref, va_ref, vb_ref, causal_ref, o_ref, *, seq):
    qi = pl.program_id(1)
    body = functools.partial(_mla_body, qi=qi, q_ref=q_ref, k_ref=k_ref, va_ref=va_ref, vb_ref=vb_ref,
                             causal_ref=causal_ref, o_ref=o_ref)
    _for_causal_width(qi, seq, body, classes=2 * WIDTH_CLASSES)


def _mla(q, k, va, vb):
    b, seq, _ = q.shape
    tri = np.where(np.arange(LANE)[:, None] >= np.arange(LANE)[None, :], 0.0, NEG_INF)
    causal = jnp.asarray(np.stack([np.full((LANE, LANE), NEG_INF), tri, np.zeros((LANE, LANE))])[None], F32)
    return pl.pallas_call(
        functools.partial(_mla_kernel, seq=seq),
        grid=(b, seq // ATT_ROWS),
        in_specs=[pl.BlockSpec((1, ATT_ROWS, N_HEADS * MLA_SLOT), lambda bi, qi: (bi, qi, 0)),
                  pl.BlockSpec((1, seq, N_HEADS * MLA_SLOT), lambda bi, qi: (bi, 0, 0)),
                  pl.BlockSpec((1, seq, 256), lambda bi, qi: (bi, 0, 0)),
                  pl.BlockSpec((1, seq, 256), lambda bi, qi: (bi, 0, 0)),
                  _full(causal.shape)],
        out_specs=pl.BlockSpec((1, ATT_ROWS, 256), lambda bi, qi: (bi, qi, 0)),
        out_shape=jax.ShapeDtypeStruct((b, seq, 256), BF16),
        compiler_params=_params("parallel", "parallel"),
        name="mla",
    )(q, k, va, vb, causal)


def _dil_group(dil, seq, toep_ref, g, q_ref, k_ref, v_ref, m_ref, l_ref, acc_ref):
    n_band = seq // (dil * LANE)

    def unit(u, carry):
        r = lax.div(u, n_band)
        n = lax.rem(u, n_band)
        if dil == 1:
            cur = pl.ds(pl.multiple_of(n * LANE, LANE), LANE)
            prev = pl.ds(pl.multiple_of(jnp.maximum(n - 1, 0) * LANE, LANE), LANE)
        else:
            cur = pl.ds(n * (LANE * dil) + r, LANE, stride=dil)
            prev = pl.ds(jnp.maximum(n - 1, 0) * (LANE * dil) + r, LANE, stride=dil)
        prev_tile = jnp.where(n > 0, 2, 0)
        first = lax.broadcasted_iota(jnp.int32, (LANE, LANE), 1) < HEAD_DIM
        kk = [jnp.concatenate([k_ref[pair, prev, :], k_ref[pair, cur, :]], axis=0).astype(BF16)
              for pair in range(N_HEADS // 2)]
        vv = [jnp.concatenate([v_ref[pair, prev, :], v_ref[pair, cur, :]], axis=0).astype(BF16)
              for pair in range(N_HEADS // 2)]
        logits = []
        for h in range(N_HEADS):
            mine = first if h % 2 == 0 else ~first
            bias = jnp.concatenate([toep_ref[g, h, prev_tile], toep_ref[g, h, 1]], axis=1)
            logits.append(_dot_t(jnp.where(mine, q_ref[h // 2, cur, :], 0.0).astype(BF16), kk[h // 2]) + bias)
        ms = [jnp.max(x, axis=1, keepdims=True) for x in logits]
        ps = [jnp.exp(x - m) for x, m in zip(logits, ms)]
        ls = [jnp.sum(p, axis=1, keepdims=True) for p in ps]
        pvs = [_dot(p.astype(BF16), vv[h // 2]) for h, p in enumerate(ps)]
        for pair in range(N_HEADS // 2):
            m_new, l_new, pv = (jnp.where(first, x[2 * pair], x[2 * pair + 1]) for x in (ms, ls, pvs))
            m_old = m_ref[pair, cur, :]
            m_tot = jnp.maximum(m_old, m_new)
            a_old = jnp.exp(m_old - m_tot)
            a_new = jnp.exp(m_new - m_tot)
            m_ref[pair, cur, :] = m_tot
            l_ref[pair, cur, :] = a_old * l_ref[pair, cur, :] + a_new * l_new
            acc_ref[pair, cur, :] = a_old * acc_ref[pair, cur, :] + a_new * pv
        return carry

    lax.fori_loop(0, dil * n_band, unit, 0, unroll=2)


def _get_rows(ref, rows):
    return jnp.concatenate([ref[0, rows, :], ref[1, rows, :]], axis=1)


def _dil_kernel(q_ref, k_ref, v_ref, toep_ref, o_ref, m_ref, l_ref, acc_ref, *, seq):
    g = pl.program_id(1)
    everything = pl.ds(0, seq)

    @pl.when(g == 0)
    def _():
        m_ref[...] = jnp.full(m_ref.shape, NEG_INF, F32)
        l_ref[...] = jnp.zeros_like(l_ref)
        acc_ref[...] = jnp.zeros_like(acc_ref)

    for gi, (_, dil) in enumerate(DIL_PATTERNS):
        pl.when(g == gi)(functools.partial(_dil_group, dil, seq, toep_ref, gi, q_ref, k_ref, v_ref,
                                           m_ref, l_ref, acc_ref))

    @pl.when(g == DIL_GROUPS - 1)
    def _():
        o_ref[0] = (_get_rows(acc_ref, everything) / _get_rows(l_ref, everything)).astype(o_ref.dtype)


def _dil(q, k, v, toep, seq):
    b = q.shape[1] // seq
    group = pl.BlockSpec((2, seq, LANE), lambda bi, g: (g, bi, 0))
    state = pltpu.VMEM((2, seq, LANE), F32)
    return pl.pallas_call(
        functools.partial(_dil_kernel, seq=seq),
        grid=(b, DIL_GROUPS),
        in_specs=[group, group, group, _full(toep.shape)],
        out_specs=pl.BlockSpec((1, seq, 256), lambda bi, g: (bi, 0, 0)),
        out_shape=jax.ShapeDtypeStruct((b, seq, 256), BF16),
        scratch_shapes=[state] * 3,
        compiler_params=_params("parallel", "arbitrary"),
        name="dil",
    )(q, k, v, toep)


def _merge_kernel(x_ref, xn_ref, oa_ref, ob_ref, oc_ref, od_ref, wg_ref, bg_ref, wb_ref, wo_ref, xo_ref):
    branches = (oa_ref, ob_ref, oc_ref, od_ref)
    half = ROW_TILE // 2
    for s in range(2):
        rows = slice(s * half, (s + 1) * half)
        xn = xn_ref[rows, :]
        total = None
        for n in range(N_BRANCH):
            cols = slice(n * D_MODEL, (n + 1) * D_MODEL)
            gate = _sigmoid(_dot(xn, wg_ref[:, cols]) + bg_ref[:, cols])
            term = gate * _dot(branches[n][rows, :], wb_ref[n])
            total = term if total is None else total + term
        xo_ref[rows, :] = x_ref[rows, :] + _dot(total.astype(BF16), wo_ref[...])


def _merge(x, xn, outs, wg, bg, wb, wo):
    n = x.shape[0]
    rows = lambda width: pl.BlockSpec((ROW_TILE, width), lambda i: (i, 0))
    return pl.pallas_call(
        _merge_kernel,
        grid=(n // ROW_TILE,),
        in_specs=[rows(D_MODEL), rows(D_MODEL)] + [rows(256)] * N_BRANCH
                 + [_full(wg.shape), _full(bg.shape), _full(wb.shape), _full(wo.shape)],
        out_specs=rows(D_MODEL),
        out_shape=jax.ShapeDtypeStruct((n, D_MODEL), F32),
        compiler_params=_params("parallel"),
        name="merge",
    )(x, xn, *outs, wg, bg, wb, wo)


def _bucket_of_distance():
    d = np.arange(REL_MAX_DIST + 1)
    exact = REL_BUCKETS // 2
    nf = np.maximum(d, 1).astype(np.float32)
    log_b = exact + (np.log(nf / exact) / math.log(REL_MAX_DIST / exact) * (REL_BUCKETS - exact)).astype(np.int32)
    return np.where(d < exact, d, np.minimum(log_b, REL_BUCKETS - 1))


def _toeplitz_tiles(table, n_off, dil, max_steps=None):
    span = 2 * LANE
    m = np.arange(span)
    off = np.arange(n_off)[:, None]
    steps = off * LANE + LANE - 1 - m[None, :]
    live = (steps >= 0) if max_steps is None else (steps >= 0) & (steps <= max_steps)
    v = table[:, _bucket_of_distance()[np.clip(steps * dil, 0, REL_MAX_DIST)]]
    v = jnp.where(live[None], v, NEG_INF)
    flat = jnp.tile(v, (1, 1, LANE))[..., :LANE * (span - 1)]
    tiles = flat.reshape(table.shape[0], n_off, LANE, span - 1)[..., LANE - 1:]
    return jnp.concatenate([jnp.full((table.shape[0], 1, LANE, LANE), NEG_INF, F32), tiles], axis=1)


def _same_head(width, head):
    idx = np.arange(width) // head
    return jnp.asarray(idx[:, None] == idx[None, :], BF16)


def _rope_tables(seq):
    half = MLA_ROPE // 2
    freqs = ROPE_THETA ** (-np.arange(half, dtype=np.float64) / half)
    ang = np.arange(seq, dtype=np.float64)[:, None] * freqs[None, :]
    pad = np.zeros((seq, MLA_SLOT - MLA_QK))
    cos_h = np.concatenate([np.ones((seq, MLA_NOPE)), np.cos(ang), np.cos(ang), pad], axis=1)
    sin_h = np.concatenate([np.zeros((seq, MLA_NOPE)), np.sin(ang), np.sin(ang), pad], axis=1)
    return (jnp.asarray(np.tile(cos_h, (1, N_HEADS)), F32), jnp.asarray(np.tile(sin_h, (1, N_HEADS)), F32))


def _rot_half_cols():
    half = MLA_ROPE // 2
    src = np.arange(MLA_SLOT)
    sign = np.zeros(MLA_SLOT, np.float32)
    src[MLA_NOPE:MLA_NOPE + half] = np.arange(MLA_NOPE + half, MLA_QK)
    sign[MLA_NOPE:MLA_NOPE + half] = -1.0
    src[MLA_NOPE + half:MLA_QK] = np.arange(MLA_NOPE, MLA_NOPE + half)
    sign[MLA_NOPE + half:MLA_QK] = 1.0
    return src, sign


def _mixer(x, xn, seq, w_in, b_gate, qk_a, qk_b, qk_c, qk_d, mla_nq, w_uq, mla_nkv, w_ukv,
           w_branch, w_out, toeps, consts):
    n = x.shape[0]
    b = n // seq
    e64, e_slots, tri, cos, sin = consts
    toep_a, toep_b, toep_d = toeps
    row = lambda a: a.reshape(1, -1)
    tile4 = lambda g: jnp.tile(g, N_HEADS).reshape(1, -1)
    r3 = lambda a: a.reshape(b, seq, a.shape[-1])

    off_iq = OFF_A + 6 * HEAD_DIM
    off_ik = off_iq + N_HEADS * IDX_DIM
    iq_rep = jnp.concatenate([jnp.tile(w_in[:, off_iq + h * IDX_DIM:off_iq + (h + 1) * IDX_DIM], (1, 4))
                              for h in range(N_HEADS)], axis=1)
    ik_rep = jnp.tile(w_in[:, off_ik:off_ik + IDX_DIM], (1, 4))
    iw_pad = jnp.pad(w_in[:, off_ik + IDX_DIM:OFF_A + W_A], ((0, 0), (0, LANE - N_HEADS)))
    half_pad = ((0, 0), (0, LANE - HEAD_DIM))
    q_slots = jnp.concatenate([jnp.pad(w_in[:, OFF_A + h * HEAD_DIM:OFF_A + (h + 1) * HEAD_DIM], half_pad)
                               for h in range(N_HEADS)], axis=1)
    off_k = OFF_A + N_HEADS * HEAD_DIM
    k_slot = jnp.pad(w_in[:, off_k:off_k + HEAD_DIM], half_pad)
    v_twice = jnp.tile(w_in[:, off_k + HEAD_DIM:off_iq], (1, 2))
    wa = jnp.concatenate([q_slots, k_slot, v_twice, iq_rep, ik_rep, iw_pad], axis=1).astype(BF16)
    gq_slots = jnp.tile(jnp.pad(qk_a[0], (0, LANE - HEAD_DIM)), N_HEADS).reshape(1, -1)
    gk_slot = jnp.pad(qk_a[1], (0, LANE - HEAD_DIM)).reshape(1, -1)
    piece_a = _proj_a(n, wa, e_slots, gq_slots, gk_slot)

    wb_in = w_in[:, OFF_B:OFF_B + W_B].astype(BF16)
    piece_b = _proj_b(n, wb_in, e64, tile4(qk_b[0]), tile4(qk_b[1]))

    src, sign = _rot_half_cols()
    src4 = np.concatenate([h * MLA_SLOT + src for h in range(N_HEADS)])
    sign4 = jnp.asarray(np.tile(sign, N_HEADS))
    slot_pad = MLA_SLOT - MLA_QK
    w_kr = w_in[:, OFF_C + MLA_Q_LORA + MLA_KV_LORA:OFF_C + W_C]
    kr_slots = jnp.pad(w_kr, ((0, 0), (MLA_NOPE, slot_pad)))
    kr_slots = jnp.tile(kr_slots, (1, N_HEADS))
    wc = jnp.concatenate([w_in[:, OFF_C:OFF_C + MLA_Q_LORA + MLA_KV_LORA], kr_slots,
                          kr_slots[:, src4] * sign4], axis=1).astype(BF16)
    slots = lambda w: jnp.pad(w.reshape(w.shape[0], N_HEADS, -1),
                              ((0, 0), (0, 0), (0, MLA_SLOT - w.shape[1] // N_HEADS))).reshape(w.shape[0], -1)
    wuq = slots(w_uq)
    ukv = w_ukv.reshape(MLA_KV_LORA, N_HEADS, MLA_NOPE + MLA_V)
    wuk = slots(ukv[:, :, :MLA_NOPE].reshape(MLA_KV_LORA, -1))
    wuv = ukv[:, :, MLA_NOPE:].reshape(MLA_KV_LORA, -1)
    gq = jnp.tile(jnp.pad(qk_c[0], (0, slot_pad)), N_HEADS)
    gk = jnp.tile(jnp.pad(qk_c[1], (0, slot_pad)), N_HEADS)
    piece_c = _proj_c(
        n, wc, e_slots, row(mla_nq), row(mla_nkv), wuq.astype(BF16), (wuq[:, src4] * sign4).astype(BF16),
        wuk.astype(BF16), wuv.astype(BF16), row(gq), row(gq[src4]), row(gk), row(gk[src4]), cos, sin, seq)

    wd = w_in[:, OFF_D:OFF_D + W_D].astype(BF16)
    piece_d = _proj_d(n, wd, e64, tile4(qk_d[0]), tile4(qk_d[1]))

    proj_a, proj_b, proj_c, proj_d = _project(xn, (piece_a, piece_b, piece_c, piece_d))
    out_a = _dsa(*map(r3, proj_a), toep_a, tri).reshape(n, 256)
    qb, kb, vb_first, vb_second, kmean = proj_b
    out_b = _moba(r3(qb), r3(kb), r3(vb_first), r3(vb_second),
                  kmean.reshape(b, seq // MOBA_BLOCK, BRANCH_WIDTH), toep_b).reshape(n, 256)
    out_c = _mla(*map(r3, proj_c)).reshape(n, 256)
    out_d = _dil(*proj_d, toep_d, seq).reshape(n, 256)

    wg = w_in[:, OFF_G:OFF_G + W_G].astype(BF16)
    return _merge(x, xn, (out_a, out_b, out_c, out_d), wg, b_gate.reshape(1, -1),
                  w_branch.astype(BF16), w_out.astype(BF16))


def kernel(x, norm_gain, w_in, b_gate, qk_gain_a, qk_gain_b, qk_gain_c, qk_gain_d, mla_norm_q,
           w_mla_uq, mla_norm_kv, w_mla_ukv, w_branch, w_out, rel_bias, w_ffn_in, w_ffn_out):
    b, seq, d = x.shape
    depth = norm_gain.shape[0]
    toep_a = _toeplitz_tiles(rel_bias[0:4], seq // LANE, 1)
    toep_b = _toeplitz_tiles(rel_bias[4:8], seq // LANE, 1)
    toep_d = jnp.stack([_toeplitz_tiles(rel_bias[8 + 4 * g:12 + 4 * g], 2, dil, max_steps=window // dil)
                        for g, (window, dil) in enumerate(DIL_PATTERNS)])
    tri = jnp.asarray(np.arange(LANE)[:, None] < np.arange(LANE)[None, :], BF16)
    consts = (_same_head(256, HEAD_DIM), _same_head(256, MLA_SLOT), tri) + _rope_tables(seq)

    x = x.reshape(b * seq, d)
    for l in range(depth):
        g = norm_gain[l]
        x, xn = _ffn(x, g[0:1], g[1:2], w_ffn_in[l, 0].astype(BF16), w_ffn_out[l, 0].astype(BF16))
        x = _mixer(x, xn, seq, w_in[l], b_gate[l], qk_gain_a[l], qk_gain_b[l], qk_gain_c[l],
                   qk_gain_d[l], mla_norm_q[l], w_mla_uq[l], mla_norm_kv[l], w_mla_ukv[l],
                   w_branch[l], w_out[l], (toep_a, toep_b, toep_d), consts)
        x, _ = _ffn(x, g[2:3], g[2:3], w_ffn_in[l, 1].astype(BF16), w_ffn_out[l, 1].astype(BF16))
    return x.reshape(b, seq, d)
```
